```python
import math
import jax
import jax.numpy as jnp
from jax import lax
import numpy as np

D_MODEL = 2048
BATCH = 2
SEQ = 8192
DEPTH = 2

CHUNK = 64
N_META = 16
EPS = 1e-6

A_HEAD_DIM = 128
A_HEADS = D_MODEL // (2 * A_HEAD_DIM)
A_WIDTH = A_HEADS * A_HEAD_DIM
Q_BLOCK = 128

B_CHANNELS = D_MODEL // 2
B_CONV_WIDTH = 31

C_HEAD_DIM = 128
C_HEADS = D_MODEL // (2 * C_HEAD_DIM)
C_WIDTH = C_HEADS * C_HEAD_DIM
C_CONV_WIDTH = 4

D_WIDTH = D_MODEL // 2
D_BLOCK_DIM = 128
D_BLOCKS = D_WIDTH // D_BLOCK_DIM
D_CONV_WIDTH = 4
LRU_C = 8.0

N_GROUPS = 4
EXPERTS_PER_GROUP = 8
N_EXPERTS = N_GROUPS * EXPERTS_PER_GROUP
TOP_K = 2
D_EXPERT = D_MODEL // 4
MOE_BLOCK = 128

EVEN_IN_SIZES = (A_WIDTH, A_WIDTH, A_WIDTH, A_HEADS, B_CHANNELS, B_CHANNELS)
ODD_IN_SIZES = (3 * C_WIDTH, C_HEADS, C_HEADS, C_WIDTH, D_WIDTH, D_WIDTH)
EVEN_OUT = A_WIDTH + B_CHANNELS
ODD_OUT = C_WIDTH + D_WIDTH
N_EVEN = (DEPTH + 1) // 2
N_ODD = DEPTH // 2

kernel_name = "hybrid_fox_conformer_gdn_rglru_hmoe"


def _split(z, sizes):
    return jnp.split(z, [int(s) for s in np.cumsum(sizes)[:-1]], axis=-1)


def rms_norm(x, g):
    xf = x.astype(jnp.float32)
    y = xf * lax.rsqrt(jnp.mean(xf * xf, axis=-1, keepdims=True) + EPS)
    return (y * g.astype(jnp.float32)).astype(x.dtype)


def layer_norm(x, g, b):
    xf = x.astype(jnp.float32)
    mu = jnp.mean(xf, axis=-1, keepdims=True)
    var = jnp.mean(jnp.square(xf - mu), axis=-1, keepdims=True)
    y = (xf - mu) * lax.rsqrt(var + EPS)
    return (y * g.astype(jnp.float32) + b.astype(jnp.float32)).astype(x.dtype)


def l2_normalize(x):
    return x * lax.rsqrt(jnp.sum(x * x, axis=-1, keepdims=True) + EPS)


def causal_depthwise_conv(x, w):
    K, T = w.shape[0], x.shape[1]
    xp = jnp.pad(x, ((0, 0), (K - 1, 0), (0, 0)))
    y = xp[:, 0:T] * w[0]
    for j in range(1, K):
        y = y + xp[:, j:j + T] * w[j]
    return y


def forgetting_attention(q, k, v, log_f):
    Bsz, T, H, dh = q.shape
    Tp = -(-T // Q_BLOCK) * Q_BLOCK
    q, k, v = (jnp.pad(t, ((0, 0), (0, Tp - T), (0, 0), (0, 0))).transpose(0, 2, 1, 3)
               for t in (q, k, v))
    c = jnp.cumsum(jnp.pad(log_f, ((0, 0), (0, Tp - T), (0, 0))), axis=1).transpose(0, 2, 1)
    scale = dh ** -0.5
    diag = jnp.tril(jnp.ones((Q_BLOCK, Q_BLOCK), bool))
    outs = []
    for i in range(Tp // Q_BLOCK):
        q0, q1 = i * Q_BLOCK, (i + 1) * Q_BLOCK
        s = jnp.einsum('bhqd,bhkd->bhqk', q[:, :, q0:q1], k[:, :, :q1]).astype(jnp.float32) * scale
        s = s + (c[:, :, q0:q1, None] - c[:, :, None, :q1])
        mask = jnp.concatenate([jnp.ones((Q_BLOCK, q0), bool), diag], axis=1)
        p = jax.nn.softmax(jnp.where(mask, s, -jnp.inf), axis=-1)
        outs.append(jnp.einsum('bhqk,bhkd->bqhd', p.astype(v.dtype), v[:, :, :q1]))
    return jnp.concatenate(outs, axis=1)[:, :T]


def gated_delta_rule(q, k, v, g, beta):
    Bsz, T, H, dk = q.shape
    dv = v.shape[-1]
    front = (-N_META) % CHUNK
    back = (-(T + front)) % CHUNK
    Tp = T + front + back
    n = Tp // CHUNK

    def chunks(t):
        t = jnp.pad(t, ((0, 0), (front, back)) + ((0, 0),) * (t.ndim - 2))
        return jnp.moveaxis(t.reshape((Bsz, n, CHUNK, H) + t.shape[3:]), 3, 1)

    q = chunks(q) * dk ** -0.5
    k, v, g, beta = chunks(k), chunks(v), chunks(g), chunks(beta)
    gc = jnp.cumsum(g, axis=-1)
    incl = jnp.tril(jnp.ones((CHUNK, CHUNK), bool))
    strict = jnp.tril(jnp.ones((CHUNK, CHUNK), bool), -1)
    decay = jnp.exp(jnp.where(incl, gc[..., :, None] - gc[..., None, :], -jnp.inf))
    kb = k * beta[..., None]
    a_mat = jnp.where(strict, jnp.einsum('bhnid,bhnjd->bhnij', kb, k) * decay, 0.0)
    rhs = jnp.concatenate([v * beta[..., None], kb * jnp.exp(gc)[..., None]], axis=-1)
    eye = jnp.eye(CHUNK, dtype=a_mat.dtype)
    uw = lax.linalg.triangular_solve(a_mat + eye, rhs, left_side=True, lower=True,
                                     unit_diagonal=True)
    u, w = uw[..., :dv], uw[..., dv:]
    attn = jnp.einsum('bhnid,bhnjd->bhnij', q, k) * decay
    q_dec = q * jnp.exp(gc)[..., None]
    k_dec = k * jnp.exp(gc[..., -1:] - gc)[..., None]
    g_tot = jnp.exp(gc[..., -1])

    def step(S, xs):
        a_c, qd, kd, u_c, w_c, gt = xs
        v_new = u_c - jnp.einsum('bhik,bhkv->bhiv', w_c, S)
        o = jnp.einsum('bhik,bhkv->bhiv', qd, S) + jnp.einsum('bhij,bhjv->bhiv', a_c, v_new)
        S = S * gt[..., None, None] + jnp.einsum('bhik,bhiv->bhkv', kd, v_new)
        return S, o

    xs = tuple(jnp.moveaxis(t, 2, 0) for t in (attn, q_dec, k_dec, u, w, g_tot))
    S0 = jnp.zeros((Bsz, H, dk, dv), jnp.float32)
    _, o = lax.scan(step, S0, xs)
    o = jnp.transpose(o, (1, 0, 3, 2, 4)).reshape(Bsz, Tp, H, dv)
    return o[:, front:front + T]


def rg_lru(x, w_r, b_r, w_i, b_i, lam):
    Bsz, T, W = x.shape
    xb = x.reshape(Bsz, T, D_BLOCKS, D_BLOCK_DIM)
    r = jax.nn.sigmoid(jnp.einsum('btni,nio->btno', xb, w_r).reshape(Bsz, T, W) + b_r)
    i = jax.nn.sigmoid(jnp.einsum('btni,nio->btno', xb, w_i).reshape(Bsz, T, W) + b_i)
    log_a = -LRU_C * jax.nn.softplus(-lam.astype(jnp.float32)) * r.astype(jnp.float32)
    a = jnp.exp(log_a)
    b = jnp.sqrt(-jnp.expm1(2.0 * log_a)) * (i * x).astype(jnp.float32)

    def combine(left, right):
        a1, b1 = left
        a2, b2 = right
        return a1 * a2, a2 * b1 + b2

    _, h = lax.associative_scan(combine, (a, b), axis=1)
    return h.astype(x.dtype)


def even_mixer(h, w_in, forget_b, conv_w, conv_b, ln_g, ln_b, w_out):
    Bsz, T, _ = h.shape
    q, k, v, f_logit, glu_a, glu_b = _split(h @ w_in, EVEN_IN_SIZES)
    heads = lambda t: t.reshape(Bsz, T, A_HEADS, A_HEAD_DIM)
    log_f = jax.nn.log_sigmoid((f_logit + forget_b).astype(jnp.float32))
    y_a = forgetting_attention(heads(q), heads(k), heads(v), log_f).reshape(Bsz, T, A_WIDTH)
    u = glu_a * jax.nn.sigmoid(glu_b)
    u = causal_depthwise_conv(u, conv_w) + conv_b
    u = jax.nn.silu(layer_norm(u, ln_g, ln_b))
    return jnp.concatenate([y_a, u], axis=-1) @ w_out


def odd_mixer(h, w_in, qkv_conv_w, a_log, dt_bias, out_norm_g, lru_conv_w, lru_conv_b,
              w_rg, b_rg, w_ig, b_ig, lru_lambda, w_out):
    Bsz, T, _ = h.shape
    f32 = jnp.float32
    qkv, a_in, b_in, z_c, x_d, gate_d = _split(h @ w_in, ODD_IN_SIZES)
    qkv = jax.nn.silu(causal_depthwise_conv(qkv, qkv_conv_w)).astype(f32)
    q, k, v = (t.reshape(Bsz, T, C_HEADS, C_HEAD_DIM) for t in jnp.split(qkv, 3, axis=-1))
    q, k = l2_normalize(q), l2_normalize(k)
    g = -jnp.exp(a_log.astype(f32)) * jax.nn.softplus(a_in.astype(f32) + dt_bias.astype(f32))
    beta = jax.nn.sigmoid(b_in.astype(f32))
    o = gated_delta_rule(q, k, v, g, beta)
    o = rms_norm(o, out_norm_g) * jax.nn.silu(z_c.reshape(Bsz, T, C_HEADS, C_HEAD_DIM).astype(f32))
    y_c = o.reshape(Bsz, T, C_WIDTH).astype(h.dtype)
    x_r = causal_depthwise_conv(x_d, lru_conv_w) + lru_conv_b
    y_d = rg_lru(x_r, w_rg, b_rg, w_ig, b_ig, lru_lambda) * jax.nn.gelu(gate_d)
    return jnp.concatenate([y_c, y_d], axis=-1) @ w_out


def hierarchical_moe(h, w_group, b_group, w_expert, b_expert, w_gate, w_up, w_down):
    Bsz, T, D = h.shape
    xf = h.reshape(-1, D)
    N = xf.shape[0]
    g_logits = (xf @ w_group).astype(jnp.float32) + b_group
    g_sel = jnp.argmax(g_logits, axis=-1)
    g_prob = jnp.take_along_axis(jax.nn.softmax(g_logits, axis=-1), g_sel[:, None], axis=-1)
    e_logits = ((xf @ w_expert).astype(jnp.float32) + b_expert).reshape(N, N_GROUPS, EXPERTS_PER_GROUP)
    e_logits = jnp.take_along_axis(e_logits, g_sel[:, None, None], axis=1)[:, 0]
    top_p, top_i = lax.top_k(jax.nn.softmax(e_logits, axis=-1), TOP_K)
    weight = g_prob * top_p / jnp.sum(top_p, axis=-1, keepdims=True)
    expert = (g_sel[:, None] * EXPERTS_PER_GROUP + top_i).astype(jnp.int32)

    n_pairs = N * TOP_K
    e_flat = expert.reshape(-1)
    w_flat = weight.reshape(-1)
    tok = jnp.repeat(jnp.arange(N, dtype=jnp.int32), TOP_K)
    order = jnp.argsort(e_flat)
    e_sorted = e_flat[order]
    counts = jnp.bincount(e_flat, length=N_EXPERTS)
    starts = jnp.cumsum(counts) - counts
    padded = (counts + MOE_BLOCK - 1) // MOE_BLOCK * MOE_BLOCK
    p_ends = jnp.cumsum(padded)
    p_starts = p_ends - padded
    dest = p_starts[e_sorted] + jnp.arange(n_pairs) - starts[e_sorted]
    n_blocks = -(-(n_pairs + N_EXPERTS * (MOE_BLOCK - 1)) // MOE_BLOCK)
    R = n_blocks * MOE_BLOCK
    row_tok = jnp.full((R,), N, jnp.int32).at[dest].set(tok[order])
    row_w = jnp.zeros((R,), jnp.float32).at[dest].set(w_flat[order])
    block_expert = jnp.minimum(
        jnp.searchsorted(p_ends, jnp.arange(n_blocks) * MOE_BLOCK, side='right'),
        N_EXPERTS - 1).astype(jnp.int32)
    x_pad = jnp.concatenate([xf, jnp.zeros((1, D), xf.dtype)], axis=0)
    xb = x_pad[row_tok].reshape(n_blocks, MOE_BLOCK, D)

    def expert_block(args):
        x_blk, e = args
        hid = jax.nn.silu(x_blk @ w_gate[e]) * (x_blk @ w_up[e])
        return hid @ w_down[e]

    yb = lax.map(expert_block, (xb, block_expert)).reshape(R, D)
    y = jnp.zeros((N + 1, D), yb.dtype).at[row_tok].add(yb * row_w[:, None].astype(yb.dtype))
    return y[:N].reshape(Bsz, T, D)


def setup_inputs(seed: int = 0) -> dict:
    key = jax.random.key(seed)
    ks = iter(jax.random.split(key, 64))
    f32 = jnp.float32

    def nrm(shape, scale):
        return jax.random.normal(next(ks), shape, f32) * scale

    def gain(shape):
        return 1.0 + nrm(shape, 0.02)

    NE, NO = N_EVEN, N_ODD
    forget_b = jax.random.uniform(next(ks), (NE, A_HEADS), f32, 2.0, 5.0)
    a_log = jnp.log(jax.random.uniform(next(ks), (NO, C_HEADS), f32, 1.0, 16.0))
    dt = jnp.exp(jax.random.uniform(next(ks), (NO, C_HEADS), f32, math.log(1e-3), math.log(1e-1)))
    dt_bias = dt + jnp.log(-jnp.expm1(-dt))
    a0 = jax.random.uniform(next(ks), (NO, D_WIDTH), f32, 0.9, 0.999)
    s = a0 ** (1.0 / LRU_C)
    lru_lambda = jnp.log(s) - jnp.log1p(-s)
    return {
        "x": nrm((BATCH, SEQ, D_MODEL), 1.0),
        "meta_tokens": nrm((N_META, D_MODEL), 1.0),
        "norm_mix_g": gain((DEPTH, D_MODEL)),
        "norm_ffn_g": gain((DEPTH, D_MODEL)),
        "norm_final_g": gain((D_MODEL,)),
        "ab_w_in": nrm((NE, D_MODEL, sum(EVEN_IN_SIZES)), D_MODEL ** -0.5),
        "ab_forget_b": forget_b,
        "ab_conv_w": nrm((NE, B_CONV_WIDTH, B_CHANNELS), B_CONV_WIDTH ** -0.5),
        "ab_conv_b": nrm((NE, B_CHANNELS), 0.02),
        "ab_ln_g": gain((NE, B_CHANNELS)),
        "ab_ln_b": nrm((NE, B_CHANNELS), 0.02),
        "ab_w_out": nrm((NE, EVEN_OUT, D_MODEL), EVEN_OUT ** -0.5),
        "cd_w_in": nrm((NO, D_MODEL, sum(ODD_IN_SIZES)), D_MODEL ** -0.5),
        "cd_qkv_conv_w": nrm((NO, C_CONV_WIDTH, 3 * C_WIDTH), C_CONV_WIDTH ** -0.5),
        "cd_a_log": a_log,
        "cd_dt_bias": dt_bias,
        "cd_out_norm_g": gain((NO, C_HEAD_DIM)),
        "cd_lru_conv_w": nrm((NO, D_CONV_WIDTH, D_WIDTH), D_CONV_WIDTH ** -0.5),
        "cd_lru_conv_b": nrm((NO, D_WIDTH), 0.02),
        "cd_w_rg": nrm((NO, D_BLOCKS, D_BLOCK_DIM, D_BLOCK_DIM), D_BLOCK_DIM ** -0.5),
        "cd_b_rg": nrm((NO, D_WIDTH), 0.02),
        "cd_w_ig": nrm((NO, D_BLOCKS, D_BLOCK_DIM, D_BLOCK_DIM), D_BLOCK_DIM ** -0.5),
        "cd_b_ig": nrm((NO, D_WIDTH), 0.02),
        "cd_lru_lambda": lru_lambda,
        "cd_w_out": nrm((NO, ODD_OUT, D_MODEL), ODD_OUT ** -0.5),
        "moe_w_group": nrm((DEPTH, D_MODEL, N_GROUPS), D_MODEL ** -0.5),
        "moe_b_group": nrm((DEPTH, N_GROUPS), 0.01),
        "moe_w_expert": nrm((DEPTH, D_MODEL, N_EXPERTS), D_MODEL ** -0.5),
        "moe_b_expert": nrm((DEPTH, N_EXPERTS), 0.01),
        "moe_w_gate": nrm((DEPTH, N_EXPERTS, D_MODEL, D_EXPERT), D_MODEL ** -0.5),
        "moe_w_up": nrm((DEPTH, N_EXPERTS, D_MODEL, D_EXPERT), D_MODEL ** -0.5),
        "moe_w_down": nrm((DEPTH, N_EXPERTS, D_EXPERT, D_MODEL), D_EXPERT ** -0.5),
    }


def reference(x, meta_tokens, norm_mix_g, norm_ffn_g, norm_final_g,
              ab_w_in, ab_forget_b, ab_conv_w, ab_conv_b, ab_ln_g, ab_ln_b, ab_w_out,
              cd_w_in, cd_qkv_conv_w, cd_a_log, cd_dt_bias, cd_out_norm_g,
              cd_lru_conv_w, cd_lru_conv_b, cd_w_rg, cd_b_rg, cd_w_ig, cd_b_ig,
              cd_lru_lambda, cd_w_out,
              moe_w_group, moe_b_group, moe_w_expert, moe_b_expert,
              moe_w_gate, moe_w_up, moe_w_down):
    Bsz = x.shape[0]
    meta = jnp.broadcast_to(meta_tokens[None].astype(x.dtype), (Bsz, N_META, D_MODEL))
    h = jnp.concatenate([meta, x], axis=1)
    for layer in range(DEPTH):
        i = layer // 2
        hn = rms_norm(h, norm_mix_g[layer])
        if layer % 2 == 0:
            h = h + even_mixer(hn, ab_w_in[i], ab_forget_b[i], ab_conv_w[i], ab_conv_b[i],
                               ab_ln_g[i], ab_ln_b[i], ab_w_out[i])
        else:
            h = h + odd_mixer(hn, cd_w_in[i], cd_qkv_conv_w[i], cd_a_log[i], cd_dt_bias[i],
                              cd_out_norm_g[i], cd_lru_conv_w[i], cd_lru_conv_b[i],
                              cd_w_rg[i], cd_b_rg[i], cd_w_ig[i], cd_b_ig[i],
                              cd_lru_lambda[i], cd_w_out[i])
        h = h + hierarchical_moe(rms_norm(h, norm_ffn_g[layer]), moe_w_group[layer],
                                 moe_b_group[layer], moe_w_expert[layer], moe_b_expert[layer],
                                 moe_w_gate[layer], moe_w_up[layer], moe_w_down[layer])
    return rms_norm(h, norm_final_g)[:, N_META:]
```

```python
import functools

import jax
import jax.numpy as jnp
from jax import lax
from jax.experimental import pallas as pl
from jax.experimental.pallas import tpu as pltpu

F32 = jnp.float32
BF16 = jnp.bfloat16
I32 = jnp.int32

EPS = 1e-6
N_META = 16
CHUNK = 64
FRONT = CHUNK - N_META
OFF = FRONT + N_META
LANES = 128
UNIT = 256
HEAD_DIM = 128
N_HEADS = 8
CONF_K = 31
SHORT_K = 4
LRU_C = 8.0
N_GROUPS = 4
EXPERTS_PER_GROUP = 8
N_EXPERTS = N_GROUPS * EXPERTS_PER_GROUP
MOE_BM = 256
NEG = -1e30
VMEM_LIMIT = 56 * 1024 * 1024


def _cparams(sem):
    return pltpu.CompilerParams(dimension_semantics=sem, vmem_limit_bytes=VMEM_LIMIT)


def _row_valid(pos0, rows, n_real):
    pos = pos0 + lax.broadcasted_iota(I32, (rows, 1), 0)
    return (pos >= FRONT) & (pos < FRONT + n_real)


def _sigmoid(x):
    return 1.0 / (1.0 + jnp.exp(-x))


def _softplus(x):
    return jnp.maximum(x, 0.0) + jnp.log1p(jnp.exp(-jnp.abs(x)))


def _silu(x):
    return x * _sigmoid(x)


def _norm_inproj_body(h_ref, g_ref, w_ref, wst_ref, wsc_ref, o_ref, ot_ref, oc_ref, xn_ref):
    @pl.when(pl.program_id(1) == 0)
    def _():
        x = h_ref[...]
        ms = jnp.mean(x * x, axis=-1, keepdims=True)
        xn = (x * lax.rsqrt(ms + EPS) * g_ref[...]).astype(BF16)
        xn_ref[...] = xn
        ot_ref[...] = lax.dot_general(wst_ref[...], xn, (((1,), (1,)), ((), ())),
                                      preferred_element_type=F32)
        oc_ref[...] = jnp.dot(xn, wsc_ref[...], preferred_element_type=F32)

    o_ref[...] = jnp.dot(xn_ref[...], w_ref[...], preferred_element_type=F32).astype(o_ref.dtype)


def _norm_inproj(h, g, w_main, w_small_t, w_small_c, bm, bn):
    n, d = h.shape
    nw = w_main.shape[1]
    return pl.pallas_call(
        _norm_inproj_body,
        out_shape=[jax.ShapeDtypeStruct((n, nw), BF16),
                   jax.ShapeDtypeStruct((16, n), F32),
                   jax.ShapeDtypeStruct((n, LANES), F32)],
        grid=(n // bm, nw // bn),
        in_specs=[pl.BlockSpec((bm, d), lambda i, j: (i, 0)),
                  pl.BlockSpec((1, d), lambda i, j: (0, 0)),
                  pl.BlockSpec((d, bn), lambda i, j: (0, j)),
                  pl.BlockSpec((16, d), lambda i, j: (0, 0)),
                  pl.BlockSpec((d, LANES), lambda i, j: (0, 0))],
        out_specs=[pl.BlockSpec((bm, bn), lambda i, j: (i, j)),
                   pl.BlockSpec((16, bm), lambda i, j: (0, i)),
                   pl.BlockSpec((bm, LANES), lambda i, j: (i, 0))],
        scratch_shapes=[pltpu.VMEM((bm, d), BF16)],
        compiler_params=_cparams(("arbitrary", "arbitrary")),
        name="norm_inproj",
    )(h, g, w_main, w_small_t, w_small_c)


def _fox_gate_body(ft_ref, fb_ref, c_ref, *, tp, n_real):
    f = ft_ref[0:N_HEADS, :] + fb_ref[:, 0:1]
    lf = jnp.minimum(f, 0.0) - jnp.log1p(jnp.exp(-jnp.abs(f)))
    col = lax.broadcasted_iota(I32, (N_HEADS, tp), 1)
    lf = jnp.where((col >= FRONT) & (col < FRONT + n_real), lf, 0.0)
    r = lax.broadcasted_iota(I32, (LANES, LANES), 0)
    c = lax.broadcasted_iota(I32, (LANES, LANES), 1)
    upper = (r <= c).astype(F32)
    carry = jnp.zeros((N_HEADS, 1), F32)
    for j in range(tp // LANES):
        blk = lf[:, j * LANES:(j + 1) * LANES]
        cs = jnp.dot(blk, upper, preferred_element_type=F32, precision=lax.Precision.HIGHEST) + carry
        c_ref[:, j * LANES:(j + 1) * LANES] = cs
        carry = cs[:, LANES - 1:LANES]
    c_ref[:, 0:LANES] = jnp.where(col[:, 0:LANES] < FRONT, -NEG, c_ref[:, 0:LANES])


def _fox_gate(ft, forget_b, bsz, tp, n_real):
    n = ft.shape[1]
    fb = jnp.broadcast_to(forget_b.astype(F32)[:, None], (N_HEADS, LANES))
    return pl.pallas_call(
        functools.partial(_fox_gate_body, tp=tp, n_real=n_real),
        out_shape=jax.ShapeDtypeStruct((N_HEADS, n), F32),
        grid=(bsz,),
        in_specs=[pl.BlockSpec((16, tp), lambda b: (0, b)),
                  pl.BlockSpec((N_HEADS, LANES), lambda b: (0, 0))],
        out_specs=pl.BlockSpec((N_HEADS, tp), lambda b: (0, b)),
        compiler_params=_cparams(("arbitrary",)),
        name="fox_gate",
    )(ft, fb)


def _fox_attn_body(q_ref, k_ref, v_ref, ck_ref, cq_ref, o_ref, *, blk, scale):
    qi = pl.program_id(2)
    q = q_ref[...]
    c_last = cq_ref[0, :, blk - 1:blk]

    def step(j, carry, diagonal):
        m, l, acc = carry
        start = pl.multiple_of(j * blk, blk)
        k = k_ref[pl.ds(start, blk), :]
        v = v_ref[pl.ds(start, blk), :]
        s = lax.dot_general(q, k, (((1,), (1,)), ((), ())), preferred_element_type=F32) * scale
        s = s + (c_last - ck_ref[0, :, pl.ds(start, blk)])
        if diagonal:
            row = lax.broadcasted_iota(I32, (blk, blk), 0)
            col = lax.broadcasted_iota(I32, (blk, blk), 1)
            s = jnp.where(col <= row, s, NEG)
        m_new = jnp.maximum(m, jnp.max(s, axis=1, keepdims=True))
        alpha = jnp.exp(m - m_new)
        p = jnp.exp(s - m_new)
        l = alpha * l + jnp.sum(p, axis=1, keepdims=True)
        acc = alpha * acc + jnp.dot(p.astype(BF16), v, preferred_element_type=F32)
        return m_new, l, acc

    init = (jnp.full((blk, 1), NEG, F32), jnp.zeros((blk, 1), F32), jnp.zeros((blk, HEAD_DIM), F32))
    carry = lax.fori_loop(0, qi, lambda j, c: step(j, c, False), init)
    _, l, acc = step(qi, carry, True)
    o_ref[...] = (acc / l).astype(o_ref.dtype)


def _fox_attn(z, c3, bsz, tp):
    n = z.shape[0]
    blk = UNIT
    nq = tp // blk
    return pl.pallas_call(
        functools.partial(_fox_attn_body, blk=blk, scale=HEAD_DIM ** -0.5),
        out_shape=jax.ShapeDtypeStruct((n, N_HEADS * HEAD_DIM), BF16),
        grid=(bsz, N_HEADS, nq),
        in_specs=[pl.BlockSpec((blk, HEAD_DIM), lambda b, h, i: (b * nq + i, h)),
                  pl.BlockSpec((tp, HEAD_DIM), lambda b, h, i: (b, N_HEADS + h)),
                  pl.BlockSpec((tp, HEAD_DIM), lambda b, h, i: (b, 2 * N_HEADS + h)),
                  pl.BlockSpec((1, 1, tp), lambda b, h, i: (h, 0, b)),
                  pl.BlockSpec((1, 1, blk), lambda b, h, i: (h, 0, b * nq + i))],
        out_specs=pl.BlockSpec((blk, HEAD_DIM), lambda b, h, i: (b * nq + i, h)),
        compiler_params=_cparams(("arbitrary", "arbitrary", "arbitrary")),
        name="fox_attn",
    )(z, z, z, c3, c3)


def _shifted_taps(x_ext, halo, rows, n_taps):
    for r in range(min(8, n_taps)):
        rolled = x_ext if r == 0 else pltpu.roll(x_ext, r, 0)
        for q in range(halo // 8):
            s = 8 * q + r
            if s > n_taps - 1:
                continue
            yield n_taps - 1 - s, rolled[halo - 8 * q:halo - 8 * q + rows]


def _conformer_body(a_ref, b_ref, w_ref, cb_ref, lg_ref, lb_ref, o_ref, halo_ref, *, bt):
    halo = halo_ref.shape[0]

    @pl.when(pl.program_id(1) == 0)
    def _():
        halo_ref[...] = jnp.zeros_like(halo_ref)

    u = a_ref[...].astype(F32) * _sigmoid(b_ref[...].astype(F32))
    x_ext = jnp.concatenate([halo_ref[...], u], axis=0)
    halo_ref[...] = u[bt - halo:bt]
    acc = jnp.zeros_like(u)
    for j, tap in _shifted_taps(x_ext, halo, bt, CONF_K):
        acc = acc + w_ref[j:j + 1, :] * tap
    y = acc + cb_ref[...]
    mu = jnp.mean(y, axis=-1, keepdims=True)
    yc = y - mu
    var = jnp.mean(yc * yc, axis=-1, keepdims=True)
    yn = yc * lax.rsqrt(var + EPS) * lg_ref[...] + lb_ref[...]
    o_ref[...] = _silu(yn).astype(o_ref.dtype)


def _conformer(z, conv_w, conv_b, ln_g, ln_b, bsz, tp, col_a, col_b):
    n = z.shape[0]
    c = conv_w.shape[1]
    bt = UNIT
    nt = tp // bt
    w = jnp.pad(conv_w.astype(F32), ((0, 32 - CONF_K), (0, 0)))
    row = lambda v: v.astype(F32).reshape(1, c)
    return pl.pallas_call(
        functools.partial(_conformer_body, bt=bt),
        out_shape=jax.ShapeDtypeStruct((n, c), BF16),
        grid=(bsz, nt),
        in_specs=[pl.BlockSpec((bt, c), lambda b, i: (b * nt + i, col_a)),
                  pl.BlockSpec((bt, c), lambda b, i: (b * nt + i, col_b)),
                  pl.BlockSpec((32, c), lambda b, i: (0, 0)),
                  pl.BlockSpec((1, c), lambda b, i: (0, 0)),
                  pl.BlockSpec((1, c), lambda b, i: (0, 0)),
                  pl.BlockSpec((1, c), lambda b, i: (0, 0))],
        out_specs=pl.BlockSpec((bt, c), lambda b, i: (b * nt + i, 0)),
        scratch_shapes=[pltpu.VMEM((32, c), F32)],
        compiler_params=_cparams(("arbitrary", "arbitrary")),
        name="conformer_conv",
    )(z, z, w, row(conv_b), row(ln_g), row(ln_b))


def _outproj_router_body(ya_ref, yb_ref, h_ref, w_ref, g_ref, wr_ref, br_ref, ho_ref, r_ref,
                         *, bm, blocks_per_seq, n_real):
    half = ya_ref.shape[1]
    y = jnp.dot(ya_ref[...], w_ref[0:half, :], preferred_element_type=F32)
    y = y + jnp.dot(yb_ref[...], w_ref[half:2 * half, :], preferred_element_type=F32)
    pos0 = (pl.program_id(0) % blocks_per_seq) * bm
    hn = jnp.where(_row_valid(pos0, bm, n_real), h_ref[...] + y, 0.0)
    ho_ref[...] = hn

    ms = jnp.mean(hn * hn, axis=-1, keepdims=True)
    xn = hn * lax.rsqrt(ms + EPS) * g_ref[...]
    lg = jnp.dot(xn, wr_ref[...], preferred_element_type=F32,
                 precision=lax.Precision.HIGHEST) + br_ref[...]
    lane = lax.broadcasted_iota(I32, lg.shape, 1)
    lane_f = lane.astype(F32)
    big = float(LANES)

    is_grp = (lane >= N_EXPERTS) & (lane < N_EXPERTS + N_GROUPS)
    gl = jnp.where(is_grp, lg, -jnp.inf)
    gmax = jnp.max(gl, axis=1, keepdims=True)
    gidx = jnp.min(jnp.where(gl == gmax, lane_f, big), axis=1, keepdims=True) - N_EXPERTS
    g_prob = 1.0 / jnp.sum(jnp.where(is_grp, jnp.exp(lg - gmax), 0.0), axis=1, keepdims=True)

    lo = gidx * EXPERTS_PER_GROUP
    in_grp = (lane_f >= lo) & (lane_f < lo + EXPERTS_PER_GROUP)
    el = jnp.where(in_grp, lg, -jnp.inf)
    m1 = jnp.max(el, axis=1, keepdims=True)
    i1 = jnp.min(jnp.where(el == m1, lane_f, big), axis=1, keepdims=True)
    el2 = jnp.where(lane_f == i1, -jnp.inf, el)
    m2 = jnp.max(el2, axis=1, keepdims=True)
    i2 = jnp.min(jnp.where(el2 == m2, lane_f, big), axis=1, keepdims=True)
    e2 = jnp.exp(m2 - m1)
    w1 = g_prob / (1.0 + e2)
    w2 = g_prob * e2 / (1.0 + e2)
    r_ref[...] = jnp.where(lane == 0, i1, jnp.where(lane == 1, i2,
                           jnp.where(lane == 2, w1, jnp.where(lane == 3, w2, 0.0))))


def _outproj_router(ya, yb, h, w_out, g_ffn, w_router, b_router, tp, n_real):
    n, d = h.shape
    half = ya.shape[1]
    bm = UNIT
    return pl.pallas_call(
        functools.partial(_outproj_router_body, bm=bm, blocks_per_seq=tp // bm, n_real=n_real),
        out_shape=[jax.ShapeDtypeStruct((n, d), F32), jax.ShapeDtypeStruct((n, LANES), F32)],
        grid=(n // bm,),
        in_specs=[pl.BlockSpec((bm, half), lambda i: (i, 0)),
                  pl.BlockSpec((bm, half), lambda i: (i, 0)),
                  pl.BlockSpec((bm, d), lambda i: (i, 0)),
                  pl.BlockSpec((2 * half, d), lambda i: (0, 0)),
                  pl.BlockSpec((1, d), lambda i: (0, 0)),
                  pl.BlockSpec((d, LANES), lambda i: (0, 0)),
                  pl.BlockSpec((1, LANES), lambda i: (0, 0))],
        out_specs=[pl.BlockSpec((bm, d), lambda i: (i, 0)),
                   pl.BlockSpec((bm, LANES), lambda i: (i, 0))],
        compiler_params=_cparams(("arbitrary",)),
        name="outproj_router",
    )(ya, yb, h, w_out, g_ffn, w_router, b_router)


def _dispatch_body(meta_ref, dest_ref, h_hbm, zero_hbm, xs_hbm, sem, zsem, *, tb):
    i = pl.program_id(0)
    base = i * tb

    def issue(t, _):
        for k in range(2):
            d = dest_ref[0, 0, 2 * t + k]
            pltpu.make_async_copy(h_hbm.at[pl.ds(base + t, 1)], xs_hbm.at[pl.ds(d, 1)], sem).start()
        return 0

    lax.fori_loop(0, tb, issue, 0)

    @pl.when(i == 0)
    def _():
        def per_expert(e, _):
            cnt = meta_ref[e]
            start = meta_ref[N_EXPERTS + e]
            padded = meta_ref[2 * N_EXPERTS + e]

            def zissue(r, _):
                pltpu.make_async_copy(zero_hbm.at[pl.ds(0, 1)], xs_hbm.at[pl.ds(start + r, 1)],
                                      zsem).start()
                return 0

            def zwait(r, _):
                pltpu.make_async_copy(zero_hbm.at[pl.ds(0, 1)], xs_hbm.at[pl.ds(start + r, 1)],
                                      zsem).wait()
                return 0

            lax.fori_loop(cnt, padded, zissue, 0)
            lax.fori_loop(cnt, padded, zwait, 0)
            return 0

        lax.fori_loop(0, N_EXPERTS, per_expert, 0)

        def tail_copy(blk):
            return pltpu.make_async_copy(zero_hbm, xs_hbm.at[pl.ds(blk * MOE_BM, MOE_BM)], zsem)

        def tissue(blk, _):
            tail_copy(blk).start()
            return 0

        def twait(blk, _):
            tail_copy(blk).wait()
            return 0

        n_used = meta_ref[3 * N_EXPERTS]
        lax.fori_loop(n_used, xs_hbm.shape[0] // MOE_BM, tissue, 0)
        lax.fori_loop(n_used, xs_hbm.shape[0] // MOE_BM, twait, 0)

    pltpu.make_async_copy(h_hbm.at[pl.ds(0, 2 * tb)], xs_hbm.at[pl.ds(0, 2 * tb)], sem).wait()


def _dispatch(h, dest3, meta, n_rows):
    n, d = h.shape
    tb = UNIT
    zero = jnp.zeros((MOE_BM, d), F32)
    return pl.pallas_call(
        functools.partial(_dispatch_body, tb=tb),
        out_shape=jax.ShapeDtypeStruct((n_rows, d), F32),
        grid_spec=pltpu.PrefetchScalarGridSpec(
            num_scalar_prefetch=1,
            grid=(n // tb,),
            in_specs=[pl.BlockSpec((1, 1, 2 * tb), lambda i, m: (i, 0, 0), memory_space=pltpu.SMEM),
                      pl.BlockSpec(memory_space=pl.ANY),
                      pl.BlockSpec(memory_space=pl.ANY)],
            out_specs=pl.BlockSpec(memory_space=pl.ANY),
            scratch_shapes=[pltpu.SemaphoreType.DMA(()), pltpu.SemaphoreType.DMA(())]),
        compiler_params=_cparams(("arbitrary",)),
        name="moe_dispatch",
    )(meta, dest3, h, zero)


def _experts_body(be_ref, nu_ref, xs_ref, g_ref, wg_ref, wu_ref, wd_ref, ys_ref,
                  wg_bf, wu_bf, wd_bf):
    i = pl.program_id(0)
    n_used = nu_ref[0]

    @pl.when(i < n_used)
    def _():
        prev = be_ref[jnp.maximum(i - 1, 0)]

        @pl.when((i == 0) | (be_ref[i] != prev))
        def _():
            wg_bf[...] = wg_ref[0].astype(BF16)
            wu_bf[...] = wu_ref[0].astype(BF16)
            wd_bf[...] = wd_ref[0].astype(BF16)

        x = xs_ref[...]
        ms = jnp.mean(x * x, axis=-1, keepdims=True)
        xn = (x * lax.rsqrt(ms + EPS) * g_ref[...]).astype(BF16)
        hg = jnp.dot(xn, wg_bf[...], preferred_element_type=F32)
        hu = jnp.dot(xn, wu_bf[...], preferred_element_type=F32)
        hid = (_silu(hg) * hu).astype(BF16)
        ys_ref[...] = jnp.dot(hid, wd_bf[...], preferred_element_type=F32)

    @pl.when(i >= n_used)
    def _():
        ys_ref[...] = jnp.zeros_like(ys_ref)


def _experts(xs, g_ffn, w_gate, w_up, w_down, block_expert, n_used):
    r, d = xs.shape
    de = w_gate.shape[2]
    nb = r // MOE_BM

    def blk(i, be, nu):
        return jnp.minimum(i, nu[0] - 1)

    return pl.pallas_call(
        _experts_body,
        out_shape=jax.ShapeDtypeStruct((r, d), F32),
        grid_spec=pltpu.PrefetchScalarGridSpec(
            num_scalar_prefetch=2,
            grid=(nb,),
            in_specs=[pl.BlockSpec((MOE_BM, d), lambda i, be, nu: (i, 0)),
                      pl.BlockSpec((1, d), lambda i, be, nu: (0, 0)),
                      pl.BlockSpec((1, d, de), lambda i, be, nu: (be[blk(i, be, nu)], 0, 0)),
                      pl.BlockSpec((1, d, de), lambda i, be, nu: (be[blk(i, be, nu)], 0, 0)),
                      pl.BlockSpec((1, de, d), lambda i, be, nu: (be[blk(i, be, nu)], 0, 0))],
            out_specs=pl.BlockSpec((MOE_BM, d), lambda i, be, nu: (i, 0)),
            scratch_shapes=[pltpu.VMEM((d, de), BF16), pltpu.VMEM((d, de), BF16),
                            pltpu.VMEM((de, d), BF16)]),
        compiler_params=_cparams(("arbitrary",)),
        name="moe_experts",
    )(block_expert, n_used, xs, g_ffn, w_gate, w_up, w_down)


def _combine_body(dest_ref, h_ref, r_ref, gf_ref, ys_hbm, o_ref, buf, sem,
                  *, tb, blocks_per_seq, n_real, final_norm):
    def issue(t, _):
        for k in range(2):
            d = dest_ref[0, 0, 2 * t + k]
            pltpu.make_async_copy(ys_hbm.at[pl.ds(d, 1)], buf.at[k, pl.ds(t, 1)], sem).start()
        return 0

    lax.fori_loop(0, tb, issue, 0)
    for k in range(2):
        pltpu.make_async_copy(ys_hbm.at[pl.ds(0, tb)], buf.at[k], sem).wait()

    r = r_ref[...]
    y = r[:, 2:3] * buf[0] + r[:, 3:4] * buf[1]
    pos0 = (pl.program_id(0) % blocks_per_seq) * tb
    hn = jnp.where(_row_valid(pos0, tb, n_real), h_ref[...] + y, 0.0)
    if final_norm:
        ms = jnp.mean(hn * hn, axis=-1, keepdims=True)
        hn = hn * lax.rsqrt(ms + EPS) * gf_ref[...]
    o_ref[...] = hn


def _combine(h, route, dest3, ys, g_final, tp, n_real, final_norm):
    n, d = h.shape
    tb = UNIT
    return pl.pallas_call(
        functools.partial(_combine_body, tb=tb, blocks_per_seq=tp // tb, n_real=n_real,
                          final_norm=final_norm),
        out_shape=jax.ShapeDtypeStruct((n, d), F32),
        grid=(n // tb,),
        in_specs=[pl.BlockSpec((1, 1, 2 * tb), lambda i: (i, 0, 0), memory_space=pltpu.SMEM),
                  pl.BlockSpec((tb, d), lambda i: (i, 0)),
                  pl.BlockSpec((tb, LANES), lambda i: (i, 0)),
                  pl.BlockSpec((1, d), lambda i: (0, 0)),
                  pl.BlockSpec(memory_space=pl.ANY)],
        out_specs=pl.BlockSpec((tb, d), lambda i: (i, 0)),
        scratch_shapes=[pltpu.VMEM((2, tb, d), F32), pltpu.SemaphoreType.DMA(())],
        compiler_params=_cparams(("arbitrary",)),
        name="moe_combine",
    )(dest3, h, route, g_final, ys)


def _moe(h, route, g_ffn, w_gate, w_up, w_down, g_final, tp, n_real, final_norm):
    n, d = h.shape
    n_pairs = 2 * n
    e_flat = route[:, 0:2].astype(I32).reshape(n_pairs)
    onehot = (e_flat[:, None] == jnp.arange(N_EXPERTS, dtype=I32)[None, :]).astype(I32)
    csum = jnp.cumsum(onehot, axis=0)
    rank = jnp.sum(csum * onehot, axis=1) - 1
    counts = csum[-1]
    padded = (counts + MOE_BM - 1) // MOE_BM * MOE_BM
    p_ends = jnp.cumsum(padded)
    p_starts = p_ends - padded
    dest = jnp.sum(onehot * p_starts[None, :], axis=1) + rank
    n_blocks = -(-(n_pairs + N_EXPERTS * (MOE_BM - 1)) // MOE_BM)
    block_expert = jnp.minimum(
        jnp.searchsorted(p_ends, jnp.arange(n_blocks, dtype=I32) * MOE_BM, side="right"),
        N_EXPERTS - 1).astype(I32)
    n_used = (p_ends[-1:] // MOE_BM).astype(I32)
    meta = jnp.concatenate([counts, p_starts, padded, n_used]).astype(I32)
    dest3 = dest.astype(I32).reshape(n // UNIT, 1, 2 * UNIT)

    xs = _dispatch(h, dest3, meta, n_blocks * MOE_BM)
    ys = _experts(xs, g_ffn, w_gate, w_up, w_down, block_expert, n_used)
    return _combine(h, route, dest3, ys, g_final, tp, n_real, final_norm)


def _short_conv(x, halo_ref, w_ref, first):
    rows = x.shape[0]

    @pl.when(first)
    def _():
        halo_ref[...] = jnp.zeros_like(halo_ref)

    x_ext = jnp.concatenate([halo_ref[...], x], axis=0)
    halo_ref[...] = x[rows - 8:rows]
    acc = jnp.zeros_like(x)
    for j, tap in _shifted_taps(x_ext, 8, rows, SHORT_K):
        acc = acc + w_ref[j:j + 1, :] * tap
    return acc


def _bdot(a, b):
    return jnp.dot(a.astype(BF16), b.astype(BF16), preferred_element_type=F32)


def _gdn_pre_body(z_ref, sc_ref, st_ref, cw_ref, al_r_ref, dt_r_ref, al_c_ref, dt_c_ref,
                  u_ref, w_ref, qd_ref, kd_ref, at_ref, eg_ref, halo_ref):
    rows = z_ref.shape[0]
    hw = N_HEADS * HEAD_DIM
    x = _silu(_short_conv(z_ref[...].astype(F32), halo_ref, cw_ref, pl.program_id(1) == 0))

    sc = sc_ref[...]
    g_cols = -jnp.exp(al_r_ref[...]) * _softplus(sc + dt_r_ref[...])
    beta_cols = _sigmoid(sc)
    st = st_ref[...]
    g_rows = -jnp.exp(al_c_ref[:, 0:1]) * _softplus(st + dt_c_ref[:, 0:1])

    ri = lax.broadcasted_iota(I32, (rows, rows), 0)
    ci = lax.broadcasted_iota(I32, (rows, rows), 1)
    same64 = (ri >> 6) == (ci >> 6)
    same32 = (ri >> 5) == (ci >> 5)
    same16 = (ri >> 4) == (ci >> 4)
    lower = ri >= ci
    strict = ri > ci
    lane = lax.broadcasted_iota(I32, (rows, LANES), 1)
    eg_slab = jnp.zeros((rows, LANES), F32)
    scale = HEAD_DIM ** -0.5

    for h in range(N_HEADS):
        sl = slice(h * HEAD_DIM, (h + 1) * HEAD_DIM)
        q = x[:, sl]
        k = x[:, hw + h * HEAD_DIM:hw + (h + 1) * HEAD_DIM]
        v = x[:, 2 * hw + h * HEAD_DIM:2 * hw + (h + 1) * HEAD_DIM]
        q = q * lax.rsqrt(jnp.sum(q * q, axis=-1, keepdims=True) + EPS)
        k = k * lax.rsqrt(jnp.sum(k * k, axis=-1, keepdims=True) + EPS)
        g_col = g_cols[:, h:h + 1]
        beta = beta_cols[:, N_HEADS + h:N_HEADS + h + 1]
        g_row = g_rows[h:h + 1, :]

        gc_col = jnp.sum(jnp.where(same64 & lower, g_row, 0.0), axis=1, keepdims=True)
        gc_row = jnp.sum(jnp.where(same64 & (ri <= ci), g_col, 0.0), axis=0, keepdims=True)
        gtot_col = jnp.sum(jnp.where(same64, g_row, 0.0), axis=1, keepdims=True)
        decay = jnp.exp(jnp.where(same64 & lower, gc_col - gc_row, NEG))
        eg = jnp.exp(gc_col)
        ekd = jnp.exp(gtot_col - gc_col)

        kb = k * beta
        kbf = k.astype(BF16)
        kk = lax.dot_general(kb.astype(BF16), kbf, (((1,), (1,)), ((), ())), preferred_element_type=F32)
        qs = q * scale
        qk = lax.dot_general(qs.astype(BF16), kbf, (((1,), (1,)), ((), ())), preferred_element_type=F32)
        attn = qk * decay
        a_mat = jnp.where(strict, kk * decay, 0.0)

        p = jnp.where(same16, -a_mat, 0.0)
        nmat = p
        qq = p
        for _ in range(3):
            qq = _bdot(qq, qq)
            nmat = nmat + qq + _bdot(qq, nmat)
        for level_mask, inner_mask in ((same32, same16), (same64, same32)):
            off = jnp.where(level_mask & jnp.logical_not(inner_mask), a_mat, 0.0)
            bmat = off + _bdot(nmat, off)
            nmat = nmat - bmat - _bdot(bmat, nmat)

        rhs = jnp.concatenate([v * beta, kb * eg], axis=1)
        uw = rhs + _bdot(nmat, rhs)
        u_ref[:, sl] = uw[:, 0:HEAD_DIM]
        w_ref[:, sl] = uw[:, HEAD_DIM:2 * HEAD_DIM].astype(BF16)
        qd_ref[:, sl] = (qs * eg).astype(BF16)
        kd_ref[:, sl] = (k * ekd).astype(BF16)
        for c in range(rows // CHUNK):
            cs = slice(c * CHUNK, (c + 1) * CHUNK)
            at_ref[h, cs, :] = attn[cs, cs].astype(BF16)
        eg_slab = jnp.where(lane == h, eg, eg_slab)
    eg_ref[...] = eg_slab


def _gdn_pre(z, sc, st, conv_w, a_log, dt_bias, bsz, tp):
    n = z.shape[0]
    hw = N_HEADS * HEAD_DIM
    rows = UNIT
    nu = tp // rows
    pad_r = lambda v: jnp.pad(v.astype(F32), (0, LANES - N_HEADS)).reshape(1, LANES)
    pad_c = lambda v: jnp.broadcast_to(jnp.pad(v.astype(F32), (0, 16 - N_HEADS))[:, None], (16, LANES))
    outs = pl.pallas_call(
        _gdn_pre_body,
        out_shape=[jax.ShapeDtypeStruct((n, hw), F32),
                   jax.ShapeDtypeStruct((n, hw), BF16),
                   jax.ShapeDtypeStruct((n, hw), BF16),
                   jax.ShapeDtypeStruct((n, hw), BF16),
                   jax.ShapeDtypeStruct((N_HEADS, n, CHUNK), BF16),
                   jax.ShapeDtypeStruct((n, LANES), F32)],
        grid=(bsz, nu),
        in_specs=[pl.BlockSpec((rows, 3 * hw), lambda b, i: (b * nu + i, 0)),
                  pl.BlockSpec((rows, LANES), lambda b, i: (b * nu + i, 0)),
                  pl.BlockSpec((16, rows), lambda b, i: (0, b * nu + i)),
                  pl.BlockSpec((8, 3 * hw), lambda b, i: (0, 0)),
                  pl.BlockSpec((1, LANES), lambda b, i: (0, 0)),
                  pl.BlockSpec((1, LANES), lambda b, i: (0, 0)),
                  pl.BlockSpec((16, LANES), lambda b, i: (0, 0)),
                  pl.BlockSpec((16, LANES), lambda b, i: (0, 0))],
        out_specs=[pl.BlockSpec((rows, hw), lambda b, i: (b * nu + i, 0)),
                   pl.BlockSpec((rows, hw), lambda b, i: (b * nu + i, 0)),
                   pl.BlockSpec((rows, hw), lambda b, i: (b * nu + i, 0)),
                   pl.BlockSpec((rows, hw), lambda b, i: (b * nu + i, 0)),
                   pl.BlockSpec((N_HEADS, rows, CHUNK), lambda b, i: (0, b * nu + i, 0)),
                   pl.BlockSpec((rows, LANES), lambda b, i: (b * nu + i, 0))],
        scratch_shapes=[pltpu.VMEM((8, 3 * hw), F32)],
        compiler_params=_cparams(("arbitrary", "arbitrary")),
        name="gdn_pre",
    )(z, sc, st, jnp.pad(conv_w.astype(F32), ((0, 8 - SHORT_K), (0, 0))),
      pad_r(a_log), pad_r(dt_bias), pad_c(a_log), pad_c(dt_bias))
    return outs


def _gdn_scan_body(u_ref, w_ref, qd_ref, kd_ref, at_ref, eg_ref, z_ref, gn_ref, o_ref, s_ref, *, bsz):
    @pl.when(pl.program_id(0) == 0)
    def _():
        s_ref[...] = jnp.zeros_like(s_ref)

    for b in range(bsz):
        for h in range(N_HEADS):
            sl = slice(h * HEAD_DIM, (h + 1) * HEAD_DIM)
            s = s_ref[b * N_HEADS + h]
            wq = jnp.concatenate([w_ref[b, :, sl], qd_ref[b, :, sl]], axis=0)
            r = jnp.dot(wq, s.astype(BF16), preferred_element_type=F32)
            v_new = u_ref[b, :, sl] - r[0:CHUNK]
            vb = v_new.astype(BF16)
            o = r[CHUNK:2 * CHUNK] + jnp.dot(at_ref[h, b], vb, preferred_element_type=F32)
            gt = eg_ref[b, CHUNK - 1:CHUNK, h:h + 1]
            s_ref[b * N_HEADS + h] = s * gt + lax.dot_general(
                kd_ref[b, :, sl], vb, (((0,), (0,)), ((), ())), preferred_element_type=F32)
            on = o * lax.rsqrt(jnp.mean(o * o, axis=-1, keepdims=True) + EPS) * gn_ref[...]
            o_ref[b, :, sl] = (on * _silu(z_ref[b, :, sl].astype(F32))).astype(o_ref.dtype)


def _gdn_scan(u, w, qd, kd, attn, eg, z, out_norm_g, bsz, tp, z_col):
    hw = N_HEADS * HEAD_DIM
    nc = tp // CHUNK
    v3 = lambda a: a.reshape(bsz, tp, a.shape[-1])
    blk3 = pl.BlockSpec((bsz, CHUNK, hw), lambda c: (0, c, 0))
    return pl.pallas_call(
        functools.partial(_gdn_scan_body, bsz=bsz),
        out_shape=jax.ShapeDtypeStruct((bsz, tp, hw), BF16),
        grid=(nc,),
        in_specs=[blk3, blk3, blk3, blk3,
                  pl.BlockSpec((N_HEADS, bsz, CHUNK, CHUNK), lambda c: (0, 0, c, 0)),
                  pl.BlockSpec((bsz, CHUNK, LANES), lambda c: (0, c, 0)),
                  pl.BlockSpec((bsz, CHUNK, hw), lambda c: (0, c, z_col)),
                  pl.BlockSpec((1, HEAD_DIM), lambda c: (0, 0))],
        out_specs=blk3,
        scratch_shapes=[pltpu.VMEM((bsz * N_HEADS, HEAD_DIM, HEAD_DIM), F32)],
        compiler_params=_cparams(("arbitrary",)),
        name="gdn_scan",
    )(v3(u), v3(w), v3(qd), v3(kd), attn.reshape(N_HEADS, bsz, tp, CHUNK), v3(eg), v3(z),
      out_norm_g.astype(F32).reshape(1, HEAD_DIM)).reshape(bsz * tp, hw)


def _lru_body(x_ref, gate_ref, cw_ref, cb_ref, wr_ref, br_ref, wi_ref, bi_ref, lam_ref, o_ref,
              halo_ref, hc_ref, *, bt, n_real):
    first = pl.program_id(1) == 0

    @pl.when(first)
    def _():
        hc_ref[...] = jnp.zeros_like(hc_ref)

    x = _short_conv(x_ref[...].astype(F32), halo_ref, cw_ref, first) + cb_ref[...]
    nblk = wr_ref.shape[0]
    bd = wr_ref.shape[1]
    xb = x.astype(BF16)
    rg = jnp.concatenate([jnp.dot(xb[:, n * bd:(n + 1) * bd], wr_ref[n], preferred_element_type=F32)
                          for n in range(nblk)], axis=1)
    ig = jnp.concatenate([jnp.dot(xb[:, n * bd:(n + 1) * bd], wi_ref[n], preferred_element_type=F32)
                          for n in range(nblk)], axis=1)
    r = _sigmoid(rg + br_ref[...])
    ig = _sigmoid(ig + bi_ref[...])
    log_a = (-LRU_C * _softplus(-lam_ref[...])) * r
    a = jnp.exp(log_a)
    th = jnp.tanh(log_a)
    b = jnp.sqrt(-2.0 * th / (1.0 - th)) * (ig * x)
    b = jnp.where(_row_valid(pl.program_id(1) * bt, bt, n_real), b, 0.0)

    row = lax.broadcasted_iota(I32, (bt, 1), 0)
    d = 1
    while d < bt:
        keep = row >= d
        a_sh = jnp.where(keep, pltpu.roll(a, d, 0), 1.0)
        b_sh = jnp.where(keep, pltpu.roll(b, d, 0), 0.0)
        b = a * b_sh + b
        a = a * a_sh
        d *= 2
    hs = a * hc_ref[0:1, :] + b
    hc_ref[...] = jnp.broadcast_to(hs[bt - 1:bt, :], hc_ref.shape)
    o_ref[...] = (hs * jax.nn.gelu(gate_ref[...].astype(F32))).astype(o_ref.dtype)


def _lru(z, conv_w, conv_b, w_rg, b_rg, w_ig, b_ig, lam, bsz, tp, n_real, x_col, gate_col):
    n = z.shape[0]
    c = conv_w.shape[1]
    bt = UNIT
    nt = tp // bt
    row = lambda v: v.astype(F32).reshape(1, c)
    wspec = pl.BlockSpec(w_rg.shape, lambda b, i: (0, 0, 0))
    vspec = pl.BlockSpec((1, c), lambda b, i: (0, 0))
    return pl.pallas_call(
        functools.partial(_lru_body, bt=bt, n_real=n_real),
        out_shape=jax.ShapeDtypeStruct((n, c), BF16),
        grid=(bsz, nt),
        in_specs=[pl.BlockSpec((bt, c), lambda b, i: (b * nt + i, x_col)),
                  pl.BlockSpec((bt, c), lambda b, i: (b * nt + i, gate_col)),
                  pl.BlockSpec((8, c), lambda b, i: (0, 0)),
                  vspec, wspec, vspec, wspec, vspec, vspec],
        out_specs=pl.BlockSpec((bt, c), lambda b, i: (b * nt + i, 0)),
        scratch_shapes=[pltpu.VMEM((8, c), F32), pltpu.VMEM((8, c), F32)],
        compiler_params=_cparams(("arbitrary", "arbitrary")),
        name="rg_lru",
    )(z, z, jnp.pad(conv_w.astype(F32), ((0, 8 - SHORT_K), (0, 0))), row(conv_b),
      w_rg.astype(BF16), row(b_rg), w_ig.astype(BF16), row(b_ig), row(lam))


def _pick(n, candidates):
    for c in candidates:
        if n % c == 0:
            return c
    raise ValueError(f"no block size in {candidates} divides {n}")


def _small_weights(cols):
    k = cols.shape[1]
    return (jnp.pad(cols.T, ((0, 16 - k), (0, 0))).astype(BF16),
            jnp.pad(cols, ((0, 0), (0, LANES - k))).astype(BF16))


def _router_weights(w_group, b_group, w_expert, b_expert):
    w = jnp.concatenate([w_expert, w_group], axis=1).astype(F32)
    b = jnp.concatenate([b_expert, b_group]).astype(F32)
    k = w.shape[1]
    return jnp.pad(w, ((0, 0), (0, LANES - k))), jnp.pad(b, (0, LANES - k)).reshape(1, LANES)


def kernel(x, meta_tokens, norm_mix_g, norm_ffn_g, norm_final_g, ab_w_in, ab_forget_b, ab_conv_w, ab_conv_b, ab_ln_g, ab_ln_b, ab_w_out, cd_w_in, cd_qkv_conv_w, cd_a_log, cd_dt_bias, cd_out_norm_g, cd_lru_conv_w, cd_lru_conv_b, cd_w_rg, cd_b_rg, cd_w_ig, cd_b_ig, cd_lru_lambda, cd_w_out, moe_w_group, moe_b_group, moe_w_expert, moe_b_expert, moe_w_gate, moe_w_up, moe_w_down):
    bsz, seq, d = x.shape
    depth = norm_mix_g.shape[0]
    n_real = N_META + seq
    tp = -(-(OFF + seq) // UNIT) * UNIT
    n = bsz * tp
    hw = N_HEADS * HEAD_DIM
    bm = _pick(n, (768, 512, 256))
    bn = 512

    meta = jnp.broadcast_to(meta_tokens[None].astype(F32), (bsz, N_META, d))
    h = jnp.concatenate([jnp.zeros((bsz, FRONT, d), F32), meta, x.astype(F32),
                         jnp.zeros((bsz, tp - OFF - seq, d), F32)], axis=1).reshape(n, d)
    row = lambda v: v.astype(F32).reshape(1, -1)

    for layer in range(depth):
        i = layer // 2
        g_mix = row(norm_mix_g[layer])
        if layer % 2 == 0:
            w_in = ab_w_in[i]
            w_main = jnp.concatenate([w_in[:, :3 * hw], w_in[:, 3 * hw + N_HEADS:]], axis=1).astype(BF16)
            wst, wsc = _small_weights(w_in[:, 3 * hw:3 * hw + N_HEADS])
            z, zt, _ = _norm_inproj(h, g_mix, w_main, wst, wsc, bm, bn)
            c = _fox_gate(zt, ab_forget_b[i], bsz, tp, n_real)
            ya = _fox_attn(z, c.reshape(N_HEADS, 1, n), bsz, tp)
            yb = _conformer(z, ab_conv_w[i], ab_conv_b[i], ab_ln_g[i], ab_ln_b[i], bsz, tp, 3, 4)
            w_out = ab_w_out[i].astype(BF16)
        else:
            w_in = cd_w_in[i]
            w_main = jnp.concatenate([w_in[:, :3 * hw], w_in[:, 3 * hw + 2 * N_HEADS:]], axis=1).astype(BF16)
            wst, wsc = _small_weights(w_in[:, 3 * hw:3 * hw + 2 * N_HEADS])
            z, zt, zc = _norm_inproj(h, g_mix, w_main, wst, wsc, bm, bn)
            u, w, qd, kd, attn, eg = _gdn_pre(z, zc, zt, cd_qkv_conv_w[i], cd_a_log[i], cd_dt_bias[i],
                                              bsz, tp)
            ya = _gdn_scan(u, w, qd, kd, attn, eg, z, cd_out_norm_g[i], bsz, tp, 3)
            yb = _lru(z, cd_lru_conv_w[i], cd_lru_conv_b[i], cd_w_rg[i], cd_b_rg[i], cd_w_ig[i],
                      cd_b_ig[i], cd_lru_lambda[i], bsz, tp, n_real, 4, 5)
            w_out = cd_w_out[i].astype(BF16)
        g_ffn = row(norm_ffn_g[layer])
        w_r, b_r = _router_weights(moe_w_group[layer], moe_b_group[layer],
                                   moe_w_expert[layer], moe_b_expert[layer])
        h, route = _outproj_router(ya, yb, h, w_out, g_ffn, w_r, b_r, tp, n_real)
        h = _moe(h, route, g_ffn, moe_w_gate[layer], moe_w_up[layer], moe_w_down[layer],
                 row(norm_final_g), tp, n_real, final_norm=(layer == depth - 1))
    return h.reshape(bsz, tp, d)[:, OFF:OFF + seq].astype(x.dtype)
```

```python
import functools

import jax
import jax.numpy as jnp
from jax import lax
from jax.experimental import pallas as pl
from jax.experimental.pallas import tpu as pltpu

F32 = jnp.float32
BF16 = jnp.bfloat16
I32 = jnp.int32

EPS = 1e-6
N_META = 16
CHUNK = 64
FRONT = CHUNK - N_META
OFF = FRONT + N_META
LANES = 128
UNIT = 256
HEAD_DIM = 128
N_HEADS = 8
CONF_K = 31
SHORT_K = 4
LRU_C = 8.0
N_GROUPS = 4
EXPERTS_PER_GROUP = 8
N_EXPERTS = N_GROUPS * EXPERTS_PER_GROUP
MOE_BM = 256
NEG = -1e30
LOG2E = 1.4426950408889634
VMEM_LIMIT = 56 * 1024 * 1024


def _cparams(sem):
    return pltpu.CompilerParams(dimension_semantics=sem, vmem_limit_bytes=VMEM_LIMIT)


def _row_valid(pos0, rows, n_real):
    pos = pos0 + lax.broadcasted_iota(I32, (rows, 1), 0)
    return (pos >= FRONT) & (pos < FRONT + n_real)


def _sigmoid(x):
    return 1.0 / (1.0 + jnp.exp(-x))


def _softplus(x):
    return jnp.maximum(x, 0.0) + jnp.log1p(jnp.exp(-jnp.abs(x)))


def _silu(x):
    return x * _sigmoid(x)


def _norm_inproj_body(h_ref, g_ref, w_ref, wst_ref, wsc_ref, o_ref, ot_ref, oc_ref, xn_ref):
    @pl.when(pl.program_id(1) == 0)
    def _():
        x = h_ref[...]
        ms = jnp.mean(x * x, axis=-1, keepdims=True)
        xn = (x * lax.rsqrt(ms + EPS) * g_ref[...]).astype(BF16)
        xn_ref[...] = xn
        ot_ref[...] = lax.dot_general(wst_ref[...], xn, (((1,), (1,)), ((), ())),
                                      preferred_element_type=F32)
        oc_ref[...] = jnp.dot(xn, wsc_ref[...], preferred_element_type=F32)

    o_ref[...] = jnp.dot(xn_ref[...], w_ref[...], preferred_element_type=F32).astype(o_ref.dtype)


def _norm_inproj(h, g, w_main, w_small_t, w_small_c, bm, bn):
    n, d = h.shape
    nw = w_main.shape[1]
    return pl.pallas_call(
        _norm_inproj_body,
        out_shape=[jax.ShapeDtypeStruct((n, nw), BF16),
                   jax.ShapeDtypeStruct((16, n), F32),
                   jax.ShapeDtypeStruct((n, LANES), F32)],
        grid=(n // bm, nw // bn),
        in_specs=[pl.BlockSpec((bm, d), lambda i, j: (i, 0)),
                  pl.BlockSpec((1, d), lambda i, j: (0, 0)),
                  pl.BlockSpec((d, bn), lambda i, j: (0, j)),
                  pl.BlockSpec((16, d), lambda i, j: (0, 0)),
                  pl.BlockSpec((d, LANES), lambda i, j: (0, 0))],
        out_specs=[pl.BlockSpec((bm, bn), lambda i, j: (i, j)),
                   pl.BlockSpec((16, bm), lambda i, j: (0, i)),
                   pl.BlockSpec((bm, LANES), lambda i, j: (i, 0))],
        scratch_shapes=[pltpu.VMEM((bm, d), BF16)],
        compiler_params=_cparams(("arbitrary", "arbitrary")),
        name="norm_inproj",
    )(h, g, w_main, w_small_t, w_small_c)


def _fox_gate_body(ft_ref, fb_ref, c_ref, *, tp, n_real):
    f = ft_ref[0:N_HEADS, :] + fb_ref[:, 0:1]
    lf = jnp.minimum(f, 0.0) - jnp.log1p(jnp.exp(-jnp.abs(f)))
    col = lax.broadcasted_iota(I32, (N_HEADS, tp), 1)
    lf = jnp.where((col >= FRONT) & (col < FRONT + n_real), lf, 0.0)
    r = lax.broadcasted_iota(I32, (LANES, LANES), 0)
    c = lax.broadcasted_iota(I32, (LANES, LANES), 1)
    upper = (r <= c).astype(F32)
    carry = jnp.zeros((N_HEADS, 1), F32)
    for j in range(tp // LANES):
        blk = lf[:, j * LANES:(j + 1) * LANES]
        cs = jnp.dot(blk, upper, preferred_element_type=F32, precision=lax.Precision.HIGHEST) + carry
        c_ref[:, j * LANES:(j + 1) * LANES] = cs
        carry = cs[:, LANES - 1:LANES]
    c_ref[:, 0:LANES] = jnp.where(col[:, 0:LANES] < FRONT, -NEG, c_ref[:, 0:LANES])


def _fox_gate(ft, forget_b, bsz, tp, n_real):
    n = ft.shape[1]
    fb = jnp.broadcast_to(forget_b.astype(F32)[:, None], (N_HEADS, LANES))
    return pl.pallas_call(
        functools.partial(_fox_gate_body, tp=tp, n_real=n_real),
        out_shape=jax.ShapeDtypeStruct((N_HEADS, n), F32),
        grid=(bsz,),
        in_specs=[pl.BlockSpec((16, tp), lambda b: (0, b)),
                  pl.BlockSpec((N_HEADS, LANES), lambda b: (0, 0))],
        out_specs=pl.BlockSpec((N_HEADS, tp), lambda b: (0, b)),
        compiler_params=_cparams(("arbitrary",)),
        name="fox_gate",
    )(ft, fb)


def _fox_attn_body(q_ref, k_ref, v_ref, ck_ref, cq_ref, o_ref, *, blk, scale):
    qi = pl.program_id(2)
    q = q_ref[...]
    c_last = cq_ref[0, :, blk - 1:blk]

    def step(j, carry, diagonal):
        m, l, acc = carry
        start = pl.multiple_of(j * blk, blk)
        k = k_ref[pl.ds(start, blk), :]
        v = v_ref[pl.ds(start, blk), :]
        bias = (c_last - ck_ref[0, :, pl.ds(start, blk)]) * LOG2E
        s = lax.dot_general(q, k, (((1,), (1,)), ((), ())), preferred_element_type=F32)
        s = s * (scale * LOG2E) + bias
        if diagonal:
            row = lax.broadcasted_iota(I32, (blk, blk), 0)
            col = lax.broadcasted_iota(I32, (blk, blk), 1)
            s = jnp.where(col <= row, s, NEG)
        m_new = jnp.maximum(m, jnp.max(s, axis=1, keepdims=True))
        alpha = jnp.exp2(m - m_new)
        p = jnp.exp2(s - m_new)
        l = alpha * l + jnp.sum(p, axis=1, keepdims=True)
        acc = alpha * acc + jnp.dot(p.astype(BF16), v, preferred_element_type=F32)
        return m_new, l, acc

    init = (jnp.full((blk, 1), NEG, F32), jnp.zeros((blk, 1), F32), jnp.zeros((blk, HEAD_DIM), F32))
    carry = lax.fori_loop(0, qi, lambda j, c: step(j, c, False), init)
    _, l, acc = step(qi, carry, True)
    o_ref[...] = (acc / l).astype(o_ref.dtype)


def _fox_attn(z, c3, bsz, tp):
    n = z.shape[0]
    blk = _pick(tp, (768, 512, 256))
    nq = tp // blk
    return pl.pallas_call(
        functools.partial(_fox_attn_body, blk=blk, scale=HEAD_DIM ** -0.5),
        out_shape=jax.ShapeDtypeStruct((n, N_HEADS * HEAD_DIM), BF16),
        grid=(bsz, N_HEADS, nq),
        in_specs=[pl.BlockSpec((blk, HEAD_DIM), lambda b, h, i: (b * nq + i, h)),
                  pl.BlockSpec((tp, HEAD_DIM), lambda b, h, i: (b, N_HEADS + h)),
                  pl.BlockSpec((tp, HEAD_DIM), lambda b, h, i: (b, 2 * N_HEADS + h)),
                  pl.BlockSpec((1, 1, tp), lambda b, h, i: (h, 0, b)),
                  pl.BlockSpec((1, 1, blk), lambda b, h, i: (h, 0, b * nq + i))],
        out_specs=pl.BlockSpec((blk, HEAD_DIM), lambda b, h, i: (b * nq + i, h)),
        compiler_params=_cparams(("arbitrary", "arbitrary", "arbitrary")),
        name="fox_attn",
    )(z, z, z, c3, c3)


def _shifted_taps(x_ext, halo, rows, n_taps):
    for r in range(min(8, n_taps)):
        rolled = x_ext if r == 0 else pltpu.roll(x_ext, r, 0)
        for q in range(halo // 8):
            s = 8 * q + r
            if s > n_taps - 1:
                continue
            yield n_taps - 1 - s, rolled[halo - 8 * q:halo - 8 * q + rows]


def _conformer_body(a_ref, b_ref, w_ref, cb_ref, lg_ref, lb_ref, o_ref, halo_ref, *, bt):
    halo = halo_ref.shape[0]

    @pl.when(pl.program_id(1) == 0)
    def _():
        halo_ref[...] = jnp.zeros_like(halo_ref)

    u = a_ref[...].astype(F32) * _sigmoid(b_ref[...].astype(F32))
    x_ext = jnp.concatenate([halo_ref[...], u], axis=0)
    halo_ref[...] = u[bt - halo:bt]
    acc = jnp.zeros_like(u)
    for j, tap in _shifted_taps(x_ext, halo, bt, CONF_K):
        acc = acc + w_ref[j:j + 1, :] * tap
    y = acc + cb_ref[...]
    mu = jnp.mean(y, axis=-1, keepdims=True)
    yc = y - mu
    var = jnp.mean(yc * yc, axis=-1, keepdims=True)
    yn = yc * lax.rsqrt(var + EPS) * lg_ref[...] + lb_ref[...]
    o_ref[...] = _silu(yn).astype(o_ref.dtype)


def _conformer(z, conv_w, conv_b, ln_g, ln_b, bsz, tp, col_a, col_b):
    n = z.shape[0]
    c = conv_w.shape[1]
    bt = UNIT
    nt = tp // bt
    w = jnp.pad(conv_w.astype(F32), ((0, 32 - CONF_K), (0, 0)))
    row = lambda v: v.astype(F32).reshape(1, c)
    return pl.pallas_call(
        functools.partial(_conformer_body, bt=bt),
        out_shape=jax.ShapeDtypeStruct((n, c), BF16),
        grid=(bsz, nt),
        in_specs=[pl.BlockSpec((bt, c), lambda b, i: (b * nt + i, col_a)),
                  pl.BlockSpec((bt, c), lambda b, i: (b * nt + i, col_b)),
                  pl.BlockSpec((32, c), lambda b, i: (0, 0)),
                  pl.BlockSpec((1, c), lambda b, i: (0, 0)),
                  pl.BlockSpec((1, c), lambda b, i: (0, 0)),
                  pl.BlockSpec((1, c), lambda b, i: (0, 0))],
        out_specs=pl.BlockSpec((bt, c), lambda b, i: (b * nt + i, 0)),
        scratch_shapes=[pltpu.VMEM((32, c), F32)],
        compiler_params=_cparams(("arbitrary", "arbitrary")),
        name="conformer_conv",
    )(z, z, w, row(conv_b), row(ln_g), row(ln_b))


def _outproj_router_body(ya_ref, yb_ref, h_ref, w_ref, g_ref, wr_ref, br_ref, ho_ref, r_ref,
                         *, bm, blocks_per_seq, n_real):
    half = ya_ref.shape[1]
    y = jnp.dot(ya_ref[...], w_ref[0:half, :], preferred_element_type=F32)
    y = y + jnp.dot(yb_ref[...], w_ref[half:2 * half, :], preferred_element_type=F32)
    pos0 = (pl.program_id(0) % blocks_per_seq) * bm
    hn = jnp.where(_row_valid(pos0, bm, n_real), h_ref[...] + y, 0.0)
    ho_ref[...] = hn

    ms = jnp.mean(hn * hn, axis=-1, keepdims=True)
    xn = hn * lax.rsqrt(ms + EPS) * g_ref[...]
    xh = xn.astype(BF16)
    xl = (xn - xh.astype(F32)).astype(BF16)
    t = jnp.dot(xh, wr_ref[...], preferred_element_type=F32)
    lg = (t[:, 0:LANES] + t[:, LANES:2 * LANES]
          + jnp.dot(xl, wr_ref[:, 0:LANES], preferred_element_type=F32) + br_ref[...])
    lane = lax.broadcasted_iota(I32, lg.shape, 1)
    lane_f = lane.astype(F32)
    big = float(LANES)

    is_grp = (lane >= N_EXPERTS) & (lane < N_EXPERTS + N_GROUPS)
    gl = jnp.where(is_grp, lg, -jnp.inf)
    gmax = jnp.max(gl, axis=1, keepdims=True)
    gidx = jnp.min(jnp.where(gl == gmax, lane_f, big), axis=1, keepdims=True) - N_EXPERTS
    g_prob = 1.0 / jnp.sum(jnp.where(is_grp, jnp.exp(lg - gmax), 0.0), axis=1, keepdims=True)

    lo = gidx * EXPERTS_PER_GROUP
    in_grp = (lane_f >= lo) & (lane_f < lo + EXPERTS_PER_GROUP)
    el = jnp.where(in_grp, lg, -jnp.inf)
    m1 = jnp.max(el, axis=1, keepdims=True)
    i1 = jnp.min(jnp.where(el == m1, lane_f, big), axis=1, keepdims=True)
    el2 = jnp.where(lane_f == i1, -jnp.inf, el)
    m2 = jnp.max(el2, axis=1, keepdims=True)
    i2 = jnp.min(jnp.where(el2 == m2, lane_f, big), axis=1, keepdims=True)
    e2 = jnp.exp(m2 - m1)
    w1 = g_prob / (1.0 + e2)
    w2 = g_prob * e2 / (1.0 + e2)
    r_ref[...] = jnp.where(lane == 0, i1, jnp.where(lane == 1, i2,
                           jnp.where(lane == 2, w1, jnp.where(lane == 3, w2, 0.0))))


def _outproj_router(ya, yb, h, w_out, g_ffn, w_router, b_router, tp, n_real):
    n, d = h.shape
    half = ya.shape[1]
    bm = UNIT
    return pl.pallas_call(
        functools.partial(_outproj_router_body, bm=bm, blocks_per_seq=tp // bm, n_real=n_real),
        out_shape=[jax.ShapeDtypeStruct((n, d), F32), jax.ShapeDtypeStruct((n, LANES), F32)],
        grid=(n // bm,),
        in_specs=[pl.BlockSpec((bm, half), lambda i: (i, 0)),
                  pl.BlockSpec((bm, half), lambda i: (i, 0)),
                  pl.BlockSpec((bm, d), lambda i: (i, 0)),
                  pl.BlockSpec((2 * half, d), lambda i: (0, 0)),
                  pl.BlockSpec((1, d), lambda i: (0, 0)),
                  pl.BlockSpec((d, 2 * LANES), lambda i: (0, 0)),
                  pl.BlockSpec((1, LANES), lambda i: (0, 0))],
        out_specs=[pl.BlockSpec((bm, d), lambda i: (i, 0)),
                   pl.BlockSpec((bm, LANES), lambda i: (i, 0))],
        compiler_params=_cparams(("arbitrary",)),
        name="outproj_router",
    )(ya, yb, h, w_out, g_ffn, w_router, b_router)


def _dispatch_body(meta_ref, dest_ref, h_ref, xs_hbm, zero_ref, sem, zsem, *, tb):
    i = pl.program_id(0)

    def issue(t, _):
        for k in range(2):
            d = dest_ref[0, 0, 2 * t + k]
            pltpu.make_async_copy(h_ref.at[pl.ds(t, 1)], xs_hbm.at[pl.ds(d, 1)], sem).start()
        return 0

    lax.fori_loop(0, tb, issue, 0)

    @pl.when(i == 0)
    def _():
        zero_ref[...] = jnp.zeros_like(zero_ref)

        def per_expert(e, _):
            cnt = meta_ref[e]
            start = meta_ref[N_EXPERTS + e]
            padded = meta_ref[2 * N_EXPERTS + e]

            def pad_copy(r):
                return pltpu.make_async_copy(zero_ref.at[pl.ds(0, 1)], xs_hbm.at[pl.ds(start + r, 1)], zsem)

            def zissue(r, _):
                pad_copy(r).start()
                return 0

            def zwait(r, _):
                pad_copy(r).wait()
                return 0

            lax.fori_loop(cnt, padded, zissue, 0)
            lax.fori_loop(cnt, padded, zwait, 0)
            return 0

        lax.fori_loop(0, N_EXPERTS, per_expert, 0)

        def tail_copy(blk):
            return pltpu.make_async_copy(zero_ref, xs_hbm.at[pl.ds(blk * MOE_BM, MOE_BM)], zsem)

        def tissue(blk, _):
            tail_copy(blk).start()
            return 0

        def twait(blk, _):
            tail_copy(blk).wait()
            return 0

        n_used = meta_ref[3 * N_EXPERTS]
        lax.fori_loop(n_used, xs_hbm.shape[0] // MOE_BM, tissue, 0)
        lax.fori_loop(n_used, xs_hbm.shape[0] // MOE_BM, twait, 0)

    for _ in range(2):
        pltpu.make_async_copy(h_ref, xs_hbm.at[pl.ds(0, tb)], sem).wait()


def _dispatch(h, dest3, meta, n_rows):
    n, d = h.shape
    tb = UNIT
    return pl.pallas_call(
        functools.partial(_dispatch_body, tb=tb),
        out_shape=jax.ShapeDtypeStruct((n_rows, d), F32),
        grid_spec=pltpu.PrefetchScalarGridSpec(
            num_scalar_prefetch=1,
            grid=(n // tb,),
            in_specs=[pl.BlockSpec((1, 1, 2 * tb), lambda i, m: (i, 0, 0), memory_space=pltpu.SMEM),
                      pl.BlockSpec((tb, d), lambda i, m: (i, 0))],
            out_specs=pl.BlockSpec(memory_space=pl.ANY),
            scratch_shapes=[pltpu.VMEM((MOE_BM, d), F32),
                            pltpu.SemaphoreType.DMA(()), pltpu.SemaphoreType.DMA(())]),
        compiler_params=_cparams(("arbitrary",)),
        name="moe_dispatch",
    )(meta, dest3, h)


def _experts_body(be_ref, nu_ref, xs_ref, g_ref, wg_ref, wu_ref, wd_ref, ys_ref,
                  wg_bf, wu_bf, wd_bf):
    i = pl.program_id(0)
    n_used = nu_ref[0]

    @pl.when(i < n_used)
    def _():
        prev = be_ref[jnp.maximum(i - 1, 0)]

        @pl.when((i == 0) | (be_ref[i] != prev))
        def _():
            wg_bf[...] = wg_ref[0].astype(BF16)
            wu_bf[...] = wu_ref[0].astype(BF16)
            wd_bf[...] = wd_ref[0].astype(BF16)

        x = xs_ref[...]
        ms = jnp.mean(x * x, axis=-1, keepdims=True)
        xn = (x * lax.rsqrt(ms + EPS) * g_ref[...]).astype(BF16)
        hg = jnp.dot(xn, wg_bf[...], preferred_element_type=F32)
        hu = jnp.dot(xn, wu_bf[...], preferred_element_type=F32)
        hid = (_silu(hg) * hu).astype(BF16)
        ys_ref[...] = jnp.dot(hid, wd_bf[...], preferred_element_type=F32)

    @pl.when(i >= n_used)
    def _():
        ys_ref[...] = jnp.zeros_like(ys_ref)


def _experts(xs, g_ffn, w_gate, w_up, w_down, block_expert, n_used):
    r, d = xs.shape
    de = w_gate.shape[2]
    nb = r // MOE_BM

    def blk(i, be, nu):
        return jnp.minimum(i, nu[0] - 1)

    return pl.pallas_call(
        _experts_body,
        out_shape=jax.ShapeDtypeStruct((r, d), F32),
        grid_spec=pltpu.PrefetchScalarGridSpec(
            num_scalar_prefetch=2,
            grid=(nb,),
            in_specs=[pl.BlockSpec((MOE_BM, d), lambda i, be, nu: (i, 0)),
                      pl.BlockSpec((1, d), lambda i, be, nu: (0, 0)),
                      pl.BlockSpec((1, d, de), lambda i, be, nu: (be[blk(i, be, nu)], 0, 0)),
                      pl.BlockSpec((1, d, de), lambda i, be, nu: (be[blk(i, be, nu)], 0, 0)),
                      pl.BlockSpec((1, de, d), lambda i, be, nu: (be[blk(i, be, nu)], 0, 0))],
            out_specs=pl.BlockSpec((MOE_BM, d), lambda i, be, nu: (i, 0)),
            scratch_shapes=[pltpu.VMEM((d, de), BF16), pltpu.VMEM((d, de), BF16),
                            pltpu.VMEM((de, d), BF16)]),
        compiler_params=_cparams(("arbitrary",)),
        name="moe_experts",
    )(block_expert, n_used, xs, g_ffn, w_gate, w_up, w_down)


def _combine_body(dest_ref, h_ref, r_ref, gf_ref, ys_hbm, o_ref, buf, sem,
                  *, tb, blocks_per_seq, n_real, final_norm):
    def issue(t, _):
        for k in range(2):
            d = dest_ref[0, 0, 2 * t + k]
            pltpu.make_async_copy(ys_hbm.at[pl.ds(d, 1)], buf.at[k, pl.ds(t, 1)], sem).start()
        return 0

    lax.fori_loop(0, tb, issue, 0)
    for k in range(2):
        pltpu.make_async_copy(ys_hbm.at[pl.ds(0, tb)], buf.at[k], sem).wait()

    r = r_ref[...]
    y = r[:, 2:3] * buf[0] + r[:, 3:4] * buf[1]
    pos0 = (pl.program_id(0) % blocks_per_seq) * tb
    hn = jnp.where(_row_valid(pos0, tb, n_real), h_ref[...] + y, 0.0)
    if final_norm:
        ms = jnp.mean(hn * hn, axis=-1, keepdims=True)
        hn = hn * lax.rsqrt(ms + EPS) * gf_ref[...]
    o_ref[...] = hn


def _combine(h, route, dest3, ys, g_final, tp, n_real, final_norm):
    n, d = h.shape
    tb = UNIT
    return pl.pallas_call(
        functools.partial(_combine_body, tb=tb, blocks_per_seq=tp // tb, n_real=n_real,
                          final_norm=final_norm),
        out_shape=jax.ShapeDtypeStruct((n, d), F32),
        grid=(n // tb,),
        in_specs=[pl.BlockSpec((1, 1, 2 * tb), lambda i: (i, 0, 0), memory_space=pltpu.SMEM),
                  pl.BlockSpec((tb, d), lambda i: (i, 0)),
                  pl.BlockSpec((tb, LANES), lambda i: (i, 0)),
                  pl.BlockSpec((1, d), lambda i: (0, 0)),
                  pl.BlockSpec(memory_space=pl.ANY)],
        out_specs=pl.BlockSpec((tb, d), lambda i: (i, 0)),
        scratch_shapes=[pltpu.VMEM((2, tb, d), F32), pltpu.SemaphoreType.DMA(())],
        compiler_params=_cparams(("arbitrary",)),
        name="moe_combine",
    )(dest3, h, route, g_final, ys)


def _moe(h, route, g_ffn, w_gate, w_up, w_down, g_final, tp, n_real, final_norm):
    n, d = h.shape
    n_pairs = 2 * n
    e_flat = route[:, 0:2].astype(I32).reshape(n_pairs)
    onehot = (e_flat[:, None] == jnp.arange(N_EXPERTS, dtype=I32)[None, :]).astype(I32)
    csum = jnp.cumsum(onehot, axis=0)
    rank = jnp.sum(csum * onehot, axis=1) - 1
    counts = csum[-1]
    padded = (counts + MOE_BM - 1) // MOE_BM * MOE_BM
    p_ends = jnp.cumsum(padded)
    p_starts = p_ends - padded
    dest = jnp.sum(onehot * p_starts[None, :], axis=1) + rank
    n_blocks = -(-(n_pairs + N_EXPERTS * (MOE_BM - 1)) // MOE_BM)
    block_expert = jnp.minimum(
        jnp.searchsorted(p_ends, jnp.arange(n_blocks, dtype=I32) * MOE_BM, side="right"),
        N_EXPERTS - 1).astype(I32)
    n_used = (p_ends[-1:] // MOE_BM).astype(I32)
    meta = jnp.concatenate([counts, p_starts, padded, n_used]).astype(I32)
    dest3 = dest.astype(I32).reshape(n // UNIT, 1, 2 * UNIT)

    xs = _dispatch(h, dest3, meta, n_blocks * MOE_BM)
    ys = _experts(xs, g_ffn, w_gate, w_up, w_down, block_expert, n_used)
    return _combine(h, route, dest3, ys, g_final, tp, n_real, final_norm)


def _short_conv(x, halo_ref, w_ref, first):
    rows = x.shape[0]

    @pl.when(first)
    def _():
        halo_ref[...] = jnp.zeros_like(halo_ref)

    x_ext = jnp.concatenate([halo_ref[...], x], axis=0)
    halo_ref[...] = x[rows - 8:rows]
    acc = jnp.zeros_like(x)
    for j, tap in _shifted_taps(x_ext, 8, rows, SHORT_K):
        acc = acc + w_ref[j:j + 1, :] * tap
    return acc


def _bdot(a, b):
    return jnp.dot(a.astype(BF16), b.astype(BF16), preferred_element_type=F32)


def _gdn_pre_body(z_ref, sc_ref, st_ref, cw_ref, al_r_ref, dt_r_ref, al_c_ref, dt_c_ref,
                  u_ref, w_ref, qd_ref, kd_ref, at_ref, eg_ref, halo_ref):
    rows = z_ref.shape[0]
    hw = N_HEADS * HEAD_DIM
    x = _silu(_short_conv(z_ref[...].astype(F32), halo_ref, cw_ref, pl.program_id(1) == 0))

    sc = sc_ref[...]
    g_cols = -jnp.exp(al_r_ref[...]) * _softplus(sc + dt_r_ref[...])
    beta_cols = _sigmoid(sc)
    st = st_ref[...]
    g_rows = -jnp.exp(al_c_ref[:, 0:1]) * _softplus(st + dt_c_ref[:, 0:1])

    ri = lax.broadcasted_iota(I32, (rows, rows), 0)
    ci = lax.broadcasted_iota(I32, (rows, rows), 1)
    same64 = (ri >> 6) == (ci >> 6)
    same32 = (ri >> 5) == (ci >> 5)
    same16 = (ri >> 4) == (ci >> 4)
    lower = ri >= ci
    strict = ri > ci
    lane = lax.broadcasted_iota(I32, (rows, LANES), 1)
    eg_slab = jnp.zeros((rows, LANES), F32)
    scale = HEAD_DIM ** -0.5

    for h in range(N_HEADS):
        sl = slice(h * HEAD_DIM, (h + 1) * HEAD_DIM)
        q = x[:, sl]
        k = x[:, hw + h * HEAD_DIM:hw + (h + 1) * HEAD_DIM]
        v = x[:, 2 * hw + h * HEAD_DIM:2 * hw + (h + 1) * HEAD_DIM]
        q = q * lax.rsqrt(jnp.sum(q * q, axis=-1, keepdims=True) + EPS)
        k = k * lax.rsqrt(jnp.sum(k * k, axis=-1, keepdims=True) + EPS)
        g_col = g_cols[:, h:h + 1]
        beta = beta_cols[:, N_HEADS + h:N_HEADS + h + 1]
        g_row = g_rows[h:h + 1, :]

        gc_col = jnp.sum(jnp.where(same64 & lower, g_row, 0.0), axis=1, keepdims=True)
        gc_row = jnp.sum(jnp.where(same64 & (ri <= ci), g_col, 0.0), axis=0, keepdims=True)
        gtot_col = jnp.sum(jnp.where(same64, g_row, 0.0), axis=1, keepdims=True)
        decay = jnp.exp(jnp.where(same64 & lower, gc_col - gc_row, NEG))
        eg = jnp.exp(gc_col)
        ekd = jnp.exp(gtot_col - gc_col)

        kb = k * beta
        kbf = k.astype(BF16)
        kk = lax.dot_general(kb.astype(BF16), kbf, (((1,), (1,)), ((), ())), preferred_element_type=F32)
        qs = q * scale
        qk = lax.dot_general(qs.astype(BF16), kbf, (((1,), (1,)), ((), ())), preferred_element_type=F32)
        attn = qk * decay
        a_mat = jnp.where(strict, kk * decay, 0.0)

        p = jnp.where(same16, -a_mat, 0.0)
        nmat = p
        qq = p
        for _ in range(3):
            qq = _bdot(qq, qq)
            nmat = nmat + qq + _bdot(qq, nmat)
        for level_mask, inner_mask in ((same32, same16), (same64, same32)):
            off = jnp.where(level_mask & jnp.logical_not(inner_mask), a_mat, 0.0)
            bmat = off + _bdot(nmat, off)
            nmat = nmat - bmat - _bdot(bmat, nmat)

        rhs = jnp.concatenate([v * beta, kb * eg], axis=1)
        uw = rhs + _bdot(nmat, rhs)
        u_ref[:, sl] = uw[:, 0:HEAD_DIM]
        w_ref[:, sl] = uw[:, HEAD_DIM:2 * HEAD_DIM].astype(BF16)
        qd_ref[:, sl] = (qs * eg).astype(BF16)
        kd_ref[:, sl] = (k * ekd).astype(BF16)
        for c in range(rows // CHUNK):
            cs = slice(c * CHUNK, (c + 1) * CHUNK)
            at_ref[h, cs, :] = attn[cs, cs].astype(BF16)
        eg_slab = jnp.where(lane == h, eg, eg_slab)
    eg_ref[...] = eg_slab


def _gdn_pre(z, sc, st, conv_w, a_log, dt_bias, bsz, tp):
    n = z.shape[0]
    hw = N_HEADS * HEAD_DIM
    rows = UNIT
    nu = tp // rows
    pad_r = lambda v: jnp.pad(v.astype(F32), (0, LANES - N_HEADS)).reshape(1, LANES)
    pad_c = lambda v: jnp.broadcast_to(jnp.pad(v.astype(F32), (0, 16 - N_HEADS))[:, None], (16, LANES))
    outs = pl.pallas_call(
        _gdn_pre_body,
        out_shape=[jax.ShapeDtypeStruct((n, hw), F32),
                   jax.ShapeDtypeStruct((n, hw), BF16),
                   jax.ShapeDtypeStruct((n, hw), BF16),
                   jax.ShapeDtypeStruct((n, hw), BF16),
                   jax.ShapeDtypeStruct((N_HEADS, n, CHUNK), BF16),
                   jax.ShapeDtypeStruct((n, LANES), F32)],
        grid=(bsz, nu),
        in_specs=[pl.BlockSpec((rows, 3 * hw), lambda b, i: (b * nu + i, 0)),
                  pl.BlockSpec((rows, LANES), lambda b, i: (b * nu + i, 0)),
                  pl.BlockSpec((16, rows), lambda b, i: (0, b * nu + i)),
                  pl.BlockSpec((8, 3 * hw), lambda b, i: (0, 0)),
                  pl.BlockSpec((1, LANES), lambda b, i: (0, 0)),
                  pl.BlockSpec((1, LANES), lambda b, i: (0, 0)),
                  pl.BlockSpec((16, LANES), lambda b, i: (0, 0)),
                  pl.BlockSpec((16, LANES), lambda b, i: (0, 0))],
        out_specs=[pl.BlockSpec((rows, hw), lambda b, i: (b * nu + i, 0)),
                   pl.BlockSpec((rows, hw), lambda b, i: (b * nu + i, 0)),
                   pl.BlockSpec((rows, hw), lambda b, i: (b * nu + i, 0)),
                   pl.BlockSpec((rows, hw), lambda b, i: (b * nu + i, 0)),
                   pl.BlockSpec((N_HEADS, rows, CHUNK), lambda b, i: (0, b * nu + i, 0)),
                   pl.BlockSpec((rows, LANES), lambda b, i: (b * nu + i, 0))],
        scratch_shapes=[pltpu.VMEM((8, 3 * hw), F32)],
        compiler_params=_cparams(("arbitrary", "arbitrary")),
        name="gdn_pre",
    )(z, sc, st, jnp.pad(conv_w.astype(F32), ((0, 8 - SHORT_K), (0, 0))),
      pad_r(a_log), pad_r(dt_bias), pad_c(a_log), pad_c(dt_bias))
    return outs


def _gdn_scan_body(u_ref, w_ref, qd_ref, kd_ref, at_ref, eg_ref, z_ref, gn_ref, o_ref, s_ref, *, bsz):
    @pl.when(pl.program_id(0) == 0)
    def _():
        s_ref[...] = jnp.zeros_like(s_ref)

    for b in range(bsz):
        for h in range(N_HEADS):
            sl = slice(h * HEAD_DIM, (h + 1) * HEAD_DIM)
            s = s_ref[b * N_HEADS + h]
            wq = jnp.concatenate([w_ref[b, :, sl], qd_ref[b, :, sl]], axis=0)
            r = jnp.dot(wq, s.astype(BF16), preferred_element_type=F32)
            v_new = u_ref[b, :, sl] - r[0:CHUNK]
            vb = v_new.astype(BF16)
            o = r[CHUNK:2 * CHUNK] + jnp.dot(at_ref[h, b], vb, preferred_element_type=F32)
            gt = eg_ref[b, CHUNK - 1:CHUNK, h:h + 1]
            s_ref[b * N_HEADS + h] = s * gt + lax.dot_general(
                kd_ref[b, :, sl], vb, (((0,), (0,)), ((), ())), preferred_element_type=F32)
            on = o * lax.rsqrt(jnp.mean(o * o, axis=-1, keepdims=True) + EPS) * gn_ref[...]
            o_ref[b, :, sl] = (on * _silu(z_ref[b, :, sl].astype(F32))).astype(o_ref.dtype)


def _gdn_scan(u, w, qd, kd, attn, eg, z, out_norm_g, bsz, tp, z_col):
    hw = N_HEADS * HEAD_DIM
    nc = tp // CHUNK
    v3 = lambda a: a.reshape(bsz, tp, a.shape[-1])
    blk3 = pl.BlockSpec((bsz, CHUNK, hw), lambda c: (0, c, 0))
    return pl.pallas_call(
        functools.partial(_gdn_scan_body, bsz=bsz),
        out_shape=jax.ShapeDtypeStruct((bsz, tp, hw), BF16),
        grid=(nc,),
        in_specs=[blk3, blk3, blk3, blk3,
                  pl.BlockSpec((N_HEADS, bsz, CHUNK, CHUNK), lambda c: (0, 0, c, 0)),
                  pl.BlockSpec((bsz, CHUNK, LANES), lambda c: (0, c, 0)),
                  pl.BlockSpec((bsz, CHUNK, hw), lambda c: (0, c, z_col)),
                  pl.BlockSpec((1, HEAD_DIM), lambda c: (0, 0))],
        out_specs=blk3,
        scratch_shapes=[pltpu.VMEM((bsz * N_HEADS, HEAD_DIM, HEAD_DIM), F32)],
        compiler_params=_cparams(("arbitrary",)),
        name="gdn_scan",
    )(v3(u), v3(w), v3(qd), v3(kd), attn.reshape(N_HEADS, bsz, tp, CHUNK), v3(eg), v3(z),
      out_norm_g.astype(F32).reshape(1, HEAD_DIM)).reshape(bsz * tp, hw)


def _lru_body(x_ref, gate_ref, cw_ref, cb_ref, wr_ref, br_ref, wi_ref, bi_ref, lam_ref, o_ref,
              halo_ref, hc_ref, *, bt, n_real):
    first = pl.program_id(1) == 0

    @pl.when(first)
    def _():
        hc_ref[...] = jnp.zeros_like(hc_ref)

    x = _short_conv(x_ref[...].astype(F32), halo_ref, cw_ref, first) + cb_ref[...]
    nblk = wr_ref.shape[0]
    bd = wr_ref.shape[1]
    xb = x.astype(BF16)
    rg = jnp.concatenate([jnp.dot(xb[:, n * bd:(n + 1) * bd], wr_ref[n], preferred_element_type=F32)
                          for n in range(nblk)], axis=1)
    ig = jnp.concatenate([jnp.dot(xb[:, n * bd:(n + 1) * bd], wi_ref[n], preferred_element_type=F32)
                          for n in range(nblk)], axis=1)
    r = _sigmoid(rg + br_ref[...])
    ig = _sigmoid(ig + bi_ref[...])
    log_a = (-LRU_C * _softplus(-lam_ref[...])) * r
    a = jnp.exp(log_a)
    th = jnp.tanh(log_a)
    b = jnp.sqrt(-2.0 * th / (1.0 - th)) * (ig * x)
    b = jnp.where(_row_valid(pl.program_id(1) * bt, bt, n_real), b, 0.0)

    row = lax.broadcasted_iota(I32, (bt, 1), 0)
    d = 1
    while d < bt:
        keep = row >= d
        a_sh = jnp.where(keep, pltpu.roll(a, d, 0), 1.0)
        b_sh = jnp.where(keep, pltpu.roll(b, d, 0), 0.0)
        b = a * b_sh + b
        a = a * a_sh
        d *= 2
    hs = a * hc_ref[0:1, :] + b
    hc_ref[...] = jnp.broadcast_to(hs[bt - 1:bt, :], hc_ref.shape)
    o_ref[...] = (hs * jax.nn.gelu(gate_ref[...].astype(F32))).astype(o_ref.dtype)


def _lru(z, conv_w, conv_b, w_rg, b_rg, w_ig, b_ig, lam, bsz, tp, n_real, x_col, gate_col):
    n = z.shape[0]
    c = conv_w.shape[1]
    bt = UNIT
    nt = tp // bt
    row = lambda v: v.astype(F32).reshape(1, c)
    wspec = pl.BlockSpec(w_rg.shape, lambda b, i: (0, 0, 0))
    vspec = pl.BlockSpec((1, c), lambda b, i: (0, 0))
    return pl.pallas_call(
        functools.partial(_lru_body, bt=bt, n_real=n_real),
        out_shape=jax.ShapeDtypeStruct((n, c), BF16),
        grid=(bsz, nt),
        in_specs=[pl.BlockSpec((bt, c), lambda b, i: (b * nt + i, x_col)),
                  pl.BlockSpec((bt, c), lambda b, i: (b * nt + i, gate_col)),
                  pl.BlockSpec((8, c), lambda b, i: (0, 0)),
                  vspec, wspec, vspec, wspec, vspec, vspec],
        out_specs=pl.BlockSpec((bt, c), lambda b, i: (b * nt + i, 0)),
        scratch_shapes=[pltpu.VMEM((8, c), F32), pltpu.VMEM((8, c), F32)],
        compiler_params=_cparams(("arbitrary", "arbitrary")),
        name="rg_lru",
    )(z, z, jnp.pad(conv_w.astype(F32), ((0, 8 - SHORT_K), (0, 0))), row(conv_b),
      w_rg.astype(BF16), row(b_rg), w_ig.astype(BF16), row(b_ig), row(lam))


def _pick(n, candidates):
    for c in candidates:
        if n % c == 0:
            return c
    raise ValueError(f"no block size in {candidates} divides {n}")


def _small_weights(cols):
    k = cols.shape[1]
    return (jnp.pad(cols.T, ((0, 16 - k), (0, 0))).astype(BF16),
            jnp.pad(cols, ((0, 0), (0, LANES - k))).astype(BF16))


def _router_weights(w_group, b_group, w_expert, b_expert):
    w = jnp.concatenate([w_expert, w_group], axis=1).astype(F32)
    b = jnp.concatenate([b_expert, b_group]).astype(F32)
    k = w.shape[1]
    w = jnp.pad(w, ((0, 0), (0, LANES - k)))
    w_hi = w.astype(BF16)
    w_lo = (w - w_hi.astype(F32)).astype(BF16)
    return jnp.concatenate([w_hi, w_lo], axis=1), jnp.pad(b, (0, LANES - k)).reshape(1, LANES)


def kernel(x, meta_tokens, norm_mix_g, norm_ffn_g, norm_final_g, ab_w_in, ab_forget_b, ab_conv_w, ab_conv_b, ab_ln_g, ab_ln_b, ab_w_out, cd_w_in, cd_qkv_conv_w, cd_a_log, cd_dt_bias, cd_out_norm_g, cd_lru_conv_w, cd_lru_conv_b, cd_w_rg, cd_b_rg, cd_w_ig, cd_b_ig, cd_lru_lambda, cd_w_out, moe_w_group, moe_b_group, moe_w_expert, moe_b_expert, moe_w_gate, moe_w_up, moe_w_down):
    bsz, seq, d = x.shape
    depth = norm_mix_g.shape[0]
    n_real = N_META + seq
    tp = -(-(OFF + seq) // UNIT) * UNIT
    n = bsz * tp
    hw = N_HEADS * HEAD_DIM
    bm = _pick(n, (768, 512, 256))
    bn = 512

    meta = jnp.broadcast_to(meta_tokens[None].astype(F32), (bsz, N_META, d))
    h = jnp.concatenate([jnp.zeros((bsz, FRONT, d), F32), meta, x.astype(F32),
                         jnp.zeros((bsz, tp - OFF - seq, d), F32)], axis=1).reshape(n, d)
    row = lambda v: v.astype(F32).reshape(1, -1)

    for layer in range(depth):
        i = layer // 2
        g_mix = row(norm_mix_g[layer])
        if layer % 2 == 0:
            w_in = ab_w_in[i]
            w_main = jnp.concatenate([w_in[:, :3 * hw], w_in[:, 3 * hw + N_HEADS:]], axis=1).astype(BF16)
            wst, wsc = _small_weights(w_in[:, 3 * hw:3 * hw + N_HEADS])
            z, zt, _ = _norm_inproj(h, g_mix, w_main, wst, wsc, bm, bn)
            c = _fox_gate(zt, ab_forget_b[i], bsz, tp, n_real)
            ya = _fox_attn(z, c.reshape(N_HEADS, 1, n), bsz, tp)
            yb = _conformer(z, ab_conv_w[i], ab_conv_b[i], ab_ln_g[i], ab_ln_b[i], bsz, tp, 3, 4)
            w_out = ab_w_out[i].astype(BF16)
        else:
            w_in = cd_w_in[i]
            w_main = jnp.concatenate([w_in[:, :3 * hw], w_in[:, 3 * hw + 2 * N_HEADS:]], axis=1).astype(BF16)
            wst, wsc = _small_weights(w_in[:, 3 * hw:3 * hw + 2 * N_HEADS])
            z, zt, zc = _norm_inproj(h, g_mix, w_main, wst, wsc, bm, bn)
            u, w, qd, kd, attn, eg = _gdn_pre(z, zc, zt, cd_qkv_conv_w[i], cd_a_log[i], cd_dt_bias[i],
                                              bsz, tp)
            ya = _gdn_scan(u, w, qd, kd, attn, eg, z, cd_out_norm_g[i], bsz, tp, 3)
            yb = _lru(z, cd_lru_conv_w[i], cd_lru_conv_b[i], cd_w_rg[i], cd_b_rg[i], cd_w_ig[i],
                      cd_b_ig[i], cd_lru_lambda[i], bsz, tp, n_real, 4, 5)
            w_out = cd_w_out[i].astype(BF16)
        g_ffn = row(norm_ffn_g[layer])
        w_r, b_r = _router_weights(moe_w_group[layer], moe_b_group[layer],
                                   moe_w_expert[layer], moe_b_expert[layer])
        h, route = _outproj_router(ya, yb, h, w_out, g_ffn, w_r, b_r, tp, n_real)
        h = _moe(h, route, g_ffn, moe_w_gate[layer], moe_w_up[layer], moe_w_down[layer],
                 row(norm_final_g), tp, n_real, final_norm=(layer == depth - 1))
    return h.reshape(bsz, tp, d)[:, OFF:OFF + seq].astype(x.dtype)
```

```python
import functools

import jax
import jax.numpy as jnp
from jax import lax
from jax.experimental import pallas as pl
from jax.experimental.pallas import tpu as pltpu

F32 = jnp.float32
BF16 = jnp.bfloat16
I32 = jnp.int32

EPS = 1e-6
N_META = 16
CHUNK = 64
FRONT = CHUNK - N_META
OFF = FRONT + N_META
LANES = 128
UNIT = 256
HEAD_DIM = 128
N_HEADS = 8
CONF_K = 31
SHORT_K = 4
LRU_C = 8.0
N_GROUPS = 4
EXPERTS_PER_GROUP = 8
N_EXPERTS = N_GROUPS * EXPERTS_PER_GROUP
MOE_BM = 256
NEG = -1e30
LOG2E = 1.4426950408889634
VMEM_LIMIT = 56 * 1024 * 1024


def _cparams(sem):
    return pltpu.CompilerParams(dimension_semantics=sem, vmem_limit_bytes=VMEM_LIMIT)


def _row_valid(pos0, rows, n_real):
    pos = pos0 + lax.broadcasted_iota(I32, (rows, 1), 0)
    return (pos >= FRONT) & (pos < FRONT + n_real)


def _sigmoid(x):
    return 1.0 / (1.0 + jnp.exp(-x))


def _softplus(x):
    return jnp.maximum(x, 0.0) + jnp.log1p(jnp.exp(-jnp.abs(x)))


def _silu(x):
    return x * _sigmoid(x)


def _norm_inproj_body(h_ref, g_ref, w_ref, wst_ref, wsc_ref, o_ref, ot_ref, oc_ref, xn_ref):
    @pl.when(pl.program_id(1) == 0)
    def _():
        x = h_ref[...]
        ms = jnp.mean(x * x, axis=-1, keepdims=True)
        xn = (x * lax.rsqrt(ms + EPS) * g_ref[...]).astype(BF16)
        xn_ref[...] = xn
        ot_ref[...] = lax.dot_general(wst_ref[...], xn, (((1,), (1,)), ((), ())),
                                      preferred_element_type=F32)
        oc_ref[...] = jnp.dot(xn, wsc_ref[...], preferred_element_type=F32)

    o_ref[...] = jnp.dot(xn_ref[...], w_ref[...], preferred_element_type=F32).astype(o_ref.dtype)


def _norm_inproj(h, g, w_main, w_small_t, w_small_c, bm, bn):
    n, d = h.shape
    nw = w_main.shape[1]
    return pl.pallas_call(
        _norm_inproj_body,
        out_shape=[jax.ShapeDtypeStruct((n, nw), BF16),
                   jax.ShapeDtypeStruct((16, n), F32),
                   jax.ShapeDtypeStruct((n, LANES), F32)],
        grid=(n // bm, nw // bn),
        in_specs=[pl.BlockSpec((bm, d), lambda i, j: (i, 0)),
                  pl.BlockSpec((1, d), lambda i, j: (0, 0)),
                  pl.BlockSpec((d, bn), lambda i, j: (0, j)),
                  pl.BlockSpec((16, d), lambda i, j: (0, 0)),
                  pl.BlockSpec((d, LANES), lambda i, j: (0, 0))],
        out_specs=[pl.BlockSpec((bm, bn), lambda i, j: (i, j)),
                   pl.BlockSpec((16, bm), lambda i, j: (0, i)),
                   pl.BlockSpec((bm, LANES), lambda i, j: (i, 0))],
        scratch_shapes=[pltpu.VMEM((bm, d), BF16)],
        compiler_params=_cparams(("arbitrary", "arbitrary")),
        name="norm_inproj",
    )(h, g, w_main, w_small_t, w_small_c)


def _fox_gate_body(ft_ref, fb_ref, c_ref, *, tp, n_real):
    f = ft_ref[0:N_HEADS, :] + fb_ref[:, 0:1]
    lf = jnp.minimum(f, 0.0) - jnp.log1p(jnp.exp(-jnp.abs(f)))
    col = lax.broadcasted_iota(I32, (N_HEADS, tp), 1)
    lf = jnp.where((col >= FRONT) & (col < FRONT + n_real), lf, 0.0)
    r = lax.broadcasted_iota(I32, (LANES, LANES), 0)
    c = lax.broadcasted_iota(I32, (LANES, LANES), 1)
    upper = (r <= c).astype(F32)
    carry = jnp.zeros((N_HEADS, 1), F32)
    for j in range(tp // LANES):
        blk = lf[:, j * LANES:(j + 1) * LANES]
        cs = jnp.dot(blk, upper, preferred_element_type=F32, precision=lax.Precision.HIGHEST) + carry
        c_ref[:, j * LANES:(j + 1) * LANES] = cs
        carry = cs[:, LANES - 1:LANES]
    c_ref[:, 0:LANES] = jnp.where(col[:, 0:LANES] < FRONT, -NEG, c_ref[:, 0:LANES])


def _fox_gate(ft, forget_b, bsz, tp, n_real):
    n = ft.shape[1]
    fb = jnp.broadcast_to(forget_b.astype(F32)[:, None], (N_HEADS, LANES))
    return pl.pallas_call(
        functools.partial(_fox_gate_body, tp=tp, n_real=n_real),
        out_shape=jax.ShapeDtypeStruct((N_HEADS, n), F32),
        grid=(bsz,),
        in_specs=[pl.BlockSpec((16, tp), lambda b: (0, b)),
                  pl.BlockSpec((N_HEADS, LANES), lambda b: (0, 0))],
        out_specs=pl.BlockSpec((N_HEADS, tp), lambda b: (0, b)),
        compiler_params=_cparams(("arbitrary",)),
        name="fox_gate",
    )(ft, fb)


def _fox_attn_body(q_ref, k_ref, v_ref, ck_ref, cq_ref, o_ref, *, blk, scale):
    qi = pl.program_id(2)
    q = q_ref[...]
    c_last = cq_ref[0, :, blk - 1:blk]

    def step(j, carry, diagonal):
        m, l, acc = carry
        start = pl.multiple_of(j * blk, blk)
        k = k_ref[pl.ds(start, blk), :]
        v = v_ref[pl.ds(start, blk), :]
        bias = (c_last - ck_ref[0, :, pl.ds(start, blk)]) * LOG2E
        s = lax.dot_general(q, k, (((1,), (1,)), ((), ())), preferred_element_type=F32)
        s = s * (scale * LOG2E) + bias
        if diagonal:
            row = lax.broadcasted_iota(I32, (blk, blk), 0)
            col = lax.broadcasted_iota(I32, (blk, blk), 1)
            s = jnp.where(col <= row, s, NEG)
        m_new = jnp.maximum(m, jnp.max(s, axis=1, keepdims=True))
        alpha = jnp.exp2(m - m_new)
        p = jnp.exp2(s - m_new)
        l = alpha * l + jnp.sum(p, axis=1, keepdims=True)
        acc = alpha * acc + jnp.dot(p.astype(BF16), v, preferred_element_type=F32)
        return m_new, l, acc

    init = (jnp.full((blk, 1), NEG, F32), jnp.zeros((blk, 1), F32), jnp.zeros((blk, HEAD_DIM), F32))
    carry = lax.fori_loop(0, qi, lambda j, c: step(j, c, False), init)
    _, l, acc = step(qi, carry, True)
    o_ref[...] = (acc / l).astype(o_ref.dtype)


def _fox_attn(z, c3, bsz, tp):
    n = z.shape[0]
    blk = _pick(tp, (768, 512, 256))
    nq = tp // blk
    return pl.pallas_call(
        functools.partial(_fox_attn_body, blk=blk, scale=HEAD_DIM ** -0.5),
        out_shape=jax.ShapeDtypeStruct((n, N_HEADS * HEAD_DIM), BF16),
        grid=(bsz, N_HEADS, nq),
        in_specs=[pl.BlockSpec((blk, HEAD_DIM), lambda b, h, i: (b * nq + i, h)),
                  pl.BlockSpec((tp, HEAD_DIM), lambda b, h, i: (b, N_HEADS + h)),
                  pl.BlockSpec((tp, HEAD_DIM), lambda b, h, i: (b, 2 * N_HEADS + h)),
                  pl.BlockSpec((1, 1, tp), lambda b, h, i: (h, 0, b)),
                  pl.BlockSpec((1, 1, blk), lambda b, h, i: (h, 0, b * nq + i))],
        out_specs=pl.BlockSpec((blk, HEAD_DIM), lambda b, h, i: (b * nq + i, h)),
        compiler_params=_cparams(("arbitrary", "arbitrary", "arbitrary")),
        name="fox_attn",
    )(z, z, z, c3, c3)


def _shifted_taps(x_ext, halo, rows, n_taps):
    for r in range(min(8, n_taps)):
        rolled = x_ext if r == 0 else pltpu.roll(x_ext, r, 0)
        for q in range(halo // 8):
            s = 8 * q + r
            if s > n_taps - 1:
                continue
            yield n_taps - 1 - s, rolled[halo - 8 * q:halo - 8 * q + rows]


def _conformer_body(a_ref, b_ref, w_ref, cb_ref, lg_ref, lb_ref, o_ref, halo_ref, *, bt):
    halo = halo_ref.shape[0]

    @pl.when(pl.program_id(1) == 0)
    def _():
        halo_ref[...] = jnp.zeros_like(halo_ref)

    u = a_ref[...].astype(F32) * _sigmoid(b_ref[...].astype(F32))
    x_ext = jnp.concatenate([halo_ref[...], u], axis=0)
    halo_ref[...] = u[bt - halo:bt]
    acc = jnp.zeros_like(u)
    for j, tap in _shifted_taps(x_ext, halo, bt, CONF_K):
        acc = acc + w_ref[j:j + 1, :] * tap
    y = acc + cb_ref[...]
    mu = jnp.mean(y, axis=-1, keepdims=True)
    yc = y - mu
    var = jnp.mean(yc * yc, axis=-1, keepdims=True)
    yn = yc * lax.rsqrt(var + EPS) * lg_ref[...] + lb_ref[...]
    o_ref[...] = _silu(yn).astype(o_ref.dtype)


def _conformer(z, conv_w, conv_b, ln_g, ln_b, bsz, tp, col_a, col_b):
    n = z.shape[0]
    c = conv_w.shape[1]
    bt = UNIT
    nt = tp // bt
    w = jnp.pad(conv_w.astype(F32), ((0, 32 - CONF_K), (0, 0)))
    row = lambda v: v.astype(F32).reshape(1, c)
    return pl.pallas_call(
        functools.partial(_conformer_body, bt=bt),
        out_shape=jax.ShapeDtypeStruct((n, c), BF16),
        grid=(bsz, nt),
        in_specs=[pl.BlockSpec((bt, c), lambda b, i: (b * nt + i, col_a)),
                  pl.BlockSpec((bt, c), lambda b, i: (b * nt + i, col_b)),
                  pl.BlockSpec((32, c), lambda b, i: (0, 0)),
                  pl.BlockSpec((1, c), lambda b, i: (0, 0)),
                  pl.BlockSpec((1, c), lambda b, i: (0, 0)),
                  pl.BlockSpec((1, c), lambda b, i: (0, 0))],
        out_specs=pl.BlockSpec((bt, c), lambda b, i: (b * nt + i, 0)),
        scratch_shapes=[pltpu.VMEM((32, c), F32)],
        compiler_params=_cparams(("arbitrary", "arbitrary")),
        name="conformer_conv",
    )(z, z, w, row(conv_b), row(ln_g), row(ln_b))


def _outproj_router_body(ya_ref, yb_ref, h_ref, w_ref, g_ref, wr_ref, br_ref, ho_ref, r_ref, cnt_ref,
                         *, bm, blocks_per_seq, n_real):
    half = ya_ref.shape[1]
    y = jnp.dot(ya_ref[...], w_ref[0:half, :], preferred_element_type=F32)
    y = y + jnp.dot(yb_ref[...], w_ref[half:2 * half, :], preferred_element_type=F32)
    pos0 = (pl.program_id(0) % blocks_per_seq) * bm
    hn = jnp.where(_row_valid(pos0, bm, n_real), h_ref[...] + y, 0.0)
    ho_ref[...] = hn

    ms = jnp.mean(hn * hn, axis=-1, keepdims=True)
    xn = hn * lax.rsqrt(ms + EPS) * g_ref[...]
    xh = xn.astype(BF16)
    xl = (xn - xh.astype(F32)).astype(BF16)
    t = jnp.dot(xh, wr_ref[...], preferred_element_type=F32)
    lg = (t[:, 0:LANES] + t[:, LANES:2 * LANES]
          + jnp.dot(xl, wr_ref[:, 0:LANES], preferred_element_type=F32) + br_ref[...])
    lane = lax.broadcasted_iota(I32, lg.shape, 1)
    lane_f = lane.astype(F32)
    big = float(LANES)

    is_grp = (lane >= N_EXPERTS) & (lane < N_EXPERTS + N_GROUPS)
    gl = jnp.where(is_grp, lg, -jnp.inf)
    gmax = jnp.max(gl, axis=1, keepdims=True)
    gidx = jnp.min(jnp.where(gl == gmax, lane_f, big), axis=1, keepdims=True) - N_EXPERTS
    g_prob = 1.0 / jnp.sum(jnp.where(is_grp, jnp.exp(lg - gmax), 0.0), axis=1, keepdims=True)

    lo = gidx * EXPERTS_PER_GROUP
    in_grp = (lane_f >= lo) & (lane_f < lo + EXPERTS_PER_GROUP)
    el = jnp.where(in_grp, lg, -jnp.inf)
    m1 = jnp.max(el, axis=1, keepdims=True)
    i1 = jnp.min(jnp.where(el == m1, lane_f, big), axis=1, keepdims=True)
    el2 = jnp.where(lane_f == i1, -jnp.inf, el)
    m2 = jnp.max(el2, axis=1, keepdims=True)
    i2 = jnp.min(jnp.where(el2 == m2, lane_f, big), axis=1, keepdims=True)
    e2 = jnp.exp(m2 - m1)
    w1 = g_prob / (1.0 + e2)
    w2 = g_prob * e2 / (1.0 + e2)

    @pl.when(pl.program_id(0) == 0)
    def _():
        cnt_ref[...] = jnp.zeros_like(cnt_ref)

    oh1 = lane_f == i1
    oh2 = lane_f == i2
    onehots = jnp.concatenate([jnp.where(oh1, 1.0, 0.0), jnp.where(oh2, 1.0, 0.0)], axis=1).astype(BF16)
    rr = lax.broadcasted_iota(I32, (bm, bm), 0)
    cc = lax.broadcasted_iota(I32, (bm, bm), 1)
    tri = jnp.where(rr >= cc, 1.0, 0.0).astype(BF16)
    csum = jnp.dot(tri, onehots, preferred_element_type=F32)
    c1 = csum[:, 0:LANES]
    c2 = csum[:, LANES:2 * LANES]
    tot1 = c1[bm - 1:bm, :]
    tot2 = c2[bm - 1:bm, :]
    before = cnt_ref[0:1, :]
    rank1 = jnp.sum(jnp.where(oh1, before + c1 - 1.0, 0.0), axis=1, keepdims=True)
    rank2 = jnp.sum(jnp.where(oh2, before + tot1 + c2 - 1.0, 0.0), axis=1, keepdims=True)
    cnt_ref[...] = jnp.broadcast_to(before + tot1 + tot2, cnt_ref.shape)

    r_ref[...] = jnp.where(lane == 0, i1, jnp.where(lane == 1, i2,
                           jnp.where(lane == 2, w1, jnp.where(lane == 3, w2,
                                     jnp.where(lane == 4, rank1, jnp.where(lane == 5, rank2, 0.0))))))


def _outproj_router(ya, yb, h, w_out, g_ffn, w_router, b_router, tp, n_real):
    n, d = h.shape
    half = ya.shape[1]
    bm = UNIT
    return pl.pallas_call(
        functools.partial(_outproj_router_body, bm=bm, blocks_per_seq=tp // bm, n_real=n_real),
        out_shape=[jax.ShapeDtypeStruct((n, d), F32), jax.ShapeDtypeStruct((n, LANES), F32),
                   jax.ShapeDtypeStruct((8, LANES), F32)],
        grid=(n // bm,),
        in_specs=[pl.BlockSpec((bm, half), lambda i: (i, 0)),
                  pl.BlockSpec((bm, half), lambda i: (i, 0)),
                  pl.BlockSpec((bm, d), lambda i: (i, 0)),
                  pl.BlockSpec((2 * half, d), lambda i: (0, 0)),
                  pl.BlockSpec((1, d), lambda i: (0, 0)),
                  pl.BlockSpec((d, 2 * LANES), lambda i: (0, 0)),
                  pl.BlockSpec((1, LANES), lambda i: (0, 0))],
        out_specs=[pl.BlockSpec((bm, d), lambda i: (i, 0)),
                   pl.BlockSpec((bm, LANES), lambda i: (i, 0)),
                   pl.BlockSpec((8, LANES), lambda i: (0, 0))],
        compiler_params=_cparams(("arbitrary",)),
        name="outproj_router",
    )(ya, yb, h, w_out, g_ffn, w_router, b_router)


def _dispatch_body(meta_ref, dest_ref, h_ref, xs_hbm, zero_ref, sem, zsem, *, tb):
    i = pl.program_id(0)

    def issue(t, _):
        for k in range(2):
            d = dest_ref[0, 0, 2 * t + k]
            pltpu.make_async_copy(h_ref.at[pl.ds(t, 1)], xs_hbm.at[pl.ds(d, 1)], sem).start()
        return 0

    lax.fori_loop(0, tb, issue, 0, unroll=8)

    @pl.when(i == 0)
    def _():
        zero_ref[...] = jnp.zeros_like(zero_ref)

        def per_expert(e, _):
            cnt = meta_ref[e]
            start = meta_ref[N_EXPERTS + e]
            padded = meta_ref[2 * N_EXPERTS + e]

            def pad_copy(r):
                return pltpu.make_async_copy(zero_ref.at[pl.ds(0, 1)], xs_hbm.at[pl.ds(start + r, 1)], zsem)

            def zissue(r, _):
                pad_copy(r).start()
                return 0

            def zwait(r, _):
                pad_copy(r).wait()
                return 0

            lax.fori_loop(cnt, padded, zissue, 0)
            lax.fori_loop(cnt, padded, zwait, 0)
            return 0

        lax.fori_loop(0, N_EXPERTS, per_expert, 0)

        def tail_copy(blk):
            return pltpu.make_async_copy(zero_ref, xs_hbm.at[pl.ds(blk * MOE_BM, MOE_BM)], zsem)

        def tissue(blk, _):
            tail_copy(blk).start()
            return 0

        def twait(blk, _):
            tail_copy(blk).wait()
            return 0

        n_used = meta_ref[3 * N_EXPERTS]
        lax.fori_loop(n_used, xs_hbm.shape[0] // MOE_BM, tissue, 0)
        lax.fori_loop(n_used, xs_hbm.shape[0] // MOE_BM, twait, 0)

    for _ in range(2):
        pltpu.make_async_copy(h_ref, xs_hbm.at[pl.ds(0, tb)], sem).wait()


def _dispatch(h, dest3, meta, n_rows):
    n, d = h.shape
    tb = UNIT
    return pl.pallas_call(
        functools.partial(_dispatch_body, tb=tb),
        out_shape=jax.ShapeDtypeStruct((n_rows, d), F32),
        grid_spec=pltpu.PrefetchScalarGridSpec(
            num_scalar_prefetch=1,
            grid=(n // tb,),
            in_specs=[pl.BlockSpec((1, 1, 2 * tb), lambda i, m: (i, 0, 0), memory_space=pltpu.SMEM),
                      pl.BlockSpec((tb, d), lambda i, m: (i, 0))],
            out_specs=pl.BlockSpec(memory_space=pl.ANY),
            scratch_shapes=[pltpu.VMEM((MOE_BM, d), F32),
                            pltpu.SemaphoreType.DMA(()), pltpu.SemaphoreType.DMA(())]),
        compiler_params=_cparams(("arbitrary",)),
        name="moe_dispatch",
    )(meta, dest3, h)


def _experts_body(be_ref, nu_ref, xs_ref, g_ref, wg_ref, wu_ref, wd_ref, ys_ref,
                  wg_bf, wu_bf, wd_bf):
    i = pl.program_id(0)
    n_used = nu_ref[0]

    @pl.when(i < n_used)
    def _():
        prev = be_ref[jnp.maximum(i - 1, 0)]

        @pl.when((i == 0) | (be_ref[i] != prev))
        def _():
            wg_bf[...] = wg_ref[0].astype(BF16)
            wu_bf[...] = wu_ref[0].astype(BF16)
            wd_bf[...] = wd_ref[0].astype(BF16)

        x = xs_ref[...]
        ms = jnp.mean(x * x, axis=-1, keepdims=True)
        xn = (x * lax.rsqrt(ms + EPS) * g_ref[...]).astype(BF16)
        hg = jnp.dot(xn, wg_bf[...], preferred_element_type=F32)
        hu = jnp.dot(xn, wu_bf[...], preferred_element_type=F32)
        hid = (_silu(hg) * hu).astype(BF16)
        ys_ref[...] = jnp.dot(hid, wd_bf[...], preferred_element_type=F32)

    @pl.when(i >= n_used)
    def _():
        ys_ref[...] = jnp.zeros_like(ys_ref)


def _experts(xs, g_ffn, w_gate, w_up, w_down, block_expert, n_used):
    r, d = xs.shape
    de = w_gate.shape[2]
    nb = r // MOE_BM

    def blk(i, be, nu):
        return jnp.minimum(i, nu[0] - 1)

    return pl.pallas_call(
        _experts_body,
        out_shape=jax.ShapeDtypeStruct((r, d), F32),
        grid_spec=pltpu.PrefetchScalarGridSpec(
            num_scalar_prefetch=2,
            grid=(nb,),
            in_specs=[pl.BlockSpec((MOE_BM, d), lambda i, be, nu: (i, 0)),
                      pl.BlockSpec((1, d), lambda i, be, nu: (0, 0)),
                      pl.BlockSpec((1, d, de), lambda i, be, nu: (be[blk(i, be, nu)], 0, 0)),
                      pl.BlockSpec((1, d, de), lambda i, be, nu: (be[blk(i, be, nu)], 0, 0)),
                      pl.BlockSpec((1, de, d), lambda i, be, nu: (be[blk(i, be, nu)], 0, 0))],
            out_specs=pl.BlockSpec((MOE_BM, d), lambda i, be, nu: (i, 0)),
            scratch_shapes=[pltpu.VMEM((d, de), BF16), pltpu.VMEM((d, de), BF16),
                            pltpu.VMEM((de, d), BF16)]),
        compiler_params=_cparams(("arbitrary",)),
        name="moe_experts",
    )(block_expert, n_used, xs, g_ffn, w_gate, w_up, w_down)


def _combine_body(dest_ref, h_ref, r_ref, gf_ref, ys_hbm, o_ref, buf, sem,
                  *, tb, blocks_per_seq, n_real, final_norm):
    def issue(t, _):
        for k in range(2):
            d = dest_ref[0, 0, 2 * t + k]
            pltpu.make_async_copy(ys_hbm.at[pl.ds(d, 1)], buf.at[k, pl.ds(t, 1)], sem).start()
        return 0

    lax.fori_loop(0, tb, issue, 0, unroll=8)
    for k in range(2):
        pltpu.make_async_copy(ys_hbm.at[pl.ds(0, tb)], buf.at[k], sem).wait()

    r = r_ref[...]
    y = r[:, 2:3] * buf[0] + r[:, 3:4] * buf[1]
    pos0 = (pl.program_id(0) % blocks_per_seq) * tb
    hn = jnp.where(_row_valid(pos0, tb, n_real), h_ref[...] + y, 0.0)
    if final_norm:
        ms = jnp.mean(hn * hn, axis=-1, keepdims=True)
        hn = hn * lax.rsqrt(ms + EPS) * gf_ref[...]
    o_ref[...] = hn


def _combine(h, route, dest3, ys, g_final, tp, n_real, final_norm):
    n, d = h.shape
    tb = UNIT
    return pl.pallas_call(
        functools.partial(_combine_body, tb=tb, blocks_per_seq=tp // tb, n_real=n_real,
                          final_norm=final_norm),
        out_shape=jax.ShapeDtypeStruct((n, d), F32),
        grid=(n // tb,),
        in_specs=[pl.BlockSpec((1, 1, 2 * tb), lambda i: (i, 0, 0), memory_space=pltpu.SMEM),
                  pl.BlockSpec((tb, d), lambda i: (i, 0)),
                  pl.BlockSpec((tb, LANES), lambda i: (i, 0)),
                  pl.BlockSpec((1, d), lambda i: (0, 0)),
                  pl.BlockSpec(memory_space=pl.ANY)],
        out_specs=pl.BlockSpec((tb, d), lambda i: (i, 0)),
        scratch_shapes=[pltpu.VMEM((2, tb, d), F32), pltpu.SemaphoreType.DMA(())],
        compiler_params=_cparams(("arbitrary",)),
        name="moe_combine",
    )(dest3, h, route, g_final, ys)


def _moe(h, route, counts_f, g_ffn, w_gate, w_up, w_down, g_final, tp, n_real, final_norm):
    n, d = h.shape
    n_pairs = 2 * n
    counts = counts_f[0, 0:N_EXPERTS].astype(I32)
    padded = (counts + MOE_BM - 1) // MOE_BM * MOE_BM
    p_ends = jnp.cumsum(padded)
    p_starts = p_ends - padded
    dest = jnp.take(p_starts, route[:, 0:2].astype(I32)) + route[:, 4:6].astype(I32)
    n_blocks = -(-(n_pairs + N_EXPERTS * (MOE_BM - 1)) // MOE_BM)
    block_expert = jnp.minimum(
        jnp.searchsorted(p_ends, jnp.arange(n_blocks, dtype=I32) * MOE_BM, side="right"),
        N_EXPERTS - 1).astype(I32)
    n_used = (p_ends[-1:] // MOE_BM).astype(I32)
    meta = jnp.concatenate([counts, p_starts, padded, n_used]).astype(I32)
    dest3 = dest.astype(I32).reshape(n // UNIT, 1, 2 * UNIT)

    xs = _dispatch(h, dest3, meta, n_blocks * MOE_BM)
    ys = _experts(xs, g_ffn, w_gate, w_up, w_down, block_expert, n_used)
    return _combine(h, route, dest3, ys, g_final, tp, n_real, final_norm)


def _short_conv(x, halo_ref, w_ref, first):
    rows = x.shape[0]

    @pl.when(first)
    def _():
        halo_ref[...] = jnp.zeros_like(halo_ref)

    x_ext = jnp.concatenate([halo_ref[...], x], axis=0)
    halo_ref[...] = x[rows - 8:rows]
    acc = jnp.zeros_like(x)
    for j, tap in _shifted_taps(x_ext, 8, rows, SHORT_K):
        acc = acc + w_ref[j:j + 1, :] * tap
    return acc


def _bdot(a, b):
    return jnp.dot(a.astype(BF16), b.astype(BF16), preferred_element_type=F32)


def _gdn_pre_body(z_ref, sc_ref, st_ref, cw_ref, al_r_ref, dt_r_ref, al_c_ref, dt_c_ref,
                  u_ref, w_ref, qd_ref, kd_ref, at_ref, eg_ref, halo_ref):
    rows = z_ref.shape[0]
    hw = N_HEADS * HEAD_DIM
    x = _silu(_short_conv(z_ref[...].astype(F32), halo_ref, cw_ref, pl.program_id(1) == 0))

    sc = sc_ref[...]
    g_cols = -jnp.exp(al_r_ref[...]) * _softplus(sc + dt_r_ref[...])
    beta_cols = _sigmoid(sc)
    st = st_ref[...]
    g_rows = -jnp.exp(al_c_ref[:, 0:1]) * _softplus(st + dt_c_ref[:, 0:1])

    ri = lax.broadcasted_iota(I32, (rows, rows), 0)
    ci = lax.broadcasted_iota(I32, (rows, rows), 1)
    same64 = (ri >> 6) == (ci >> 6)
    same32 = (ri >> 5) == (ci >> 5)
    same16 = (ri >> 4) == (ci >> 4)
    lower = ri >= ci
    strict = ri > ci
    lane = lax.broadcasted_iota(I32, (rows, LANES), 1)
    eg_slab = jnp.zeros((rows, LANES), F32)
    scale = HEAD_DIM ** -0.5

    for h in range(N_HEADS):
        sl = slice(h * HEAD_DIM, (h + 1) * HEAD_DIM)
        q = x[:, sl]
        k = x[:, hw + h * HEAD_DIM:hw + (h + 1) * HEAD_DIM]
        v = x[:, 2 * hw + h * HEAD_DIM:2 * hw + (h + 1) * HEAD_DIM]
        q = q * lax.rsqrt(jnp.sum(q * q, axis=-1, keepdims=True) + EPS)
        k = k * lax.rsqrt(jnp.sum(k * k, axis=-1, keepdims=True) + EPS)
        g_col = g_cols[:, h:h + 1]
        beta = beta_cols[:, N_HEADS + h:N_HEADS + h + 1]
        g_row = g_rows[h:h + 1, :]

        gc_col = jnp.sum(jnp.where(same64 & lower, g_row, 0.0), axis=1, keepdims=True)
        gc_row = jnp.sum(jnp.where(same64 & (ri <= ci), g_col, 0.0), axis=0, keepdims=True)
        gtot_col = jnp.sum(jnp.where(same64, g_row, 0.0), axis=1, keepdims=True)
        decay = jnp.exp(jnp.where(same64 & lower, gc_col - gc_row, NEG))
        eg = jnp.exp(gc_col)
        ekd = jnp.exp(gtot_col - gc_col)

        kb = k * beta
        kbf = k.astype(BF16)
        kk = lax.dot_general(kb.astype(BF16), kbf, (((1,), (1,)), ((), ())), preferred_element_type=F32)
        qs = q * scale
        qk = lax.dot_general(qs.astype(BF16), kbf, (((1,), (1,)), ((), ())), preferred_element_type=F32)
        attn = qk * decay
        a_mat = jnp.where(strict, kk * decay, 0.0)

        p = jnp.where(same16, -a_mat, 0.0)
        nmat = p
        qq = p
        for _ in range(3):
            qq = _bdot(qq, qq)
            nmat = nmat + qq + _bdot(qq, nmat)
        for level_mask, inner_mask in ((same32, same16), (same64, same32)):
            off = jnp.where(level_mask & jnp.logical_not(inner_mask), a_mat, 0.0)
            bmat = off + _bdot(nmat, off)
            nmat = nmat - bmat - _bdot(bmat, nmat)

        rhs = jnp.concatenate([v * beta, kb * eg], axis=1)
        uw = rhs + _bdot(nmat, rhs)
        u_ref[:, sl] = uw[:, 0:HEAD_DIM]
        w_ref[:, sl] = uw[:, HEAD_DIM:2 * HEAD_DIM].astype(BF16)
        qd_ref[:, sl] = (qs * eg).astype(BF16)
        kd_ref[:, sl] = (k * ekd).astype(BF16)
        for c in range(rows // CHUNK):
            cs = slice(c * CHUNK, (c + 1) * CHUNK)
            at_ref[h, cs, :] = attn[cs, cs].astype(BF16)
        eg_slab = jnp.where(lane == h, eg, eg_slab)
    eg_ref[...] = eg_slab


def _gdn_pre(z, sc, st, conv_w, a_log, dt_bias, bsz, tp):
    n = z.shape[0]
    hw = N_HEADS * HEAD_DIM
    rows = UNIT
    nu = tp // rows
    pad_r = lambda v: jnp.pad(v.astype(F32), (0, LANES - N_HEADS)).reshape(1, LANES)
    pad_c = lambda v: jnp.broadcast_to(jnp.pad(v.astype(F32), (0, 16 - N_HEADS))[:, None], (16, LANES))
    outs = pl.pallas_call(
        _gdn_pre_body,
        out_shape=[jax.ShapeDtypeStruct((n, hw), F32),
                   jax.ShapeDtypeStruct((n, hw), BF16),
                   jax.ShapeDtypeStruct((n, hw), BF16),
                   jax.ShapeDtypeStruct((n, hw), BF16),
                   jax.ShapeDtypeStruct((N_HEADS, n, CHUNK), BF16),
                   jax.ShapeDtypeStruct((n, LANES), F32)],
        grid=(bsz, nu),
        in_specs=[pl.BlockSpec((rows, 3 * hw), lambda b, i: (b * nu + i, 0)),
                  pl.BlockSpec((rows, LANES), lambda b, i: (b * nu + i, 0)),
                  pl.BlockSpec((16, rows), lambda b, i: (0, b * nu + i)),
                  pl.BlockSpec((8, 3 * hw), lambda b, i: (0, 0)),
                  pl.BlockSpec((1, LANES), lambda b, i: (0, 0)),
                  pl.BlockSpec((1, LANES), lambda b, i: (0, 0)),
                  pl.BlockSpec((16, LANES), lambda b, i: (0, 0)),
                  pl.BlockSpec((16, LANES), lambda b, i: (0, 0))],
        out_specs=[pl.BlockSpec((rows, hw), lambda b, i: (b * nu + i, 0)),
                   pl.BlockSpec((rows, hw), lambda b, i: (b * nu + i, 0)),
                   pl.BlockSpec((rows, hw), lambda b, i: (b * nu + i, 0)),
                   pl.BlockSpec((rows, hw), lambda b, i: (b * nu + i, 0)),
                   pl.BlockSpec((N_HEADS, rows, CHUNK), lambda b, i: (0, b * nu + i, 0)),
                   pl.BlockSpec((rows, LANES), lambda b, i: (b * nu + i, 0))],
        scratch_shapes=[pltpu.VMEM((8, 3 * hw), F32)],
        compiler_params=_cparams(("arbitrary", "arbitrary")),
        name="gdn_pre",
    )(z, sc, st, jnp.pad(conv_w.astype(F32), ((0, 8 - SHORT_K), (0, 0))),
      pad_r(a_log), pad_r(dt_bias), pad_c(a_log), pad_c(dt_bias))
    return outs


def _gdn_scan_body(u_ref, w_ref, qd_ref, kd_ref, at_ref, eg_ref, z_ref, gn_ref, o_ref, s_ref, *, bsz):
    @pl.when(pl.program_id(0) == 0)
    def _():
        s_ref[...] = jnp.zeros_like(s_ref)

    for b in range(bsz):
        for h in range(N_HEADS):
            sl = slice(h * HEAD_DIM, (h + 1) * HEAD_DIM)
            s = s_ref[b * N_HEADS + h]
            wq = jnp.concatenate([w_ref[b, :, sl], qd_ref[b, :, sl]], axis=0)
            r = jnp.dot(wq, s.astype(BF16), preferred_element_type=F32)
            v_new = u_ref[b, :, sl] - r[0:CHUNK]
            vb = v_new.astype(BF16)
            o = r[CHUNK:2 * CHUNK] + jnp.dot(at_ref[h, b], vb, preferred_element_type=F32)
            gt = eg_ref[b, CHUNK - 1:CHUNK, h:h + 1]
            s_ref[b * N_HEADS + h] = s * gt + lax.dot_general(
                kd_ref[b, :, sl], vb, (((0,), (0,)), ((), ())), preferred_element_type=F32)
            on = o * lax.rsqrt(jnp.mean(o * o, axis=-1, keepdims=True) + EPS) * gn_ref[...]
            o_ref[b, :, sl] = (on * _silu(z_ref[b, :, sl].astype(F32))).astype(o_ref.dtype)


def _gdn_scan(u, w, qd, kd, attn, eg, z, out_norm_g, bsz, tp, z_col):
    hw = N_HEADS * HEAD_DIM
    nc = tp // CHUNK
    v3 = lambda a: a.reshape(bsz, tp, a.shape[-1])
    blk3 = pl.BlockSpec((bsz, CHUNK, hw), lambda c: (0, c, 0))
    return pl.pallas_call(
        functools.partial(_gdn_scan_body, bsz=bsz),
        out_shape=jax.ShapeDtypeStruct((bsz, tp, hw), BF16),
        grid=(nc,),
        in_specs=[blk3, blk3, blk3, blk3,
                  pl.BlockSpec((N_HEADS, bsz, CHUNK, CHUNK), lambda c: (0, 0, c, 0)),
                  pl.BlockSpec((bsz, CHUNK, LANES), lambda c: (0, c, 0)),
                  pl.BlockSpec((bsz, CHUNK, hw), lambda c: (0, c, z_col)),
                  pl.BlockSpec((1, HEAD_DIM), lambda c: (0, 0))],
        out_specs=blk3,
        scratch_shapes=[pltpu.VMEM((bsz * N_HEADS, HEAD_DIM, HEAD_DIM), F32)],
        compiler_params=_cparams(("arbitrary",)),
        name="gdn_scan",
    )(v3(u), v3(w), v3(qd), v3(kd), attn.reshape(N_HEADS, bsz, tp, CHUNK), v3(eg), v3(z),
      out_norm_g.astype(F32).reshape(1, HEAD_DIM)).reshape(bsz * tp, hw)


def _lru_body(x_ref, gate_ref, cw_ref, cb_ref, wr_ref, br_ref, wi_ref, bi_ref, lam_ref, o_ref,
              halo_ref, hc_ref, *, bt, n_real):
    first = pl.program_id(1) == 0

    @pl.when(first)
    def _():
        hc_ref[...] = jnp.zeros_like(hc_ref)

    x = _short_conv(x_ref[...].astype(F32), halo_ref, cw_ref, first) + cb_ref[...]
    nblk = wr_ref.shape[0]
    bd = wr_ref.shape[1]
    xb = x.astype(BF16)
    rg = jnp.concatenate([jnp.dot(xb[:, n * bd:(n + 1) * bd], wr_ref[n], preferred_element_type=F32)
                          for n in range(nblk)], axis=1)
    ig = jnp.concatenate([jnp.dot(xb[:, n * bd:(n + 1) * bd], wi_ref[n], preferred_element_type=F32)
                          for n in range(nblk)], axis=1)
    r = _sigmoid(rg + br_ref[...])
    ig = _sigmoid(ig + bi_ref[...])
    log_a = (-LRU_C * _softplus(-lam_ref[...])) * r
    a = jnp.exp(log_a)
    th = jnp.tanh(log_a)
    b = jnp.sqrt(-2.0 * th / (1.0 - th)) * (ig * x)
    b = jnp.where(_row_valid(pl.program_id(1) * bt, bt, n_real), b, 0.0)

    row = lax.broadcasted_iota(I32, (bt, 1), 0)
    d = 1
    while d < bt:
        keep = row >= d
        a_sh = jnp.where(keep, pltpu.roll(a, d, 0), 1.0)
        b_sh = jnp.where(keep, pltpu.roll(b, d, 0), 0.0)
        b = a * b_sh + b
        a = a * a_sh
        d *= 2
    hs = a * hc_ref[0:1, :] + b
    hc_ref[...] = jnp.broadcast_to(hs[bt - 1:bt, :], hc_ref.shape)
    o_ref[...] = (hs * jax.nn.gelu(gate_ref[...].astype(F32))).astype(o_ref.dtype)


def _lru(z, conv_w, conv_b, w_rg, b_rg, w_ig, b_ig, lam, bsz, tp, n_real, x_col, gate_col):
    n = z.shape[0]
    c = conv_w.shape[1]
    bt = UNIT
    nt = tp // bt
    row = lambda v: v.astype(F32).reshape(1, c)
    wspec = pl.BlockSpec(w_rg.shape, lambda b, i: (0, 0, 0))
    vspec = pl.BlockSpec((1, c), lambda b, i: (0, 0))
    return pl.pallas_call(
        functools.partial(_lru_body, bt=bt, n_real=n_real),
        out_shape=jax.ShapeDtypeStruct((n, c), BF16),
        grid=(bsz, nt),
        in_specs=[pl.BlockSpec((bt, c), lambda b, i: (b * nt + i, x_col)),
                  pl.BlockSpec((bt, c), lambda b, i: (b * nt + i, gate_col)),
                  pl.BlockSpec((8, c), lambda b, i: (0, 0)),
                  vspec, wspec, vspec, wspec, vspec, vspec],
        out_specs=pl.BlockSpec((bt, c), lambda b, i: (b * nt + i, 0)),
        scratch_shapes=[pltpu.VMEM((8, c), F32), pltpu.VMEM((8, c), F32)],
        compiler_params=_cparams(("arbitrary", "arbitrary")),
        name="rg_lru",
    )(z, z, jnp.pad(conv_w.astype(F32), ((0, 8 - SHORT_K), (0, 0))), row(conv_b),
      w_rg.astype(BF16), row(b_rg), w_ig.astype(BF16), row(b_ig), row(lam))


def _pick(n, candidates):
    for c in candidates:
        if n % c == 0:
            return c
    raise ValueError(f"no block size in {candidates} divides {n}")


def _pack_weights_body(a_ref, b_ref, o_ref, *, head_tiles, gap):
    j = pl.program_id(0)
    tile = o_ref.shape[1]

    @pl.when(j < head_tiles)
    def _():
        o_ref[...] = a_ref[...].astype(BF16)

    @pl.when(j >= head_tiles)
    def _():
        x = jnp.concatenate([a_ref[...], b_ref[...]], axis=1)
        o_ref[...] = x[:, gap:gap + tile].astype(BF16)


def _pack_inproj_weights(w_in, head, gap):
    d, total = w_in.shape
    tile = 512
    n_out = total - gap
    return pl.pallas_call(
        functools.partial(_pack_weights_body, head_tiles=head // tile, gap=gap),
        out_shape=jax.ShapeDtypeStruct((d, n_out), BF16),
        grid=(n_out // tile,),
        in_specs=[pl.BlockSpec((d, tile), lambda j: (0, j)),
                  pl.BlockSpec((d, LANES), lambda j: (0, (tile // LANES) * (j + 1)))],
        out_specs=pl.BlockSpec((d, tile), lambda j: (0, j)),
        compiler_params=_cparams(("arbitrary",)),
        name="pack_inproj_weights",
    )(w_in, w_in)


def _small_weights(cols):
    k = cols.shape[1]
    return (jnp.pad(cols.T, ((0, 16 - k), (0, 0))).astype(BF16),
            jnp.pad(cols, ((0, 0), (0, LANES - k))).astype(BF16))


def _router_weights(w_group, b_group, w_expert, b_expert):
    w = jnp.concatenate([w_expert, w_group], axis=1).astype(F32)
    b = jnp.concatenate([b_expert, b_group]).astype(F32)
    k = w.shape[1]
    w = jnp.pad(w, ((0, 0), (0, LANES - k)))
    w_hi = w.astype(BF16)
    w_lo = (w - w_hi.astype(F32)).astype(BF16)
    return jnp.concatenate([w_hi, w_lo], axis=1), jnp.pad(b, (0, LANES - k)).reshape(1, LANES)


def kernel(x, meta_tokens, norm_mix_g, norm_ffn_g, norm_final_g, ab_w_in, ab_forget_b, ab_conv_w, ab_conv_b, ab_ln_g, ab_ln_b, ab_w_out, cd_w_in, cd_qkv_conv_w, cd_a_log, cd_dt_bias, cd_out_norm_g, cd_lru_conv_w, cd_lru_conv_b, cd_w_rg, cd_b_rg, cd_w_ig, cd_b_ig, cd_lru_lambda, cd_w_out, moe_w_group, moe_b_group, moe_w_expert, moe_b_expert, moe_w_gate, moe_w_up, moe_w_down):
    bsz, seq, d = x.shape
    depth = norm_mix_g.shape[0]
    n_real = N_META + seq
    tp = -(-(OFF + seq) // UNIT) * UNIT
    n = bsz * tp
    hw = N_HEADS * HEAD_DIM
    bm = _pick(n, (768, 512, 256))
    bn = 512

    meta = jnp.broadcast_to(meta_tokens[None].astype(F32), (bsz, N_META, d))
    h = jnp.concatenate([jnp.zeros((bsz, FRONT, d), F32), meta, x.astype(F32),
                         jnp.zeros((bsz, tp - OFF - seq, d), F32)], axis=1).reshape(n, d)
    row = lambda v: v.astype(F32).reshape(1, -1)

    for layer in range(depth):
        i = layer // 2
        g_mix = row(norm_mix_g[layer])
        if layer % 2 == 0:
            w_in = ab_w_in[i]
            w_main = _pack_inproj_weights(w_in.astype(F32), 3 * hw, N_HEADS)
            wst, wsc = _small_weights(w_in[:, 3 * hw:3 * hw + N_HEADS])
            z, zt, _ = _norm_inproj(h, g_mix, w_main, wst, wsc, bm, bn)
            c = _fox_gate(zt, ab_forget_b[i], bsz, tp, n_real)
            ya = _fox_attn(z, c.reshape(N_HEADS, 1, n), bsz, tp)
            yb = _conformer(z, ab_conv_w[i], ab_conv_b[i], ab_ln_g[i], ab_ln_b[i], bsz, tp, 3, 4)
            w_out = ab_w_out[i].astype(BF16)
        else:
            w_in = cd_w_in[i]
            w_main = _pack_inproj_weights(w_in.astype(F32), 3 * hw, 2 * N_HEADS)
            wst, wsc = _small_weights(w_in[:, 3 * hw:3 * hw + 2 * N_HEADS])
            z, zt, zc = _norm_inproj(h, g_mix, w_main, wst, wsc, bm, bn)
            u, w, qd, kd, attn, eg = _gdn_pre(z, zc, zt, cd_qkv_conv_w[i], cd_a_log[i], cd_dt_bias[i],
                                              bsz, tp)
            ya = _gdn_scan(u, w, qd, kd, attn, eg, z, cd_out_norm_g[i], bsz, tp, 3)
            yb = _lru(z, cd_lru_conv_w[i], cd_lru_conv_b[i], cd_w_rg[i], cd_b_rg[i], cd_w_ig[i],
                      cd_b_ig[i], cd_lru_lambda[i], bsz, tp, n_real, 4, 5)
            w_out = cd_w_out[i].astype(BF16)
        g_ffn = row(norm_ffn_g[layer])
        w_r, b_r = _router_weights(moe_w_group[layer], moe_b_group[layer],
                                   moe_w_expert[layer], moe_b_expert[layer])
        h, route, counts = _outproj_router(ya, yb, h, w_out, g_ffn, w_r, b_r, tp, n_real)
        h = _moe(h, route, counts, g_ffn, moe_w_gate[layer], moe_w_up[layer], moe_w_down[layer],
                 row(norm_final_g), tp, n_real, final_norm=(layer == depth - 1))
    return h.reshape(bsz, tp, d)[:, OFF:OFF + seq].astype(x.dtype)
```

```python
import functools

import jax
import jax.numpy as jnp
from jax import lax
from jax.experimental import pallas as pl
from jax.experimental.pallas import tpu as pltpu

F32 = jnp.float32
BF16 = jnp.bfloat16
I32 = jnp.int32

EPS = 1e-6
N_META = 16
CHUNK = 64
FRONT = CHUNK - N_META
OFF = FRONT + N_META
LANES = 128
UNIT = 256
HEAD_DIM = 128
N_HEADS = 8
CONF_K = 31
SHORT_K = 4
LRU_C = 8.0
N_GROUPS = 4
EXPERTS_PER_GROUP = 8
N_EXPERTS = N_GROUPS * EXPERTS_PER_GROUP
MOE_BM = 256
NEG = -1e30
LOG2E = 1.4426950408889634
VMEM_LIMIT = 56 * 1024 * 1024


def _cparams(sem):
    return pltpu.CompilerParams(dimension_semantics=sem, vmem_limit_bytes=VMEM_LIMIT)


def _row_valid(pos0, rows, n_real):
    pos = pos0 + lax.broadcasted_iota(I32, (rows, 1), 0)
    return (pos >= FRONT) & (pos < FRONT + n_real)


def _sigmoid(x):
    return 1.0 / (1.0 + jnp.exp(-x))


def _softplus(x):
    return jnp.maximum(x, 0.0) + jnp.log1p(jnp.exp(-jnp.abs(x)))


def _silu(x):
    return x * _sigmoid(x)


def _norm_inproj_body(h_ref, g_ref, w_ref, wst_ref, wsc_ref, o_ref, ot_ref, oc_ref, xn_ref):
    @pl.when(pl.program_id(1) == 0)
    def _():
        x = h_ref[...]
        ms = jnp.mean(x * x, axis=-1, keepdims=True)
        xn = (x * lax.rsqrt(ms + EPS) * g_ref[...]).astype(BF16)
        xn_ref[...] = xn
        ot_ref[...] = lax.dot_general(wst_ref[...], xn, (((1,), (1,)), ((), ())),
                                      preferred_element_type=F32)
        oc_ref[...] = jnp.dot(xn, wsc_ref[...], preferred_element_type=F32)

    o_ref[...] = jnp.dot(xn_ref[...], w_ref[...], preferred_element_type=F32).astype(o_ref.dtype)


def _norm_inproj(h, g, w_main, w_small_t, w_small_c, bm, bn):
    n, d = h.shape
    nw = w_main.shape[1]
    return pl.pallas_call(
        _norm_inproj_body,
        out_shape=[jax.ShapeDtypeStruct((n, nw), BF16),
                   jax.ShapeDtypeStruct((16, n), F32),
                   jax.ShapeDtypeStruct((n, LANES), F32)],
        grid=(n // bm, nw // bn),
        in_specs=[pl.BlockSpec((bm, d), lambda i, j: (i, 0)),
                  pl.BlockSpec((1, d), lambda i, j: (0, 0)),
                  pl.BlockSpec((d, bn), lambda i, j: (0, j)),
                  pl.BlockSpec((16, d), lambda i, j: (0, 0)),
                  pl.BlockSpec((d, LANES), lambda i, j: (0, 0))],
        out_specs=[pl.BlockSpec((bm, bn), lambda i, j: (i, j)),
                   pl.BlockSpec((16, bm), lambda i, j: (0, i)),
                   pl.BlockSpec((bm, LANES), lambda i, j: (i, 0))],
        scratch_shapes=[pltpu.VMEM((bm, d), BF16)],
        compiler_params=_cparams(("arbitrary", "arbitrary")),
        name="norm_inproj",
    )(h, g, w_main, w_small_t, w_small_c)


def _split3(x):
    p1 = x.astype(BF16)
    r1 = x - p1.astype(F32)
    p2 = r1.astype(BF16)
    p3 = (r1 - p2.astype(F32)).astype(BF16)
    return p1, p2, p3


def _fox_prep_body(k_ref, v_ref, f_ref, fb_ref, kx_ref, vx_ref, carry_ref, *, rows, n_real):
    i = pl.program_id(1)

    @pl.when(i == 0)
    def _():
        carry_ref[...] = jnp.zeros_like(carry_ref)

    f = f_ref[...] + fb_ref[...]
    lf = jnp.minimum(f, 0.0) - jnp.log1p(jnp.exp(-jnp.abs(f)))
    pos = i * rows + lax.broadcasted_iota(I32, (rows, 1), 0)
    lf = jnp.where((pos >= FRONT) & (pos < FRONT + n_real), lf, 0.0)
    rr = lax.broadcasted_iota(I32, (rows, rows), 0)
    cc = lax.broadcasted_iota(I32, (rows, rows), 1)
    tri = jnp.where(rr >= cc, 1.0, 0.0).astype(BF16)
    c = carry_ref[0:1, :]
    for piece in _split3(lf):
        c = c + jnp.dot(tri, piece, preferred_element_type=F32)
    carry_ref[...] = jnp.broadcast_to(c[rows - 1:rows, :], carry_ref.shape)
    cs = jnp.where(pos < FRONT, NEG, -LOG2E * c)
    p1, p2, p3 = (p.astype(F32) for p in _split3(cs))
    lane = lax.broadcasted_iota(I32, (rows, LANES), 1)
    ones_col = jnp.where(lane == 0, 1.0, 0.0).astype(BF16)
    for h in range(N_HEADS):
        ext = jnp.where(lane == 0, p1[:, h:h + 1],
                        jnp.where(lane == 1, p2[:, h:h + 1], jnp.where(lane == 2, p3[:, h:h + 1], 0.0)))
        kx_ref[:, (2 * h) * HEAD_DIM:(2 * h + 1) * HEAD_DIM] = k_ref[:, h * HEAD_DIM:(h + 1) * HEAD_DIM]
        kx_ref[:, (2 * h + 1) * HEAD_DIM:(2 * h + 2) * HEAD_DIM] = ext.astype(BF16)
        vx_ref[:, (2 * h) * HEAD_DIM:(2 * h + 1) * HEAD_DIM] = v_ref[:, h * HEAD_DIM:(h + 1) * HEAD_DIM]
        vx_ref[:, (2 * h + 1) * HEAD_DIM:(2 * h + 2) * HEAD_DIM] = ones_col


def _fox_prep(z, f_slab, forget_b, bsz, tp, n_real):
    n = z.shape[0]
    hw = N_HEADS * HEAD_DIM
    rows = UNIT
    nt = tp // rows
    fb = jnp.pad(forget_b.astype(F32), (0, LANES - N_HEADS)).reshape(1, LANES)
    return pl.pallas_call(
        functools.partial(_fox_prep_body, rows=rows, n_real=n_real),
        out_shape=[jax.ShapeDtypeStruct((n, 2 * hw), BF16), jax.ShapeDtypeStruct((n, 2 * hw), BF16)],
        grid=(bsz, nt),
        in_specs=[pl.BlockSpec((rows, hw), lambda b, i: (b * nt + i, 1)),
                  pl.BlockSpec((rows, hw), lambda b, i: (b * nt + i, 2)),
                  pl.BlockSpec((rows, LANES), lambda b, i: (b * nt + i, 0)),
                  pl.BlockSpec((1, LANES), lambda b, i: (0, 0))],
        out_specs=[pl.BlockSpec((rows, 2 * hw), lambda b, i: (b * nt + i, 0)),
                   pl.BlockSpec((rows, 2 * hw), lambda b, i: (b * nt + i, 0))],
        scratch_shapes=[pltpu.VMEM((8, LANES), F32)],
        compiler_params=_cparams(("arbitrary", "arbitrary")),
        name="fox_prep",
    )(z, z, f_slab, fb)


def _fox_attn_body(q_ref, k_ref, v_ref, o_ref, *, blk, n_split):
    qi = pl.program_id(2)
    half = blk // n_split
    lane = lax.broadcasted_iota(I32, (blk, HEAD_DIM), 1)
    q = jnp.concatenate([q_ref[...], jnp.where(lane < 3, 1.0, 0.0).astype(BF16)], axis=1)
    qs = tuple(q[r * half:(r + 1) * half] for r in range(n_split))

    def step(j, carry, diagonal):
        start = pl.multiple_of(j * blk, blk)
        k = k_ref[pl.ds(start, blk), :]
        v = v_ref[pl.ds(start, blk), :]
        ss = [lax.dot_general(qh, k, (((1,), (1,)), ((), ())), preferred_element_type=F32) for qh in qs]
        ps, scales, ms = [], [], []
        for r, s in enumerate(ss):
            m = carry[2 * r]
            if diagonal:
                row = r * half + lax.broadcasted_iota(I32, (half, blk), 0)
                col = lax.broadcasted_iota(I32, (half, blk), 1)
                s = jnp.where(col <= row, s, NEG)
            m_new = jnp.maximum(m, jnp.max(s, axis=1, keepdims=True))
            ps.append(jnp.exp2(s - m_new).astype(BF16))
            scales.append(jnp.exp2(m - m_new))
            ms.append(m_new)
        pvs = [jnp.dot(p, v, preferred_element_type=F32) for p in ps]
        out = []
        for r in range(n_split):
            out += [ms[r], scales[r] * carry[2 * r + 1] + pvs[r]]
        return tuple(out)

    init = (jnp.full((half, 1), NEG, F32), jnp.zeros((half, 2 * HEAD_DIM), F32)) * n_split
    carry = lax.fori_loop(0, qi, lambda j, c: step(j, c, False), init)
    res = step(qi, carry, True)
    for r in range(n_split):
        acc = res[2 * r + 1]
        o_ref[r * half:(r + 1) * half, :] = (acc[:, 0:HEAD_DIM] / acc[:, HEAD_DIM:HEAD_DIM + 1]
                                             ).astype(o_ref.dtype)


def _fox_attn(z, kx, vx, bsz, tp):
    n = z.shape[0]
    blk = _pick(tp, (768, 512, 256))
    nq = tp // blk
    return pl.pallas_call(
        functools.partial(_fox_attn_body, blk=blk, n_split=2),
        out_shape=jax.ShapeDtypeStruct((n, N_HEADS * HEAD_DIM), BF16),
        grid=(bsz, N_HEADS, nq),
        in_specs=[pl.BlockSpec((blk, HEAD_DIM), lambda b, h, i: (b * nq + i, h)),
                  pl.BlockSpec((tp, 2 * HEAD_DIM), lambda b, h, i: (b, h)),
                  pl.BlockSpec((tp, 2 * HEAD_DIM), lambda b, h, i: (b, h))],
        out_specs=pl.BlockSpec((blk, HEAD_DIM), lambda b, h, i: (b * nq + i, h)),
        compiler_params=_cparams(("arbitrary", "arbitrary", "arbitrary")),
        name="fox_attn",
    )(z, kx, vx)


def _shifted_taps(x_ext, halo, rows, n_taps):
    for r in range(min(8, n_taps)):
        rolled = x_ext if r == 0 else pltpu.roll(x_ext, r, 0)
        for q in range(halo // 8):
            s = 8 * q + r
            if s > n_taps - 1:
                continue
            yield n_taps - 1 - s, rolled[halo - 8 * q:halo - 8 * q + rows]


def _conformer_body(a_ref, b_ref, w_ref, cb_ref, lg_ref, lb_ref, o_ref, halo_ref, *, bt):
    halo = halo_ref.shape[0]

    @pl.when(pl.program_id(1) == 0)
    def _():
        halo_ref[...] = jnp.zeros_like(halo_ref)

    u = a_ref[...].astype(F32) * _sigmoid(b_ref[...].astype(F32))
    x_ext = jnp.concatenate([halo_ref[...], u], axis=0)
    halo_ref[...] = u[bt - halo:bt]
    acc = jnp.zeros_like(u)
    for j, tap in _shifted_taps(x_ext, halo, bt, CONF_K):
        acc = acc + w_ref[j:j + 1, :] * tap
    y = acc + cb_ref[...]
    mu = jnp.mean(y, axis=-1, keepdims=True)
    yc = y - mu
    var = jnp.mean(yc * yc, axis=-1, keepdims=True)
    yn = yc * lax.rsqrt(var + EPS) * lg_ref[...] + lb_ref[...]
    o_ref[...] = _silu(yn).astype(o_ref.dtype)


def _conformer(z, conv_w, conv_b, ln_g, ln_b, bsz, tp, col_a, col_b):
    n = z.shape[0]
    c = conv_w.shape[1]
    bt = UNIT
    nt = tp // bt
    w = jnp.pad(conv_w.astype(F32), ((0, 32 - CONF_K), (0, 0)))
    row = lambda v: v.astype(F32).reshape(1, c)
    return pl.pallas_call(
        functools.partial(_conformer_body, bt=bt),
        out_shape=jax.ShapeDtypeStruct((n, c), BF16),
        grid=(bsz, nt),
        in_specs=[pl.BlockSpec((bt, c), lambda b, i: (b * nt + i, col_a)),
                  pl.BlockSpec((bt, c), lambda b, i: (b * nt + i, col_b)),
                  pl.BlockSpec((32, c), lambda b, i: (0, 0)),
                  pl.BlockSpec((1, c), lambda b, i: (0, 0)),
                  pl.BlockSpec((1, c), lambda b, i: (0, 0)),
                  pl.BlockSpec((1, c), lambda b, i: (0, 0))],
        out_specs=pl.BlockSpec((bt, c), lambda b, i: (b * nt + i, 0)),
        scratch_shapes=[pltpu.VMEM((32, c), F32)],
        compiler_params=_cparams(("arbitrary", "arbitrary")),
        name="conformer_conv",
    )(z, z, w, row(conv_b), row(ln_g), row(ln_b))


def _outproj_router_body(ya_ref, yb_ref, h_ref, w_ref, g_ref, wr_ref, br_ref, ho_ref, r_ref, cnt_ref,
                         *, bm, blocks_per_seq, n_real):
    half = ya_ref.shape[1]
    y = jnp.dot(ya_ref[...], w_ref[0:half, :], preferred_element_type=F32)
    y = y + jnp.dot(yb_ref[...], w_ref[half:2 * half, :], preferred_element_type=F32)
    pos0 = (pl.program_id(0) % blocks_per_seq) * bm
    hn = jnp.where(_row_valid(pos0, bm, n_real), h_ref[...] + y, 0.0)
    ho_ref[...] = hn

    ms = jnp.mean(hn * hn, axis=-1, keepdims=True)
    xn = hn * lax.rsqrt(ms + EPS) * g_ref[...]
    xh = xn.astype(BF16)
    xl = (xn - xh.astype(F32)).astype(BF16)
    t = jnp.dot(xh, wr_ref[...], preferred_element_type=F32)
    lg = (t[:, 0:LANES] + t[:, LANES:2 * LANES]
          + jnp.dot(xl, wr_ref[:, 0:LANES], preferred_element_type=F32) + br_ref[...])
    lane = lax.broadcasted_iota(I32, lg.shape, 1)
    lane_f = lane.astype(F32)
    big = float(LANES)

    is_grp = (lane >= N_EXPERTS) & (lane < N_EXPERTS + N_GROUPS)
    gl = jnp.where(is_grp, lg, -jnp.inf)
    gmax = jnp.max(gl, axis=1, keepdims=True)
    gidx = jnp.min(jnp.where(gl == gmax, lane_f, big), axis=1, keepdims=True) - N_EXPERTS
    g_prob = 1.0 / jnp.sum(jnp.where(is_grp, jnp.exp(lg - gmax), 0.0), axis=1, keepdims=True)

    lo = gidx * EXPERTS_PER_GROUP
    in_grp = (lane_f >= lo) & (lane_f < lo + EXPERTS_PER_GROUP)
    el = jnp.where(in_grp, lg, -jnp.inf)
    m1 = jnp.max(el, axis=1, keepdims=True)
    i1 = jnp.min(jnp.where(el == m1, lane_f, big), axis=1, keepdims=True)
    el2 = jnp.where(lane_f == i1, -jnp.inf, el)
    m2 = jnp.max(el2, axis=1, keepdims=True)
    i2 = jnp.min(jnp.where(el2 == m2, lane_f, big), axis=1, keepdims=True)
    e2 = jnp.exp(m2 - m1)
    w1 = g_prob / (1.0 + e2)
    w2 = g_prob * e2 / (1.0 + e2)

    @pl.when(pl.program_id(0) == 0)
    def _():
        cnt_ref[...] = jnp.zeros_like(cnt_ref)

    oh1 = lane_f == i1
    oh2 = lane_f == i2
    onehots = jnp.concatenate([jnp.where(oh1, 1.0, 0.0), jnp.where(oh2, 1.0, 0.0)], axis=1).astype(BF16)
    rr = lax.broadcasted_iota(I32, (bm, bm), 0)
    cc = lax.broadcasted_iota(I32, (bm, bm), 1)
    tri = jnp.where(rr >= cc, 1.0, 0.0).astype(BF16)
    csum = jnp.dot(tri, onehots, preferred_element_type=F32)
    c1 = csum[:, 0:LANES]
    c2 = csum[:, LANES:2 * LANES]
    tot1 = c1[bm - 1:bm, :]
    tot2 = c2[bm - 1:bm, :]
    before = cnt_ref[0:1, :]
    rank1 = jnp.sum(jnp.where(oh1, before + c1 - 1.0, 0.0), axis=1, keepdims=True)
    rank2 = jnp.sum(jnp.where(oh2, before + tot1 + c2 - 1.0, 0.0), axis=1, keepdims=True)
    cnt_ref[...] = jnp.broadcast_to(before + tot1 + tot2, cnt_ref.shape)

    r_ref[...] = jnp.where(lane == 0, i1, jnp.where(lane == 1, i2,
                           jnp.where(lane == 2, w1, jnp.where(lane == 3, w2,
                                     jnp.where(lane == 4, rank1, jnp.where(lane == 5, rank2, 0.0))))))


def _outproj_router(ya, yb, h, w_out, g_ffn, w_router, b_router, tp, n_real):
    n, d = h.shape
    half = ya.shape[1]
    bm = UNIT
    return pl.pallas_call(
        functools.partial(_outproj_router_body, bm=bm, blocks_per_seq=tp // bm, n_real=n_real),
        out_shape=[jax.ShapeDtypeStruct((n, d), F32), jax.ShapeDtypeStruct((n, LANES), F32),
                   jax.ShapeDtypeStruct((8, LANES), F32)],
        grid=(n // bm,),
        in_specs=[pl.BlockSpec((bm, half), lambda i: (i, 0)),
                  pl.BlockSpec((bm, half), lambda i: (i, 0)),
                  pl.BlockSpec((bm, d), lambda i: (i, 0)),
                  pl.BlockSpec((2 * half, d), lambda i: (0, 0)),
                  pl.BlockSpec((1, d), lambda i: (0, 0)),
                  pl.BlockSpec((d, 2 * LANES), lambda i: (0, 0)),
                  pl.BlockSpec((1, LANES), lambda i: (0, 0))],
        out_specs=[pl.BlockSpec((bm, d), lambda i: (i, 0)),
                   pl.BlockSpec((bm, LANES), lambda i: (i, 0)),
                   pl.BlockSpec((8, LANES), lambda i: (0, 0))],
        compiler_params=_cparams(("arbitrary",)),
        name="outproj_router",
    )(ya, yb, h, w_out, g_ffn, w_router, b_router)


def _dispatch_body(meta_ref, dest_ref, h_ref, xs_hbm, zero_ref, sem, zsem, *, tb):
    i = pl.program_id(0)

    def issue(t, _):
        for k in range(2):
            d = dest_ref[0, 0, 2 * t + k]
            pltpu.make_async_copy(h_ref.at[pl.ds(t, 1)], xs_hbm.at[pl.ds(d, 1)], sem).start()
        return 0

    lax.fori_loop(0, tb, issue, 0, unroll=8)

    @pl.when(i == 0)
    def _():
        zero_ref[...] = jnp.zeros_like(zero_ref)

        def per_expert(e, _):
            cnt = meta_ref[e]
            start = meta_ref[N_EXPERTS + e]
            padded = meta_ref[2 * N_EXPERTS + e]

            def pad_copy(r):
                return pltpu.make_async_copy(zero_ref.at[pl.ds(0, 1)], xs_hbm.at[pl.ds(start + r, 1)], zsem)

            def zissue(r, _):
                pad_copy(r).start()
                return 0

            def zwait(r, _):
                pad_copy(r).wait()
                return 0

            lax.fori_loop(cnt, padded, zissue, 0)
            lax.fori_loop(cnt, padded, zwait, 0)
            return 0

        lax.fori_loop(0, N_EXPERTS, per_expert, 0)

        def tail_copy(blk):
            return pltpu.make_async_copy(zero_ref, xs_hbm.at[pl.ds(blk * MOE_BM, MOE_BM)], zsem)

        def tissue(blk, _):
            tail_copy(blk).start()
            return 0

        def twait(blk, _):
            tail_copy(blk).wait()
            return 0

        n_used = meta_ref[3 * N_EXPERTS]
        lax.fori_loop(n_used, xs_hbm.shape[0] // MOE_BM, tissue, 0)
        lax.fori_loop(n_used, xs_hbm.shape[0] // MOE_BM, twait, 0)

    for _ in range(2):
        pltpu.make_async_copy(h_ref, xs_hbm.at[pl.ds(0, tb)], sem).wait()


def _dispatch(h, dest3, meta, n_rows):
    n, d = h.shape
    tb = UNIT
    return pl.pallas_call(
        functools.partial(_dispatch_body, tb=tb),
        out_shape=jax.ShapeDtypeStruct((n_rows, d), F32),
        grid_spec=pltpu.PrefetchScalarGridSpec(
            num_scalar_prefetch=1,
            grid=(n // tb,),
            in_specs=[pl.BlockSpec((1, 1, 2 * tb), lambda i, m: (i, 0, 0), memory_space=pltpu.SMEM),
                      pl.BlockSpec((tb, d), lambda i, m: (i, 0))],
            out_specs=pl.BlockSpec(memory_space=pl.ANY),
            scratch_shapes=[pltpu.VMEM((MOE_BM, d), F32),
                            pltpu.SemaphoreType.DMA(()), pltpu.SemaphoreType.DMA(())]),
        compiler_params=_cparams(("arbitrary",)),
        name="moe_dispatch",
    )(meta, dest3, h)


def _experts_body(be_ref, nu_ref, xs_ref, g_ref, wg_ref, wu_ref, wd_ref, ys_ref,
                  wg_bf, wu_bf, wd_bf):
    i = pl.program_id(0)
    n_used = nu_ref[0]

    @pl.when(i < n_used)
    def _():
        prev = be_ref[jnp.maximum(i - 1, 0)]

        @pl.when((i == 0) | (be_ref[i] != prev))
        def _():
            wg_bf[...] = wg_ref[0, 0].astype(BF16)
            wu_bf[...] = wu_ref[0, 0].astype(BF16)
            wd_bf[...] = wd_ref[0, 0].astype(BF16)

        x = xs_ref[...]
        ms = jnp.mean(x * x, axis=-1, keepdims=True)
        xn = (x * lax.rsqrt(ms + EPS) * g_ref[...]).astype(BF16)
        hg = jnp.dot(xn, wg_bf[...], preferred_element_type=F32)
        hu = jnp.dot(xn, wu_bf[...], preferred_element_type=F32)
        hid = (_silu(hg) * hu).astype(BF16)
        ys_ref[...] = jnp.dot(hid, wd_bf[...], preferred_element_type=F32)

    @pl.when(i >= n_used)
    def _():
        ys_ref[...] = jnp.zeros_like(ys_ref)


def _experts(xs, g_ffn, w_gate, w_up, w_down, layer, block_expert, n_used):
    r, d = xs.shape
    de = w_gate.shape[3]
    nb = r // MOE_BM

    def wblk(i, be, nu):
        return (layer, be[jnp.minimum(i, nu[0] - 1)], 0, 0)

    return pl.pallas_call(
        _experts_body,
        out_shape=jax.ShapeDtypeStruct((r, d), F32),
        grid_spec=pltpu.PrefetchScalarGridSpec(
            num_scalar_prefetch=2,
            grid=(nb,),
            in_specs=[pl.BlockSpec((MOE_BM, d), lambda i, be, nu: (i, 0)),
                      pl.BlockSpec((1, d), lambda i, be, nu: (0, 0)),
                      pl.BlockSpec((1, 1, d, de), wblk),
                      pl.BlockSpec((1, 1, d, de), wblk),
                      pl.BlockSpec((1, 1, de, d), wblk)],
            out_specs=pl.BlockSpec((MOE_BM, d), lambda i, be, nu: (i, 0)),
            scratch_shapes=[pltpu.VMEM((d, de), BF16), pltpu.VMEM((d, de), BF16),
                            pltpu.VMEM((de, d), BF16)]),
        compiler_params=_cparams(("arbitrary",)),
        name="moe_experts",
    )(block_expert, n_used, xs, g_ffn, w_gate, w_up, w_down)


def _combine_body(dest_ref, h_ref, r_ref, gf_ref, ys_hbm, o_ref, buf, sem,
                  *, tb, blocks_per_seq, n_real, final_norm):
    def issue(t, _):
        for k in range(2):
            d = dest_ref[0, 0, 2 * t + k]
            pltpu.make_async_copy(ys_hbm.at[pl.ds(d, 1)], buf.at[k, pl.ds(t, 1)], sem).start()
        return 0

    lax.fori_loop(0, tb, issue, 0, unroll=8)
    for k in range(2):
        pltpu.make_async_copy(ys_hbm.at[pl.ds(0, tb)], buf.at[k], sem).wait()

    r = r_ref[...]
    y = r[:, 2:3] * buf[0] + r[:, 3:4] * buf[1]
    pos0 = (pl.program_id(0) % blocks_per_seq) * tb
    hn = jnp.where(_row_valid(pos0, tb, n_real), h_ref[...] + y, 0.0)
    if final_norm:
        ms = jnp.mean(hn * hn, axis=-1, keepdims=True)
        hn = hn * lax.rsqrt(ms + EPS) * gf_ref[...]
    o_ref[...] = hn


def _combine(h, route, dest3, ys, g_final, tp, n_real, final_norm):
    n, d = h.shape
    tb = UNIT
    return pl.pallas_call(
        functools.partial(_combine_body, tb=tb, blocks_per_seq=tp // tb, n_real=n_real,
                          final_norm=final_norm),
        out_shape=jax.ShapeDtypeStruct((n, d), F32),
        grid=(n // tb,),
        in_specs=[pl.BlockSpec((1, 1, 2 * tb), lambda i: (i, 0, 0), memory_space=pltpu.SMEM),
                  pl.BlockSpec((tb, d), lambda i: (i, 0)),
                  pl.BlockSpec((tb, LANES), lambda i: (i, 0)),
                  pl.BlockSpec((1, d), lambda i: (0, 0)),
                  pl.BlockSpec(memory_space=pl.ANY)],
        out_specs=pl.BlockSpec((tb, d), lambda i: (i, 0)),
        scratch_shapes=[pltpu.VMEM((2, tb, d), F32), pltpu.SemaphoreType.DMA(())],
        compiler_params=_cparams(("arbitrary",)),
        name="moe_combine",
    )(dest3, h, route, g_final, ys)


def _moe(h, route, counts_f, g_ffn, w_gate, w_up, w_down, layer, g_final, tp, n_real, final_norm):
    n, d = h.shape
    n_pairs = 2 * n
    counts = counts_f[0, 0:N_EXPERTS].astype(I32)
    padded = (counts + MOE_BM - 1) // MOE_BM * MOE_BM
    p_ends = jnp.cumsum(padded)
    p_starts = p_ends - padded
    dest = jnp.take(p_starts, route[:, 0:2].astype(I32)) + route[:, 4:6].astype(I32)
    n_blocks = -(-(n_pairs + N_EXPERTS * (MOE_BM - 1)) // MOE_BM)
    block_row0 = jnp.arange(n_blocks, dtype=I32) * MOE_BM
    block_expert = jnp.minimum(jnp.sum((p_ends[None, :] <= block_row0[:, None]).astype(I32), axis=1),
                               N_EXPERTS - 1)
    n_used = (p_ends[-1:] // MOE_BM).astype(I32)
    meta = jnp.concatenate([counts, p_starts, padded, n_used]).astype(I32)
    dest3 = dest.astype(I32).reshape(n // UNIT, 1, 2 * UNIT)

    xs = _dispatch(h, dest3, meta, n_blocks * MOE_BM)
    ys = _experts(xs, g_ffn, w_gate, w_up, w_down, layer, block_expert, n_used)
    return _combine(h, route, dest3, ys, g_final, tp, n_real, final_norm)


def _short_conv(x, halo_ref, w_ref, first):
    rows = x.shape[0]

    @pl.when(first)
    def _():
        halo_ref[...] = jnp.zeros_like(halo_ref)

    x_ext = jnp.concatenate([halo_ref[...], x], axis=0)
    halo_ref[...] = x[rows - 8:rows]
    acc = jnp.zeros_like(x)
    for j, tap in _shifted_taps(x_ext, 8, rows, SHORT_K):
        acc = acc + w_ref[j:j + 1, :] * tap
    return acc


def _bdot(a, b):
    return jnp.dot(a.astype(BF16), b.astype(BF16), preferred_element_type=F32)


def _gdn_pre_body(z_ref, sc_ref, st_ref, cw_ref, al_r_ref, dt_r_ref, al_c_ref, dt_c_ref,
                  u_ref, w_ref, qd_ref, kd_ref, at_ref, eg_ref, halo_ref):
    rows = z_ref.shape[0]
    hw = N_HEADS * HEAD_DIM
    x = _silu(_short_conv(z_ref[...].astype(F32), halo_ref, cw_ref, pl.program_id(1) == 0))

    sc = sc_ref[...]
    g_cols = -jnp.exp(al_r_ref[...]) * _softplus(sc + dt_r_ref[...])
    beta_cols = _sigmoid(sc)
    st = st_ref[...]
    g_rows = -jnp.exp(al_c_ref[:, 0:1]) * _softplus(st + dt_c_ref[:, 0:1])

    ri = lax.broadcasted_iota(I32, (rows, rows), 0)
    ci = lax.broadcasted_iota(I32, (rows, rows), 1)
    same64 = (ri >> 6) == (ci >> 6)
    same32 = (ri >> 5) == (ci >> 5)
    same16 = (ri >> 4) == (ci >> 4)
    lower = ri >= ci
    strict = ri > ci
    lane = lax.broadcasted_iota(I32, (rows, LANES), 1)
    eg_slab = jnp.zeros((rows, LANES), F32)
    scale = HEAD_DIM ** -0.5

    heads = range(N_HEADS)
    in_chunk_lower = same64 & lower
    in_chunk_upper = same64 & (ri <= ci)
    a_mats, nmats, rhss = [], [], []
    for h in heads:
        sl = slice(h * HEAD_DIM, (h + 1) * HEAD_DIM)
        q = x[:, sl]
        k = x[:, hw + h * HEAD_DIM:hw + (h + 1) * HEAD_DIM]
        v = x[:, 2 * hw + h * HEAD_DIM:2 * hw + (h + 1) * HEAD_DIM]
        q = q * lax.rsqrt(jnp.sum(q * q, axis=-1, keepdims=True) + EPS)
        k = k * lax.rsqrt(jnp.sum(k * k, axis=-1, keepdims=True) + EPS)
        g_col = g_cols[:, h:h + 1]
        beta = beta_cols[:, N_HEADS + h:N_HEADS + h + 1]
        g_row = g_rows[h:h + 1, :]

        gc_col = jnp.sum(jnp.where(in_chunk_lower, g_row, 0.0), axis=1, keepdims=True)
        gc_row = jnp.sum(jnp.where(in_chunk_upper, g_col, 0.0), axis=0, keepdims=True)
        gtot_col = jnp.sum(jnp.where(same64, g_row, 0.0), axis=1, keepdims=True)
        decay = jnp.exp(jnp.where(in_chunk_lower, gc_col - gc_row, NEG))
        eg = jnp.exp(gc_col)
        ekd = jnp.exp(gtot_col - gc_col)

        kb = k * beta
        kbf = k.astype(BF16)
        kk = lax.dot_general(kb.astype(BF16), kbf, (((1,), (1,)), ((), ())), preferred_element_type=F32)
        qs = q * scale
        qk = lax.dot_general(qs.astype(BF16), kbf, (((1,), (1,)), ((), ())), preferred_element_type=F32)
        attn = qk * decay
        a_mats.append(jnp.where(strict, kk * decay, 0.0))
        rhss.append(jnp.concatenate([v * beta, kb * eg], axis=1))
        qd_ref[:, sl] = (qs * eg).astype(BF16)
        kd_ref[:, sl] = (k * ekd).astype(BF16)
        for c in range(rows // CHUNK):
            cs = slice(c * CHUNK, (c + 1) * CHUNK)
            at_ref[h, cs, :] = attn[cs, cs].astype(BF16)
        eg_slab = jnp.where(lane == h, eg, eg_slab)
    eg_ref[...] = eg_slab

    nmats = [jnp.where(same16, -a, 0.0) for a in a_mats]
    qqs = nmats
    for _ in range(3):
        qqs = [_bdot(qq, qq) for qq in qqs]
        prods = [_bdot(qq, nm) for qq, nm in zip(qqs, nmats)]
        nmats = [nm + qq + pr for nm, qq, pr in zip(nmats, qqs, prods)]
    for level_mask, inner_mask in ((same32, same16), (same64, same32)):
        sel = level_mask & jnp.logical_not(inner_mask)
        offs = [jnp.where(sel, a, 0.0) for a in a_mats]
        bmats = [off + _bdot(nm, off) for nm, off in zip(nmats, offs)]
        prods = [_bdot(bm, nm) for bm, nm in zip(bmats, nmats)]
        nmats = [nm - bm - pr for nm, bm, pr in zip(nmats, bmats, prods)]

    for h in heads:
        sl = slice(h * HEAD_DIM, (h + 1) * HEAD_DIM)
        uw = rhss[h] + _bdot(nmats[h], rhss[h])
        u_ref[:, sl] = uw[:, 0:HEAD_DIM]
        w_ref[:, sl] = uw[:, HEAD_DIM:2 * HEAD_DIM].astype(BF16)


def _gdn_pre(z, sc, st, conv_w, a_log, dt_bias, bsz, tp):
    n = z.shape[0]
    hw = N_HEADS * HEAD_DIM
    rows = UNIT
    nu = tp // rows
    pad_r = lambda v: jnp.pad(v.astype(F32), (0, LANES - N_HEADS)).reshape(1, LANES)
    pad_c = lambda v: jnp.broadcast_to(jnp.pad(v.astype(F32), (0, 16 - N_HEADS))[:, None], (16, LANES))
    outs = pl.pallas_call(
        _gdn_pre_body,
        out_shape=[jax.ShapeDtypeStruct((n, hw), F32),
                   jax.ShapeDtypeStruct((n, hw), BF16),
                   jax.ShapeDtypeStruct((n, hw), BF16),
                   jax.ShapeDtypeStruct((n, hw), BF16),
                   jax.ShapeDtypeStruct((N_HEADS, n, CHUNK), BF16),
                   jax.ShapeDtypeStruct((n, LANES), F32)],
        grid=(bsz, nu),
        in_specs=[pl.BlockSpec((rows, 3 * hw), lambda b, i: (b * nu + i, 0)),
                  pl.BlockSpec((rows, LANES), lambda b, i: (b * nu + i, 0)),
                  pl.BlockSpec((16, rows), lambda b, i: (0, b * nu + i)),
                  pl.BlockSpec((8, 3 * hw), lambda b, i: (0, 0)),
                  pl.BlockSpec((1, LANES), lambda b, i: (0, 0)),
                  pl.BlockSpec((1, LANES), lambda b, i: (0, 0)),
                  pl.BlockSpec((16, LANES), lambda b, i: (0, 0)),
                  pl.BlockSpec((16, LANES), lambda b, i: (0, 0))],
        out_specs=[pl.BlockSpec((rows, hw), lambda b, i: (b * nu + i, 0)),
                   pl.BlockSpec((rows, hw), lambda b, i: (b * nu + i, 0)),
                   pl.BlockSpec((rows, hw), lambda b, i: (b * nu + i, 0)),
                   pl.BlockSpec((rows, hw), lambda b, i: (b * nu + i, 0)),
                   pl.BlockSpec((N_HEADS, rows, CHUNK), lambda b, i: (0, b * nu + i, 0)),
                   pl.BlockSpec((rows, LANES), lambda b, i: (b * nu + i, 0))],
        scratch_shapes=[pltpu.VMEM((8, 3 * hw), F32)],
        compiler_params=_cparams(("arbitrary", "arbitrary")),
        name="gdn_pre",
    )(z, sc, st, jnp.pad(conv_w.astype(F32), ((0, 8 - SHORT_K), (0, 0))),
      pad_r(a_log), pad_r(dt_bias), pad_c(a_log), pad_c(dt_bias))
    return outs


def _gdn_scan_body(u_ref, w_ref, qd_ref, kd_ref, at_ref, eg_ref, z_ref, gn_ref, o_ref, s_ref, *, bsz):
    @pl.when(pl.program_id(0) == 0)
    def _():
        s_ref[...] = jnp.zeros_like(s_ref)

    for b in range(bsz):
        for h in range(N_HEADS):
            sl = slice(h * HEAD_DIM, (h + 1) * HEAD_DIM)
            s = s_ref[b * N_HEADS + h]
            wq = jnp.concatenate([w_ref[b, :, sl], qd_ref[b, :, sl]], axis=0)
            r = jnp.dot(wq, s.astype(BF16), preferred_element_type=F32)
            v_new = u_ref[b, :, sl] - r[0:CHUNK]
            vb = v_new.astype(BF16)
            o = r[CHUNK:2 * CHUNK] + jnp.dot(at_ref[h, b], vb, preferred_element_type=F32)
            gt = eg_ref[b, CHUNK - 1:CHUNK, h:h + 1]
            s_ref[b * N_HEADS + h] = s * gt + lax.dot_general(
                kd_ref[b, :, sl], vb, (((0,), (0,)), ((), ())), preferred_element_type=F32)
            on = o * lax.rsqrt(jnp.mean(o * o, axis=-1, keepdims=True) + EPS) * gn_ref[...]
            o_ref[b, :, sl] = (on * _silu(z_ref[b, :, sl].astype(F32))).astype(o_ref.dtype)


def _gdn_scan(u, w, qd, kd, attn, eg, z, out_norm_g, bsz, tp, z_col):
    hw = N_HEADS * HEAD_DIM
    nc = tp // CHUNK
    v3 = lambda a: a.reshape(bsz, tp, a.shape[-1])
    blk3 = pl.BlockSpec((bsz, CHUNK, hw), lambda c: (0, c, 0))
    return pl.pallas_call(
        functools.partial(_gdn_scan_body, bsz=bsz),
        out_shape=jax.ShapeDtypeStruct((bsz, tp, hw), BF16),
        grid=(nc,),
        in_specs=[blk3, blk3, blk3, blk3,
                  pl.BlockSpec((N_HEADS, bsz, CHUNK, CHUNK), lambda c: (0, 0, c, 0)),
                  pl.BlockSpec((bsz, CHUNK, LANES), lambda c: (0, c, 0)),
                  pl.BlockSpec((bsz, CHUNK, hw), lambda c: (0, c, z_col)),
                  pl.BlockSpec((1, HEAD_DIM), lambda c: (0, 0))],
        out_specs=blk3,
        scratch_shapes=[pltpu.VMEM((bsz * N_HEADS, HEAD_DIM, HEAD_DIM), F32)],
        compiler_params=_cparams(("arbitrary",)),
        name="gdn_scan",
    )(v3(u), v3(w), v3(qd), v3(kd), attn.reshape(N_HEADS, bsz, tp, CHUNK), v3(eg), v3(z),
      out_norm_g.astype(F32).reshape(1, HEAD_DIM)).reshape(bsz * tp, hw)


def _lru_body(x_ref, gate_ref, cw_ref, cb_ref, wr_ref, br_ref, wi_ref, bi_ref, lam_ref, o_ref,
              halo_ref, hc_ref, *, bt, n_real):
    first = pl.program_id(1) == 0

    @pl.when(first)
    def _():
        hc_ref[...] = jnp.zeros_like(hc_ref)

    x = _short_conv(x_ref[...].astype(F32), halo_ref, cw_ref, first) + cb_ref[...]
    nblk = wr_ref.shape[0]
    bd = wr_ref.shape[1]
    xb = x.astype(BF16)
    rg = jnp.concatenate([jnp.dot(xb[:, n * bd:(n + 1) * bd], wr_ref[n], preferred_element_type=F32)
                          for n in range(nblk)], axis=1)
    ig = jnp.concatenate([jnp.dot(xb[:, n * bd:(n + 1) * bd], wi_ref[n], preferred_element_type=F32)
                          for n in range(nblk)], axis=1)
    r = _sigmoid(rg + br_ref[...])
    ig = _sigmoid(ig + bi_ref[...])
    log_a = (-LRU_C * _softplus(-lam_ref[...])) * r
    a = jnp.exp(log_a)
    th = jnp.tanh(log_a)
    b = jnp.sqrt(-2.0 * th / (1.0 - th)) * (ig * x)
    b = jnp.where(_row_valid(pl.program_id(1) * bt, bt, n_real), b, 0.0)

    row = lax.broadcasted_iota(I32, (bt, 1), 0)
    d = 1
    while d < bt:
        keep = row >= d
        a_sh = jnp.where(keep, pltpu.roll(a, d, 0), 1.0)
        b_sh = jnp.where(keep, pltpu.roll(b, d, 0), 0.0)
        b = a * b_sh + b
        a = a * a_sh
        d *= 2
    hs = a * hc_ref[0:1, :] + b
    hc_ref[...] = jnp.broadcast_to(hs[bt - 1:bt, :], hc_ref.shape)
    o_ref[...] = (hs * jax.nn.gelu(gate_ref[...].astype(F32))).astype(o_ref.dtype)


def _lru(z, conv_w, conv_b, w_rg, b_rg, w_ig, b_ig, lam, bsz, tp, n_real, x_col, gate_col):
    n = z.shape[0]
    c = conv_w.shape[1]
    bt = UNIT
    nt = tp // bt
    row = lambda v: v.astype(F32).reshape(1, c)
    wspec = pl.BlockSpec(w_rg.shape, lambda b, i: (0, 0, 0))
    vspec = pl.BlockSpec((1, c), lambda b, i: (0, 0))
    return pl.pallas_call(
        functools.partial(_lru_body, bt=bt, n_real=n_real),
        out_shape=jax.ShapeDtypeStruct((n, c), BF16),
        grid=(bsz, nt),
        in_specs=[pl.BlockSpec((bt, c), lambda b, i: (b * nt + i, x_col)),
                  pl.BlockSpec((bt, c), lambda b, i: (b * nt + i, gate_col)),
                  pl.BlockSpec((8, c), lambda b, i: (0, 0)),
                  vspec, wspec, vspec, wspec, vspec, vspec],
        out_specs=pl.BlockSpec((bt, c), lambda b, i: (b * nt + i, 0)),
        scratch_shapes=[pltpu.VMEM((8, c), F32), pltpu.VMEM((8, c), F32)],
        compiler_params=_cparams(("arbitrary", "arbitrary")),
        name="rg_lru",
    )(z, z, jnp.pad(conv_w.astype(F32), ((0, 8 - SHORT_K), (0, 0))), row(conv_b),
      w_rg.astype(BF16), row(b_rg), w_ig.astype(BF16), row(b_ig), row(lam))


def _pick(n, candidates):
    for c in candidates:
        if n % c == 0:
            return c
    raise ValueError(f"no block size in {candidates} divides {n}")


def _pack_weights_body(a_ref, b_ref, o_ref, *, scaled_tiles, scale, head_tiles, gap):
    j = pl.program_id(0)
    tile = o_ref.shape[1]

    @pl.when(j < scaled_tiles)
    def _():
        o_ref[...] = (a_ref[...] * scale).astype(BF16)

    @pl.when((j >= scaled_tiles) & (j < head_tiles))
    def _():
        o_ref[...] = a_ref[...].astype(BF16)

    @pl.when(j >= head_tiles)
    def _():
        x = jnp.concatenate([a_ref[...], b_ref[...]], axis=1)
        o_ref[...] = x[:, gap:gap + tile].astype(BF16)


def _pack_inproj_weights(w_in, head, gap, scaled=0, scale=1.0):
    d, total = w_in.shape
    tile = 512
    n_out = total - gap
    return pl.pallas_call(
        functools.partial(_pack_weights_body, scaled_tiles=scaled // tile, scale=scale,
                          head_tiles=head // tile, gap=gap),
        out_shape=jax.ShapeDtypeStruct((d, n_out), BF16),
        grid=(n_out // tile,),
        in_specs=[pl.BlockSpec((d, tile), lambda j: (0, j)),
                  pl.BlockSpec((d, LANES), lambda j: (0, (tile // LANES) * (j + 1)))],
        out_specs=pl.BlockSpec((d, tile), lambda j: (0, j)),
        compiler_params=_cparams(("arbitrary",)),
        name="pack_inproj_weights",
    )(w_in, w_in)


def _small_weights(cols):
    k = cols.shape[1]
    return (jnp.pad(cols.T, ((0, 16 - k), (0, 0))).astype(BF16),
            jnp.pad(cols, ((0, 0), (0, LANES - k))).astype(BF16))


def _router_weights(w_group, b_group, w_expert, b_expert):
    w = jnp.concatenate([w_expert, w_group], axis=1).astype(F32)
    b = jnp.concatenate([b_expert, b_group]).astype(F32)
    k = w.shape[1]
    w = jnp.pad(w, ((0, 0), (0, LANES - k)))
    w_hi = w.astype(BF16)
    w_lo = (w - w_hi.astype(F32)).astype(BF16)
    return jnp.concatenate([w_hi, w_lo], axis=1), jnp.pad(b, (0, LANES - k)).reshape(1, LANES)


def kernel(x, meta_tokens, norm_mix_g, norm_ffn_g, norm_final_g, ab_w_in, ab_forget_b, ab_conv_w, ab_conv_b, ab_ln_g, ab_ln_b, ab_w_out, cd_w_in, cd_qkv_conv_w, cd_a_log, cd_dt_bias, cd_out_norm_g, cd_lru_conv_w, cd_lru_conv_b, cd_w_rg, cd_b_rg, cd_w_ig, cd_b_ig, cd_lru_lambda, cd_w_out, moe_w_group, moe_b_group, moe_w_expert, moe_b_expert, moe_w_gate, moe_w_up, moe_w_down):
    bsz, seq, d = x.shape
    depth = norm_mix_g.shape[0]
    n_real = N_META + seq
    tp = -(-(OFF + seq) // UNIT) * UNIT
    n = bsz * tp
    hw = N_HEADS * HEAD_DIM
    bm = _pick(n, (768, 512, 256))
    bn = 512

    meta = jnp.broadcast_to(meta_tokens[None].astype(F32), (bsz, N_META, d))
    h = jnp.concatenate([jnp.zeros((bsz, FRONT, d), F32), meta, x.astype(F32),
                         jnp.zeros((bsz, tp - OFF - seq, d), F32)], axis=1).reshape(n, d)
    row = lambda v: v.astype(F32).reshape(1, -1)

    for layer in range(depth):
        i = layer // 2
        g_mix = row(norm_mix_g[layer])
        if layer % 2 == 0:
            w_in = ab_w_in[i]
            w_main = _pack_inproj_weights(w_in.astype(F32), 3 * hw, N_HEADS, scaled=hw,
                                          scale=LOG2E * HEAD_DIM ** -0.5)
            wst, wsc = _small_weights(w_in[:, 3 * hw:3 * hw + N_HEADS])
            z, _, f_slab = _norm_inproj(h, g_mix, w_main, wst, wsc, bm, bn)
            kx, vx = _fox_prep(z, f_slab, ab_forget_b[i], bsz, tp, n_real)
            ya = _fox_attn(z, kx, vx, bsz, tp)
            yb = _conformer(z, ab_conv_w[i], ab_conv_b[i], ab_ln_g[i], ab_ln_b[i], bsz, tp, 3, 4)
            w_out = ab_w_out[i].astype(BF16)
        else:
            w_in = cd_w_in[i]
            w_main = _pack_inproj_weights(w_in.astype(F32), 3 * hw, 2 * N_HEADS)
            wst, wsc = _small_weights(w_in[:, 3 * hw:3 * hw + 2 * N_HEADS])
            z, zt, zc = _norm_inproj(h, g_mix, w_main, wst, wsc, bm, bn)
            u, w, qd, kd, attn, eg = _gdn_pre(z, zc, zt, cd_qkv_conv_w[i], cd_a_log[i], cd_dt_bias[i],
                                              bsz, tp)
            ya = _gdn_scan(u, w, qd, kd, attn, eg, z, cd_out_norm_g[i], bsz, tp, 3)
            yb = _lru(z, cd_lru_conv_w[i], cd_lru_conv_b[i], cd_w_rg[i], cd_b_rg[i], cd_w_ig[i],
                      cd_b_ig[i], cd_lru_lambda[i], bsz, tp, n_real, 4, 5)
            w_out = cd_w_out[i].astype(BF16)
        g_ffn = row(norm_ffn_g[layer])
        w_r, b_r = _router_weights(moe_w_group[layer], moe_b_group[layer],
                                   moe_w_expert[layer], moe_b_expert[layer])
        h, route, counts = _outproj_router(ya, yb, h, w_out, g_ffn, w_r, b_r, tp, n_real)
        h = _moe(h, route, counts, g_ffn, moe_w_gate, moe_w_up, moe_w_down, layer,
                 row(norm_final_g), tp, n_real, final_norm=(layer == depth - 1))
    return h.reshape(bsz, tp, d)[:, OFF:OFF + seq].astype(x.dtype)
```

```python
import functools

import jax
import jax.numpy as jnp
from jax import lax
from jax.experimental import pallas as pl
from jax.experimental.pallas import tpu as pltpu

F32 = jnp.float32
BF16 = jnp.bfloat16
I32 = jnp.int32
U32 = jnp.uint32

EPS = 1e-6
N_META = 16
CHUNK = 64
FRONT = CHUNK - N_META
OFF = FRONT + N_META
LANES = 128
UNIT = 256
HEAD_DIM = 128
N_HEADS = 8
CONF_K = 31
SHORT_K = 4
LRU_C = 8.0
N_GROUPS = 4
EXPERTS_PER_GROUP = 8
N_EXPERTS = N_GROUPS * EXPERTS_PER_GROUP
MOE_BM = 256
NEG = -1e30
LOG2E = 1.4426950408889634
VMEM_LIMIT = 56 * 1024 * 1024


def _cparams(sem):
    return pltpu.CompilerParams(dimension_semantics=sem, vmem_limit_bytes=VMEM_LIMIT)


def _row_valid(pos0, rows, n_real):
    pos = pos0 + lax.broadcasted_iota(I32, (rows, 1), 0)
    return (pos >= FRONT) & (pos < FRONT + n_real)


def _sigmoid(x):
    return 1.0 / (1.0 + jnp.exp(-x))


def _softplus(x):
    return jnp.maximum(x, 0.0) + jnp.log1p(jnp.exp(-jnp.abs(x)))


def _silu(x):
    return x * _sigmoid(x)


def _norm_inproj_body(h_ref, g_ref, w_ref, wst_ref, wsc_ref, o_ref, ot_ref, oc_ref, xn_ref):
    @pl.when(pl.program_id(1) == 0)
    def _():
        x = h_ref[...]
        ms = jnp.mean(x * x, axis=-1, keepdims=True)
        xn = (x * lax.rsqrt(ms + EPS) * g_ref[...]).astype(BF16)
        xn_ref[...] = xn
        ot_ref[...] = lax.dot_general(wst_ref[...], xn, (((1,), (1,)), ((), ())),
                                      preferred_element_type=F32)
        oc_ref[...] = jnp.dot(xn, wsc_ref[...], preferred_element_type=F32)

    o_ref[...] = jnp.dot(xn_ref[...], w_ref[...], preferred_element_type=F32).astype(o_ref.dtype)


def _norm_inproj(h, g, w_main, w_small_t, w_small_c, bm, bn):
    n, d = h.shape
    nw = w_main.shape[1]
    return pl.pallas_call(
        _norm_inproj_body,
        out_shape=[jax.ShapeDtypeStruct((n, nw), BF16),
                   jax.ShapeDtypeStruct((16, n), F32),
                   jax.ShapeDtypeStruct((n, LANES), F32)],
        grid=(n // bm, nw // bn),
        in_specs=[pl.BlockSpec((bm, d), lambda i, j: (i, 0)),
                  pl.BlockSpec((1, d), lambda i, j: (0, 0)),
                  pl.BlockSpec((d, bn), lambda i, j: (0, j)),
                  pl.BlockSpec((16, d), lambda i, j: (0, 0)),
                  pl.BlockSpec((d, LANES), lambda i, j: (0, 0))],
        out_specs=[pl.BlockSpec((bm, bn), lambda i, j: (i, j)),
                   pl.BlockSpec((16, bm), lambda i, j: (0, i)),
                   pl.BlockSpec((bm, LANES), lambda i, j: (i, 0))],
        scratch_shapes=[pltpu.VMEM((bm, d), BF16)],
        compiler_params=_cparams(("arbitrary", "arbitrary")),
        name="norm_inproj",
    )(h, g, w_main, w_small_t, w_small_c)


def _split3(x):
    p1 = x.astype(BF16)
    r1 = x - p1.astype(F32)
    p2 = r1.astype(BF16)
    p3 = (r1 - p2.astype(F32)).astype(BF16)
    return p1, p2, p3


def _fox_prep_body(k_ref, v_ref, f_ref, fb_ref, kx_ref, vx_ref, carry_ref, *, rows, n_real):
    i = pl.program_id(1)

    @pl.when(i == 0)
    def _():
        carry_ref[...] = jnp.zeros_like(carry_ref)

    f = f_ref[...] + fb_ref[...]
    lf = jnp.minimum(f, 0.0) - jnp.log1p(jnp.exp(-jnp.abs(f)))
    pos = i * rows + lax.broadcasted_iota(I32, (rows, 1), 0)
    lf = jnp.where((pos >= FRONT) & (pos < FRONT + n_real), lf, 0.0)
    rr = lax.broadcasted_iota(I32, (rows, rows), 0)
    cc = lax.broadcasted_iota(I32, (rows, rows), 1)
    tri = jnp.where(rr >= cc, 1.0, 0.0).astype(BF16)
    c = carry_ref[0:1, :]
    for piece in _split3(lf):
        c = c + jnp.dot(tri, piece, preferred_element_type=F32)
    carry_ref[...] = jnp.broadcast_to(c[rows - 1:rows, :], carry_ref.shape)
    cs = jnp.where(pos < FRONT, NEG, -LOG2E * c)
    p1, p2, p3 = (p.astype(F32) for p in _split3(cs))
    lane = lax.broadcasted_iota(I32, (rows, LANES), 1)
    ones_col = jnp.where(lane == 0, 1.0, 0.0).astype(BF16)
    for h in range(N_HEADS):
        ext = jnp.where(lane == 0, p1[:, h:h + 1],
                        jnp.where(lane == 1, p2[:, h:h + 1], jnp.where(lane == 2, p3[:, h:h + 1], 0.0)))
        kx_ref[:, (2 * h) * HEAD_DIM:(2 * h + 1) * HEAD_DIM] = k_ref[:, h * HEAD_DIM:(h + 1) * HEAD_DIM]
        kx_ref[:, (2 * h + 1) * HEAD_DIM:(2 * h + 2) * HEAD_DIM] = ext.astype(BF16)
        vx_ref[:, (2 * h) * HEAD_DIM:(2 * h + 1) * HEAD_DIM] = v_ref[:, h * HEAD_DIM:(h + 1) * HEAD_DIM]
        vx_ref[:, (2 * h + 1) * HEAD_DIM:(2 * h + 2) * HEAD_DIM] = ones_col


def _fox_prep(z, f_slab, forget_b, bsz, tp, n_real):
    n = z.shape[0]
    hw = N_HEADS * HEAD_DIM
    rows = UNIT
    nt = tp // rows
    fb = jnp.pad(forget_b.astype(F32), (0, LANES - N_HEADS)).reshape(1, LANES)
    return pl.pallas_call(
        functools.partial(_fox_prep_body, rows=rows, n_real=n_real),
        out_shape=[jax.ShapeDtypeStruct((n, 2 * hw), BF16), jax.ShapeDtypeStruct((n, 2 * hw), BF16)],
        grid=(bsz, nt),
        in_specs=[pl.BlockSpec((rows, hw), lambda b, i: (b * nt + i, 1)),
                  pl.BlockSpec((rows, hw), lambda b, i: (b * nt + i, 2)),
                  pl.BlockSpec((rows, LANES), lambda b, i: (b * nt + i, 0)),
                  pl.BlockSpec((1, LANES), lambda b, i: (0, 0))],
        out_specs=[pl.BlockSpec((rows, 2 * hw), lambda b, i: (b * nt + i, 0)),
                   pl.BlockSpec((rows, 2 * hw), lambda b, i: (b * nt + i, 0))],
        scratch_shapes=[pltpu.VMEM((8, LANES), F32)],
        compiler_params=_cparams(("arbitrary", "arbitrary")),
        name="fox_prep",
    )(z, z, f_slab, fb)


def _fox_attn_body(q_ref, k_ref, v_ref, o_ref, *, blk, n_split):
    qi = pl.program_id(2)
    half = blk // n_split
    lane = lax.broadcasted_iota(I32, (blk, HEAD_DIM), 1)
    q = jnp.concatenate([q_ref[...], jnp.where(lane < 3, 1.0, 0.0).astype(BF16)], axis=1)
    qs = tuple(q[r * half:(r + 1) * half] for r in range(n_split))

    def step(js, carry, diagonal):
        starts = [pl.multiple_of(j * blk, blk) for j in js]
        ks = [k_ref[pl.ds(st, blk), :] for st in starts]
        vs = [v_ref[pl.ds(st, blk), :] for st in starts]
        ss = [[lax.dot_general(qh, k, (((1,), (1,)), ((), ())), preferred_element_type=F32) for k in ks]
              for qh in qs]
        out = []
        for r in range(n_split):
            m = carry[2 * r]
            srow = ss[r]
            if diagonal:
                row = r * half + lax.broadcasted_iota(I32, (half, blk), 0)
                col = lax.broadcasted_iota(I32, (half, blk), 1)
                srow = [jnp.where(col <= row, s, NEG) for s in srow]
            m_new = m
            for s in srow:
                m_new = jnp.maximum(m_new, jnp.max(s, axis=1, keepdims=True))
            acc = jnp.exp2(m - m_new) * carry[2 * r + 1]
            for s, v in zip(srow, vs):
                acc = acc + jnp.dot(jnp.exp2(s - m_new).astype(BF16), v, preferred_element_type=F32)
            out += [m_new, acc]
        return tuple(out)

    init = (jnp.full((half, 1), NEG, F32), jnp.zeros((half, 2 * HEAD_DIM), F32)) * n_split
    carry = lax.fori_loop(0, qi // 2, lambda j, c: step((2 * j, 2 * j + 1), c, False), init)
    carry = lax.cond(qi % 2 == 1, lambda c: step((qi - 1,), c, False), lambda c: c, carry)
    res = step((qi,), carry, True)
    for r in range(n_split):
        acc = res[2 * r + 1]
        o_ref[r * half:(r + 1) * half, :] = (acc[:, 0:HEAD_DIM] / acc[:, HEAD_DIM:HEAD_DIM + 1]
                                             ).astype(o_ref.dtype)


def _fox_attn(z, kx, vx, bsz, tp):
    n = z.shape[0]
    blk = _pick(tp, (768, 512, 256))
    nq = tp // blk
    return pl.pallas_call(
        functools.partial(_fox_attn_body, blk=blk, n_split=2),
        out_shape=jax.ShapeDtypeStruct((n, N_HEADS * HEAD_DIM), BF16),
        grid=(bsz, N_HEADS, nq),
        in_specs=[pl.BlockSpec((blk, HEAD_DIM), lambda b, h, i: (b * nq + i, h)),
                  pl.BlockSpec((tp, 2 * HEAD_DIM), lambda b, h, i: (b, h)),
                  pl.BlockSpec((tp, 2 * HEAD_DIM), lambda b, h, i: (b, h))],
        out_specs=pl.BlockSpec((blk, HEAD_DIM), lambda b, h, i: (b * nq + i, h)),
        compiler_params=_cparams(("arbitrary", "arbitrary", "arbitrary")),
        name="fox_attn",
    )(z, kx, vx)


def _shifted_taps(x_ext, halo, rows, n_taps):
    for r in range(min(8, n_taps)):
        rolled = x_ext if r == 0 else pltpu.roll(x_ext, r, 0)
        for q in range(halo // 8):
            s = 8 * q + r
            if s > n_taps - 1:
                continue
            yield n_taps - 1 - s, rolled[halo - 8 * q:halo - 8 * q + rows]


def _conformer_body(a_ref, b_ref, w_ref, cb_ref, lg_ref, lb_ref, o_ref, halo_ref, *, bt):
    halo = halo_ref.shape[0]

    @pl.when(pl.program_id(1) == 0)
    def _():
        halo_ref[...] = jnp.zeros_like(halo_ref)

    u = a_ref[...].astype(F32) * _sigmoid(b_ref[...].astype(F32))
    x_ext = jnp.concatenate([halo_ref[...], u], axis=0)
    halo_ref[...] = u[bt - halo:bt]
    acc = jnp.zeros_like(u)
    for j, tap in _shifted_taps(x_ext, halo, bt, CONF_K):
        acc = acc + w_ref[j:j + 1, :] * tap
    y = acc + cb_ref[...]
    mu = jnp.mean(y, axis=-1, keepdims=True)
    yc = y - mu
    var = jnp.mean(yc * yc, axis=-1, keepdims=True)
    yn = yc * lax.rsqrt(var + EPS) * lg_ref[...] + lb_ref[...]
    o_ref[...] = _silu(yn).astype(o_ref.dtype)


def _conformer(z, conv_w, conv_b, ln_g, ln_b, bsz, tp, col_a, col_b):
    n = z.shape[0]
    c = conv_w.shape[1]
    bt = UNIT
    nt = tp // bt
    w = jnp.pad(conv_w.astype(F32), ((0, 32 - CONF_K), (0, 0)))
    row = lambda v: v.astype(F32).reshape(1, c)
    return pl.pallas_call(
        functools.partial(_conformer_body, bt=bt),
        out_shape=jax.ShapeDtypeStruct((n, c), BF16),
        grid=(bsz, nt),
        in_specs=[pl.BlockSpec((bt, c), lambda b, i: (b * nt + i, col_a)),
                  pl.BlockSpec((bt, c), lambda b, i: (b * nt + i, col_b)),
                  pl.BlockSpec((32, c), lambda b, i: (0, 0)),
                  pl.BlockSpec((1, c), lambda b, i: (0, 0)),
                  pl.BlockSpec((1, c), lambda b, i: (0, 0)),
                  pl.BlockSpec((1, c), lambda b, i: (0, 0))],
        out_specs=pl.BlockSpec((bt, c), lambda b, i: (b * nt + i, 0)),
        scratch_shapes=[pltpu.VMEM((32, c), F32)],
        compiler_params=_cparams(("arbitrary", "arbitrary")),
        name="conformer_conv",
    )(z, z, w, row(conv_b), row(ln_g), row(ln_b))


def _outproj_router_body(ya_ref, yb_ref, h_ref, w_ref, g_ref, wr_ref, br_ref, ho_ref, r_ref, cnt_ref,
                         *, bm, blocks_per_seq, n_real):
    half = ya_ref.shape[1]
    y = jnp.dot(ya_ref[...], w_ref[0:half, :], preferred_element_type=F32)
    y = y + jnp.dot(yb_ref[...], w_ref[half:2 * half, :], preferred_element_type=F32)
    pos0 = (pl.program_id(0) % blocks_per_seq) * bm
    hn = jnp.where(_row_valid(pos0, bm, n_real), h_ref[...] + y, 0.0)
    ho_ref[...] = hn

    ms = jnp.mean(hn * hn, axis=-1, keepdims=True)
    xn = hn * lax.rsqrt(ms + EPS) * g_ref[...]
    xh = xn.astype(BF16)
    xl = (xn - xh.astype(F32)).astype(BF16)
    t = jnp.dot(xh, wr_ref[...], preferred_element_type=F32)
    lg = (t[:, 0:LANES] + t[:, LANES:2 * LANES]
          + jnp.dot(xl, wr_ref[:, 0:LANES], preferred_element_type=F32) + br_ref[...])
    lane = lax.broadcasted_iota(I32, lg.shape, 1)
    lane_f = lane.astype(F32)
    big = float(LANES)

    is_grp = (lane >= N_EXPERTS) & (lane < N_EXPERTS + N_GROUPS)
    gl = jnp.where(is_grp, lg, -jnp.inf)
    gmax = jnp.max(gl, axis=1, keepdims=True)
    gidx = jnp.min(jnp.where(gl == gmax, lane_f, big), axis=1, keepdims=True) - N_EXPERTS
    g_prob = 1.0 / jnp.sum(jnp.where(is_grp, jnp.exp(lg - gmax), 0.0), axis=1, keepdims=True)

    lo = gidx * EXPERTS_PER_GROUP
    in_grp = (lane_f >= lo) & (lane_f < lo + EXPERTS_PER_GROUP)
    el = jnp.where(in_grp, lg, -jnp.inf)
    m1 = jnp.max(el, axis=1, keepdims=True)
    i1 = jnp.min(jnp.where(el == m1, lane_f, big), axis=1, keepdims=True)
    el2 = jnp.where(lane_f == i1, -jnp.inf, el)
    m2 = jnp.max(el2, axis=1, keepdims=True)
    i2 = jnp.min(jnp.where(el2 == m2, lane_f, big), axis=1, keepdims=True)
    e2 = jnp.exp(m2 - m1)
    w1 = g_prob / (1.0 + e2)
    w2 = g_prob * e2 / (1.0 + e2)

    @pl.when(pl.program_id(0) == 0)
    def _():
        cnt_ref[...] = jnp.zeros_like(cnt_ref)

    oh1 = lane_f == i1
    oh2 = lane_f == i2
    onehots = jnp.concatenate([jnp.where(oh1, 1.0, 0.0), jnp.where(oh2, 1.0, 0.0)], axis=1).astype(BF16)
    rr = lax.broadcasted_iota(I32, (bm, bm), 0)
    cc = lax.broadcasted_iota(I32, (bm, bm), 1)
    tri = jnp.where(rr >= cc, 1.0, 0.0).astype(BF16)
    csum = jnp.dot(tri, onehots, preferred_element_type=F32)
    c1 = csum[:, 0:LANES]
    c2 = csum[:, LANES:2 * LANES]
    tot1 = c1[bm - 1:bm, :]
    tot2 = c2[bm - 1:bm, :]
    before = cnt_ref[0:1, :]
    rank1 = jnp.sum(jnp.where(oh1, before + c1 - 1.0, 0.0), axis=1, keepdims=True)
    rank2 = jnp.sum(jnp.where(oh2, before + tot1 + c2 - 1.0, 0.0), axis=1, keepdims=True)
    cnt_ref[...] = jnp.broadcast_to(before + tot1 + tot2, cnt_ref.shape)

    r_ref[...] = jnp.where(lane == 0, i1, jnp.where(lane == 1, i2,
                           jnp.where(lane == 2, w1, jnp.where(lane == 3, w2,
                                     jnp.where(lane == 4, rank1, jnp.where(lane == 5, rank2, 0.0))))))


def _outproj_router(ya, yb, h, w_out, g_ffn, w_router, b_router, tp, n_real):
    n, d = h.shape
    half = ya.shape[1]
    bm = UNIT
    return pl.pallas_call(
        functools.partial(_outproj_router_body, bm=bm, blocks_per_seq=tp // bm, n_real=n_real),
        out_shape=[jax.ShapeDtypeStruct((n, d), F32), jax.ShapeDtypeStruct((n, LANES), F32),
                   jax.ShapeDtypeStruct((8, LANES), F32)],
        grid=(n // bm,),
        in_specs=[pl.BlockSpec((bm, half), lambda i: (i, 0)),
                  pl.BlockSpec((bm, half), lambda i: (i, 0)),
                  pl.BlockSpec((bm, d), lambda i: (i, 0)),
                  pl.BlockSpec((2 * half, d), lambda i: (0, 0)),
                  pl.BlockSpec((1, d), lambda i: (0, 0)),
                  pl.BlockSpec((d, 2 * LANES), lambda i: (0, 0)),
                  pl.BlockSpec((1, LANES), lambda i: (0, 0))],
        out_specs=[pl.BlockSpec((bm, d), lambda i: (i, 0)),
                   pl.BlockSpec((bm, LANES), lambda i: (i, 0)),
                   pl.BlockSpec((8, LANES), lambda i: (0, 0))],
        compiler_params=_cparams(("arbitrary",)),
        name="outproj_router",
    )(ya, yb, h, w_out, g_ffn, w_router, b_router)


def _pack_pairs(x):
    c = x.shape[1] // 2
    hi = lax.bitcast_convert_type(x[:, :c].astype(BF16).astype(F32), U32)
    lo = lax.bitcast_convert_type(x[:, c:].astype(BF16).astype(F32), U32)
    return hi | (lo >> 16)


def _unpack_pairs(p):
    hi = lax.bitcast_convert_type(p & jnp.uint32(0xFFFF0000), F32)
    lo = lax.bitcast_convert_type(p << 16, F32)
    return hi, lo


def _dispatch_body(meta_ref, dest_ref, h_ref, g_ref, xs_hbm, xp_ref, zero_ref, sem, zsem, *, tb):
    i = pl.program_id(0)
    slot = i % 2

    def drain(s):
        for _ in range(2):
            pltpu.make_async_copy(xp_ref.at[s], xs_hbm.at[pl.ds(0, tb)], sem.at[s]).wait()

    x = h_ref[...]
    ms = jnp.mean(x * x, axis=-1, keepdims=True)
    xp_ref[slot] = _pack_pairs(x * lax.rsqrt(ms + EPS) * g_ref[...])

    def issue(t, _):
        for k in range(2):
            d = dest_ref[0, 0, 2 * t + k]
            pltpu.make_async_copy(xp_ref.at[slot, pl.ds(t, 1)], xs_hbm.at[pl.ds(d, 1)],
                                  sem.at[slot]).start()
        return 0

    lax.fori_loop(0, tb, issue, 0, unroll=8)

    @pl.when(i > 0)
    def _():
        drain(1 - slot)

    @pl.when(i == pl.num_programs(0) - 1)
    def _():
        drain(slot)

    @pl.when(i == 0)
    def _():
        zero_ref[...] = jnp.zeros_like(zero_ref)

        def per_expert(e, _):
            cnt = meta_ref[e]
            start = meta_ref[N_EXPERTS + e]
            padded = meta_ref[2 * N_EXPERTS + e]

            def pad_copy(r):
                return pltpu.make_async_copy(zero_ref.at[pl.ds(0, 1)], xs_hbm.at[pl.ds(start + r, 1)], zsem)

            def zissue(r, _):
                pad_copy(r).start()
                return 0

            def zwait(r, _):
                pad_copy(r).wait()
                return 0

            lax.fori_loop(cnt, padded, zissue, 0)
            lax.fori_loop(cnt, padded, zwait, 0)
            return 0

        lax.fori_loop(0, N_EXPERTS, per_expert, 0)

        def tail_copy(blk):
            return pltpu.make_async_copy(zero_ref, xs_hbm.at[pl.ds(blk * MOE_BM, MOE_BM)], zsem)

        def tissue(blk, _):
            tail_copy(blk).start()
            return 0

        def twait(blk, _):
            tail_copy(blk).wait()
            return 0

        n_used = meta_ref[3 * N_EXPERTS]
        lax.fori_loop(n_used, xs_hbm.shape[0] // MOE_BM, tissue, 0)
        lax.fori_loop(n_used, xs_hbm.shape[0] // MOE_BM, twait, 0)


def _dispatch(h, g_ffn, dest3, meta, n_rows):
    n, d = h.shape
    tb = UNIT
    return pl.pallas_call(
        functools.partial(_dispatch_body, tb=tb),
        out_shape=jax.ShapeDtypeStruct((n_rows, d // 2), U32),
        grid_spec=pltpu.PrefetchScalarGridSpec(
            num_scalar_prefetch=1,
            grid=(n // tb,),
            in_specs=[pl.BlockSpec((1, 1, 2 * tb), lambda i, m: (i, 0, 0), memory_space=pltpu.SMEM),
                      pl.BlockSpec((tb, d), lambda i, m: (i, 0)),
                      pl.BlockSpec((1, d), lambda i, m: (0, 0))],
            out_specs=pl.BlockSpec(memory_space=pl.ANY),
            scratch_shapes=[pltpu.VMEM((2, tb, d // 2), U32), pltpu.VMEM((MOE_BM, d // 2), U32),
                            pltpu.SemaphoreType.DMA((2,)), pltpu.SemaphoreType.DMA(())]),
        compiler_params=_cparams(("arbitrary",)),
        name="moe_dispatch",
    )(meta, dest3, h, g_ffn)


def _experts_body(be_ref, nu_ref, xs_ref, wg_ref, wu_ref, wd_ref, ys_ref, wg_bf, wu_bf, wd_bf):
    i = pl.program_id(0)
    n_used = nu_ref[0]

    @pl.when(i < n_used)
    def _():
        prev = be_ref[jnp.maximum(i - 1, 0)]

        @pl.when((i == 0) | (be_ref[i] != prev))
        def _():
            wg_bf[...] = wg_ref[0, 0].astype(BF16)
            wu_bf[...] = wu_ref[0, 0].astype(BF16)
            wd_bf[...] = wd_ref[0, 0].astype(BF16)

        half = MOE_BM // 2
        kh = wg_bf.shape[0] // 2
        xns = []
        for r in range(2):
            hi, lo = _unpack_pairs(xs_ref[r * half:(r + 1) * half, :])
            xns.append((hi.astype(BF16), lo.astype(BF16)))

        def up(w_bf):
            return [jnp.dot(xa, w_bf[0:kh, :], preferred_element_type=F32)
                    + jnp.dot(xb, w_bf[kh:2 * kh, :], preferred_element_type=F32) for xa, xb in xns]

        hgs = up(wg_bf)
        hus = up(wu_bf)
        hids = [(_silu(hg) * hu).astype(BF16) for hg, hu in zip(hgs, hus)]
        for r in range(2):
            ys_ref[r * half:(r + 1) * half, :] = _pack_pairs(
                jnp.dot(hids[r], wd_bf[...], preferred_element_type=F32))

    @pl.when(i >= n_used)
    def _():
        ys_ref[...] = jnp.zeros_like(ys_ref)


def _experts(xs, w_gate, w_up, w_down, layer, block_expert, n_used):
    r, dp = xs.shape
    d = 2 * dp
    de = w_gate.shape[3]
    nb = r // MOE_BM

    def wblk(i, be, nu):
        return (layer, be[jnp.minimum(i, nu[0] - 1)], 0, 0)

    return pl.pallas_call(
        _experts_body,
        out_shape=jax.ShapeDtypeStruct((r, dp), U32),
        grid_spec=pltpu.PrefetchScalarGridSpec(
            num_scalar_prefetch=2,
            grid=(nb,),
            in_specs=[pl.BlockSpec((MOE_BM, dp), lambda i, be, nu: (i, 0)),
                      pl.BlockSpec((1, 1, d, de), wblk),
                      pl.BlockSpec((1, 1, d, de), wblk),
                      pl.BlockSpec((1, 1, de, d), wblk)],
            out_specs=pl.BlockSpec((MOE_BM, dp), lambda i, be, nu: (i, 0)),
            scratch_shapes=[pltpu.VMEM((d, de), BF16), pltpu.VMEM((d, de), BF16),
                            pltpu.VMEM((de, d), BF16)]),
        compiler_params=_cparams(("arbitrary",)),
        name="moe_experts",
    )(block_expert, n_used, xs, w_gate, w_up, w_down)


def _combine_body(dest_ref, dnext_ref, h_ref, r_ref, gf_ref, ys_hbm, o_ref, buf, sem,
                  *, tb, blocks_per_seq, n_real, final_norm):
    i = pl.program_id(0)
    slot = i % 2

    def gather(d_ref, s):
        def issue(t, _):
            for k in range(2):
                d = d_ref[0, 0, 2 * t + k]
                pltpu.make_async_copy(ys_hbm.at[pl.ds(d, 1)], buf.at[s, k, pl.ds(t, 1)], sem.at[s]).start()
            return 0

        lax.fori_loop(0, tb, issue, 0, unroll=8)

    @pl.when(i == 0)
    def _():
        gather(dest_ref, 0)

    @pl.when(i + 1 < pl.num_programs(0))
    def _():
        gather(dnext_ref, 1 - slot)

    for k in range(2):
        pltpu.make_async_copy(ys_hbm.at[pl.ds(0, tb)], buf.at[slot, k], sem.at[slot]).wait()

    r = r_ref[...]
    w1 = r[:, 2:3]
    w2 = r[:, 3:4]
    y1 = _unpack_pairs(buf[slot, 0])
    y2 = _unpack_pairs(buf[slot, 1])
    y = jnp.concatenate([w1 * y1[0] + w2 * y2[0], w1 * y1[1] + w2 * y2[1]], axis=1)
    pos0 = (i % blocks_per_seq) * tb
    hn = jnp.where(_row_valid(pos0, tb, n_real), h_ref[...] + y, 0.0)
    if final_norm:
        ms = jnp.mean(hn * hn, axis=-1, keepdims=True)
        hn = hn * lax.rsqrt(ms + EPS) * gf_ref[...]
    o_ref[...] = hn


def _combine(h, route, dest3, ys, g_final, tp, n_real, final_norm):
    n, d = h.shape
    tb = UNIT
    last = n // tb - 1
    return pl.pallas_call(
        functools.partial(_combine_body, tb=tb, blocks_per_seq=tp // tb, n_real=n_real,
                          final_norm=final_norm),
        out_shape=jax.ShapeDtypeStruct((n, d), F32),
        grid=(n // tb,),
        in_specs=[pl.BlockSpec((1, 1, 2 * tb), lambda i: (i, 0, 0), memory_space=pltpu.SMEM),
                  pl.BlockSpec((1, 1, 2 * tb), lambda i: (jnp.minimum(i + 1, last), 0, 0),
                               memory_space=pltpu.SMEM),
                  pl.BlockSpec((tb, d), lambda i: (i, 0)),
                  pl.BlockSpec((tb, LANES), lambda i: (i, 0)),
                  pl.BlockSpec((1, d), lambda i: (0, 0)),
                  pl.BlockSpec(memory_space=pl.ANY)],
        out_specs=pl.BlockSpec((tb, d), lambda i: (i, 0)),
        scratch_shapes=[pltpu.VMEM((2, 2, tb, d // 2), U32), pltpu.SemaphoreType.DMA((2,))],
        compiler_params=_cparams(("arbitrary",)),
        name="moe_combine",
    )(dest3, dest3, h, route, g_final, ys)


def _moe(h, route, counts_f, g_ffn, w_gate, w_up, w_down, layer, g_final, tp, n_real, final_norm):
    n, d = h.shape
    n_pairs = 2 * n
    counts = counts_f[0, 0:N_EXPERTS].astype(I32)
    padded = (counts + MOE_BM - 1) // MOE_BM * MOE_BM
    p_ends = jnp.cumsum(padded)
    p_starts = p_ends - padded
    dest = jnp.take(p_starts, route[:, 0:2].astype(I32)) + route[:, 4:6].astype(I32)
    n_blocks = -(-(n_pairs + N_EXPERTS * (MOE_BM - 1)) // MOE_BM)
    block_row0 = jnp.arange(n_blocks, dtype=I32) * MOE_BM
    block_expert = jnp.minimum(jnp.sum((p_ends[None, :] <= block_row0[:, None]).astype(I32), axis=1),
                               N_EXPERTS - 1)
    n_used = (p_ends[-1:] // MOE_BM).astype(I32)
    meta = jnp.concatenate([counts, p_starts, padded, n_used]).astype(I32)
    dest3 = dest.astype(I32).reshape(n // UNIT, 1, 2 * UNIT)

    xs = _dispatch(h, g_ffn, dest3, meta, n_blocks * MOE_BM)
    ys = _experts(xs, w_gate, w_up, w_down, layer, block_expert, n_used)
    return _combine(h, route, dest3, ys, g_final, tp, n_real, final_norm)


def _short_conv(x, halo_ref, w_ref, first):
    rows = x.shape[0]

    @pl.when(first)
    def _():
        halo_ref[...] = jnp.zeros_like(halo_ref)

    x_ext = jnp.concatenate([halo_ref[...], x], axis=0)
    halo_ref[...] = x[rows - 8:rows]
    acc = jnp.zeros_like(x)
    for j, tap in _shifted_taps(x_ext, 8, rows, SHORT_K):
        acc = acc + w_ref[j:j + 1, :] * tap
    return acc


def _bdot(a, b):
    return jnp.dot(a.astype(BF16), b.astype(BF16), preferred_element_type=F32)


def _gdn_pre_body(z_ref, sc_ref, st_ref, cw_ref, al_r_ref, dt_r_ref, al_c_ref, dt_c_ref,
                  u_ref, w_ref, qd_ref, kd_ref, at_ref, eg_ref, halo_ref):
    rows = z_ref.shape[0]
    hw = N_HEADS * HEAD_DIM
    x = _silu(_short_conv(z_ref[...].astype(F32), halo_ref, cw_ref, pl.program_id(1) == 0))

    sc = sc_ref[...]
    g_cols = -jnp.exp(al_r_ref[...]) * _softplus(sc + dt_r_ref[...])
    beta_cols = _sigmoid(sc)
    st = st_ref[...]
    g_rows = -jnp.exp(al_c_ref[:, 0:1]) * _softplus(st + dt_c_ref[:, 0:1])

    ri = lax.broadcasted_iota(I32, (rows, rows), 0)
    ci = lax.broadcasted_iota(I32, (rows, rows), 1)
    same64 = (ri >> 6) == (ci >> 6)
    same32 = (ri >> 5) == (ci >> 5)
    same16 = (ri >> 4) == (ci >> 4)
    lower = ri >= ci
    strict = ri > ci
    lane = lax.broadcasted_iota(I32, (rows, LANES), 1)
    eg_slab = jnp.zeros((rows, LANES), F32)
    scale = HEAD_DIM ** -0.5

    heads = range(N_HEADS)
    in_chunk_lower = same64 & lower
    in_chunk_upper = same64 & (ri <= ci)
    a_mats, nmats, rhss = [], [], []
    for h in heads:
        sl = slice(h * HEAD_DIM, (h + 1) * HEAD_DIM)
        q = x[:, sl]
        k = x[:, hw + h * HEAD_DIM:hw + (h + 1) * HEAD_DIM]
        v = x[:, 2 * hw + h * HEAD_DIM:2 * hw + (h + 1) * HEAD_DIM]
        q = q * lax.rsqrt(jnp.sum(q * q, axis=-1, keepdims=True) + EPS)
        k = k * lax.rsqrt(jnp.sum(k * k, axis=-1, keepdims=True) + EPS)
        g_col = g_cols[:, h:h + 1]
        beta = beta_cols[:, N_HEADS + h:N_HEADS + h + 1]
        g_row = g_rows[h:h + 1, :]

        gc_col = jnp.sum(jnp.where(in_chunk_lower, g_row, 0.0), axis=1, keepdims=True)
        gc_row = jnp.sum(jnp.where(in_chunk_upper, g_col, 0.0), axis=0, keepdims=True)
        gtot_col = jnp.sum(jnp.where(same64, g_row, 0.0), axis=1, keepdims=True)
        decay = jnp.exp(jnp.where(in_chunk_lower, gc_col - gc_row, NEG))
        eg = jnp.exp(gc_col)
        ekd = jnp.exp(gtot_col - gc_col)

        kb = k * beta
        kbf = k.astype(BF16)
        kk = lax.dot_general(kb.astype(BF16), kbf, (((1,), (1,)), ((), ())), preferred_element_type=F32)
        qs = q * scale
        qk = lax.dot_general(qs.astype(BF16), kbf, (((1,), (1,)), ((), ())), preferred_element_type=F32)
        attn = qk * decay
        a_mats.append(jnp.where(strict, kk * decay, 0.0))
        rhss.append(jnp.concatenate([v * beta, kb * eg], axis=1))
        qd_ref[:, sl] = (qs * eg).astype(BF16)
        kd_ref[:, sl] = (k * ekd).astype(BF16)
        for c in range(rows // CHUNK):
            cs = slice(c * CHUNK, (c + 1) * CHUNK)
            at_ref[h, cs, :] = attn[cs, cs].astype(BF16)
        eg_slab = jnp.where(lane == h, eg, eg_slab)
    eg_ref[...] = eg_slab

    nmats = [jnp.where(same16, -a, 0.0) for a in a_mats]
    qqs = nmats
    for _ in range(3):
        qqs = [_bdot(qq, qq) for qq in qqs]
        prods = [_bdot(qq, nm) for qq, nm in zip(qqs, nmats)]
        nmats = [nm + qq + pr for nm, qq, pr in zip(nmats, qqs, prods)]
    for level_mask, inner_mask in ((same32, same16), (same64, same32)):
        sel = level_mask & jnp.logical_not(inner_mask)
        offs = [jnp.where(sel, a, 0.0) for a in a_mats]
        bmats = [off + _bdot(nm, off) for nm, off in zip(nmats, offs)]
        prods = [_bdot(bm, nm) for bm, nm in zip(bmats, nmats)]
        nmats = [nm - bm - pr for nm, bm, pr in zip(nmats, bmats, prods)]

    for h in heads:
        sl = slice(h * HEAD_DIM, (h + 1) * HEAD_DIM)
        uw = rhss[h] + _bdot(nmats[h], rhss[h])
        u_ref[:, sl] = uw[:, 0:HEAD_DIM]
        w_ref[:, sl] = uw[:, HEAD_DIM:2 * HEAD_DIM].astype(BF16)


def _gdn_pre(z, sc, st, conv_w, a_log, dt_bias, bsz, tp):
    n = z.shape[0]
    hw = N_HEADS * HEAD_DIM
    rows = UNIT
    nu = tp // rows
    pad_r = lambda v: jnp.pad(v.astype(F32), (0, LANES - N_HEADS)).reshape(1, LANES)
    pad_c = lambda v: jnp.broadcast_to(jnp.pad(v.astype(F32), (0, 16 - N_HEADS))[:, None], (16, LANES))
    outs = pl.pallas_call(
        _gdn_pre_body,
        out_shape=[jax.ShapeDtypeStruct((n, hw), F32),
                   jax.ShapeDtypeStruct((n, hw), BF16),
                   jax.ShapeDtypeStruct((n, hw), BF16),
                   jax.ShapeDtypeStruct((n, hw), BF16),
                   jax.ShapeDtypeStruct((N_HEADS, n, CHUNK), BF16),
                   jax.ShapeDtypeStruct((n, LANES), F32)],
        grid=(bsz, nu),
        in_specs=[pl.BlockSpec((rows, 3 * hw), lambda b, i: (b * nu + i, 0)),
                  pl.BlockSpec((rows, LANES), lambda b, i: (b * nu + i, 0)),
                  pl.BlockSpec((16, rows), lambda b, i: (0, b * nu + i)),
                  pl.BlockSpec((8, 3 * hw), lambda b, i: (0, 0)),
                  pl.BlockSpec((1, LANES), lambda b, i: (0, 0)),
                  pl.BlockSpec((1, LANES), lambda b, i: (0, 0)),
                  pl.BlockSpec((16, LANES), lambda b, i: (0, 0)),
                  pl.BlockSpec((16, LANES), lambda b, i: (0, 0))],
        out_specs=[pl.BlockSpec((rows, hw), lambda b, i: (b * nu + i, 0)),
                   pl.BlockSpec((rows, hw), lambda b, i: (b * nu + i, 0)),
                   pl.BlockSpec((rows, hw), lambda b, i: (b * nu + i, 0)),
                   pl.BlockSpec((rows, hw), lambda b, i: (b * nu + i, 0)),
                   pl.BlockSpec((N_HEADS, rows, CHUNK), lambda b, i: (0, b * nu + i, 0)),
                   pl.BlockSpec((rows, LANES), lambda b, i: (b * nu + i, 0))],
        scratch_shapes=[pltpu.VMEM((8, 3 * hw), F32)],
        compiler_params=_cparams(("arbitrary", "arbitrary")),
        name="gdn_pre",
    )(z, sc, st, jnp.pad(conv_w.astype(F32), ((0, 8 - SHORT_K), (0, 0))),
      pad_r(a_log), pad_r(dt_bias), pad_c(a_log), pad_c(dt_bias))
    return outs


def _gdn_scan_body(u_ref, w_ref, qd_ref, kd_ref, at_ref, eg_ref, z_ref, gn_ref, o_ref, s_ref, *, bsz):
    @pl.when(pl.program_id(0) == 0)
    def _():
        s_ref[...] = jnp.zeros_like(s_ref)

    for b in range(bsz):
        for h in range(N_HEADS):
            sl = slice(h * HEAD_DIM, (h + 1) * HEAD_DIM)
            s = s_ref[b * N_HEADS + h]
            wq = jnp.concatenate([w_ref[b, :, sl], qd_ref[b, :, sl]], axis=0)
            r = jnp.dot(wq, s.astype(BF16), preferred_element_type=F32)
            v_new = u_ref[b, :, sl] - r[0:CHUNK]
            vb = v_new.astype(BF16)
            o = r[CHUNK:2 * CHUNK] + jnp.dot(at_ref[h, b], vb, preferred_element_type=F32)
            gt = eg_ref[b, CHUNK - 1:CHUNK, h:h + 1]
            s_ref[b * N_HEADS + h] = s * gt + lax.dot_general(
                kd_ref[b, :, sl], vb, (((0,), (0,)), ((), ())), preferred_element_type=F32)
            on = o * lax.rsqrt(jnp.mean(o * o, axis=-1, keepdims=True) + EPS) * gn_ref[...]
            o_ref[b, :, sl] = (on * _silu(z_ref[b, :, sl].astype(F32))).astype(o_ref.dtype)


def _gdn_scan(u, w, qd, kd, attn, eg, z, out_norm_g, bsz, tp, z_col):
    hw = N_HEADS * HEAD_DIM
    nc = tp // CHUNK
    v3 = lambda a: a.reshape(bsz, tp, a.shape[-1])
    blk3 = pl.BlockSpec((bsz, CHUNK, hw), lambda c: (0, c, 0))
    return pl.pallas_call(
        functools.partial(_gdn_scan_body, bsz=bsz),
        out_shape=jax.ShapeDtypeStruct((bsz, tp, hw), BF16),
        grid=(nc,),
        in_specs=[blk3, blk3, blk3, blk3,
                  pl.BlockSpec((N_HEADS, bsz, CHUNK, CHUNK), lambda c: (0, 0, c, 0)),
                  pl.BlockSpec((bsz, CHUNK, LANES), lambda c: (0, c, 0)),
                  pl.BlockSpec((bsz, CHUNK, hw), lambda c: (0, c, z_col)),
                  pl.BlockSpec((1, HEAD_DIM), lambda c: (0, 0))],
        out_specs=blk3,
        scratch_shapes=[pltpu.VMEM((bsz * N_HEADS, HEAD_DIM, HEAD_DIM), F32)],
        compiler_params=_cparams(("arbitrary",)),
        name="gdn_scan",
    )(v3(u), v3(w), v3(qd), v3(kd), attn.reshape(N_HEADS, bsz, tp, CHUNK), v3(eg), v3(z),
      out_norm_g.astype(F32).reshape(1, HEAD_DIM)).reshape(bsz * tp, hw)


def _lru_body(x_ref, gate_ref, cw_ref, cb_ref, wr_ref, br_ref, wi_ref, bi_ref, lam_ref, o_ref,
              halo_ref, hc_ref, *, bt, n_real):
    first = pl.program_id(1) == 0

    @pl.when(first)
    def _():
        hc_ref[...] = jnp.zeros_like(hc_ref)

    x = _short_conv(x_ref[...].astype(F32), halo_ref, cw_ref, first) + cb_ref[...]
    nblk = wr_ref.shape[0]
    bd = wr_ref.shape[1]
    xb = x.astype(BF16)
    rg = jnp.concatenate([jnp.dot(xb[:, n * bd:(n + 1) * bd], wr_ref[n], preferred_element_type=F32)
                          for n in range(nblk)], axis=1)
    ig = jnp.concatenate([jnp.dot(xb[:, n * bd:(n + 1) * bd], wi_ref[n], preferred_element_type=F32)
                          for n in range(nblk)], axis=1)
    r = _sigmoid(rg + br_ref[...])
    ig = _sigmoid(ig + bi_ref[...])
    log_a = (-LRU_C * _softplus(-lam_ref[...])) * r
    a = jnp.exp(log_a)
    th = jnp.tanh(log_a)
    b = jnp.sqrt(-2.0 * th / (1.0 - th)) * (ig * x)
    b = jnp.where(_row_valid(pl.program_id(1) * bt, bt, n_real), b, 0.0)

    row = lax.broadcasted_iota(I32, (bt, 1), 0)
    d = 1
    while d < bt:
        keep = row >= d
        a_sh = jnp.where(keep, pltpu.roll(a, d, 0), 1.0)
        b_sh = jnp.where(keep, pltpu.roll(b, d, 0), 0.0)
        b = a * b_sh + b
        a = a * a_sh
        d *= 2
    hs = a * hc_ref[0:1, :] + b
    hc_ref[...] = jnp.broadcast_to(hs[bt - 1:bt, :], hc_ref.shape)
    o_ref[...] = (hs * jax.nn.gelu(gate_ref[...].astype(F32))).astype(o_ref.dtype)


def _lru(z, conv_w, conv_b, w_rg, b_rg, w_ig, b_ig, lam, bsz, tp, n_real, x_col, gate_col):
    n = z.shape[0]
    c = conv_w.shape[1]
    bt = UNIT
    nt = tp // bt
    row = lambda v: v.astype(F32).reshape(1, c)
    wspec = pl.BlockSpec(w_rg.shape, lambda b, i: (0, 0, 0))
    vspec = pl.BlockSpec((1, c), lambda b, i: (0, 0))
    return pl.pallas_call(
        functools.partial(_lru_body, bt=bt, n_real=n_real),
        out_shape=jax.ShapeDtypeStruct((n, c), BF16),
        grid=(bsz, nt),
        in_specs=[pl.BlockSpec((bt, c), lambda b, i: (b * nt + i, x_col)),
                  pl.BlockSpec((bt, c), lambda b, i: (b * nt + i, gate_col)),
                  pl.BlockSpec((8, c), lambda b, i: (0, 0)),
                  vspec, wspec, vspec, wspec, vspec, vspec],
        out_specs=pl.BlockSpec((bt, c), lambda b, i: (b * nt + i, 0)),
        scratch_shapes=[pltpu.VMEM((8, c), F32), pltpu.VMEM((8, c), F32)],
        compiler_params=_cparams(("arbitrary", "arbitrary")),
        name="rg_lru",
    )(z, z, jnp.pad(conv_w.astype(F32), ((0, 8 - SHORT_K), (0, 0))), row(conv_b),
      w_rg.astype(BF16), row(b_rg), w_ig.astype(BF16), row(b_ig), row(lam))


def _pick(n, candidates):
    for c in candidates:
        if n % c == 0:
            return c
    raise ValueError(f"no block size in {candidates} divides {n}")


def _pack_weights_body(a_ref, b_ref, o_ref, *, scaled_tiles, scale, head_tiles, gap):
    j = pl.program_id(0)
    tile = o_ref.shape[1]

    @pl.when(j < scaled_tiles)
    def _():
        o_ref[...] = (a_ref[...] * scale).astype(BF16)

    @pl.when((j >= scaled_tiles) & (j < head_tiles))
    def _():
        o_ref[...] = a_ref[...].astype(BF16)

    @pl.when(j >= head_tiles)
    def _():
        x = jnp.concatenate([a_ref[...], b_ref[...]], axis=1)
        o_ref[...] = x[:, gap:gap + tile].astype(BF16)


def _pack_inproj_weights(w_in, head, gap, scaled=0, scale=1.0):
    d, total = w_in.shape
    tile = 512
    n_out = total - gap
    return pl.pallas_call(
        functools.partial(_pack_weights_body, scaled_tiles=scaled // tile, scale=scale,
                          head_tiles=head // tile, gap=gap),
        out_shape=jax.ShapeDtypeStruct((d, n_out), BF16),
        grid=(n_out // tile,),
        in_specs=[pl.BlockSpec((d, tile), lambda j: (0, j)),
                  pl.BlockSpec((d, LANES), lambda j: (0, (tile // LANES) * (j + 1)))],
        out_specs=pl.BlockSpec((d, tile), lambda j: (0, j)),
        compiler_params=_cparams(("arbitrary",)),
        name="pack_inproj_weights",
    )(w_in, w_in)


def _small_weights(cols):
    k = cols.shape[1]
    return (jnp.pad(cols.T, ((0, 16 - k), (0, 0))).astype(BF16),
            jnp.pad(cols, ((0, 0), (0, LANES - k))).astype(BF16))


def _router_weights(w_group, b_group, w_expert, b_expert):
    w = jnp.concatenate([w_expert, w_group], axis=1).astype(F32)
    b = jnp.concatenate([b_expert, b_group]).astype(F32)
    k = w.shape[1]
    w = jnp.pad(w, ((0, 0), (0, LANES - k)))
    w_hi = w.astype(BF16)
    w_lo = (w - w_hi.astype(F32)).astype(BF16)
    return jnp.concatenate([w_hi, w_lo], axis=1), jnp.pad(b, (0, LANES - k)).reshape(1, LANES)


def kernel(x, meta_tokens, norm_mix_g, norm_ffn_g, norm_final_g, ab_w_in, ab_forget_b, ab_conv_w, ab_conv_b, ab_ln_g, ab_ln_b, ab_w_out, cd_w_in, cd_qkv_conv_w, cd_a_log, cd_dt_bias, cd_out_norm_g, cd_lru_conv_w, cd_lru_conv_b, cd_w_rg, cd_b_rg, cd_w_ig, cd_b_ig, cd_lru_lambda, cd_w_out, moe_w_group, moe_b_group, moe_w_expert, moe_b_expert, moe_w_gate, moe_w_up, moe_w_down):
    bsz, seq, d = x.shape
    depth = norm_mix_g.shape[0]
    n_real = N_META + seq
    tp = -(-(OFF + seq) // UNIT) * UNIT
    n = bsz * tp
    hw = N_HEADS * HEAD_DIM
    bm = _pick(n, (768, 512, 256))
    bn = 1024

    meta = jnp.broadcast_to(meta_tokens[None].astype(F32), (bsz, N_META, d))
    h = jnp.concatenate([jnp.zeros((bsz, FRONT, d), F32), meta, x.astype(F32),
                         jnp.zeros((bsz, tp - OFF - seq, d), F32)], axis=1).reshape(n, d)
    row = lambda v: v.astype(F32).reshape(1, -1)

    for layer in range(depth):
        i = layer // 2
        g_mix = row(norm_mix_g[layer])
        if layer % 2 == 0:
            w_in = ab_w_in[i]
            w_main = _pack_inproj_weights(w_in.astype(F32), 3 * hw, N_HEADS, scaled=hw,
                                          scale=LOG2E * HEAD_DIM ** -0.5)
            wst, wsc = _small_weights(w_in[:, 3 * hw:3 * hw + N_HEADS])
            z, _, f_slab = _norm_inproj(h, g_mix, w_main, wst, wsc, bm, bn)
            kx, vx = _fox_prep(z, f_slab, ab_forget_b[i], bsz, tp, n_real)
            ya = _fox_attn(z, kx, vx, bsz, tp)
            yb = _conformer(z, ab_conv_w[i], ab_conv_b[i], ab_ln_g[i], ab_ln_b[i], bsz, tp, 3, 4)
            w_out = ab_w_out[i].astype(BF16)
        else:
            w_in = cd_w_in[i]
            w_main = _pack_inproj_weights(w_in.astype(F32), 3 * hw, 2 * N_HEADS)
            wst, wsc = _small_weights(w_in[:, 3 * hw:3 * hw + 2 * N_HEADS])
            z, zt, zc = _norm_inproj(h, g_mix, w_main, wst, wsc, bm, bn)
            u, w, qd, kd, attn, eg = _gdn_pre(z, zc, zt, cd_qkv_conv_w[i], cd_a_log[i], cd_dt_bias[i],
                                              bsz, tp)
            ya = _gdn_scan(u, w, qd, kd, attn, eg, z, cd_out_norm_g[i], bsz, tp, 3)
            yb = _lru(z, cd_lru_conv_w[i], cd_lru_conv_b[i], cd_w_rg[i], cd_b_rg[i], cd_w_ig[i],
                      cd_b_ig[i], cd_lru_lambda[i], bsz, tp, n_real, 4, 5)
            w_out = cd_w_out[i].astype(BF16)
        g_ffn = row(norm_ffn_g[layer])
        w_r, b_r = _router_weights(moe_w_group[layer], moe_b_group[layer],
                                   moe_w_expert[layer], moe_b_expert[layer])
        h, route, counts = _outproj_router(ya, yb, h, w_out, g_ffn, w_r, b_r, tp, n_real)
        h = _moe(h, route, counts, g_ffn, moe_w_gate, moe_w_up, moe_w_down, layer,
                 row(norm_final_g), tp, n_real, final_norm=(layer == depth - 1))
    return h.reshape(bsz, tp, d)[:, OFF:OFF + seq].astype(x.dtype)
```

```python
import functools

import jax
import jax.numpy as jnp
from jax import lax
from jax.experimental import pallas as pl
from jax.experimental.pallas import tpu as pltpu

F32 = jnp.float32
BF16 = jnp.bfloat16
I32 = jnp.int32
U32 = jnp.uint32

EPS = 1e-6
N_META = 16
CHUNK = 64
FRONT = CHUNK - N_META
OFF = FRONT + N_META
LANES = 128
UNIT = 256
HEAD_DIM = 128
N_HEADS = 8
CONF_K = 31
SHORT_K = 4
LRU_C = 8.0
N_GROUPS = 4
EXPERTS_PER_GROUP = 8
N_EXPERTS = N_GROUPS * EXPERTS_PER_GROUP
MOE_BM = 256
NEG = -1e30
LOG2E = 1.4426950408889634
VMEM_LIMIT = 56 * 1024 * 1024


def _cparams(sem):
    return pltpu.CompilerParams(dimension_semantics=sem, vmem_limit_bytes=VMEM_LIMIT)


def _row_valid(pos0, rows, n_real):
    pos = pos0 + lax.broadcasted_iota(I32, (rows, 1), 0)
    return (pos >= FRONT) & (pos < FRONT + n_real)


def _sigmoid(x):
    return 1.0 / (1.0 + jnp.exp(-x))


def _softplus(x):
    return jnp.maximum(x, 0.0) + jnp.log1p(jnp.exp(-jnp.abs(x)))


def _silu(x):
    return x * _sigmoid(x)


def _embed_body(x_hbm, meta_ref, h_hbm, zero_ref, sem, *, seq, tp):
    b = pl.program_id(0)
    zero_ref[...] = jnp.zeros_like(zero_ref)
    base = b * tp
    tail = tp - OFF - seq
    copies = [
        pltpu.make_async_copy(x_hbm.at[pl.ds(b * seq, seq)], h_hbm.at[pl.ds(base + OFF, seq)], sem.at[0]),
        pltpu.make_async_copy(meta_ref, h_hbm.at[pl.ds(base + FRONT, N_META)], sem.at[1]),
        pltpu.make_async_copy(zero_ref.at[pl.ds(0, FRONT)], h_hbm.at[pl.ds(base, FRONT)], sem.at[2]),
    ]
    if tail:
        copies.append(pltpu.make_async_copy(zero_ref.at[pl.ds(0, tail)],
                                            h_hbm.at[pl.ds(base + OFF + seq, tail)], sem.at[3]))
    for c in copies:
        c.start()
    for c in copies:
        c.wait()


def _embed(x2d, meta_tokens, bsz, seq, tp):
    d = x2d.shape[1]
    return pl.pallas_call(
        functools.partial(_embed_body, seq=seq, tp=tp),
        out_shape=jax.ShapeDtypeStruct((bsz * tp, d), F32),
        grid=(bsz,),
        in_specs=[pl.BlockSpec(memory_space=pl.ANY),
                  pl.BlockSpec((N_META, d), lambda b: (0, 0))],
        out_specs=pl.BlockSpec(memory_space=pl.ANY),
        scratch_shapes=[pltpu.VMEM((max(FRONT, tp - OFF - seq), d), F32), pltpu.SemaphoreType.DMA((4,))],
        compiler_params=_cparams(("arbitrary",)),
        name="embed",
    )(x2d, meta_tokens)


def _norm_inproj_body(h_ref, g_ref, w_ref, wst_ref, wsc_ref, o_ref, ot_ref, oc_ref, xn_ref):
    @pl.when(pl.program_id(1) == 0)
    def _():
        x = h_ref[...]
        ms = jnp.mean(x * x, axis=-1, keepdims=True)
        xn = (x * lax.rsqrt(ms + EPS) * g_ref[...]).astype(BF16)
        xn_ref[...] = xn
        ot_ref[...] = lax.dot_general(wst_ref[...], xn, (((1,), (1,)), ((), ())),
                                      preferred_element_type=F32)
        oc_ref[...] = jnp.dot(xn, wsc_ref[...], preferred_element_type=F32)

    o_ref[...] = jnp.dot(xn_ref[...], w_ref[...], preferred_element_type=F32).astype(o_ref.dtype)


def _norm_inproj(h, g, w_main, w_small_t, w_small_c, bm, bn):
    n, d = h.shape
    nw = w_main.shape[1]
    return pl.pallas_call(
        _norm_inproj_body,
        out_shape=[jax.ShapeDtypeStruct((n, nw), BF16),
                   jax.ShapeDtypeStruct((16, n), F32),
                   jax.ShapeDtypeStruct((n, LANES), F32)],
        grid=(n // bm, nw // bn),
        in_specs=[pl.BlockSpec((bm, d), lambda i, j: (i, 0)),
                  pl.BlockSpec((1, d), lambda i, j: (0, 0)),
                  pl.BlockSpec((d, bn), lambda i, j: (0, j)),
                  pl.BlockSpec((16, d), lambda i, j: (0, 0)),
                  pl.BlockSpec((d, LANES), lambda i, j: (0, 0))],
        out_specs=[pl.BlockSpec((bm, bn), lambda i, j: (i, j)),
                   pl.BlockSpec((16, bm), lambda i, j: (0, i)),
                   pl.BlockSpec((bm, LANES), lambda i, j: (i, 0))],
        scratch_shapes=[pltpu.VMEM((bm, d), BF16)],
        compiler_params=_cparams(("arbitrary", "arbitrary")),
        name="norm_inproj",
    )(h, g, w_main, w_small_t, w_small_c)


def _split3(x):
    p1 = x.astype(BF16)
    r1 = x - p1.astype(F32)
    p2 = r1.astype(BF16)
    p3 = (r1 - p2.astype(F32)).astype(BF16)
    return p1, p2, p3


def _fox_prep_body(k_ref, v_ref, f_ref, fb_ref, kx_ref, vx_ref, carry_ref, *, rows, n_real):
    i = pl.program_id(1)

    @pl.when(i == 0)
    def _():
        carry_ref[...] = jnp.zeros_like(carry_ref)

    f = f_ref[...] + fb_ref[...]
    lf = jnp.minimum(f, 0.0) - jnp.log1p(jnp.exp(-jnp.abs(f)))
    pos = i * rows + lax.broadcasted_iota(I32, (rows, 1), 0)
    lf = jnp.where((pos >= FRONT) & (pos < FRONT + n_real), lf, 0.0)
    rr = lax.broadcasted_iota(I32, (rows, rows), 0)
    cc = lax.broadcasted_iota(I32, (rows, rows), 1)
    tri = jnp.where(rr >= cc, 1.0, 0.0).astype(BF16)
    c = carry_ref[0:1, :]
    for piece in _split3(lf):
        c = c + jnp.dot(tri, piece, preferred_element_type=F32)
    carry_ref[...] = jnp.broadcast_to(c[rows - 1:rows, :], carry_ref.shape)
    cs = jnp.where(pos < FRONT, NEG, -LOG2E * c)
    p1, p2, p3 = (p.astype(F32) for p in _split3(cs))
    lane = lax.broadcasted_iota(I32, (rows, LANES), 1)
    ones_col = jnp.where(lane == 0, 1.0, 0.0).astype(BF16)
    for h in range(N_HEADS):
        ext = jnp.where(lane == 0, p1[:, h:h + 1],
                        jnp.where(lane == 1, p2[:, h:h + 1], jnp.where(lane == 2, p3[:, h:h + 1], 0.0)))
        kx_ref[:, (2 * h) * HEAD_DIM:(2 * h + 1) * HEAD_DIM] = k_ref[:, h * HEAD_DIM:(h + 1) * HEAD_DIM]
        kx_ref[:, (2 * h + 1) * HEAD_DIM:(2 * h + 2) * HEAD_DIM] = ext.astype(BF16)
        vx_ref[:, (2 * h) * HEAD_DIM:(2 * h + 1) * HEAD_DIM] = v_ref[:, h * HEAD_DIM:(h + 1) * HEAD_DIM]
        vx_ref[:, (2 * h + 1) * HEAD_DIM:(2 * h + 2) * HEAD_DIM] = ones_col


def _fox_prep(z, f_slab, forget_b, bsz, tp, n_real):
    n = z.shape[0]
    hw = N_HEADS * HEAD_DIM
    rows = UNIT
    nt = tp // rows
    fb = jnp.pad(forget_b.astype(F32), (0, LANES - N_HEADS)).reshape(1, LANES)
    return pl.pallas_call(
        functools.partial(_fox_prep_body, rows=rows, n_real=n_real),
        out_shape=[jax.ShapeDtypeStruct((n, 2 * hw), BF16), jax.ShapeDtypeStruct((n, 2 * hw), BF16)],
        grid=(bsz, nt),
        in_specs=[pl.BlockSpec((rows, hw), lambda b, i: (b * nt + i, 1)),
                  pl.BlockSpec((rows, hw), lambda b, i: (b * nt + i, 2)),
                  pl.BlockSpec((rows, LANES), lambda b, i: (b * nt + i, 0)),
                  pl.BlockSpec((1, LANES), lambda b, i: (0, 0))],
        out_specs=[pl.BlockSpec((rows, 2 * hw), lambda b, i: (b * nt + i, 0)),
                   pl.BlockSpec((rows, 2 * hw), lambda b, i: (b * nt + i, 0))],
        scratch_shapes=[pltpu.VMEM((8, LANES), F32)],
        compiler_params=_cparams(("arbitrary", "arbitrary")),
        name="fox_prep",
    )(z, z, f_slab, fb)


def _fox_attn_body(q_ref, k_ref, v_ref, o_ref, *, blk, n_split):
    qi = pl.program_id(2)
    half = blk // n_split
    lane = lax.broadcasted_iota(I32, (blk, HEAD_DIM), 1)
    q = jnp.concatenate([q_ref[...], jnp.where(lane < 3, 1.0, 0.0).astype(BF16)], axis=1)
    qs = tuple(q[r * half:(r + 1) * half] for r in range(n_split))

    def step(js, carry, diag_last):
        starts = [pl.multiple_of(j * blk, blk) for j in js]
        ss = []
        for r, qh in enumerate(qs):
            srow = []
            for b, st in enumerate(starts):
                on_diag = diag_last and b == len(js) - 1
                ncol = (r + 1) * half if on_diag else blk
                s = lax.dot_general(qh, k_ref[pl.ds(st, ncol), :], (((1,), (1,)), ((), ())),
                                    preferred_element_type=F32)
                if on_diag:
                    row = r * half + lax.broadcasted_iota(I32, (half, ncol), 0)
                    col = lax.broadcasted_iota(I32, (half, ncol), 1)
                    s = jnp.where(col <= row, s, NEG)
                srow.append((s, st, ncol))
            ss.append(srow)
        out = []
        for r in range(n_split):
            m = carry[2 * r]
            m_new = m
            for s, _, _ in ss[r]:
                m_new = jnp.maximum(m_new, jnp.max(s, axis=1, keepdims=True))
            acc = jnp.exp2(m - m_new) * carry[2 * r + 1]
            for s, st, ncol in ss[r]:
                acc = acc + jnp.dot(jnp.exp2(s - m_new).astype(BF16), v_ref[pl.ds(st, ncol), :],
                                    preferred_element_type=F32)
            out += [m_new, acc]
        return tuple(out)

    init = (jnp.full((half, 1), NEG, F32), jnp.zeros((half, 2 * HEAD_DIM), F32)) * n_split
    carry = lax.fori_loop(0, qi // 2, lambda j, c: step((2 * j, 2 * j + 1), c, False), init)
    res = lax.cond(qi % 2 == 1, lambda c: step((qi - 1, qi), c, True),
                   lambda c: step((qi,), c, True), carry)
    for r in range(n_split):
        acc = res[2 * r + 1]
        o_ref[r * half:(r + 1) * half, :] = (acc[:, 0:HEAD_DIM] / acc[:, HEAD_DIM:HEAD_DIM + 1]
                                             ).astype(o_ref.dtype)


def _fox_attn(z, kx, vx, bsz, tp):
    n = z.shape[0]
    blk = _pick(tp, (768, 512, 256))
    nq = tp // blk
    return pl.pallas_call(
        functools.partial(_fox_attn_body, blk=blk, n_split=2),
        out_shape=jax.ShapeDtypeStruct((n, N_HEADS * HEAD_DIM), BF16),
        grid=(bsz, N_HEADS, nq),
        in_specs=[pl.BlockSpec((blk, HEAD_DIM), lambda b, h, i: (b * nq + i, h)),
                  pl.BlockSpec((tp, 2 * HEAD_DIM), lambda b, h, i: (b, h)),
                  pl.BlockSpec((tp, 2 * HEAD_DIM), lambda b, h, i: (b, h))],
        out_specs=pl.BlockSpec((blk, HEAD_DIM), lambda b, h, i: (b * nq + i, h)),
        compiler_params=_cparams(("arbitrary", "arbitrary", "arbitrary")),
        name="fox_attn",
    )(z, kx, vx)


def _shifted_taps(x_ext, halo, rows, n_taps):
    for r in range(min(8, n_taps)):
        rolled = x_ext if r == 0 else pltpu.roll(x_ext, r, 0)
        for q in range(halo // 8):
            s = 8 * q + r
            if s > n_taps - 1:
                continue
            yield n_taps - 1 - s, rolled[halo - 8 * q:halo - 8 * q + rows]


def _conformer_body(a_ref, b_ref, w_ref, cb_ref, lg_ref, lb_ref, o_ref, halo_ref, *, bt):
    halo = halo_ref.shape[0]

    @pl.when(pl.program_id(1) == 0)
    def _():
        halo_ref[...] = jnp.zeros_like(halo_ref)

    u = a_ref[...].astype(F32) * _sigmoid(b_ref[...].astype(F32))
    x_ext = jnp.concatenate([halo_ref[...], u], axis=0)
    halo_ref[...] = u[bt - halo:bt]
    acc = jnp.zeros_like(u)
    for j, tap in _shifted_taps(x_ext, halo, bt, CONF_K):
        acc = acc + w_ref[j:j + 1, :] * tap
    y = acc + cb_ref[...]
    mu = jnp.mean(y, axis=-1, keepdims=True)
    yc = y - mu
    var = jnp.mean(yc * yc, axis=-1, keepdims=True)
    yn = yc * lax.rsqrt(var + EPS) * lg_ref[...] + lb_ref[...]
    o_ref[...] = _silu(yn).astype(o_ref.dtype)


def _conformer(z, conv_w, conv_b, ln_g, ln_b, bsz, tp, col_a, col_b):
    n = z.shape[0]
    c = conv_w.shape[1]
    bt = UNIT
    nt = tp // bt
    w = jnp.pad(conv_w.astype(F32), ((0, 32 - CONF_K), (0, 0)))
    row = lambda v: v.astype(F32).reshape(1, c)
    return pl.pallas_call(
        functools.partial(_conformer_body, bt=bt),
        out_shape=jax.ShapeDtypeStruct((n, c), BF16),
        grid=(bsz, nt),
        in_specs=[pl.BlockSpec((bt, c), lambda b, i: (b * nt + i, col_a)),
                  pl.BlockSpec((bt, c), lambda b, i: (b * nt + i, col_b)),
                  pl.BlockSpec((32, c), lambda b, i: (0, 0)),
                  pl.BlockSpec((1, c), lambda b, i: (0, 0)),
                  pl.BlockSpec((1, c), lambda b, i: (0, 0)),
                  pl.BlockSpec((1, c), lambda b, i: (0, 0))],
        out_specs=pl.BlockSpec((bt, c), lambda b, i: (b * nt + i, 0)),
        scratch_shapes=[pltpu.VMEM((32, c), F32)],
        compiler_params=_cparams(("arbitrary", "arbitrary")),
        name="conformer_conv",
    )(z, z, w, row(conv_b), row(ln_g), row(ln_b))


def _outproj_router_body(ya_ref, yb_ref, h_ref, w_ref, g_ref, wr_ref, br_ref, ho_ref, r_ref, cnt_ref,
                         xp_ref, *, bm, blocks_per_seq, n_real):
    half = ya_ref.shape[1]
    y = jnp.dot(ya_ref[...], w_ref[0:half, :], preferred_element_type=F32)
    y = y + jnp.dot(yb_ref[...], w_ref[half:2 * half, :], preferred_element_type=F32)
    pos0 = (pl.program_id(0) % blocks_per_seq) * bm
    hn = jnp.where(_row_valid(pos0, bm, n_real), h_ref[...] + y, 0.0)
    ho_ref[...] = hn

    ms = jnp.mean(hn * hn, axis=-1, keepdims=True)
    xn = hn * lax.rsqrt(ms + EPS) * g_ref[...]
    xp_ref[...] = _pack_pairs(xn)
    xh = xn.astype(BF16)
    xl = (xn - xh.astype(F32)).astype(BF16)
    t = jnp.dot(xh, wr_ref[...], preferred_element_type=F32)
    lg = (t[:, 0:LANES] + t[:, LANES:2 * LANES]
          + jnp.dot(xl, wr_ref[:, 0:LANES], preferred_element_type=F32) + br_ref[...])
    lane = lax.broadcasted_iota(I32, lg.shape, 1)
    lane_f = lane.astype(F32)
    big = float(LANES)

    is_grp = (lane >= N_EXPERTS) & (lane < N_EXPERTS + N_GROUPS)
    gl = jnp.where(is_grp, lg, -jnp.inf)
    gmax = jnp.max(gl, axis=1, keepdims=True)
    gidx = jnp.min(jnp.where(gl == gmax, lane_f, big), axis=1, keepdims=True) - N_EXPERTS
    g_prob = 1.0 / jnp.sum(jnp.where(is_grp, jnp.exp(lg - gmax), 0.0), axis=1, keepdims=True)

    lo = gidx * EXPERTS_PER_GROUP
    in_grp = (lane_f >= lo) & (lane_f < lo + EXPERTS_PER_GROUP)
    el = jnp.where(in_grp, lg, -jnp.inf)
    m1 = jnp.max(el, axis=1, keepdims=True)
    i1 = jnp.min(jnp.where(el == m1, lane_f, big), axis=1, keepdims=True)
    el2 = jnp.where(lane_f == i1, -jnp.inf, el)
    m2 = jnp.max(el2, axis=1, keepdims=True)
    i2 = jnp.min(jnp.where(el2 == m2, lane_f, big), axis=1, keepdims=True)
    e2 = jnp.exp(m2 - m1)
    w1 = g_prob / (1.0 + e2)
    w2 = g_prob * e2 / (1.0 + e2)

    @pl.when(pl.program_id(0) == 0)
    def _():
        cnt_ref[...] = jnp.zeros_like(cnt_ref)

    oh1 = lane_f == i1
    oh2 = lane_f == i2
    onehots = jnp.concatenate([jnp.where(oh1, 1.0, 0.0), jnp.where(oh2, 1.0, 0.0)], axis=1).astype(BF16)
    rr = lax.broadcasted_iota(I32, (bm, bm), 0)
    cc = lax.broadcasted_iota(I32, (bm, bm), 1)
    tri = jnp.where(rr >= cc, 1.0, 0.0).astype(BF16)
    csum = jnp.dot(tri, onehots, preferred_element_type=F32)
    c1 = csum[:, 0:LANES]
    c2 = csum[:, LANES:2 * LANES]
    tot1 = c1[bm - 1:bm, :]
    tot2 = c2[bm - 1:bm, :]
    before = cnt_ref[0:1, :]
    rank1 = jnp.sum(jnp.where(oh1, before + c1 - 1.0, 0.0), axis=1, keepdims=True)
    rank2 = jnp.sum(jnp.where(oh2, before + tot1 + c2 - 1.0, 0.0), axis=1, keepdims=True)
    cnt_ref[...] = jnp.broadcast_to(before + tot1 + tot2, cnt_ref.shape)

    r_ref[...] = jnp.where(lane == 0, i1, jnp.where(lane == 1, i2,
                           jnp.where(lane == 2, w1, jnp.where(lane == 3, w2,
                                     jnp.where(lane == 4, rank1, jnp.where(lane == 5, rank2, 0.0))))))


def _outproj_router(ya, yb, h, w_out, g_ffn, w_router, b_router, tp, n_real):
    n, d = h.shape
    half = ya.shape[1]
    bm = UNIT
    return pl.pallas_call(
        functools.partial(_outproj_router_body, bm=bm, blocks_per_seq=tp // bm, n_real=n_real),
        out_shape=[jax.ShapeDtypeStruct((n, d), F32), jax.ShapeDtypeStruct((n, LANES), F32),
                   jax.ShapeDtypeStruct((8, LANES), F32), jax.ShapeDtypeStruct((n, d // 2), U32)],
        grid=(n // bm,),
        in_specs=[pl.BlockSpec((bm, half), lambda i: (i, 0)),
                  pl.BlockSpec((bm, half), lambda i: (i, 0)),
                  pl.BlockSpec((bm, d), lambda i: (i, 0)),
                  pl.BlockSpec((2 * half, d), lambda i: (0, 0)),
                  pl.BlockSpec((1, d), lambda i: (0, 0)),
                  pl.BlockSpec((d, 2 * LANES), lambda i: (0, 0)),
                  pl.BlockSpec((1, LANES), lambda i: (0, 0))],
        out_specs=[pl.BlockSpec((bm, d), lambda i: (i, 0)),
                   pl.BlockSpec((bm, LANES), lambda i: (i, 0)),
                   pl.BlockSpec((8, LANES), lambda i: (0, 0)),
                   pl.BlockSpec((bm, d // 2), lambda i: (i, 0))],
        compiler_params=_cparams(("arbitrary",)),
        name="outproj_router",
    )(ya, yb, h, w_out, g_ffn, w_router, b_router)


def _pack_pairs(x):
    c = x.shape[1] // 2
    hi = lax.bitcast_convert_type(x[:, :c].astype(BF16).astype(F32), U32)
    lo = lax.bitcast_convert_type(x[:, c:].astype(BF16).astype(F32), U32)
    return hi | (lo >> 16)


def _unpack_pairs(p):
    hi = lax.bitcast_convert_type(p & jnp.uint32(0xFFFF0000), F32)
    lo = lax.bitcast_convert_type(p << 16, F32)
    return hi, lo


def _dispatch_body(meta_ref, dest_ref, xin_ref, xs_hbm, xp_ref, zero_ref, sem, zsem, *, tb):
    i = pl.program_id(0)
    slot = i % 2

    def drain(s):
        for _ in range(2):
            pltpu.make_async_copy(xp_ref.at[s], xs_hbm.at[pl.ds(0, tb)], sem.at[s]).wait()

    xp_ref[slot] = xin_ref[...]

    def issue(t, _):
        for k in range(2):
            d = dest_ref[0, 0, 2 * t + k]
            pltpu.make_async_copy(xp_ref.at[slot, pl.ds(t, 1)], xs_hbm.at[pl.ds(d, 1)],
                                  sem.at[slot]).start()
        return 0

    lax.fori_loop(0, tb, issue, 0, unroll=8)

    @pl.when(i > 0)
    def _():
        drain(1 - slot)

    @pl.when(i == pl.num_programs(0) - 1)
    def _():
        drain(slot)

    @pl.when(i == 0)
    def _():
        zero_ref[...] = jnp.zeros_like(zero_ref)

        def per_expert(e, _):
            cnt = meta_ref[e]
            start = meta_ref[N_EXPERTS + e]
            padded = meta_ref[2 * N_EXPERTS + e]

            def pad_copy(r):
                return pltpu.make_async_copy(zero_ref.at[pl.ds(0, 1)], xs_hbm.at[pl.ds(start + r, 1)], zsem)

            def zissue(r, _):
                pad_copy(r).start()
                return 0

            def zwait(r, _):
                pad_copy(r).wait()
                return 0

            lax.fori_loop(cnt, padded, zissue, 0)
            lax.fori_loop(cnt, padded, zwait, 0)
            return 0

        lax.fori_loop(0, N_EXPERTS, per_expert, 0)

        def tail_copy(blk):
            return pltpu.make_async_copy(zero_ref, xs_hbm.at[pl.ds(blk * MOE_BM, MOE_BM)], zsem)

        def tissue(blk, _):
            tail_copy(blk).start()
            return 0

        def twait(blk, _):
            tail_copy(blk).wait()
            return 0

        n_used = meta_ref[3 * N_EXPERTS]
        lax.fori_loop(n_used, xs_hbm.shape[0] // MOE_BM, tissue, 0)
        lax.fori_loop(n_used, xs_hbm.shape[0] // MOE_BM, twait, 0)


def _dispatch(xp, dest3, meta, n_rows):
    n, dp = xp.shape
    tb = UNIT
    return pl.pallas_call(
        functools.partial(_dispatch_body, tb=tb),
        out_shape=jax.ShapeDtypeStruct((n_rows, dp), U32),
        grid_spec=pltpu.PrefetchScalarGridSpec(
            num_scalar_prefetch=1,
            grid=(n // tb,),
            in_specs=[pl.BlockSpec((1, 1, 2 * tb), lambda i, m: (i, 0, 0), memory_space=pltpu.SMEM),
                      pl.BlockSpec((tb, dp), lambda i, m: (i, 0))],
            out_specs=pl.BlockSpec(memory_space=pl.ANY),
            scratch_shapes=[pltpu.VMEM((2, tb, dp), U32), pltpu.VMEM((MOE_BM, dp), U32),
                            pltpu.SemaphoreType.DMA((2,)), pltpu.SemaphoreType.DMA(())]),
        compiler_params=_cparams(("arbitrary",)),
        name="moe_dispatch",
    )(meta, dest3, xp)


def _experts_body(be_ref, nu_ref, xs_ref, wg_ref, wu_ref, wd_ref, ys_ref, wg_bf, wu_bf, wd_bf):
    i = pl.program_id(0)
    n_used = nu_ref[0]

    @pl.when(i < n_used)
    def _():
        prev = be_ref[jnp.maximum(i - 1, 0)]

        @pl.when((i == 0) | (be_ref[i] != prev))
        def _():
            wg_bf[...] = wg_ref[0, 0].astype(BF16)
            wu_bf[...] = wu_ref[0, 0].astype(BF16)
            wd_bf[...] = wd_ref[0, 0].astype(BF16)

        half = MOE_BM // 2
        kh = wg_bf.shape[0] // 2
        xns = []
        for r in range(2):
            hi, lo = _unpack_pairs(xs_ref[r * half:(r + 1) * half, :])
            xns.append((hi.astype(BF16), lo.astype(BF16)))

        def up(w_bf):
            return [jnp.dot(xa, w_bf[0:kh, :], preferred_element_type=F32)
                    + jnp.dot(xb, w_bf[kh:2 * kh, :], preferred_element_type=F32) for xa, xb in xns]

        hgs = up(wg_bf)
        hus = up(wu_bf)
        hids = [(_silu(hg) * hu).astype(BF16) for hg, hu in zip(hgs, hus)]
        for r in range(2):
            ys_ref[r * half:(r + 1) * half, :] = _pack_pairs(
                jnp.dot(hids[r], wd_bf[...], preferred_element_type=F32))

    @pl.when(i >= n_used)
    def _():
        ys_ref[...] = jnp.zeros_like(ys_ref)


def _experts(xs, w_gate, w_up, w_down, layer, block_expert, n_used):
    r, dp = xs.shape
    d = 2 * dp
    de = w_gate.shape[3]
    nb = r // MOE_BM

    def wblk(i, be, nu):
        return (layer, be[jnp.minimum(i, nu[0] - 1)], 0, 0)

    return pl.pallas_call(
        _experts_body,
        out_shape=jax.ShapeDtypeStruct((r, dp), U32),
        grid_spec=pltpu.PrefetchScalarGridSpec(
            num_scalar_prefetch=2,
            grid=(nb,),
            in_specs=[pl.BlockSpec((MOE_BM, dp), lambda i, be, nu: (i, 0)),
                      pl.BlockSpec((1, 1, d, de), wblk),
                      pl.BlockSpec((1, 1, d, de), wblk),
                      pl.BlockSpec((1, 1, de, d), wblk)],
            out_specs=pl.BlockSpec((MOE_BM, dp), lambda i, be, nu: (i, 0)),
            scratch_shapes=[pltpu.VMEM((d, de), BF16), pltpu.VMEM((d, de), BF16),
                            pltpu.VMEM((de, d), BF16)]),
        compiler_params=_cparams(("arbitrary",)),
        name="moe_experts",
    )(block_expert, n_used, xs, w_gate, w_up, w_down)


def _combine_body(dest_ref, dnext_ref, h_ref, r_ref, gf_ref, ys_hbm, o_ref, buf, sem,
                  *, tb, blocks_per_seq, n_real, final_norm):
    i = pl.program_id(0)
    slot = i % 2

    def gather(d_ref, s):
        def issue(t, _):
            for k in range(2):
                d = d_ref[0, 0, 2 * t + k]
                pltpu.make_async_copy(ys_hbm.at[pl.ds(d, 1)], buf.at[s, k, pl.ds(t, 1)], sem.at[s]).start()
            return 0

        lax.fori_loop(0, tb, issue, 0, unroll=8)

    @pl.when(i == 0)
    def _():
        gather(dest_ref, 0)

    @pl.when(i + 1 < pl.num_programs(0))
    def _():
        gather(dnext_ref, 1 - slot)

    for k in range(2):
        pltpu.make_async_copy(ys_hbm.at[pl.ds(0, tb)], buf.at[slot, k], sem.at[slot]).wait()

    r = r_ref[...]
    w1 = r[:, 2:3]
    w2 = r[:, 3:4]
    y1 = _unpack_pairs(buf[slot, 0])
    y2 = _unpack_pairs(buf[slot, 1])
    y = jnp.concatenate([w1 * y1[0] + w2 * y2[0], w1 * y1[1] + w2 * y2[1]], axis=1)
    pos0 = (i % blocks_per_seq) * tb
    hn = jnp.where(_row_valid(pos0, tb, n_real), h_ref[...] + y, 0.0)
    if final_norm:
        ms = jnp.mean(hn * hn, axis=-1, keepdims=True)
        hn = hn * lax.rsqrt(ms + EPS) * gf_ref[...]
    o_ref[...] = hn


def _combine(h, route, dest3, ys, g_final, tp, n_real, final_norm):
    n, d = h.shape
    tb = UNIT
    last = n // tb - 1
    return pl.pallas_call(
        functools.partial(_combine_body, tb=tb, blocks_per_seq=tp // tb, n_real=n_real,
                          final_norm=final_norm),
        out_shape=jax.ShapeDtypeStruct((n, d), F32),
        grid=(n // tb,),
        in_specs=[pl.BlockSpec((1, 1, 2 * tb), lambda i: (i, 0, 0), memory_space=pltpu.SMEM),
                  pl.BlockSpec((1, 1, 2 * tb), lambda i: (jnp.minimum(i + 1, last), 0, 0),
                               memory_space=pltpu.SMEM),
                  pl.BlockSpec((tb, d), lambda i: (i, 0)),
                  pl.BlockSpec((tb, LANES), lambda i: (i, 0)),
                  pl.BlockSpec((1, d), lambda i: (0, 0)),
                  pl.BlockSpec(memory_space=pl.ANY)],
        out_specs=pl.BlockSpec((tb, d), lambda i: (i, 0)),
        scratch_shapes=[pltpu.VMEM((2, 2, tb, d // 2), U32), pltpu.SemaphoreType.DMA((2,))],
        compiler_params=_cparams(("arbitrary",)),
        name="moe_combine",
    )(dest3, dest3, h, route, g_final, ys)


def _moe(h, xp, route, counts_f, w_gate, w_up, w_down, layer, g_final, tp, n_real, final_norm):
    n, d = h.shape
    n_pairs = 2 * n
    counts = counts_f[0, 0:N_EXPERTS].astype(I32)
    padded = (counts + MOE_BM - 1) // MOE_BM * MOE_BM
    p_ends = jnp.cumsum(padded)
    p_starts = p_ends - padded
    dest = jnp.take(p_starts, route[:, 0:2].astype(I32)) + route[:, 4:6].astype(I32)
    n_blocks = -(-(n_pairs + N_EXPERTS * (MOE_BM - 1)) // MOE_BM)
    block_row0 = jnp.arange(n_blocks, dtype=I32) * MOE_BM
    block_expert = jnp.minimum(jnp.sum((p_ends[None, :] <= block_row0[:, None]).astype(I32), axis=1),
                               N_EXPERTS - 1)
    n_used = (p_ends[-1:] // MOE_BM).astype(I32)
    meta = jnp.concatenate([counts, p_starts, padded, n_used]).astype(I32)
    dest3 = dest.astype(I32).reshape(n // UNIT, 1, 2 * UNIT)

    xs = _dispatch(xp, dest3, meta, n_blocks * MOE_BM)
    ys = _experts(xs, w_gate, w_up, w_down, layer, block_expert, n_used)
    return _combine(h, route, dest3, ys, g_final, tp, n_real, final_norm)


def _short_conv(x, halo_ref, w_ref, first):
    rows = x.shape[0]

    @pl.when(first)
    def _():
        halo_ref[...] = jnp.zeros_like(halo_ref)

    x_ext = jnp.concatenate([halo_ref[...], x], axis=0)
    halo_ref[...] = x[rows - 8:rows]
    acc = jnp.zeros_like(x)
    for j, tap in _shifted_taps(x_ext, 8, rows, SHORT_K):
        acc = acc + w_ref[j:j + 1, :] * tap
    return acc


def _bdot(a, b):
    return jnp.dot(a.astype(BF16), b.astype(BF16), preferred_element_type=F32)


def _gdn_pre_body(z_ref, sc_ref, st_ref, cw_ref, al_r_ref, dt_r_ref, al_c_ref, dt_c_ref,
                  u_ref, w_ref, qd_ref, kd_ref, at_ref, eg_ref, halo_ref):
    rows = z_ref.shape[0]
    hw = N_HEADS * HEAD_DIM
    x = _silu(_short_conv(z_ref[...].astype(F32), halo_ref, cw_ref, pl.program_id(1) == 0))

    sc = sc_ref[...]
    g_cols = -jnp.exp(al_r_ref[...]) * _softplus(sc + dt_r_ref[...])
    beta_cols = _sigmoid(sc)
    st = st_ref[...]
    g_rows = -jnp.exp(al_c_ref[:, 0:1]) * _softplus(st + dt_c_ref[:, 0:1])

    ri = lax.broadcasted_iota(I32, (rows, rows), 0)
    ci = lax.broadcasted_iota(I32, (rows, rows), 1)
    same64 = (ri >> 6) == (ci >> 6)
    same32 = (ri >> 5) == (ci >> 5)
    same16 = (ri >> 4) == (ci >> 4)
    lower = ri >= ci
    strict = ri > ci
    lane = lax.broadcasted_iota(I32, (rows, LANES), 1)
    eg_slab = jnp.zeros((rows, LANES), F32)
    scale = HEAD_DIM ** -0.5

    heads = range(N_HEADS)
    in_chunk_lower = same64 & lower
    in_chunk_upper = same64 & (ri <= ci)
    a_mats, nmats, rhss = [], [], []
    for h in heads:
        sl = slice(h * HEAD_DIM, (h + 1) * HEAD_DIM)
        q = x[:, sl]
        k = x[:, hw + h * HEAD_DIM:hw + (h + 1) * HEAD_DIM]
        v = x[:, 2 * hw + h * HEAD_DIM:2 * hw + (h + 1) * HEAD_DIM]
        q = q * lax.rsqrt(jnp.sum(q * q, axis=-1, keepdims=True) + EPS)
        k = k * lax.rsqrt(jnp.sum(k * k, axis=-1, keepdims=True) + EPS)
        g_col = g_cols[:, h:h + 1]
        beta = beta_cols[:, N_HEADS + h:N_HEADS + h + 1]
        g_row = g_rows[h:h + 1, :]

        gc_col = jnp.sum(jnp.where(in_chunk_lower, g_row, 0.0), axis=1, keepdims=True)
        gc_row = jnp.sum(jnp.where(in_chunk_upper, g_col, 0.0), axis=0, keepdims=True)
        gtot_col = jnp.sum(jnp.where(same64, g_row, 0.0), axis=1, keepdims=True)
        decay = jnp.exp(jnp.where(in_chunk_lower, gc_col - gc_row, NEG))
        eg = jnp.exp(gc_col)
        ekd = jnp.exp(gtot_col - gc_col)

        kb = k * beta
        kbf = k.astype(BF16)
        kk = lax.dot_general(kb.astype(BF16), kbf, (((1,), (1,)), ((), ())), preferred_element_type=F32)
        qs = q * scale
        qk = lax.dot_general(qs.astype(BF16), kbf, (((1,), (1,)), ((), ())), preferred_element_type=F32)
        attn = qk * decay
        a_mats.append(jnp.where(strict, kk * decay, 0.0))
        rhss.append(jnp.concatenate([v * beta, kb * eg], axis=1))
        qd_ref[:, sl] = (qs * eg).astype(BF16)
        kd_ref[:, sl] = (k * ekd).astype(BF16)
        for c in range(rows // CHUNK):
            cs = slice(c * CHUNK, (c + 1) * CHUNK)
            at_ref[h, cs, :] = attn[cs, cs].astype(BF16)
        eg_slab = jnp.where(lane == h, eg, eg_slab)
    eg_ref[...] = eg_slab

    nmats = [jnp.where(same16, -a, 0.0) for a in a_mats]
    qqs = nmats
    for _ in range(3):
        qqs = [_bdot(qq, qq) for qq in qqs]
        prods = [_bdot(qq, nm) for qq, nm in zip(qqs, nmats)]
        nmats = [nm + qq + pr for nm, qq, pr in zip(nmats, qqs, prods)]
    for level_mask, inner_mask in ((same32, same16), (same64, same32)):
        sel = level_mask & jnp.logical_not(inner_mask)
        offs = [jnp.where(sel, a, 0.0) for a in a_mats]
        bmats = [off + _bdot(nm, off) for nm, off in zip(nmats, offs)]
        prods = [_bdot(bm, nm) for bm, nm in zip(bmats, nmats)]
        nmats = [nm - bm - pr for nm, bm, pr in zip(nmats, bmats, prods)]

    for h in heads:
        sl = slice(h * HEAD_DIM, (h + 1) * HEAD_DIM)
        uw = rhss[h] + _bdot(nmats[h], rhss[h])
        u_ref[:, sl] = uw[:, 0:HEAD_DIM]
        w_ref[:, sl] = uw[:, HEAD_DIM:2 * HEAD_DIM].astype(BF16)


def _gdn_pre(z, sc, st, conv_w, a_log, dt_bias, bsz, tp):
    n = z.shape[0]
    hw = N_HEADS * HEAD_DIM
    rows = UNIT
    nu = tp // rows
    pad_r = lambda v: jnp.pad(v.astype(F32), (0, LANES - N_HEADS)).reshape(1, LANES)
    pad_c = lambda v: jnp.broadcast_to(jnp.pad(v.astype(F32), (0, 16 - N_HEADS))[:, None], (16, LANES))
    outs = pl.pallas_call(
        _gdn_pre_body,
        out_shape=[jax.ShapeDtypeStruct((n, hw), F32),
                   jax.ShapeDtypeStruct((n, hw), BF16),
                   jax.ShapeDtypeStruct((n, hw), BF16),
                   jax.ShapeDtypeStruct((n, hw), BF16),
                   jax.ShapeDtypeStruct((N_HEADS, n, CHUNK), BF16),
                   jax.ShapeDtypeStruct((n, LANES), F32)],
        grid=(bsz, nu),
        in_specs=[pl.BlockSpec((rows, 3 * hw), lambda b, i: (b * nu + i, 0)),
                  pl.BlockSpec((rows, LANES), lambda b, i: (b * nu + i, 0)),
                  pl.BlockSpec((16, rows), lambda b, i: (0, b * nu + i)),
                  pl.BlockSpec((8, 3 * hw), lambda b, i: (0, 0)),
                  pl.BlockSpec((1, LANES), lambda b, i: (0, 0)),
                  pl.BlockSpec((1, LANES), lambda b, i: (0, 0)),
                  pl.BlockSpec((16, LANES), lambda b, i: (0, 0)),
                  pl.BlockSpec((16, LANES), lambda b, i: (0, 0))],
        out_specs=[pl.BlockSpec((rows, hw), lambda b, i: (b * nu + i, 0)),
                   pl.BlockSpec((rows, hw), lambda b, i: (b * nu + i, 0)),
                   pl.BlockSpec((rows, hw), lambda b, i: (b * nu + i, 0)),
                   pl.BlockSpec((rows, hw), lambda b, i: (b * nu + i, 0)),
                   pl.BlockSpec((N_HEADS, rows, CHUNK), lambda b, i: (0, b * nu + i, 0)),
                   pl.BlockSpec((rows, LANES), lambda b, i: (b * nu + i, 0))],
        scratch_shapes=[pltpu.VMEM((8, 3 * hw), F32)],
        compiler_params=_cparams(("arbitrary", "arbitrary")),
        name="gdn_pre",
    )(z, sc, st, jnp.pad(conv_w.astype(F32), ((0, 8 - SHORT_K), (0, 0))),
      pad_r(a_log), pad_r(dt_bias), pad_c(a_log), pad_c(dt_bias))
    return outs


def _gdn_scan_body(u_ref, w_ref, qd_ref, kd_ref, at_ref, eg_ref, z_ref, gn_ref, o_ref, s_ref, *, bsz):
    @pl.when(pl.program_id(0) == 0)
    def _():
        s_ref[...] = jnp.zeros_like(s_ref)

    for b in range(bsz):
        for h in range(N_HEADS):
            sl = slice(h * HEAD_DIM, (h + 1) * HEAD_DIM)
            s = s_ref[b * N_HEADS + h]
            wq = jnp.concatenate([w_ref[b, :, sl], qd_ref[b, :, sl]], axis=0)
            r = jnp.dot(wq, s.astype(BF16), preferred_element_type=F32)
            v_new = u_ref[b, :, sl] - r[0:CHUNK]
            vb = v_new.astype(BF16)
            o = r[CHUNK:2 * CHUNK] + jnp.dot(at_ref[h, b], vb, preferred_element_type=F32)
            gt = eg_ref[b, CHUNK - 1:CHUNK, h:h + 1]
            s_ref[b * N_HEADS + h] = s * gt + lax.dot_general(
                kd_ref[b, :, sl], vb, (((0,), (0,)), ((), ())), preferred_element_type=F32)
            on = o * lax.rsqrt(jnp.mean(o * o, axis=-1, keepdims=True) + EPS) * gn_ref[...]
            o_ref[b, :, sl] = (on * _silu(z_ref[b, :, sl].astype(F32))).astype(o_ref.dtype)


def _gdn_scan(u, w, qd, kd, attn, eg, z, out_norm_g, bsz, tp, z_col):
    hw = N_HEADS * HEAD_DIM
    nc = tp // CHUNK
    v3 = lambda a: a.reshape(bsz, tp, a.shape[-1])
    blk3 = pl.BlockSpec((bsz, CHUNK, hw), lambda c: (0, c, 0))
    return pl.pallas_call(
        functools.partial(_gdn_scan_body, bsz=bsz),
        out_shape=jax.ShapeDtypeStruct((bsz, tp, hw), BF16),
        grid=(nc,),
        in_specs=[blk3, blk3, blk3, blk3,
                  pl.BlockSpec((N_HEADS, bsz, CHUNK, CHUNK), lambda c: (0, 0, c, 0)),
                  pl.BlockSpec((bsz, CHUNK, LANES), lambda c: (0, c, 0)),
                  pl.BlockSpec((bsz, CHUNK, hw), lambda c: (0, c, z_col)),
                  pl.BlockSpec((1, HEAD_DIM), lambda c: (0, 0))],
        out_specs=blk3,
        scratch_shapes=[pltpu.VMEM((bsz * N_HEADS, HEAD_DIM, HEAD_DIM), F32)],
        compiler_params=_cparams(("arbitrary",)),
        name="gdn_scan",
    )(v3(u), v3(w), v3(qd), v3(kd), attn.reshape(N_HEADS, bsz, tp, CHUNK), v3(eg), v3(z),
      out_norm_g.astype(F32).reshape(1, HEAD_DIM)).reshape(bsz * tp, hw)


def _lru_body(x_ref, gate_ref, cw_ref, cb_ref, wr_ref, br_ref, wi_ref, bi_ref, lam_ref, o_ref,
              halo_ref, hc_ref, *, bt, n_real):
    first = pl.program_id(1) == 0

    @pl.when(first)
    def _():
        hc_ref[...] = jnp.zeros_like(hc_ref)

    x = _short_conv(x_ref[...].astype(F32), halo_ref, cw_ref, first) + cb_ref[...]
    nblk = wr_ref.shape[0]
    bd = wr_ref.shape[1]
    xb = x.astype(BF16)
    rg = jnp.concatenate([jnp.dot(xb[:, n * bd:(n + 1) * bd], wr_ref[n], preferred_element_type=F32)
                          for n in range(nblk)], axis=1)
    ig = jnp.concatenate([jnp.dot(xb[:, n * bd:(n + 1) * bd], wi_ref[n], preferred_element_type=F32)
                          for n in range(nblk)], axis=1)
    r = _sigmoid(rg + br_ref[...])
    ig = _sigmoid(ig + bi_ref[...])
    log_a = (-LRU_C * _softplus(-lam_ref[...])) * r
    a = jnp.exp(log_a)
    th = jnp.tanh(log_a)
    b = jnp.sqrt(-2.0 * th / (1.0 - th)) * (ig * x)
    b = jnp.where(_row_valid(pl.program_id(1) * bt, bt, n_real), b, 0.0)

    row = lax.broadcasted_iota(I32, (bt, 1), 0)
    d = 1
    while d < bt:
        keep = row >= d
        a_sh = jnp.where(keep, pltpu.roll(a, d, 0), 1.0)
        b_sh = jnp.where(keep, pltpu.roll(b, d, 0), 0.0)
        b = a * b_sh + b
        a = a * a_sh
        d *= 2
    hs = a * hc_ref[0:1, :] + b
    hc_ref[...] = jnp.broadcast_to(hs[bt - 1:bt, :], hc_ref.shape)
    o_ref[...] = (hs * jax.nn.gelu(gate_ref[...].astype(F32))).astype(o_ref.dtype)


def _lru(z, conv_w, conv_b, w_rg, b_rg, w_ig, b_ig, lam, bsz, tp, n_real, x_col, gate_col):
    n = z.shape[0]
    c = conv_w.shape[1]
    bt = UNIT
    nt = tp // bt
    row = lambda v: v.astype(F32).reshape(1, c)
    wspec = pl.BlockSpec(w_rg.shape, lambda b, i: (0, 0, 0))
    vspec = pl.BlockSpec((1, c), lambda b, i: (0, 0))
    return pl.pallas_call(
        functools.partial(_lru_body, bt=bt, n_real=n_real),
        out_shape=jax.ShapeDtypeStruct((n, c), BF16),
        grid=(bsz, nt),
        in_specs=[pl.BlockSpec((bt, c), lambda b, i: (b * nt + i, x_col)),
                  pl.BlockSpec((bt, c), lambda b, i: (b * nt + i, gate_col)),
                  pl.BlockSpec((8, c), lambda b, i: (0, 0)),
                  vspec, wspec, vspec, wspec, vspec, vspec],
        out_specs=pl.BlockSpec((bt, c), lambda b, i: (b * nt + i, 0)),
        scratch_shapes=[pltpu.VMEM((8, c), F32), pltpu.VMEM((8, c), F32)],
        compiler_params=_cparams(("arbitrary", "arbitrary")),
        name="rg_lru",
    )(z, z, jnp.pad(conv_w.astype(F32), ((0, 8 - SHORT_K), (0, 0))), row(conv_b),
      w_rg.astype(BF16), row(b_rg), w_ig.astype(BF16), row(b_ig), row(lam))


def _pick(n, candidates):
    for c in candidates:
        if n % c == 0:
            return c
    raise ValueError(f"no block size in {candidates} divides {n}")


def _pack_weights_body(a_ref, b_ref, o_ref, *, scaled_tiles, scale, head_tiles, gap):
    j = pl.program_id(0)
    tile = o_ref.shape[1]

    @pl.when(j < scaled_tiles)
    def _():
        o_ref[...] = (a_ref[...] * scale).astype(BF16)

    @pl.when((j >= scaled_tiles) & (j < head_tiles))
    def _():
        o_ref[...] = a_ref[...].astype(BF16)

    @pl.when(j >= head_tiles)
    def _():
        x = jnp.concatenate([a_ref[...], b_ref[...]], axis=1)
        o_ref[...] = x[:, gap:gap + tile].astype(BF16)


def _pack_inproj_weights(w_in, head, gap, scaled=0, scale=1.0):
    d, total = w_in.shape
    tile = 512
    n_out = total - gap
    return pl.pallas_call(
        functools.partial(_pack_weights_body, scaled_tiles=scaled // tile, scale=scale,
                          head_tiles=head // tile, gap=gap),
        out_shape=jax.ShapeDtypeStruct((d, n_out), BF16),
        grid=(n_out // tile,),
        in_specs=[pl.BlockSpec((d, tile), lambda j: (0, j)),
                  pl.BlockSpec((d, LANES), lambda j: (0, (tile // LANES) * (j + 1)))],
        out_specs=pl.BlockSpec((d, tile), lambda j: (0, j)),
        compiler_params=_cparams(("arbitrary",)),
        name="pack_inproj_weights",
    )(w_in, w_in)


def _small_weights(cols):
    k = cols.shape[1]
    return (jnp.pad(cols.T, ((0, 16 - k), (0, 0))).astype(BF16),
            jnp.pad(cols, ((0, 0), (0, LANES - k))).astype(BF16))


def _router_weights(w_group, b_group, w_expert, b_expert):
    w = jnp.concatenate([w_expert, w_group], axis=1).astype(F32)
    b = jnp.concatenate([b_expert, b_group]).astype(F32)
    k = w.shape[1]
    w = jnp.pad(w, ((0, 0), (0, LANES - k)))
    w_hi = w.astype(BF16)
    w_lo = (w - w_hi.astype(F32)).astype(BF16)
    return jnp.concatenate([w_hi, w_lo], axis=1), jnp.pad(b, (0, LANES - k)).reshape(1, LANES)


def kernel(x, meta_tokens, norm_mix_g, norm_ffn_g, norm_final_g, ab_w_in, ab_forget_b, ab_conv_w, ab_conv_b, ab_ln_g, ab_ln_b, ab_w_out, cd_w_in, cd_qkv_conv_w, cd_a_log, cd_dt_bias, cd_out_norm_g, cd_lru_conv_w, cd_lru_conv_b, cd_w_rg, cd_b_rg, cd_w_ig, cd_b_ig, cd_lru_lambda, cd_w_out, moe_w_group, moe_b_group, moe_w_expert, moe_b_expert, moe_w_gate, moe_w_up, moe_w_down):
    bsz, seq, d = x.shape
    depth = norm_mix_g.shape[0]
    n_real = N_META + seq
    tp = -(-(OFF + seq) // UNIT) * UNIT
    n = bsz * tp
    hw = N_HEADS * HEAD_DIM
    bm = _pick(n, (768, 512, 256))
    bn = 1024

    h = _embed(x.astype(F32).reshape(bsz * seq, d), meta_tokens.astype(F32), bsz, seq, tp)
    row = lambda v: v.astype(F32).reshape(1, -1)

    for layer in range(depth):
        i = layer // 2
        g_mix = row(norm_mix_g[layer])
        if layer % 2 == 0:
            w_in = ab_w_in[i]
            w_main = _pack_inproj_weights(w_in.astype(F32), 3 * hw, N_HEADS, scaled=hw,
                                          scale=LOG2E * HEAD_DIM ** -0.5)
            wst, wsc = _small_weights(w_in[:, 3 * hw:3 * hw + N_HEADS])
            z, _, f_slab = _norm_inproj(h, g_mix, w_main, wst, wsc, bm, bn)
            kx, vx = _fox_prep(z, f_slab, ab_forget_b[i], bsz, tp, n_real)
            ya = _fox_attn(z, kx, vx, bsz, tp)
            yb = _conformer(z, ab_conv_w[i], ab_conv_b[i], ab_ln_g[i], ab_ln_b[i], bsz, tp, 3, 4)
            w_out = ab_w_out[i].astype(BF16)
        else:
            w_in = cd_w_in[i]
            w_main = _pack_inproj_weights(w_in.astype(F32), 3 * hw, 2 * N_HEADS)
            wst, wsc = _small_weights(w_in[:, 3 * hw:3 * hw + 2 * N_HEADS])
            z, zt, zc = _norm_inproj(h, g_mix, w_main, wst, wsc, bm, bn)
            u, w, qd, kd, attn, eg = _gdn_pre(z, zc, zt, cd_qkv_conv_w[i], cd_a_log[i], cd_dt_bias[i],
                                              bsz, tp)
            ya = _gdn_scan(u, w, qd, kd, attn, eg, z, cd_out_norm_g[i], bsz, tp, 3)
            yb = _lru(z, cd_lru_conv_w[i], cd_lru_conv_b[i], cd_w_rg[i], cd_b_rg[i], cd_w_ig[i],
                      cd_b_ig[i], cd_lru_lambda[i], bsz, tp, n_real, 4, 5)
            w_out = cd_w_out[i].astype(BF16)
        g_ffn = row(norm_ffn_g[layer])
        w_r, b_r = _router_weights(moe_w_group[layer], moe_b_group[layer],
                                   moe_w_expert[layer], moe_b_expert[layer])
        h, route, counts, xp = _outproj_router(ya, yb, h, w_out, g_ffn, w_r, b_r, tp, n_real)
        h = _moe(h, xp, route, counts, moe_w_gate, moe_w_up, moe_w_down, layer,
                 row(norm_final_g), tp, n_real, final_norm=(layer == depth - 1))
    return h.reshape(bsz, tp, d)[:, OFF:OFF + seq].astype(x.dtype)
```

```python
import functools

import jax
import jax.numpy as jnp
from jax import lax
from jax.experimental import pallas as pl
from jax.experimental.pallas import tpu as pltpu

F32 = jnp.float32
BF16 = jnp.bfloat16
I32 = jnp.int32
U32 = jnp.uint32

EPS = 1e-6
N_META = 16
CHUNK = 64
FRONT = CHUNK - N_META
OFF = FRONT + N_META
LANES = 128
UNIT = 256
HEAD_DIM = 128
N_HEADS = 8
CONF_K = 31
SHORT_K = 4
LRU_C = 8.0
N_GROUPS = 4
EXPERTS_PER_GROUP = 8
N_EXPERTS = N_GROUPS * EXPERTS_PER_GROUP
MOE_BM = 256
NEG = -1e30
LOG2E = 1.4426950408889634
VMEM_LIMIT = 56 * 1024 * 1024


def _cparams(sem):
    return pltpu.CompilerParams(dimension_semantics=sem, vmem_limit_bytes=VMEM_LIMIT)


def _row_valid(pos0, rows, n_real):
    pos = pos0 + lax.broadcasted_iota(I32, (rows, 1), 0)
    return (pos >= FRONT) & (pos < FRONT + n_real)


def _sigmoid(x):
    return 1.0 / (1.0 + jnp.exp(-x))


def _softplus(x):
    return jnp.maximum(x, 0.0) + jnp.log1p(jnp.exp(-jnp.abs(x)))


def _silu(x):
    return x * _sigmoid(x)


def _embed_body(x_ref, meta_ref, h_hbm, zero_ref, sem, *, seq, tp, rows):
    b = pl.program_id(0)
    i = pl.program_id(1)
    base = b * tp
    tail = tp - OFF - seq
    frames = pltpu.make_async_copy(x_ref, h_hbm.at[pl.ds(base + OFF + i * rows, rows)], sem.at[0])
    frames.start()

    @pl.when(i == 0)
    def _():
        zero_ref[...] = jnp.zeros_like(zero_ref)
        copies = [
            pltpu.make_async_copy(meta_ref, h_hbm.at[pl.ds(base + FRONT, N_META)], sem.at[1]),
            pltpu.make_async_copy(zero_ref.at[pl.ds(0, FRONT)], h_hbm.at[pl.ds(base, FRONT)], sem.at[2]),
        ]
        if tail:
            copies.append(pltpu.make_async_copy(zero_ref.at[pl.ds(0, tail)],
                                                h_hbm.at[pl.ds(base + OFF + seq, tail)], sem.at[3]))
        for c in copies:
            c.start()
        for c in copies:
            c.wait()

    frames.wait()


def _embed(x2d, meta_tokens, bsz, seq, tp):
    d = x2d.shape[1]
    rows = _pick(seq, (1024, 512, 256, 128, 64, 32, 16, 8))
    per_seq = seq // rows
    return pl.pallas_call(
        functools.partial(_embed_body, seq=seq, tp=tp, rows=rows),
        out_shape=jax.ShapeDtypeStruct((bsz * tp, d), F32),
        grid=(bsz, per_seq),
        in_specs=[pl.BlockSpec((rows, d), lambda b, i: (b * per_seq + i, 0)),
                  pl.BlockSpec((N_META, d), lambda b, i: (0, 0))],
        out_specs=pl.BlockSpec(memory_space=pl.ANY),
        scratch_shapes=[pltpu.VMEM((max(FRONT, tp - OFF - seq), d), F32), pltpu.SemaphoreType.DMA((4,))],
        compiler_params=_cparams(("arbitrary", "arbitrary")),
        name="embed",
    )(x2d, meta_tokens)


def _norm_inproj_body(h_ref, g_ref, w_ref, wst_ref, wsc_ref, o_ref, ot_ref, oc_ref, xn_ref):
    @pl.when(pl.program_id(1) == 0)
    def _():
        x = h_ref[...]
        ms = jnp.mean(x * x, axis=-1, keepdims=True)
        xn = (x * lax.rsqrt(ms + EPS) * g_ref[...]).astype(BF16)
        xn_ref[...] = xn
        ot_ref[...] = lax.dot_general(wst_ref[...], xn, (((1,), (1,)), ((), ())),
                                      preferred_element_type=F32)
        oc_ref[...] = jnp.dot(xn, wsc_ref[...], preferred_element_type=F32)

    o_ref[...] = jnp.dot(xn_ref[...], w_ref[...], preferred_element_type=F32).astype(o_ref.dtype)


def _norm_inproj(h, g, w_main, w_small_t, w_small_c, bm, bn):
    n, d = h.shape
    nw = w_main.shape[1]
    return pl.pallas_call(
        _norm_inproj_body,
        out_shape=[jax.ShapeDtypeStruct((n, nw), BF16),
                   jax.ShapeDtypeStruct((16, n), F32),
                   jax.ShapeDtypeStruct((n, LANES), F32)],
        grid=(n // bm, nw // bn),
        in_specs=[pl.BlockSpec((bm, d), lambda i, j: (i, 0)),
                  pl.BlockSpec((1, d), lambda i, j: (0, 0)),
                  pl.BlockSpec((d, bn), lambda i, j: (0, j)),
                  pl.BlockSpec((16, d), lambda i, j: (0, 0)),
                  pl.BlockSpec((d, LANES), lambda i, j: (0, 0))],
        out_specs=[pl.BlockSpec((bm, bn), lambda i, j: (i, j)),
                   pl.BlockSpec((16, bm), lambda i, j: (0, i)),
                   pl.BlockSpec((bm, LANES), lambda i, j: (i, 0))],
        scratch_shapes=[pltpu.VMEM((bm, d), BF16)],
        compiler_params=_cparams(("arbitrary", "arbitrary")),
        name="norm_inproj",
    )(h, g, w_main, w_small_t, w_small_c)


def _split3(x):
    p1 = x.astype(BF16)
    r1 = x - p1.astype(F32)
    p2 = r1.astype(BF16)
    p3 = (r1 - p2.astype(F32)).astype(BF16)
    return p1, p2, p3


def _fox_prep_body(k_ref, v_ref, f_ref, fb_ref, kx_ref, vx_ref, carry_ref, *, rows, n_real):
    i = pl.program_id(1)

    @pl.when(i == 0)
    def _():
        carry_ref[...] = jnp.zeros_like(carry_ref)

    f = f_ref[...] + fb_ref[...]
    lf = jnp.minimum(f, 0.0) - jnp.log1p(jnp.exp(-jnp.abs(f)))
    pos = i * rows + lax.broadcasted_iota(I32, (rows, 1), 0)
    lf = jnp.where((pos >= FRONT) & (pos < FRONT + n_real), lf, 0.0)
    rr = lax.broadcasted_iota(I32, (rows, rows), 0)
    cc = lax.broadcasted_iota(I32, (rows, rows), 1)
    tri = jnp.where(rr >= cc, 1.0, 0.0).astype(BF16)
    c = carry_ref[0:1, :]
    for piece in _split3(lf):
        c = c + jnp.dot(tri, piece, preferred_element_type=F32)
    carry_ref[...] = jnp.broadcast_to(c[rows - 1:rows, :], carry_ref.shape)
    cs = jnp.where(pos < FRONT, NEG, -LOG2E * c)
    p1, p2, p3 = (p.astype(F32) for p in _split3(cs))
    lane = lax.broadcasted_iota(I32, (rows, LANES), 1)
    ones_col = jnp.where(lane == 0, 1.0, 0.0).astype(BF16)
    for h in range(N_HEADS):
        ext = jnp.where(lane == 0, p1[:, h:h + 1],
                        jnp.where(lane == 1, p2[:, h:h + 1], jnp.where(lane == 2, p3[:, h:h + 1], 0.0)))
        kx_ref[:, (2 * h) * HEAD_DIM:(2 * h + 1) * HEAD_DIM] = k_ref[:, h * HEAD_DIM:(h + 1) * HEAD_DIM]
        kx_ref[:, (2 * h + 1) * HEAD_DIM:(2 * h + 2) * HEAD_DIM] = ext.astype(BF16)
        vx_ref[:, (2 * h) * HEAD_DIM:(2 * h + 1) * HEAD_DIM] = v_ref[:, h * HEAD_DIM:(h + 1) * HEAD_DIM]
        vx_ref[:, (2 * h + 1) * HEAD_DIM:(2 * h + 2) * HEAD_DIM] = ones_col


def _fox_prep(z, f_slab, forget_b, bsz, tp, n_real):
    n = z.shape[0]
    hw = N_HEADS * HEAD_DIM
    rows = UNIT
    nt = tp // rows
    fb = jnp.pad(forget_b.astype(F32), (0, LANES - N_HEADS)).reshape(1, LANES)
    return pl.pallas_call(
        functools.partial(_fox_prep_body, rows=rows, n_real=n_real),
        out_shape=[jax.ShapeDtypeStruct((n, 2 * hw), BF16), jax.ShapeDtypeStruct((n, 2 * hw), BF16)],
        grid=(bsz, nt),
        in_specs=[pl.BlockSpec((rows, hw), lambda b, i: (b * nt + i, 1)),
                  pl.BlockSpec((rows, hw), lambda b, i: (b * nt + i, 2)),
                  pl.BlockSpec((rows, LANES), lambda b, i: (b * nt + i, 0)),
                  pl.BlockSpec((1, LANES), lambda b, i: (0, 0))],
        out_specs=[pl.BlockSpec((rows, 2 * hw), lambda b, i: (b * nt + i, 0)),
                   pl.BlockSpec((rows, 2 * hw), lambda b, i: (b * nt + i, 0))],
        scratch_shapes=[pltpu.VMEM((8, LANES), F32)],
        compiler_params=_cparams(("arbitrary", "arbitrary")),
        name="fox_prep",
    )(z, z, f_slab, fb)


def _fox_attn_body(q_ref, k_ref, v_ref, o_ref, *, blk, n_split):
    qi = pl.program_id(2)
    half = blk // n_split
    lane = lax.broadcasted_iota(I32, (blk, HEAD_DIM), 1)
    q = jnp.concatenate([q_ref[...], jnp.where(lane < 3, 1.0, 0.0).astype(BF16)], axis=1)
    qs = tuple(q[r * half:(r + 1) * half] for r in range(n_split))

    def step(js, carry, diag_last):
        starts = [pl.multiple_of(j * blk, blk) for j in js]
        ss = []
        for r, qh in enumerate(qs):
            srow = []
            for b, st in enumerate(starts):
                on_diag = diag_last and b == len(js) - 1
                ncol = (r + 1) * half if on_diag else blk
                s = lax.dot_general(qh, k_ref[pl.ds(st, ncol), :], (((1,), (1,)), ((), ())),
                                    preferred_element_type=F32)
                if on_diag:
                    row = r * half + lax.broadcasted_iota(I32, (half, ncol), 0)
                    col = lax.broadcasted_iota(I32, (half, ncol), 1)
                    s = jnp.where(col <= row, s, NEG)
                srow.append((s, st, ncol))
            ss.append(srow)
        out = []
        for r in range(n_split):
            m = carry[2 * r]
            m_new = m
            for s, _, _ in ss[r]:
                m_new = jnp.maximum(m_new, jnp.max(s, axis=1, keepdims=True))
            acc = jnp.exp2(m - m_new) * carry[2 * r + 1]
            for s, st, ncol in ss[r]:
                acc = acc + jnp.dot(jnp.exp2(s - m_new).astype(BF16), v_ref[pl.ds(st, ncol), :],
                                    preferred_element_type=F32)
            out += [m_new, acc]
        return tuple(out)

    init = (jnp.full((half, 1), NEG, F32), jnp.zeros((half, 2 * HEAD_DIM), F32)) * n_split
    carry = lax.fori_loop(0, qi // 2, lambda j, c: step((2 * j, 2 * j + 1), c, False), init)
    res = lax.cond(qi % 2 == 1, lambda c: step((qi - 1, qi), c, True),
                   lambda c: step((qi,), c, True), carry)
    for r in range(n_split):
        acc = res[2 * r + 1]
        o_ref[r * half:(r + 1) * half, :] = (acc[:, 0:HEAD_DIM] / acc[:, HEAD_DIM:HEAD_DIM + 1]
                                             ).astype(o_ref.dtype)


def _fox_attn(z, kx, vx, bsz, tp):
    n = z.shape[0]
    blk = _pick(tp, (768, 512, 256))
    nq = tp // blk
    return pl.pallas_call(
        functools.partial(_fox_attn_body, blk=blk, n_split=2),
        out_shape=jax.ShapeDtypeStruct((n, N_HEADS * HEAD_DIM), BF16),
        grid=(bsz, N_HEADS, nq),
        in_specs=[pl.BlockSpec((blk, HEAD_DIM), lambda b, h, i: (b * nq + i, h)),
                  pl.BlockSpec((tp, 2 * HEAD_DIM), lambda b, h, i: (b, h)),
                  pl.BlockSpec((tp, 2 * HEAD_DIM), lambda b, h, i: (b, h))],
        out_specs=pl.BlockSpec((blk, HEAD_DIM), lambda b, h, i: (b * nq + i, h)),
        compiler_params=_cparams(("arbitrary", "arbitrary", "arbitrary")),
        name="fox_attn",
    )(z, kx, vx)


def _shifted_taps(x_ext, halo, rows, n_taps):
    for r in range(min(8, n_taps)):
        rolled = x_ext if r == 0 else pltpu.roll(x_ext, r, 0)
        for q in range(halo // 8):
            s = 8 * q + r
            if s > n_taps - 1:
                continue
            yield n_taps - 1 - s, rolled[halo - 8 * q:halo - 8 * q + rows]


def _conformer_body(a_ref, b_ref, w_ref, cb_ref, lg_ref, lb_ref, o_ref, halo_ref, *, bt):
    halo = halo_ref.shape[0]

    @pl.when(pl.program_id(1) == 0)
    def _():
        halo_ref[...] = jnp.zeros_like(halo_ref)

    u = a_ref[...].astype(F32) * _sigmoid(b_ref[...].astype(F32))
    x_ext = jnp.concatenate([halo_ref[...], u], axis=0)
    halo_ref[...] = u[bt - halo:bt]
    acc = jnp.zeros_like(u)
    for j, tap in _shifted_taps(x_ext, halo, bt, CONF_K):
        acc = acc + w_ref[j:j + 1, :] * tap
    y = acc + cb_ref[...]
    mu = jnp.mean(y, axis=-1, keepdims=True)
    yc = y - mu
    var = jnp.mean(yc * yc, axis=-1, keepdims=True)
    yn = yc * lax.rsqrt(var + EPS) * lg_ref[...] + lb_ref[...]
    o_ref[...] = _silu(yn).astype(o_ref.dtype)


def _conformer(z, conv_w, conv_b, ln_g, ln_b, bsz, tp, col_a, col_b):
    n = z.shape[0]
    c = conv_w.shape[1]
    bt = UNIT
    nt = tp // bt
    w = jnp.pad(conv_w.astype(F32), ((0, 32 - CONF_K), (0, 0)))
    row = lambda v: v.astype(F32).reshape(1, c)
    return pl.pallas_call(
        functools.partial(_conformer_body, bt=bt),
        out_shape=jax.ShapeDtypeStruct((n, c), BF16),
        grid=(bsz, nt),
        in_specs=[pl.BlockSpec((bt, c), lambda b, i: (b * nt + i, col_a)),
                  pl.BlockSpec((bt, c), lambda b, i: (b * nt + i, col_b)),
                  pl.BlockSpec((32, c), lambda b, i: (0, 0)),
                  pl.BlockSpec((1, c), lambda b, i: (0, 0)),
                  pl.BlockSpec((1, c), lambda b, i: (0, 0)),
                  pl.BlockSpec((1, c), lambda b, i: (0, 0))],
        out_specs=pl.BlockSpec((bt, c), lambda b, i: (b * nt + i, 0)),
        scratch_shapes=[pltpu.VMEM((32, c), F32)],
        compiler_params=_cparams(("arbitrary", "arbitrary")),
        name="conformer_conv",
    )(z, z, w, row(conv_b), row(ln_g), row(ln_b))


def _outproj_router_body(ya_ref, yb_ref, h_ref, w_ref, g_ref, wr_ref, br_ref, ho_ref, r_ref, cnt_ref,
                         xp_ref, *, bm, blocks_per_seq, n_real):
    half = ya_ref.shape[1]
    y = jnp.dot(ya_ref[...], w_ref[0:half, :], preferred_element_type=F32)
    y = y + jnp.dot(yb_ref[...], w_ref[half:2 * half, :], preferred_element_type=F32)
    pos0 = (pl.program_id(0) % blocks_per_seq) * bm
    hn = jnp.where(_row_valid(pos0, bm, n_real), h_ref[...] + y, 0.0)
    ho_ref[...] = hn

    ms = jnp.mean(hn * hn, axis=-1, keepdims=True)
    xn = hn * lax.rsqrt(ms + EPS) * g_ref[...]
    xp_ref[...] = _pack_pairs(xn)
    xh = xn.astype(BF16)
    xl = (xn - xh.astype(F32)).astype(BF16)
    t = jnp.dot(xh, wr_ref[...], preferred_element_type=F32)
    lg = (t[:, 0:LANES] + t[:, LANES:2 * LANES]
          + jnp.dot(xl, wr_ref[:, 0:LANES], preferred_element_type=F32) + br_ref[...])
    lane = lax.broadcasted_iota(I32, lg.shape, 1)
    lane_f = lane.astype(F32)
    big = float(LANES)

    is_grp = (lane >= N_EXPERTS) & (lane < N_EXPERTS + N_GROUPS)
    gl = jnp.where(is_grp, lg, -jnp.inf)
    gmax = jnp.max(gl, axis=1, keepdims=True)
    gidx = jnp.min(jnp.where(gl == gmax, lane_f, big), axis=1, keepdims=True) - N_EXPERTS
    g_prob = 1.0 / jnp.sum(jnp.where(is_grp, jnp.exp(lg - gmax), 0.0), axis=1, keepdims=True)

    lo = gidx * EXPERTS_PER_GROUP
    in_grp = (lane_f >= lo) & (lane_f < lo + EXPERTS_PER_GROUP)
    el = jnp.where(in_grp, lg, -jnp.inf)
    m1 = jnp.max(el, axis=1, keepdims=True)
    i1 = jnp.min(jnp.where(el == m1, lane_f, big), axis=1, keepdims=True)
    el2 = jnp.where(lane_f == i1, -jnp.inf, el)
    m2 = jnp.max(el2, axis=1, keepdims=True)
    i2 = jnp.min(jnp.where(el2 == m2, lane_f, big), axis=1, keepdims=True)
    e2 = jnp.exp(m2 - m1)
    w1 = g_prob / (1.0 + e2)
    w2 = g_prob * e2 / (1.0 + e2)

    @pl.when(pl.program_id(0) == 0)
    def _():
        cnt_ref[...] = jnp.zeros_like(cnt_ref)

    oh1 = lane_f == i1
    oh2 = lane_f == i2
    onehots = jnp.concatenate([jnp.where(oh1, 1.0, 0.0), jnp.where(oh2, 1.0, 0.0)], axis=1).astype(BF16)
    rr = lax.broadcasted_iota(I32, (bm, bm), 0)
    cc = lax.broadcasted_iota(I32, (bm, bm), 1)
    tri = jnp.where(rr >= cc, 1.0, 0.0).astype(BF16)
    csum = jnp.dot(tri, onehots, preferred_element_type=F32)
    c1 = csum[:, 0:LANES]
    c2 = csum[:, LANES:2 * LANES]
    tot1 = c1[bm - 1:bm, :]
    tot2 = c2[bm - 1:bm, :]
    before = cnt_ref[0:1, :]
    rank1 = jnp.sum(jnp.where(oh1, before + c1 - 1.0, 0.0), axis=1, keepdims=True)
    rank2 = jnp.sum(jnp.where(oh2, before + tot1 + c2 - 1.0, 0.0), axis=1, keepdims=True)
    cnt_ref[...] = jnp.broadcast_to(before + tot1 + tot2, cnt_ref.shape)

    r_ref[...] = jnp.where(lane == 0, i1, jnp.where(lane == 1, i2,
                           jnp.where(lane == 2, w1, jnp.where(lane == 3, w2,
                                     jnp.where(lane == 4, rank1, jnp.where(lane == 5, rank2, 0.0))))))


def _outproj_router(ya, yb, h, w_out, g_ffn, w_router, b_router, tp, n_real):
    n, d = h.shape
    half = ya.shape[1]
    bm = UNIT
    return pl.pallas_call(
        functools.partial(_outproj_router_body, bm=bm, blocks_per_seq=tp // bm, n_real=n_real),
        out_shape=[jax.ShapeDtypeStruct((n, d), F32), jax.ShapeDtypeStruct((n, LANES), F32),
                   jax.ShapeDtypeStruct((8, LANES), F32), jax.ShapeDtypeStruct((n, d // 2), U32)],
        grid=(n // bm,),
        in_specs=[pl.BlockSpec((bm, half), lambda i: (i, 0)),
                  pl.BlockSpec((bm, half), lambda i: (i, 0)),
                  pl.BlockSpec((bm, d), lambda i: (i, 0)),
                  pl.BlockSpec((2 * half, d), lambda i: (0, 0)),
                  pl.BlockSpec((1, d), lambda i: (0, 0)),
                  pl.BlockSpec((d, 2 * LANES), lambda i: (0, 0)),
                  pl.BlockSpec((1, LANES), lambda i: (0, 0))],
        out_specs=[pl.BlockSpec((bm, d), lambda i: (i, 0)),
                   pl.BlockSpec((bm, LANES), lambda i: (i, 0)),
                   pl.BlockSpec((8, LANES), lambda i: (0, 0)),
                   pl.BlockSpec((bm, d // 2), lambda i: (i, 0))],
        compiler_params=_cparams(("arbitrary",)),
        name="outproj_router",
    )(ya, yb, h, w_out, g_ffn, w_router, b_router)


def _pack_pairs(x):
    c = x.shape[1] // 2
    hi = lax.bitcast_convert_type(x[:, :c].astype(BF16).astype(F32), U32)
    lo = lax.bitcast_convert_type(x[:, c:].astype(BF16).astype(F32), U32)
    return hi | (lo >> 16)


def _unpack_pairs(p):
    hi = lax.bitcast_convert_type(p & jnp.uint32(0xFFFF0000), F32)
    lo = lax.bitcast_convert_type(p << 16, F32)
    return hi, lo


def _dispatch_body(meta_ref, dest_ref, xin_ref, xs_hbm, xp_ref, zero_ref, sem, zsem, *, tb):
    i = pl.program_id(0)
    slot = i % 2

    def drain(s):
        for _ in range(2):
            pltpu.make_async_copy(xp_ref.at[s], xs_hbm.at[pl.ds(0, tb)], sem.at[s]).wait()

    xp_ref[slot] = xin_ref[...]

    def issue(t, _):
        for k in range(2):
            d = dest_ref[0, 0, 2 * t + k]
            pltpu.make_async_copy(xp_ref.at[slot, pl.ds(t, 1)], xs_hbm.at[pl.ds(d, 1)],
                                  sem.at[slot]).start()
        return 0

    lax.fori_loop(0, tb, issue, 0, unroll=8)

    @pl.when(i > 0)
    def _():
        drain(1 - slot)

    @pl.when(i == pl.num_programs(0) - 1)
    def _():
        drain(slot)

    @pl.when(i == 0)
    def _():
        zero_ref[...] = jnp.zeros_like(zero_ref)

        def per_expert(e, _):
            cnt = meta_ref[e]
            start = meta_ref[N_EXPERTS + e]
            padded = meta_ref[2 * N_EXPERTS + e]

            def pad_copy(r):
                return pltpu.make_async_copy(zero_ref.at[pl.ds(0, 1)], xs_hbm.at[pl.ds(start + r, 1)], zsem)

            def zissue(r, _):
                pad_copy(r).start()
                return 0

            def zwait(r, _):
                pad_copy(r).wait()
                return 0

            lax.fori_loop(cnt, padded, zissue, 0)
            lax.fori_loop(cnt, padded, zwait, 0)
            return 0

        lax.fori_loop(0, N_EXPERTS, per_expert, 0)

        def tail_copy(blk):
            return pltpu.make_async_copy(zero_ref, xs_hbm.at[pl.ds(blk * MOE_BM, MOE_BM)], zsem)

        def tissue(blk, _):
            tail_copy(blk).start()
            return 0

        def twait(blk, _):
            tail_copy(blk).wait()
            return 0

        n_used = meta_ref[3 * N_EXPERTS]
        lax.fori_loop(n_used, xs_hbm.shape[0] // MOE_BM, tissue, 0)
        lax.fori_loop(n_used, xs_hbm.shape[0] // MOE_BM, twait, 0)


def _dispatch(xp, dest3, meta, n_rows):
    n, dp = xp.shape
    tb = UNIT
    return pl.pallas_call(
        functools.partial(_dispatch_body, tb=tb),
        out_shape=jax.ShapeDtypeStruct((n_rows, dp), U32),
        grid_spec=pltpu.PrefetchScalarGridSpec(
            num_scalar_prefetch=1,
            grid=(n // tb,),
            in_specs=[pl.BlockSpec((1, 1, 2 * tb), lambda i, m: (i, 0, 0), memory_space=pltpu.SMEM),
                      pl.BlockSpec((tb, dp), lambda i, m: (i, 0))],
            out_specs=pl.BlockSpec(memory_space=pl.ANY),
            scratch_shapes=[pltpu.VMEM((2, tb, dp), U32), pltpu.VMEM((MOE_BM, dp), U32),
                            pltpu.SemaphoreType.DMA((2,)), pltpu.SemaphoreType.DMA(())]),
        compiler_params=_cparams(("arbitrary",)),
        name="moe_dispatch",
    )(meta, dest3, xp)


def _experts_body(be_ref, nu_ref, xs_ref, wg_ref, wu_ref, wd_ref, ys_ref, wg_bf, wu_bf, wd_bf):
    i = pl.program_id(0)
    n_used = nu_ref[0]

    @pl.when(i < n_used)
    def _():
        prev = be_ref[jnp.maximum(i - 1, 0)]

        @pl.when((i == 0) | (be_ref[i] != prev))
        def _():
            wg_bf[...] = wg_ref[0, 0].astype(BF16)
            wu_bf[...] = wu_ref[0, 0].astype(BF16)
            wd_bf[...] = wd_ref[0, 0].astype(BF16)

        half = MOE_BM // 2
        kh = wg_bf.shape[0] // 2
        xns = []
        for r in range(2):
            hi, lo = _unpack_pairs(xs_ref[r * half:(r + 1) * half, :])
            xns.append((hi.astype(BF16), lo.astype(BF16)))

        def up(w_bf):
            return [jnp.dot(xa, w_bf[0:kh, :], preferred_element_type=F32)
                    + jnp.dot(xb, w_bf[kh:2 * kh, :], preferred_element_type=F32) for xa, xb in xns]

        hgs = up(wg_bf)
        hus = up(wu_bf)
        hids = [(_silu(hg) * hu).astype(BF16) for hg, hu in zip(hgs, hus)]
        for r in range(2):
            ys_ref[r * half:(r + 1) * half, :] = _pack_pairs(
                jnp.dot(hids[r], wd_bf[...], preferred_element_type=F32))

    @pl.when(i >= n_used)
    def _():
        ys_ref[...] = jnp.zeros_like(ys_ref)


def _experts(xs, w_gate, w_up, w_down, layer, block_expert, n_used):
    r, dp = xs.shape
    d = 2 * dp
    de = w_gate.shape[3]
    nb = r // MOE_BM

    def wblk(i, be, nu):
        return (layer, be[jnp.minimum(i, nu[0] - 1)], 0, 0)

    return pl.pallas_call(
        _experts_body,
        out_shape=jax.ShapeDtypeStruct((r, dp), U32),
        grid_spec=pltpu.PrefetchScalarGridSpec(
            num_scalar_prefetch=2,
            grid=(nb,),
            in_specs=[pl.BlockSpec((MOE_BM, dp), lambda i, be, nu: (i, 0)),
                      pl.BlockSpec((1, 1, d, de), wblk),
                      pl.BlockSpec((1, 1, d, de), wblk),
                      pl.BlockSpec((1, 1, de, d), wblk)],
            out_specs=pl.BlockSpec((MOE_BM, dp), lambda i, be, nu: (i, 0)),
            scratch_shapes=[pltpu.VMEM((d, de), BF16), pltpu.VMEM((d, de), BF16),
                            pltpu.VMEM((de, d), BF16)]),
        compiler_params=_cparams(("arbitrary",)),
        name="moe_experts",
    )(block_expert, n_used, xs, w_gate, w_up, w_down)


def _combine_body(dest_ref, dnext_ref, h_ref, r_ref, gf_ref, ys_hbm, o_ref, buf, sem, *out_stage,
                  tb, blocks_per_seq, n_real, final_norm):
    i = pl.program_id(0)
    slot = i % 2

    def gather(d_ref, s):
        def issue(t, _):
            for k in range(2):
                d = d_ref[0, 0, 2 * t + k]
                pltpu.make_async_copy(ys_hbm.at[pl.ds(d, 1)], buf.at[s, k, pl.ds(t, 1)], sem.at[s]).start()
            return 0

        lax.fori_loop(0, tb, issue, 0, unroll=8)

    @pl.when(i == 0)
    def _():
        gather(dest_ref, 0)

    @pl.when(i + 1 < pl.num_programs(0))
    def _():
        gather(dnext_ref, 1 - slot)

    for k in range(2):
        pltpu.make_async_copy(ys_hbm.at[pl.ds(0, tb)], buf.at[slot, k], sem.at[slot]).wait()

    r = r_ref[...]
    w1 = r[:, 2:3]
    w2 = r[:, 3:4]
    y1 = _unpack_pairs(buf[slot, 0])
    y2 = _unpack_pairs(buf[slot, 1])
    y = jnp.concatenate([w1 * y1[0] + w2 * y2[0], w1 * y1[1] + w2 * y2[1]], axis=1)
    pos0 = (i % blocks_per_seq) * tb
    hn = jnp.where(_row_valid(pos0, tb, n_real), h_ref[...] + y, 0.0)
    if not final_norm:
        o_ref[...] = hn
        return

    obuf, osem = out_stage
    seq = n_real - N_META
    ms = jnp.mean(hn * hn, axis=-1, keepdims=True)
    obuf[slot] = hn * lax.rsqrt(ms + EPS) * gf_ref[...]

    def frame_copy(step, s, start):
        b = step // blocks_per_seq
        j = step % blocks_per_seq
        row0 = b * seq + j * tb - OFF
        tail = OFF + seq - (blocks_per_seq - 1) * tb

        def go(src, dst):
            cp = pltpu.make_async_copy(src, dst, osem.at[s])
            cp.start() if start else cp.wait()

        @pl.when(j == 0)
        def _():
            go(obuf.at[s, pl.ds(OFF, tb - OFF)], o_ref.at[pl.ds(b * seq, tb - OFF)])

        @pl.when((j > 0) & (j < blocks_per_seq - 1))
        def _():
            go(obuf.at[s], o_ref.at[pl.ds(row0, tb)])

        @pl.when(j == blocks_per_seq - 1)
        def _():
            go(obuf.at[s, pl.ds(0, tail)], o_ref.at[pl.ds(row0, tail)])

    frame_copy(i, slot, True)

    @pl.when(i > 0)
    def _():
        frame_copy(i - 1, 1 - slot, False)

    @pl.when(i == pl.num_programs(0) - 1)
    def _():
        frame_copy(i, slot, False)


def _combine(h, route, dest3, ys, g_final, tp, n_real, final_norm):
    n, d = h.shape
    tb = UNIT
    last = n // tb - 1
    assert tp // tb >= 2
    if final_norm:
        out_shape = jax.ShapeDtypeStruct((n // tp * (n_real - N_META), d), F32)
        out_spec = pl.BlockSpec(memory_space=pl.ANY)
        out_stage = [pltpu.VMEM((2, tb, d), F32), pltpu.SemaphoreType.DMA((2,))]
    else:
        out_shape = jax.ShapeDtypeStruct((n, d), F32)
        out_spec = pl.BlockSpec((tb, d), lambda i: (i, 0))
        out_stage = []
    return pl.pallas_call(
        functools.partial(_combine_body, tb=tb, blocks_per_seq=tp // tb, n_real=n_real,
                          final_norm=final_norm),
        out_shape=out_shape,
        grid=(n // tb,),
        in_specs=[pl.BlockSpec((1, 1, 2 * tb), lambda i: (i, 0, 0), memory_space=pltpu.SMEM),
                  pl.BlockSpec((1, 1, 2 * tb), lambda i: (jnp.minimum(i + 1, last), 0, 0),
                               memory_space=pltpu.SMEM),
                  pl.BlockSpec((tb, d), lambda i: (i, 0)),
                  pl.BlockSpec((tb, LANES), lambda i: (i, 0)),
                  pl.BlockSpec((1, d), lambda i: (0, 0)),
                  pl.BlockSpec(memory_space=pl.ANY)],
        out_specs=out_spec,
        scratch_shapes=[pltpu.VMEM((2, 2, tb, d // 2), U32), pltpu.SemaphoreType.DMA((2,))] + out_stage,
        compiler_params=_cparams(("arbitrary",)),
        name="moe_combine",
    )(dest3, dest3, h, route, g_final, ys)


def _moe(h, xp, route, counts_f, w_gate, w_up, w_down, layer, g_final, tp, n_real, final_norm):
    n, d = h.shape
    n_pairs = 2 * n
    counts = counts_f[0, 0:N_EXPERTS].astype(I32)
    padded = (counts + MOE_BM - 1) // MOE_BM * MOE_BM
    p_ends = jnp.cumsum(padded)
    p_starts = p_ends - padded
    dest = jnp.take(p_starts, route[:, 0:2].astype(I32)) + route[:, 4:6].astype(I32)
    n_blocks = -(-(n_pairs + N_EXPERTS * (MOE_BM - 1)) // MOE_BM)
    block_row0 = jnp.arange(n_blocks, dtype=I32) * MOE_BM
    block_expert = jnp.minimum(jnp.sum((p_ends[None, :] <= block_row0[:, None]).astype(I32), axis=1),
                               N_EXPERTS - 1)
    n_used = (p_ends[-1:] // MOE_BM).astype(I32)
    meta = jnp.concatenate([counts, p_starts, padded, n_used]).astype(I32)
    dest3 = dest.astype(I32).reshape(n // UNIT, 1, 2 * UNIT)

    xs = _dispatch(xp, dest3, meta, n_blocks * MOE_BM)
    ys = _experts(xs, w_gate, w_up, w_down, layer, block_expert, n_used)
    return _combine(h, route, dest3, ys, g_final, tp, n_real, final_norm)


def _short_conv(x, halo_ref, w_ref, first):
    rows = x.shape[0]

    @pl.when(first)
    def _():
        halo_ref[...] = jnp.zeros_like(halo_ref)

    x_ext = jnp.concatenate([halo_ref[...], x], axis=0)
    halo_ref[...] = x[rows - 8:rows]
    acc = jnp.zeros_like(x)
    for j, tap in _shifted_taps(x_ext, 8, rows, SHORT_K):
        acc = acc + w_ref[j:j + 1, :] * tap
    return acc


def _bdot(a, b):
    return jnp.dot(a.astype(BF16), b.astype(BF16), preferred_element_type=F32)


def _gdn_pre_body(z_ref, sc_ref, st_ref, cw_ref, al_r_ref, dt_r_ref, al_c_ref, dt_c_ref,
                  u_ref, w_ref, qd_ref, kd_ref, at_ref, eg_ref, halo_ref):
    rows = z_ref.shape[0]
    hw = N_HEADS * HEAD_DIM
    x = _silu(_short_conv(z_ref[...].astype(F32), halo_ref, cw_ref, pl.program_id(1) == 0))

    sc = sc_ref[...]
    g_cols = -jnp.exp(al_r_ref[...]) * _softplus(sc + dt_r_ref[...])
    beta_cols = _sigmoid(sc)
    st = st_ref[...]
    g_rows = -jnp.exp(al_c_ref[:, 0:1]) * _softplus(st + dt_c_ref[:, 0:1])

    ri = lax.broadcasted_iota(I32, (rows, rows), 0)
    ci = lax.broadcasted_iota(I32, (rows, rows), 1)
    same64 = (ri >> 6) == (ci >> 6)
    same32 = (ri >> 5) == (ci >> 5)
    same16 = (ri >> 4) == (ci >> 4)
    lower = ri >= ci
    strict = ri > ci
    lane = lax.broadcasted_iota(I32, (rows, LANES), 1)
    eg_slab = jnp.zeros((rows, LANES), F32)
    scale = HEAD_DIM ** -0.5

    heads = range(N_HEADS)
    in_chunk_lower = same64 & lower
    in_chunk_upper = same64 & (ri <= ci)
    a_mats, nmats, rhss = [], [], []
    for h in heads:
        sl = slice(h * HEAD_DIM, (h + 1) * HEAD_DIM)
        q = x[:, sl]
        k = x[:, hw + h * HEAD_DIM:hw + (h + 1) * HEAD_DIM]
        v = x[:, 2 * hw + h * HEAD_DIM:2 * hw + (h + 1) * HEAD_DIM]
        q = q * lax.rsqrt(jnp.sum(q * q, axis=-1, keepdims=True) + EPS)
        k = k * lax.rsqrt(jnp.sum(k * k, axis=-1, keepdims=True) + EPS)
        g_col = g_cols[:, h:h + 1]
        beta = beta_cols[:, N_HEADS + h:N_HEADS + h + 1]
        g_row = g_rows[h:h + 1, :]

        gc_col = jnp.sum(jnp.where(in_chunk_lower, g_row, 0.0), axis=1, keepdims=True)
        gc_row = jnp.sum(jnp.where(in_chunk_upper, g_col, 0.0), axis=0, keepdims=True)
        gtot_col = jnp.sum(jnp.where(same64, g_row, 0.0), axis=1, keepdims=True)
        decay = jnp.exp(jnp.where(in_chunk_lower, gc_col - gc_row, NEG))
        eg = jnp.exp(gc_col)
        ekd = jnp.exp(gtot_col - gc_col)

        kb = k * beta
        kbf = k.astype(BF16)
        kk = lax.dot_general(kb.astype(BF16), kbf, (((1,), (1,)), ((), ())), preferred_element_type=F32)
        qs = q * scale
        qk = lax.dot_general(qs.astype(BF16), kbf, (((1,), (1,)), ((), ())), preferred_element_type=F32)
        attn = qk * decay
        a_mats.append(jnp.where(strict, kk * decay, 0.0))
        rhss.append(jnp.concatenate([v * beta, kb * eg], axis=1))
        qd_ref[:, sl] = (qs * eg).astype(BF16)
        kd_ref[:, sl] = (k * ekd).astype(BF16)
        for c in range(rows // CHUNK):
            cs = slice(c * CHUNK, (c + 1) * CHUNK)
            at_ref[h, cs, :] = attn[cs, cs].astype(BF16)
        eg_slab = jnp.where(lane == h, eg, eg_slab)
    eg_ref[...] = eg_slab

    nmats = [jnp.where(same16, -a, 0.0) for a in a_mats]
    qqs = nmats
    for _ in range(3):
        qqs = [_bdot(qq, qq) for qq in qqs]
        prods = [_bdot(qq, nm) for qq, nm in zip(qqs, nmats)]
        nmats = [nm + qq + pr for nm, qq, pr in zip(nmats, qqs, prods)]
    for level_mask, inner_mask in ((same32, same16), (same64, same32)):
        sel = level_mask & jnp.logical_not(inner_mask)
        offs = [jnp.where(sel, a, 0.0) for a in a_mats]
        bmats = [off + _bdot(nm, off) for nm, off in zip(nmats, offs)]
        prods = [_bdot(bm, nm) for bm, nm in zip(bmats, nmats)]
        nmats = [nm - bm - pr for nm, bm, pr in zip(nmats, bmats, prods)]

    for h in heads:
        sl = slice(h * HEAD_DIM, (h + 1) * HEAD_DIM)
        uw = rhss[h] + _bdot(nmats[h], rhss[h])
        u_ref[:, sl] = uw[:, 0:HEAD_DIM]
        w_ref[:, sl] = uw[:, HEAD_DIM:2 * HEAD_DIM].astype(BF16)


def _gdn_pre(z, sc, st, conv_w, a_log, dt_bias, bsz, tp):
    n = z.shape[0]
    hw = N_HEADS * HEAD_DIM
    rows = UNIT
    nu = tp // rows
    pad_r = lambda v: jnp.pad(v.astype(F32), (0, LANES - N_HEADS)).reshape(1, LANES)
    pad_c = lambda v: jnp.broadcast_to(jnp.pad(v.astype(F32), (0, 16 - N_HEADS))[:, None], (16, LANES))
    outs = pl.pallas_call(
        _gdn_pre_body,
        out_shape=[jax.ShapeDtypeStruct((n, hw), F32),
                   jax.ShapeDtypeStruct((n, hw), BF16),
                   jax.ShapeDtypeStruct((n, hw), BF16),
                   jax.ShapeDtypeStruct((n, hw), BF16),
                   jax.ShapeDtypeStruct((N_HEADS, n, CHUNK), BF16),
                   jax.ShapeDtypeStruct((n, LANES), F32)],
        grid=(bsz, nu),
        in_specs=[pl.BlockSpec((rows, 3 * hw), lambda b, i: (b * nu + i, 0)),
                  pl.BlockSpec((rows, LANES), lambda b, i: (b * nu + i, 0)),
                  pl.BlockSpec((16, rows), lambda b, i: (0, b * nu + i)),
                  pl.BlockSpec((8, 3 * hw), lambda b, i: (0, 0)),
                  pl.BlockSpec((1, LANES), lambda b, i: (0, 0)),
                  pl.BlockSpec((1, LANES), lambda b, i: (0, 0)),
                  pl.BlockSpec((16, LANES), lambda b, i: (0, 0)),
                  pl.BlockSpec((16, LANES), lambda b, i: (0, 0))],
        out_specs=[pl.BlockSpec((rows, hw), lambda b, i: (b * nu + i, 0)),
                   pl.BlockSpec((rows, hw), lambda b, i: (b * nu + i, 0)),
                   pl.BlockSpec((rows, hw), lambda b, i: (b * nu + i, 0)),
                   pl.BlockSpec((rows, hw), lambda b, i: (b * nu + i, 0)),
                   pl.BlockSpec((N_HEADS, rows, CHUNK), lambda b, i: (0, b * nu + i, 0)),
                   pl.BlockSpec((rows, LANES), lambda b, i: (b * nu + i, 0))],
        scratch_shapes=[pltpu.VMEM((8, 3 * hw), F32)],
        compiler_params=_cparams(("arbitrary", "arbitrary")),
        name="gdn_pre",
    )(z, sc, st, jnp.pad(conv_w.astype(F32), ((0, 8 - SHORT_K), (0, 0))),
      pad_r(a_log), pad_r(dt_bias), pad_c(a_log), pad_c(dt_bias))
    return outs


def _gdn_scan_body(u_ref, w_ref, qd_ref, kd_ref, at_ref, eg_ref, z_ref, gn_ref, o_ref, s_ref, *, bsz):
    @pl.when(pl.program_id(0) == 0)
    def _():
        s_ref[...] = jnp.zeros_like(s_ref)

    units = [(b, h, slice(h * HEAD_DIM, (h + 1) * HEAD_DIM)) for b in range(bsz) for h in range(N_HEADS)]
    states = [s_ref[b * N_HEADS + h] for b, h, _ in units]
    rs = [jnp.dot(jnp.concatenate([w_ref[b, :, sl], qd_ref[b, :, sl]], axis=0), s.astype(BF16),
                  preferred_element_type=F32) for (b, h, sl), s in zip(units, states)]
    vbs = [(u_ref[b, :, sl] - r[0:CHUNK]).astype(BF16) for (b, h, sl), r in zip(units, rs)]
    intra = [jnp.dot(at_ref[h, b], vb, preferred_element_type=F32) for (b, h, sl), vb in zip(units, vbs)]
    outer = [lax.dot_general(kd_ref[b, :, sl], vb, (((0,), (0,)), ((), ())), preferred_element_type=F32)
             for (b, h, sl), vb in zip(units, vbs)]
    for (b, h, sl), s, r, a, kv in zip(units, states, rs, intra, outer):
        s_ref[b * N_HEADS + h] = s * eg_ref[b, CHUNK - 1:CHUNK, h:h + 1] + kv
        o = r[CHUNK:2 * CHUNK] + a
        on = o * lax.rsqrt(jnp.mean(o * o, axis=-1, keepdims=True) + EPS) * gn_ref[...]
        o_ref[b, :, sl] = (on * _silu(z_ref[b, :, sl].astype(F32))).astype(o_ref.dtype)


def _gdn_scan(u, w, qd, kd, attn, eg, z, out_norm_g, bsz, tp, z_col):
    hw = N_HEADS * HEAD_DIM
    nc = tp // CHUNK
    v3 = lambda a: a.reshape(bsz, tp, a.shape[-1])
    blk3 = pl.BlockSpec((bsz, CHUNK, hw), lambda c: (0, c, 0))
    return pl.pallas_call(
        functools.partial(_gdn_scan_body, bsz=bsz),
        out_shape=jax.ShapeDtypeStruct((bsz, tp, hw), BF16),
        grid=(nc,),
        in_specs=[blk3, blk3, blk3, blk3,
                  pl.BlockSpec((N_HEADS, bsz, CHUNK, CHUNK), lambda c: (0, 0, c, 0)),
                  pl.BlockSpec((bsz, CHUNK, LANES), lambda c: (0, c, 0)),
                  pl.BlockSpec((bsz, CHUNK, hw), lambda c: (0, c, z_col)),
                  pl.BlockSpec((1, HEAD_DIM), lambda c: (0, 0))],
        out_specs=blk3,
        scratch_shapes=[pltpu.VMEM((bsz * N_HEADS, HEAD_DIM, HEAD_DIM), F32)],
        compiler_params=_cparams(("arbitrary",)),
        name="gdn_scan",
    )(v3(u), v3(w), v3(qd), v3(kd), attn.reshape(N_HEADS, bsz, tp, CHUNK), v3(eg), v3(z),
      out_norm_g.astype(F32).reshape(1, HEAD_DIM)).reshape(bsz * tp, hw)


def _lru_body(x_ref, gate_ref, cw_ref, cb_ref, wr_ref, br_ref, wi_ref, bi_ref, lam_ref, o_ref,
              halo_ref, hc_ref, *, bt, n_real):
    first = pl.program_id(1) == 0

    @pl.when(first)
    def _():
        hc_ref[...] = jnp.zeros_like(hc_ref)

    x = _short_conv(x_ref[...].astype(F32), halo_ref, cw_ref, first) + cb_ref[...]
    nblk = wr_ref.shape[0]
    bd = wr_ref.shape[1]
    xb = x.astype(BF16)
    rg = jnp.concatenate([jnp.dot(xb[:, n * bd:(n + 1) * bd], wr_ref[n], preferred_element_type=F32)
                          for n in range(nblk)], axis=1)
    ig = jnp.concatenate([jnp.dot(xb[:, n * bd:(n + 1) * bd], wi_ref[n], preferred_element_type=F32)
                          for n in range(nblk)], axis=1)
    r = _sigmoid(rg + br_ref[...])
    ig = _sigmoid(ig + bi_ref[...])
    log_a = (-LRU_C * _softplus(-lam_ref[...])) * r
    a = jnp.exp(log_a)
    th = jnp.tanh(log_a)
    b = jnp.sqrt(-2.0 * th / (1.0 - th)) * (ig * x)
    b = jnp.where(_row_valid(pl.program_id(1) * bt, bt, n_real), b, 0.0)

    row = lax.broadcasted_iota(I32, (bt, 1), 0)
    d = 1
    while d < bt:
        keep = row >= d
        a_sh = jnp.where(keep, pltpu.roll(a, d, 0), 1.0)
        b_sh = jnp.where(keep, pltpu.roll(b, d, 0), 0.0)
        b = a * b_sh + b
        a = a * a_sh
        d *= 2
    hs = a * hc_ref[0:1, :] + b
    hc_ref[...] = jnp.broadcast_to(hs[bt - 1:bt, :], hc_ref.shape)
    o_ref[...] = (hs * jax.nn.gelu(gate_ref[...].astype(F32))).astype(o_ref.dtype)


def _lru(z, conv_w, conv_b, w_rg, b_rg, w_ig, b_ig, lam, bsz, tp, n_real, x_col, gate_col):
    n = z.shape[0]
    c = conv_w.shape[1]
    bt = UNIT
    nt = tp // bt
    row = lambda v: v.astype(F32).reshape(1, c)
    wspec = pl.BlockSpec(w_rg.shape, lambda b, i: (0, 0, 0))
    vspec = pl.BlockSpec((1, c), lambda b, i: (0, 0))
    return pl.pallas_call(
        functools.partial(_lru_body, bt=bt, n_real=n_real),
        out_shape=jax.ShapeDtypeStruct((n, c), BF16),
        grid=(bsz, nt),
        in_specs=[pl.BlockSpec((bt, c), lambda b, i: (b * nt + i, x_col)),
                  pl.BlockSpec((bt, c), lambda b, i: (b * nt + i, gate_col)),
                  pl.BlockSpec((8, c), lambda b, i: (0, 0)),
                  vspec, wspec, vspec, wspec, vspec, vspec],
        out_specs=pl.BlockSpec((bt, c), lambda b, i: (b * nt + i, 0)),
        scratch_shapes=[pltpu.VMEM((8, c), F32), pltpu.VMEM((8, c), F32)],
        compiler_params=_cparams(("arbitrary", "arbitrary")),
        name="rg_lru",
    )(z, z, jnp.pad(conv_w.astype(F32), ((0, 8 - SHORT_K), (0, 0))), row(conv_b),
      w_rg.astype(BF16), row(b_rg), w_ig.astype(BF16), row(b_ig), row(lam))


def _pick(n, candidates):
    for c in candidates:
        if n % c == 0:
            return c
    raise ValueError(f"no block size in {candidates} divides {n}")


def _pack_weights_body(a_ref, b_ref, o_ref, *, scaled_tiles, scale, head_tiles, gap):
    j = pl.program_id(0)
    tile = o_ref.shape[1]

    @pl.when(j < scaled_tiles)
    def _():
        o_ref[...] = (a_ref[...] * scale).astype(BF16)

    @pl.when((j >= scaled_tiles) & (j < head_tiles))
    def _():
        o_ref[...] = a_ref[...].astype(BF16)

    @pl.when(j >= head_tiles)
    def _():
        x = jnp.concatenate([a_ref[...], b_ref[...]], axis=1)
        o_ref[...] = x[:, gap:gap + tile].astype(BF16)


def _pack_inproj_weights(w_in, head, gap, scaled=0, scale=1.0):
    d, total = w_in.shape
    tile = 512
    n_out = total - gap
    return pl.pallas_call(
        functools.partial(_pack_weights_body, scaled_tiles=scaled // tile, scale=scale,
                          head_tiles=head // tile, gap=gap),
        out_shape=jax.ShapeDtypeStruct((d, n_out), BF16),
        grid=(n_out // tile,),
        in_specs=[pl.BlockSpec((d, tile), lambda j: (0, j)),
                  pl.BlockSpec((d, LANES), lambda j: (0, (tile // LANES) * (j + 1)))],
        out_specs=pl.BlockSpec((d, tile), lambda j: (0, j)),
        compiler_params=_cparams(("arbitrary",)),
        name="pack_inproj_weights",
    )(w_in, w_in)


def _small_weights(cols):
    k = cols.shape[1]
    return (jnp.pad(cols.T, ((0, 16 - k), (0, 0))).astype(BF16),
            jnp.pad(cols, ((0, 0), (0, LANES - k))).astype(BF16))


def _router_weights(w_group, b_group, w_expert, b_expert):
    w = jnp.concatenate([w_expert, w_group], axis=1).astype(F32)
    b = jnp.concatenate([b_expert, b_group]).astype(F32)
    k = w.shape[1]
    w = jnp.pad(w, ((0, 0), (0, LANES - k)))
    w_hi = w.astype(BF16)
    w_lo = (w - w_hi.astype(F32)).astype(BF16)
    return jnp.concatenate([w_hi, w_lo], axis=1), jnp.pad(b, (0, LANES - k)).reshape(1, LANES)


def kernel(x, meta_tokens, norm_mix_g, norm_ffn_g, norm_final_g, ab_w_in, ab_forget_b, ab_conv_w, ab_conv_b, ab_ln_g, ab_ln_b, ab_w_out, cd_w_in, cd_qkv_conv_w, cd_a_log, cd_dt_bias, cd_out_norm_g, cd_lru_conv_w, cd_lru_conv_b, cd_w_rg, cd_b_rg, cd_w_ig, cd_b_ig, cd_lru_lambda, cd_w_out, moe_w_group, moe_b_group, moe_w_expert, moe_b_expert, moe_w_gate, moe_w_up, moe_w_down):
    bsz, seq, d = x.shape
    depth = norm_mix_g.shape[0]
    n_real = N_META + seq
    tp = -(-(OFF + seq) // UNIT) * UNIT
    n = bsz * tp
    hw = N_HEADS * HEAD_DIM
    bm = _pick(n, (768, 512, 256))
    bn = 1024

    h = _embed(x.astype(F32).reshape(bsz * seq, d), meta_tokens.astype(F32), bsz, seq, tp)
    row = lambda v: v.astype(F32).reshape(1, -1)

    for layer in range(depth):
        i = layer // 2
        g_mix = row(norm_mix_g[layer])
        if layer % 2 == 0:
            w_in = ab_w_in[i]
            w_main = _pack_inproj_weights(w_in.astype(F32), 3 * hw, N_HEADS, scaled=hw,
                                          scale=LOG2E * HEAD_DIM ** -0.5)
            wst, wsc = _small_weights(w_in[:, 3 * hw:3 * hw + N_HEADS])
            z, _, f_slab = _norm_inproj(h, g_mix, w_main, wst, wsc, bm, bn)
            kx, vx = _fox_prep(z, f_slab, ab_forget_b[i], bsz, tp, n_real)
            ya = _fox_attn(z, kx, vx, bsz, tp)
            yb = _conformer(z, ab_conv_w[i], ab_conv_b[i], ab_ln_g[i], ab_ln_b[i], bsz, tp, 3, 4)
            w_out = ab_w_out[i].astype(BF16)
        else:
            w_in = cd_w_in[i]
            w_main = _pack_inproj_weights(w_in.astype(F32), 3 * hw, 2 * N_HEADS)
            wst, wsc = _small_weights(w_in[:, 3 * hw:3 * hw + 2 * N_HEADS])
            z, zt, zc = _norm_inproj(h, g_mix, w_main, wst, wsc, bm, bn)
            u, w, qd, kd, attn, eg = _gdn_pre(z, zc, zt, cd_qkv_conv_w[i], cd_a_log[i], cd_dt_bias[i],
                                              bsz, tp)
            ya = _gdn_scan(u, w, qd, kd, attn, eg, z, cd_out_norm_g[i], bsz, tp, 3)
            yb = _lru(z, cd_lru_conv_w[i], cd_lru_conv_b[i], cd_w_rg[i], cd_b_rg[i], cd_w_ig[i],
                      cd_b_ig[i], cd_lru_lambda[i], bsz, tp, n_real, 4, 5)
            w_out = cd_w_out[i].astype(BF16)
        g_ffn = row(norm_ffn_g[layer])
        w_r, b_r = _router_weights(moe_w_group[layer], moe_b_group[layer],
                                   moe_w_expert[layer], moe_b_expert[layer])
        h, route, counts, xp = _outproj_router(ya, yb, h, w_out, g_ffn, w_r, b_r, tp, n_real)
        h = _moe(h, xp, route, counts, moe_w_gate, moe_w_up, moe_w_down, layer,
                 row(norm_final_g), tp, n_real, final_norm=(layer == depth - 1))
    return h.reshape(bsz, seq, d).astype(x.dtype)
```

```python
import functools

import jax
import jax.numpy as jnp
from jax import lax
from jax.experimental import pallas as pl
from jax.experimental.pallas import tpu as pltpu

F32 = jnp.float32
BF16 = jnp.bfloat16
I32 = jnp.int32
U32 = jnp.uint32

EPS = 1e-6
N_META = 16
CHUNK = 64
FRONT = CHUNK - N_META
OFF = FRONT + N_META
LANES = 128
UNIT = 256
HEAD_DIM = 128
N_HEADS = 8
CONF_K = 31
SHORT_K = 4
LRU_C = 8.0
N_GROUPS = 4
EXPERTS_PER_GROUP = 8
N_EXPERTS = N_GROUPS * EXPERTS_PER_GROUP
MOE_BM = 256
NEG = -1e30
LOG2E = 1.4426950408889634
VMEM_LIMIT = 56 * 1024 * 1024


def _cparams(sem):
    return pltpu.CompilerParams(dimension_semantics=sem, vmem_limit_bytes=VMEM_LIMIT)


def _row_valid(pos0, rows, n_real):
    pos = pos0 + lax.broadcasted_iota(I32, (rows, 1), 0)
    return (pos >= FRONT) & (pos < FRONT + n_real)


def _sigmoid(x):
    return 1.0 / (1.0 + jnp.exp(-x))


def _softplus(x):
    return jnp.maximum(x, 0.0) + jnp.log1p(jnp.exp(-jnp.abs(x)))


def _silu(x):
    return x * _sigmoid(x)


def _embed_body(x_ref, meta_ref, h_hbm, zero_ref, sem, *, seq, tp, rows):
    b = pl.program_id(0)
    i = pl.program_id(1)
    base = b * tp
    tail = tp - OFF - seq
    frames = pltpu.make_async_copy(x_ref, h_hbm.at[pl.ds(base + OFF + i * rows, rows)], sem.at[0])
    frames.start()

    @pl.when(i == 0)
    def _():
        zero_ref[...] = jnp.zeros_like(zero_ref)
        copies = [
            pltpu.make_async_copy(meta_ref, h_hbm.at[pl.ds(base + FRONT, N_META)], sem.at[1]),
            pltpu.make_async_copy(zero_ref.at[pl.ds(0, FRONT)], h_hbm.at[pl.ds(base, FRONT)], sem.at[2]),
        ]
        if tail:
            copies.append(pltpu.make_async_copy(zero_ref.at[pl.ds(0, tail)],
                                                h_hbm.at[pl.ds(base + OFF + seq, tail)], sem.at[3]))
        for c in copies:
            c.start()
        for c in copies:
            c.wait()

    frames.wait()


def _embed(x2d, meta_tokens, bsz, seq, tp):
    d = x2d.shape[1]
    rows = _pick(seq, (1024, 512, 256, 128, 64, 32, 16, 8))
    per_seq = seq // rows
    return pl.pallas_call(
        functools.partial(_embed_body, seq=seq, tp=tp, rows=rows),
        out_shape=jax.ShapeDtypeStruct((bsz * tp, d), F32),
        grid=(bsz, per_seq),
        in_specs=[pl.BlockSpec((rows, d), lambda b, i: (b * per_seq + i, 0)),
                  pl.BlockSpec((N_META, d), lambda b, i: (0, 0))],
        out_specs=pl.BlockSpec(memory_space=pl.ANY),
        scratch_shapes=[pltpu.VMEM((max(FRONT, tp - OFF - seq), d), F32), pltpu.SemaphoreType.DMA((4,))],
        compiler_params=_cparams(("arbitrary", "arbitrary")),
        name="embed",
    )(x2d, meta_tokens)


def _norm_inproj_body(h_ref, g_ref, w_ref, wst_ref, wsc_ref, o_ref, ot_ref, oc_ref, xn_ref):
    @pl.when(pl.program_id(1) == 0)
    def _():
        x = h_ref[...]
        ms = jnp.mean(x * x, axis=-1, keepdims=True)
        xn = (x * lax.rsqrt(ms + EPS) * g_ref[...]).astype(BF16)
        xn_ref[...] = xn
        ot_ref[...] = lax.dot_general(wst_ref[...], xn, (((1,), (1,)), ((), ())),
                                      preferred_element_type=F32)
        oc_ref[...] = jnp.dot(xn, wsc_ref[...], preferred_element_type=F32)

    o_ref[...] = jnp.dot(xn_ref[...], w_ref[...], preferred_element_type=F32).astype(o_ref.dtype)


def _norm_inproj(h, g, w_main, w_small_t, w_small_c, bm, bn):
    n, d = h.shape
    nw = w_main.shape[1]
    return pl.pallas_call(
        _norm_inproj_body,
        out_shape=[jax.ShapeDtypeStruct((n, nw), BF16),
                   jax.ShapeDtypeStruct((16, n), F32),
                   jax.ShapeDtypeStruct((n, LANES), F32)],
        grid=(n // bm, nw // bn),
        in_specs=[pl.BlockSpec((bm, d), lambda i, j: (i, 0)),
                  pl.BlockSpec((1, d), lambda i, j: (0, 0)),
                  pl.BlockSpec((d, bn), lambda i, j: (0, j)),
                  pl.BlockSpec((16, d), lambda i, j: (0, 0)),
                  pl.BlockSpec((d, LANES), lambda i, j: (0, 0))],
        out_specs=[pl.BlockSpec((bm, bn), lambda i, j: (i, j)),
                   pl.BlockSpec((16, bm), lambda i, j: (0, i)),
                   pl.BlockSpec((bm, LANES), lambda i, j: (i, 0))],
        scratch_shapes=[pltpu.VMEM((bm, d), BF16)],
        compiler_params=_cparams(("arbitrary", "arbitrary")),
        name="norm_inproj",
    )(h, g, w_main, w_small_t, w_small_c)


def _split3(x):
    p1 = x.astype(BF16)
    r1 = x - p1.astype(F32)
    p2 = r1.astype(BF16)
    p3 = (r1 - p2.astype(F32)).astype(BF16)
    return p1, p2, p3


def _fox_prep_body(k_ref, v_ref, f_ref, fb_ref, kx_ref, vx_ref, carry_ref, *, rows, n_real):
    i = pl.program_id(1)

    @pl.when(i == 0)
    def _():
        carry_ref[...] = jnp.zeros_like(carry_ref)

    f = f_ref[...] + fb_ref[...]
    lf = jnp.minimum(f, 0.0) - jnp.log1p(jnp.exp(-jnp.abs(f)))
    pos = i * rows + lax.broadcasted_iota(I32, (rows, 1), 0)
    lf = jnp.where((pos >= FRONT) & (pos < FRONT + n_real), lf, 0.0)
    rr = lax.broadcasted_iota(I32, (rows, rows), 0)
    cc = lax.broadcasted_iota(I32, (rows, rows), 1)
    tri = jnp.where(rr >= cc, 1.0, 0.0).astype(BF16)
    c = carry_ref[0:1, :]
    for piece in _split3(lf):
        c = c + jnp.dot(tri, piece, preferred_element_type=F32)
    carry_ref[...] = jnp.broadcast_to(c[rows - 1:rows, :], carry_ref.shape)
    cs = jnp.where(pos < FRONT, NEG, -LOG2E * c)
    p1, p2, p3 = (p.astype(F32) for p in _split3(cs))
    lane = lax.broadcasted_iota(I32, (rows, LANES), 1)
    ones_col = jnp.where(lane == 0, 1.0, 0.0).astype(BF16)
    for h in range(N_HEADS):
        ext = jnp.where(lane == 0, p1[:, h:h + 1],
                        jnp.where(lane == 1, p2[:, h:h + 1], jnp.where(lane == 2, p3[:, h:h + 1], 0.0)))
        kx_ref[:, (2 * h) * HEAD_DIM:(2 * h + 1) * HEAD_DIM] = k_ref[:, h * HEAD_DIM:(h + 1) * HEAD_DIM]
        kx_ref[:, (2 * h + 1) * HEAD_DIM:(2 * h + 2) * HEAD_DIM] = ext.astype(BF16)
        vx_ref[:, (2 * h) * HEAD_DIM:(2 * h + 1) * HEAD_DIM] = v_ref[:, h * HEAD_DIM:(h + 1) * HEAD_DIM]
        vx_ref[:, (2 * h + 1) * HEAD_DIM:(2 * h + 2) * HEAD_DIM] = ones_col


def _fox_prep(z, f_slab, forget_b, bsz, tp, n_real):
    n = z.shape[0]
    hw = N_HEADS * HEAD_DIM
    rows = UNIT
    nt = tp // rows
    fb = jnp.pad(forget_b.astype(F32), (0, LANES - N_HEADS)).reshape(1, LANES)
    return pl.pallas_call(
        functools.partial(_fox_prep_body, rows=rows, n_real=n_real),
        out_shape=[jax.ShapeDtypeStruct((n, 2 * hw), BF16), jax.ShapeDtypeStruct((n, 2 * hw), BF16)],
        grid=(bsz, nt),
        in_specs=[pl.BlockSpec((rows, hw), lambda b, i: (b * nt + i, 1)),
                  pl.BlockSpec((rows, hw), lambda b, i: (b * nt + i, 2)),
                  pl.BlockSpec((rows, LANES), lambda b, i: (b * nt + i, 0)),
                  pl.BlockSpec((1, LANES), lambda b, i: (0, 0))],
        out_specs=[pl.BlockSpec((rows, 2 * hw), lambda b, i: (b * nt + i, 0)),
                   pl.BlockSpec((rows, 2 * hw), lambda b, i: (b * nt + i, 0))],
        scratch_shapes=[pltpu.VMEM((8, LANES), F32)],
        compiler_params=_cparams(("arbitrary", "arbitrary")),
        name="fox_prep",
    )(z, z, f_slab, fb)


def _fox_attn_body(q_ref, k_ref, v_ref, o_ref, *, blk, n_split):
    qi = pl.program_id(2)
    half = blk // n_split
    lane = lax.broadcasted_iota(I32, (blk, HEAD_DIM), 1)
    q = jnp.concatenate([q_ref[...], jnp.where(lane < 3, 1.0, 0.0).astype(BF16)], axis=1)
    qs = tuple(q[r * half:(r + 1) * half] for r in range(n_split))

    def step(js, carry, diag_last):
        starts = [pl.multiple_of(j * blk, blk) for j in js]
        ss = []
        for r, qh in enumerate(qs):
            srow = []
            for b, st in enumerate(starts):
                on_diag = diag_last and b == len(js) - 1
                ncol = (r + 1) * half if on_diag else blk
                s = lax.dot_general(qh, k_ref[pl.ds(st, ncol), :], (((1,), (1,)), ((), ())),
                                    preferred_element_type=F32)
                if on_diag:
                    row = r * half + lax.broadcasted_iota(I32, (half, ncol), 0)
                    col = lax.broadcasted_iota(I32, (half, ncol), 1)
                    s = jnp.where(col <= row, s, NEG)
                srow.append((s, st, ncol))
            ss.append(srow)
        out = []
        for r in range(n_split):
            m = carry[2 * r]
            m_new = m
            for s, _, _ in ss[r]:
                m_new = jnp.maximum(m_new, jnp.max(s, axis=1, keepdims=True))
            acc = jnp.exp2(m - m_new) * carry[2 * r + 1]
            for s, st, ncol in ss[r]:
                acc = acc + jnp.dot(jnp.exp2(s - m_new).astype(BF16), v_ref[pl.ds(st, ncol), :],
                                    preferred_element_type=F32)
            out += [m_new, acc]
        return tuple(out)

    init = (jnp.full((half, 1), NEG, F32), jnp.zeros((half, 2 * HEAD_DIM), F32)) * n_split
    carry = lax.fori_loop(0, qi // 2, lambda j, c: step((2 * j, 2 * j + 1), c, False), init)
    res = lax.cond(qi % 2 == 1, lambda c: step((qi - 1, qi), c, True),
                   lambda c: step((qi,), c, True), carry)
    for r in range(n_split):
        acc = res[2 * r + 1]
        o_ref[r * half:(r + 1) * half, :] = (acc[:, 0:HEAD_DIM] / acc[:, HEAD_DIM:HEAD_DIM + 1]
                                             ).astype(o_ref.dtype)


def _fox_attn(z, kx, vx, bsz, tp):
    n = z.shape[0]
    blk = _pick(tp, (768, 512, 256))
    nq = tp // blk
    return pl.pallas_call(
        functools.partial(_fox_attn_body, blk=blk, n_split=2),
        out_shape=jax.ShapeDtypeStruct((n, N_HEADS * HEAD_DIM), BF16),
        grid=(bsz, N_HEADS, nq),
        in_specs=[pl.BlockSpec((blk, HEAD_DIM), lambda b, h, i: (b * nq + i, h)),
                  pl.BlockSpec((tp, 2 * HEAD_DIM), lambda b, h, i: (b, h)),
                  pl.BlockSpec((tp, 2 * HEAD_DIM), lambda b, h, i: (b, h))],
        out_specs=pl.BlockSpec((blk, HEAD_DIM), lambda b, h, i: (b * nq + i, h)),
        compiler_params=_cparams(("arbitrary", "arbitrary", "arbitrary")),
        name="fox_attn",
    )(z, kx, vx)


def _shifted_taps(x_ext, halo, rows, n_taps):
    for r in range(min(8, n_taps)):
        rolled = x_ext if r == 0 else pltpu.roll(x_ext, r, 0)
        for q in range(halo // 8):
            s = 8 * q + r
            if s > n_taps - 1:
                continue
            yield n_taps - 1 - s, rolled[halo - 8 * q:halo - 8 * q + rows]


def _conformer_body(a_ref, b_ref, w_ref, cb_ref, lg_ref, lb_ref, o_ref, halo_ref, *, bt):
    halo = halo_ref.shape[0]

    @pl.when(pl.program_id(1) == 0)
    def _():
        halo_ref[...] = jnp.zeros_like(halo_ref)

    u = a_ref[...].astype(F32) * _sigmoid(b_ref[...].astype(F32))
    x_ext = jnp.concatenate([halo_ref[...], u], axis=0)
    halo_ref[...] = u[bt - halo:bt]
    acc = jnp.zeros_like(u)
    for j, tap in _shifted_taps(x_ext, halo, bt, CONF_K):
        acc = acc + w_ref[j:j + 1, :] * tap
    y = acc + cb_ref[...]
    mu = jnp.mean(y, axis=-1, keepdims=True)
    yc = y - mu
    var = jnp.mean(yc * yc, axis=-1, keepdims=True)
    yn = yc * lax.rsqrt(var + EPS) * lg_ref[...] + lb_ref[...]
    o_ref[...] = _silu(yn).astype(o_ref.dtype)


def _conformer(z, conv_w, conv_b, ln_g, ln_b, bsz, tp, col_a, col_b):
    n = z.shape[0]
    c = conv_w.shape[1]
    bt = UNIT
    nt = tp // bt
    w = jnp.pad(conv_w.astype(F32), ((0, 32 - CONF_K), (0, 0)))
    row = lambda v: v.astype(F32).reshape(1, c)
    return pl.pallas_call(
        functools.partial(_conformer_body, bt=bt),
        out_shape=jax.ShapeDtypeStruct((n, c), BF16),
        grid=(bsz, nt),
        in_specs=[pl.BlockSpec((bt, c), lambda b, i: (b * nt + i, col_a)),
                  pl.BlockSpec((bt, c), lambda b, i: (b * nt + i, col_b)),
                  pl.BlockSpec((32, c), lambda b, i: (0, 0)),
                  pl.BlockSpec((1, c), lambda b, i: (0, 0)),
                  pl.BlockSpec((1, c), lambda b, i: (0, 0)),
                  pl.BlockSpec((1, c), lambda b, i: (0, 0))],
        out_specs=pl.BlockSpec((bt, c), lambda b, i: (b * nt + i, 0)),
        scratch_shapes=[pltpu.VMEM((32, c), F32)],
        compiler_params=_cparams(("arbitrary", "arbitrary")),
        name="conformer_conv",
    )(z, z, w, row(conv_b), row(ln_g), row(ln_b))


def _outproj_router_body(ya_ref, yb_ref, h_ref, w_ref, g_ref, wr_ref, br_ref, ho_ref, r_ref, cnt_ref,
                         xp_ref, er_ref, *, bm, blocks_per_seq, n_real):
    kh = ya_ref.shape[1]
    pos0 = (pl.program_id(0) % blocks_per_seq) * bm
    rh = bm // 2
    parts = [slice(r * rh, (r + 1) * rh) for r in range(2)]
    ys = [jnp.dot(ya_ref[p, :], w_ref[0:kh, :], preferred_element_type=F32)
          + jnp.dot(yb_ref[p, :], w_ref[kh:2 * kh, :], preferred_element_type=F32) for p in parts]
    lgs = []
    for r, (p, y) in enumerate(zip(parts, ys)):
        hn = jnp.where(_row_valid(pos0 + r * rh, rh, n_real), h_ref[p, :] + y, 0.0)
        ho_ref[p, :] = hn
        ms = jnp.mean(hn * hn, axis=-1, keepdims=True)
        xn = hn * lax.rsqrt(ms + EPS) * g_ref[...]
        xp_ref[p, :] = _pack_pairs(xn)
        xh = xn.astype(BF16)
        xl = (xn - xh.astype(F32)).astype(BF16)
        t = jnp.dot(xh, wr_ref[...], preferred_element_type=F32)
        lgs.append(t[:, 0:LANES] + t[:, LANES:2 * LANES]
                   + jnp.dot(xl, wr_ref[:, 0:LANES], preferred_element_type=F32) + br_ref[...])
    lg = jnp.concatenate(lgs, axis=0)
    lane = lax.broadcasted_iota(I32, lg.shape, 1)
    lane_f = lane.astype(F32)
    big = float(LANES)

    is_grp = (lane >= N_EXPERTS) & (lane < N_EXPERTS + N_GROUPS)
    gl = jnp.where(is_grp, lg, -jnp.inf)
    gmax = jnp.max(gl, axis=1, keepdims=True)
    gidx = jnp.min(jnp.where(gl == gmax, lane_f, big), axis=1, keepdims=True) - N_EXPERTS
    g_prob = 1.0 / jnp.sum(jnp.where(is_grp, jnp.exp(lg - gmax), 0.0), axis=1, keepdims=True)

    lo = gidx * EXPERTS_PER_GROUP
    in_grp = (lane_f >= lo) & (lane_f < lo + EXPERTS_PER_GROUP)
    el = jnp.where(in_grp, lg, -jnp.inf)
    m1 = jnp.max(el, axis=1, keepdims=True)
    i1 = jnp.min(jnp.where(el == m1, lane_f, big), axis=1, keepdims=True)
    el2 = jnp.where(lane_f == i1, -jnp.inf, el)
    m2 = jnp.max(el2, axis=1, keepdims=True)
    i2 = jnp.min(jnp.where(el2 == m2, lane_f, big), axis=1, keepdims=True)
    e2 = jnp.exp(m2 - m1)
    w1 = g_prob / (1.0 + e2)
    w2 = g_prob * e2 / (1.0 + e2)

    @pl.when(pl.program_id(0) == 0)
    def _():
        cnt_ref[...] = jnp.zeros_like(cnt_ref)

    oh1 = lane_f == i1
    oh2 = lane_f == i2
    onehots = jnp.concatenate([jnp.where(oh1, 1.0, 0.0), jnp.where(oh2, 1.0, 0.0)], axis=1).astype(BF16)
    rr = lax.broadcasted_iota(I32, (bm, bm), 0)
    cc = lax.broadcasted_iota(I32, (bm, bm), 1)
    tri = jnp.where(rr >= cc, 1.0, 0.0).astype(BF16)
    csum = jnp.dot(tri, onehots, preferred_element_type=F32)
    c1 = csum[:, 0:LANES]
    c2 = csum[:, LANES:2 * LANES]
    tot1 = c1[bm - 1:bm, :]
    tot2 = c2[bm - 1:bm, :]
    before = cnt_ref[0:1, :]
    rank1 = jnp.sum(jnp.where(oh1, before + c1 - 1.0, 0.0), axis=1, keepdims=True)
    rank2 = jnp.sum(jnp.where(oh2, before + tot1 + c2 - 1.0, 0.0), axis=1, keepdims=True)
    cnt_ref[...] = jnp.broadcast_to(before + tot1 + tot2, cnt_ref.shape)

    r_ref[...] = jnp.where(lane == 0, i1, jnp.where(lane == 1, i2,
                           jnp.where(lane == 2, w1, jnp.where(lane == 3, w2, 0.0))))
    eye = rr == cc
    er_ref[0] = jnp.concatenate(
        [jnp.sum(jnp.where(eye, col, 0.0), axis=0, keepdims=True) for col in (i1, i2, rank1, rank2)], axis=0)


def _outproj_router(ya, yb, h, w_out, g_ffn, w_router, b_router, tp, n_real):
    n, d = h.shape
    half = ya.shape[1]
    bm = UNIT
    return pl.pallas_call(
        functools.partial(_outproj_router_body, bm=bm, blocks_per_seq=tp // bm, n_real=n_real),
        out_shape=[jax.ShapeDtypeStruct((n, d), F32), jax.ShapeDtypeStruct((n, LANES), F32),
                   jax.ShapeDtypeStruct((8, LANES), F32), jax.ShapeDtypeStruct((n, d // 2), U32),
                   jax.ShapeDtypeStruct((n // bm, 4, bm), F32)],
        grid=(n // bm,),
        in_specs=[pl.BlockSpec((bm, half), lambda i: (i, 0)),
                  pl.BlockSpec((bm, half), lambda i: (i, 0)),
                  pl.BlockSpec((bm, d), lambda i: (i, 0)),
                  pl.BlockSpec((2 * half, d), lambda i: (0, 0)),
                  pl.BlockSpec((1, d), lambda i: (0, 0)),
                  pl.BlockSpec((d, 2 * LANES), lambda i: (0, 0)),
                  pl.BlockSpec((1, LANES), lambda i: (0, 0))],
        out_specs=[pl.BlockSpec((bm, d), lambda i: (i, 0)),
                   pl.BlockSpec((bm, LANES), lambda i: (i, 0)),
                   pl.BlockSpec((8, LANES), lambda i: (0, 0)),
                   pl.BlockSpec((bm, d // 2), lambda i: (i, 0)),
                   pl.BlockSpec((1, 4, bm), lambda i: (i, 0, 0))],
        compiler_params=_cparams(("arbitrary",)),
        name="outproj_router",
    )(ya, yb, h, w_out, g_ffn, w_router, b_router)


def _pack_pairs(x):
    c = x.shape[1] // 2
    hi = lax.bitcast_convert_type(x[:, :c].astype(BF16).astype(F32), U32)
    lo = lax.bitcast_convert_type(x[:, c:].astype(BF16).astype(F32), U32)
    return hi | (lo >> 16)


def _unpack_pairs(p):
    hi = lax.bitcast_convert_type(p & jnp.uint32(0xFFFF0000), F32)
    lo = lax.bitcast_convert_type(p << 16, F32)
    return hi, lo


def _dispatch_body(meta_ref, dest_ref, xin_ref, xs_hbm, xp_ref, zero_ref, sem, zsem, *, tb):
    i = pl.program_id(0)
    slot = i % 2

    def drain(s):
        for _ in range(2):
            pltpu.make_async_copy(xp_ref.at[s], xs_hbm.at[pl.ds(0, tb)], sem.at[s]).wait()

    xp_ref[slot] = xin_ref[...]

    def issue(t, _):
        for k in range(2):
            d = dest_ref[0, 0, k * tb + t]
            pltpu.make_async_copy(xp_ref.at[slot, pl.ds(t, 1)], xs_hbm.at[pl.ds(d, 1)],
                                  sem.at[slot]).start()
        return 0

    lax.fori_loop(0, tb, issue, 0, unroll=8)

    @pl.when(i > 0)
    def _():
        drain(1 - slot)

    @pl.when(i == pl.num_programs(0) - 1)
    def _():
        drain(slot)

    @pl.when(i == 0)
    def _():
        zero_ref[...] = jnp.zeros_like(zero_ref)

        def per_expert(e, _):
            cnt = meta_ref[e]
            start = meta_ref[N_EXPERTS + e]
            padded = meta_ref[2 * N_EXPERTS + e]

            def pad_copy(r):
                return pltpu.make_async_copy(zero_ref.at[pl.ds(0, 1)], xs_hbm.at[pl.ds(start + r, 1)], zsem)

            def zissue(r, _):
                pad_copy(r).start()
                return 0

            def zwait(r, _):
                pad_copy(r).wait()
                return 0

            lax.fori_loop(cnt, padded, zissue, 0)
            lax.fori_loop(cnt, padded, zwait, 0)
            return 0

        lax.fori_loop(0, N_EXPERTS, per_expert, 0)

        def tail_copy(blk):
            return pltpu.make_async_copy(zero_ref, xs_hbm.at[pl.ds(blk * MOE_BM, MOE_BM)], zsem)

        def tissue(blk, _):
            tail_copy(blk).start()
            return 0

        def twait(blk, _):
            tail_copy(blk).wait()
            return 0

        n_used = meta_ref[3 * N_EXPERTS]
        lax.fori_loop(n_used, xs_hbm.shape[0] // MOE_BM, tissue, 0)
        lax.fori_loop(n_used, xs_hbm.shape[0] // MOE_BM, twait, 0)


def _dispatch(xp, dest3, meta, n_rows):
    n, dp = xp.shape
    tb = UNIT
    return pl.pallas_call(
        functools.partial(_dispatch_body, tb=tb),
        out_shape=jax.ShapeDtypeStruct((n_rows, dp), U32),
        grid_spec=pltpu.PrefetchScalarGridSpec(
            num_scalar_prefetch=1,
            grid=(n // tb,),
            in_specs=[pl.BlockSpec((1, 1, 2 * tb), lambda i, m: (i, 0, 0), memory_space=pltpu.SMEM),
                      pl.BlockSpec((tb, dp), lambda i, m: (i, 0))],
            out_specs=pl.BlockSpec(memory_space=pl.ANY),
            scratch_shapes=[pltpu.VMEM((2, tb, dp), U32), pltpu.VMEM((MOE_BM, dp), U32),
                            pltpu.SemaphoreType.DMA((2,)), pltpu.SemaphoreType.DMA(())]),
        compiler_params=_cparams(("arbitrary",)),
        name="moe_dispatch",
    )(meta, dest3, xp)


def _experts_body(be_ref, nu_ref, xs_ref, wg_ref, wu_ref, wd_ref, ys_ref, wg_bf, wu_bf, wd_bf):
    i = pl.program_id(0)
    n_used = nu_ref[0]

    @pl.when(i < n_used)
    def _():
        prev = be_ref[jnp.maximum(i - 1, 0)]

        @pl.when((i == 0) | (be_ref[i] != prev))
        def _():
            wg_bf[...] = wg_ref[0, 0].astype(BF16)
            wu_bf[...] = wu_ref[0, 0].astype(BF16)
            wd_bf[...] = wd_ref[0, 0].astype(BF16)

        half = MOE_BM // 2
        kh = wg_bf.shape[0] // 2
        xns = []
        for r in range(2):
            hi, lo = _unpack_pairs(xs_ref[r * half:(r + 1) * half, :])
            xns.append((hi.astype(BF16), lo.astype(BF16)))

        def up(w_bf):
            return [jnp.dot(xa, w_bf[0:kh, :], preferred_element_type=F32)
                    + jnp.dot(xb, w_bf[kh:2 * kh, :], preferred_element_type=F32) for xa, xb in xns]

        hgs = up(wg_bf)
        hus = up(wu_bf)
        hids = [(_silu(hg) * hu).astype(BF16) for hg, hu in zip(hgs, hus)]
        for r in range(2):
            ys_ref[r * half:(r + 1) * half, :] = _pack_pairs(
                jnp.dot(hids[r], wd_bf[...], preferred_element_type=F32))

    @pl.when(i >= n_used)
    def _():
        ys_ref[...] = jnp.zeros_like(ys_ref)


def _experts(xs, w_gate, w_up, w_down, layer, block_expert, n_used):
    r, dp = xs.shape
    d = 2 * dp
    de = w_gate.shape[3]
    nb = r // MOE_BM

    def wblk(i, be, nu):
        return (layer, be[jnp.minimum(i, nu[0] - 1)], 0, 0)

    return pl.pallas_call(
        _experts_body,
        out_shape=jax.ShapeDtypeStruct((r, dp), U32),
        grid_spec=pltpu.PrefetchScalarGridSpec(
            num_scalar_prefetch=2,
            grid=(nb,),
            in_specs=[pl.BlockSpec((MOE_BM, dp), lambda i, be, nu: (i, 0)),
                      pl.BlockSpec((1, 1, d, de), wblk),
                      pl.BlockSpec((1, 1, d, de), wblk),
                      pl.BlockSpec((1, 1, de, d), wblk)],
            out_specs=pl.BlockSpec((MOE_BM, dp), lambda i, be, nu: (i, 0)),
            scratch_shapes=[pltpu.VMEM((d, de), BF16), pltpu.VMEM((d, de), BF16),
                            pltpu.VMEM((de, d), BF16)]),
        compiler_params=_cparams(("arbitrary",)),
        name="moe_experts",
    )(block_expert, n_used, xs, w_gate, w_up, w_down)


def _combine_body(dest_ref, dnext_ref, h_ref, r_ref, gf_ref, ys_hbm, o_ref, buf, sem, *out_stage,
                  tb, blocks_per_seq, n_real, final_norm):
    i = pl.program_id(0)
    slot = i % 2

    def gather(d_ref, s):
        def issue(t, _):
            for k in range(2):
                d = d_ref[0, 0, k * tb + t]
                pltpu.make_async_copy(ys_hbm.at[pl.ds(d, 1)], buf.at[s, k, pl.ds(t, 1)], sem.at[s]).start()
            return 0

        lax.fori_loop(0, tb, issue, 0, unroll=8)

    @pl.when(i == 0)
    def _():
        gather(dest_ref, 0)

    @pl.when(i + 1 < pl.num_programs(0))
    def _():
        gather(dnext_ref, 1 - slot)

    for k in range(2):
        pltpu.make_async_copy(ys_hbm.at[pl.ds(0, tb)], buf.at[slot, k], sem.at[slot]).wait()

    r = r_ref[...]
    w1 = r[:, 2:3]
    w2 = r[:, 3:4]
    y1 = _unpack_pairs(buf[slot, 0])
    y2 = _unpack_pairs(buf[slot, 1])
    y = jnp.concatenate([w1 * y1[0] + w2 * y2[0], w1 * y1[1] + w2 * y2[1]], axis=1)
    pos0 = (i % blocks_per_seq) * tb
    hn = jnp.where(_row_valid(pos0, tb, n_real), h_ref[...] + y, 0.0)
    if not final_norm:
        o_ref[...] = hn
        return

    obuf, osem = out_stage
    seq = n_real - N_META
    ms = jnp.mean(hn * hn, axis=-1, keepdims=True)
    obuf[slot] = hn * lax.rsqrt(ms + EPS) * gf_ref[...]

    def frame_copy(step, s, start):
        b = step // blocks_per_seq
        j = step % blocks_per_seq
        row0 = b * seq + j * tb - OFF
        tail = OFF + seq - (blocks_per_seq - 1) * tb

        def go(src, dst):
            cp = pltpu.make_async_copy(src, dst, osem.at[s])
            cp.start() if start else cp.wait()

        @pl.when(j == 0)
        def _():
            go(obuf.at[s, pl.ds(OFF, tb - OFF)], o_ref.at[pl.ds(b * seq, tb - OFF)])

        @pl.when((j > 0) & (j < blocks_per_seq - 1))
        def _():
            go(obuf.at[s], o_ref.at[pl.ds(row0, tb)])

        @pl.when(j == blocks_per_seq - 1)
        def _():
            go(obuf.at[s, pl.ds(0, tail)], o_ref.at[pl.ds(row0, tail)])

    frame_copy(i, slot, True)

    @pl.when(i > 0)
    def _():
        frame_copy(i - 1, 1 - slot, False)

    @pl.when(i == pl.num_programs(0) - 1)
    def _():
        frame_copy(i, slot, False)


def _combine(h, route, dest3, ys, g_final, tp, n_real, final_norm):
    n, d = h.shape
    tb = UNIT
    last = n // tb - 1
    assert tp // tb >= 2
    if final_norm:
        out_shape = jax.ShapeDtypeStruct((n // tp * (n_real - N_META), d), F32)
        out_spec = pl.BlockSpec(memory_space=pl.ANY)
        out_stage = [pltpu.VMEM((2, tb, d), F32), pltpu.SemaphoreType.DMA((2,))]
    else:
        out_shape = jax.ShapeDtypeStruct((n, d), F32)
        out_spec = pl.BlockSpec((tb, d), lambda i: (i, 0))
        out_stage = []
    return pl.pallas_call(
        functools.partial(_combine_body, tb=tb, blocks_per_seq=tp // tb, n_real=n_real,
                          final_norm=final_norm),
        out_shape=out_shape,
        grid=(n // tb,),
        in_specs=[pl.BlockSpec((1, 1, 2 * tb), lambda i: (i, 0, 0), memory_space=pltpu.SMEM),
                  pl.BlockSpec((1, 1, 2 * tb), lambda i: (jnp.minimum(i + 1, last), 0, 0),
                               memory_space=pltpu.SMEM),
                  pl.BlockSpec((tb, d), lambda i: (i, 0)),
                  pl.BlockSpec((tb, LANES), lambda i: (i, 0)),
                  pl.BlockSpec((1, d), lambda i: (0, 0)),
                  pl.BlockSpec(memory_space=pl.ANY)],
        out_specs=out_spec,
        scratch_shapes=[pltpu.VMEM((2, 2, tb, d // 2), U32), pltpu.SemaphoreType.DMA((2,))] + out_stage,
        compiler_params=_cparams(("arbitrary",)),
        name="moe_combine",
    )(dest3, dest3, h, route, g_final, ys)


def _moe(h, xp, route, counts_f, er, w_gate, w_up, w_down, layer, g_final, tp, n_real, final_norm):
    n, d = h.shape
    n_pairs = 2 * n
    counts = counts_f[0, 0:N_EXPERTS].astype(I32)
    padded = (counts + MOE_BM - 1) // MOE_BM * MOE_BM
    p_ends = jnp.cumsum(padded)
    p_starts = p_ends - padded
    e_t = er[:, 0:2, :].astype(I32)
    start_of = jnp.sum(jnp.where(e_t[..., None] == jnp.arange(N_EXPERTS, dtype=I32), p_starts, 0), axis=-1)
    dest = start_of + er[:, 2:4, :].astype(I32)
    n_blocks = -(-(n_pairs + N_EXPERTS * (MOE_BM - 1)) // MOE_BM)
    block_row0 = jnp.arange(n_blocks, dtype=I32) * MOE_BM
    block_expert = jnp.minimum(jnp.sum((p_ends[None, :] <= block_row0[:, None]).astype(I32), axis=1),
                               N_EXPERTS - 1)
    n_used = (p_ends[-1:] // MOE_BM).astype(I32)
    meta = jnp.concatenate([counts, p_starts, padded, n_used]).astype(I32)
    dest3 = dest.reshape(n // UNIT, 1, 2 * UNIT)

    xs = _dispatch(xp, dest3, meta, n_blocks * MOE_BM)
    ys = _experts(xs, w_gate, w_up, w_down, layer, block_expert, n_used)
    return _combine(h, route, dest3, ys, g_final, tp, n_real, final_norm)


def _short_conv(x, halo_ref, w_ref, first):
    rows = x.shape[0]

    @pl.when(first)
    def _():
        halo_ref[...] = jnp.zeros_like(halo_ref)

    x_ext = jnp.concatenate([halo_ref[...], x], axis=0)
    halo_ref[...] = x[rows - 8:rows]
    acc = jnp.zeros_like(x)
    for j, tap in _shifted_taps(x_ext, 8, rows, SHORT_K):
        acc = acc + w_ref[j:j + 1, :] * tap
    return acc


def _bdot(a, b):
    return jnp.dot(a.astype(BF16), b.astype(BF16), preferred_element_type=F32)


def _gdn_pre_body(z_ref, sc_ref, st_ref, cw_ref, al_r_ref, dt_r_ref, al_c_ref, dt_c_ref,
                  u_ref, w_ref, qd_ref, kd_ref, at_ref, eg_ref, halo_ref):
    rows = z_ref.shape[0]
    hw = N_HEADS * HEAD_DIM
    x = _silu(_short_conv(z_ref[...].astype(F32), halo_ref, cw_ref, pl.program_id(1) == 0))

    sc = sc_ref[...]
    g_cols = -jnp.exp(al_r_ref[...]) * _softplus(sc + dt_r_ref[...])
    beta_cols = _sigmoid(sc)
    st = st_ref[...]
    g_rows = -jnp.exp(al_c_ref[:, 0:1]) * _softplus(st + dt_c_ref[:, 0:1])

    ri = lax.broadcasted_iota(I32, (rows, rows), 0)
    ci = lax.broadcasted_iota(I32, (rows, rows), 1)
    same64 = (ri >> 6) == (ci >> 6)
    same32 = (ri >> 5) == (ci >> 5)
    same16 = (ri >> 4) == (ci >> 4)
    lower = ri >= ci
    strict = ri > ci
    lane = lax.broadcasted_iota(I32, (rows, LANES), 1)
    eg_slab = jnp.zeros((rows, LANES), F32)
    scale = HEAD_DIM ** -0.5

    heads = range(N_HEADS)
    in_chunk_lower = same64 & lower
    in_chunk_upper = same64 & (ri <= ci)
    a_mats, nmats, rhss = [], [], []
    for h in heads:
        sl = slice(h * HEAD_DIM, (h + 1) * HEAD_DIM)
        q = x[:, sl]
        k = x[:, hw + h * HEAD_DIM:hw + (h + 1) * HEAD_DIM]
        v = x[:, 2 * hw + h * HEAD_DIM:2 * hw + (h + 1) * HEAD_DIM]
        q = q * lax.rsqrt(jnp.sum(q * q, axis=-1, keepdims=True) + EPS)
        k = k * lax.rsqrt(jnp.sum(k * k, axis=-1, keepdims=True) + EPS)
        g_col = g_cols[:, h:h + 1]
        beta = beta_cols[:, N_HEADS + h:N_HEADS + h + 1]
        g_row = g_rows[h:h + 1, :]

        gc_col = jnp.sum(jnp.where(in_chunk_lower, g_row, 0.0), axis=1, keepdims=True)
        gc_row = jnp.sum(jnp.where(in_chunk_upper, g_col, 0.0), axis=0, keepdims=True)
        gtot_col = jnp.sum(jnp.where(same64, g_row, 0.0), axis=1, keepdims=True)
        decay = jnp.exp(jnp.where(in_chunk_lower, gc_col - gc_row, NEG))
        eg = jnp.exp(gc_col)
        ekd = jnp.exp(gtot_col - gc_col)

        kb = k * beta
        kbf = k.astype(BF16)
        kk = lax.dot_general(kb.astype(BF16), kbf, (((1,), (1,)), ((), ())), preferred_element_type=F32)
        qs = q * scale
        qk = lax.dot_general(qs.astype(BF16), kbf, (((1,), (1,)), ((), ())), preferred_element_type=F32)
        attn = qk * decay
        a_mats.append(jnp.where(strict, kk * decay, 0.0))
        rhss.append(jnp.concatenate([v * beta, kb * eg], axis=1))
        qd_ref[:, sl] = (qs * eg).astype(BF16)
        kd_ref[:, sl] = (k * ekd).astype(BF16)
        for c in range(rows // CHUNK):
            cs = slice(c * CHUNK, (c + 1) * CHUNK)
            at_ref[h, cs, :] = attn[cs, cs].astype(BF16)
        eg_slab = jnp.where(lane == h, eg, eg_slab)
    eg_ref[...] = eg_slab

    nmats = [jnp.where(same16, -a, 0.0) for a in a_mats]
    qqs = nmats
    for _ in range(3):
        qqs = [_bdot(qq, qq) for qq in qqs]
        prods = [_bdot(qq, nm) for qq, nm in zip(qqs, nmats)]
        nmats = [nm + qq + pr for nm, qq, pr in zip(nmats, qqs, prods)]
    for level_mask, inner_mask in ((same32, same16), (same64, same32)):
        sel = level_mask & jnp.logical_not(inner_mask)
        offs = [jnp.where(sel, a, 0.0) for a in a_mats]
        bmats = [off + _bdot(nm, off) for nm, off in zip(nmats, offs)]
        prods = [_bdot(bm, nm) for bm, nm in zip(bmats, nmats)]
        nmats = [nm - bm - pr for nm, bm, pr in zip(nmats, bmats, prods)]

    for h in heads:
        sl = slice(h * HEAD_DIM, (h + 1) * HEAD_DIM)
        uw = rhss[h] + _bdot(nmats[h], rhss[h])
        u_ref[:, sl] = uw[:, 0:HEAD_DIM]
        w_ref[:, sl] = uw[:, HEAD_DIM:2 * HEAD_DIM].astype(BF16)


def _gdn_pre(z, sc, st, conv_w, a_log, dt_bias, bsz, tp):
    n = z.shape[0]
    hw = N_HEADS * HEAD_DIM
    rows = UNIT
    nu = tp // rows
    pad_r = lambda v: jnp.pad(v.astype(F32), (0, LANES - N_HEADS)).reshape(1, LANES)
    pad_c = lambda v: jnp.broadcast_to(jnp.pad(v.astype(F32), (0, 16 - N_HEADS))[:, None], (16, LANES))
    outs = pl.pallas_call(
        _gdn_pre_body,
        out_shape=[jax.ShapeDtypeStruct((n, hw), F32),
                   jax.ShapeDtypeStruct((n, hw), BF16),
                   jax.ShapeDtypeStruct((n, hw), BF16),
                   jax.ShapeDtypeStruct((n, hw), BF16),
                   jax.ShapeDtypeStruct((N_HEADS, n, CHUNK), BF16),
                   jax.ShapeDtypeStruct((n, LANES), F32)],
        grid=(bsz, nu),
        in_specs=[pl.BlockSpec((rows, 3 * hw), lambda b, i: (b * nu + i, 0)),
                  pl.BlockSpec((rows, LANES), lambda b, i: (b * nu + i, 0)),
                  pl.BlockSpec((16, rows), lambda b, i: (0, b * nu + i)),
                  pl.BlockSpec((8, 3 * hw), lambda b, i: (0, 0)),
                  pl.BlockSpec((1, LANES), lambda b, i: (0, 0)),
                  pl.BlockSpec((1, LANES), lambda b, i: (0, 0)),
                  pl.BlockSpec((16, LANES), lambda b, i: (0, 0)),
                  pl.BlockSpec((16, LANES), lambda b, i: (0, 0))],
        out_specs=[pl.BlockSpec((rows, hw), lambda b, i: (b * nu + i, 0)),
                   pl.BlockSpec((rows, hw), lambda b, i: (b * nu + i, 0)),
                   pl.BlockSpec((rows, hw), lambda b, i: (b * nu + i, 0)),
                   pl.BlockSpec((rows, hw), lambda b, i: (b * nu + i, 0)),
                   pl.BlockSpec((N_HEADS, rows, CHUNK), lambda b, i: (0, b * nu + i, 0)),
                   pl.BlockSpec((rows, LANES), lambda b, i: (b * nu + i, 0))],
        scratch_shapes=[pltpu.VMEM((8, 3 * hw), F32)],
        compiler_params=_cparams(("arbitrary", "arbitrary")),
        name="gdn_pre",
    )(z, sc, st, jnp.pad(conv_w.astype(F32), ((0, 8 - SHORT_K), (0, 0))),
      pad_r(a_log), pad_r(dt_bias), pad_c(a_log), pad_c(dt_bias))
    return outs


def _gdn_scan_body(u_ref, w_ref, qd_ref, kd_ref, at_ref, eg_ref, z_ref, gn_ref, o_ref, s_ref, *, bsz):
    @pl.when(pl.program_id(0) == 0)
    def _():
        s_ref[...] = jnp.zeros_like(s_ref)

    units = [(b, h, slice(h * HEAD_DIM, (h + 1) * HEAD_DIM)) for b in range(bsz) for h in range(N_HEADS)]
    states = [s_ref[b * N_HEADS + h] for b, h, _ in units]
    rs = [jnp.dot(jnp.concatenate([w_ref[b, :, sl], qd_ref[b, :, sl]], axis=0), s.astype(BF16),
                  preferred_element_type=F32) for (b, h, sl), s in zip(units, states)]
    vbs = [(u_ref[b, :, sl] - r[0:CHUNK]).astype(BF16) for (b, h, sl), r in zip(units, rs)]
    intra = [jnp.dot(at_ref[h, b], vb, preferred_element_type=F32) for (b, h, sl), vb in zip(units, vbs)]
    outer = [lax.dot_general(kd_ref[b, :, sl], vb, (((0,), (0,)), ((), ())), preferred_element_type=F32)
             for (b, h, sl), vb in zip(units, vbs)]
    for (b, h, sl), s, r, a, kv in zip(units, states, rs, intra, outer):
        s_ref[b * N_HEADS + h] = s * eg_ref[b, CHUNK - 1:CHUNK, h:h + 1] + kv
        o = r[CHUNK:2 * CHUNK] + a
        on = o * lax.rsqrt(jnp.mean(o * o, axis=-1, keepdims=True) + EPS) * gn_ref[...]
        o_ref[b, :, sl] = (on * _silu(z_ref[b, :, sl].astype(F32))).astype(o_ref.dtype)


def _gdn_scan(u, w, qd, kd, attn, eg, z, out_norm_g, bsz, tp, z_col):
    hw = N_HEADS * HEAD_DIM
    nc = tp // CHUNK
    v3 = lambda a: a.reshape(bsz, tp, a.shape[-1])
    blk3 = pl.BlockSpec((bsz, CHUNK, hw), lambda c: (0, c, 0))
    return pl.pallas_call(
        functools.partial(_gdn_scan_body, bsz=bsz),
        out_shape=jax.ShapeDtypeStruct((bsz, tp, hw), BF16),
        grid=(nc,),
        in_specs=[blk3, blk3, blk3, blk3,
                  pl.BlockSpec((N_HEADS, bsz, CHUNK, CHUNK), lambda c: (0, 0, c, 0)),
                  pl.BlockSpec((bsz, CHUNK, LANES), lambda c: (0, c, 0)),
                  pl.BlockSpec((bsz, CHUNK, hw), lambda c: (0, c, z_col)),
                  pl.BlockSpec((1, HEAD_DIM), lambda c: (0, 0))],
        out_specs=blk3,
        scratch_shapes=[pltpu.VMEM((bsz * N_HEADS, HEAD_DIM, HEAD_DIM), F32)],
        compiler_params=_cparams(("arbitrary",)),
        name="gdn_scan",
    )(v3(u), v3(w), v3(qd), v3(kd), attn.reshape(N_HEADS, bsz, tp, CHUNK), v3(eg), v3(z),
      out_norm_g.astype(F32).reshape(1, HEAD_DIM)).reshape(bsz * tp, hw)


def _lru_body(x_ref, gate_ref, cw_ref, cb_ref, wr_ref, br_ref, wi_ref, bi_ref, lam_ref, o_ref,
              halo_ref, hc_ref, *, bt, n_real):
    first = pl.program_id(1) == 0

    @pl.when(first)
    def _():
        hc_ref[...] = jnp.zeros_like(hc_ref)

    x = _short_conv(x_ref[...].astype(F32), halo_ref, cw_ref, first) + cb_ref[...]
    nblk = wr_ref.shape[0]
    bd = wr_ref.shape[1]
    xb = x.astype(BF16)
    rg = jnp.concatenate([jnp.dot(xb[:, n * bd:(n + 1) * bd], wr_ref[n], preferred_element_type=F32)
                          for n in range(nblk)], axis=1)
    ig = jnp.concatenate([jnp.dot(xb[:, n * bd:(n + 1) * bd], wi_ref[n], preferred_element_type=F32)
                          for n in range(nblk)], axis=1)
    r = _sigmoid(rg + br_ref[...])
    ig = _sigmoid(ig + bi_ref[...])
    log_a = (-LRU_C * _softplus(-lam_ref[...])) * r
    a = jnp.exp(log_a)
    th = jnp.tanh(log_a)
    b = jnp.sqrt(-2.0 * th / (1.0 - th)) * (ig * x)
    b = jnp.where(_row_valid(pl.program_id(1) * bt, bt, n_real), b, 0.0)

    row = lax.broadcasted_iota(I32, (bt, 1), 0)
    d = 1
    while d < bt:
        keep = row >= d
        a_sh = jnp.where(keep, pltpu.roll(a, d, 0), 1.0)
        b_sh = jnp.where(keep, pltpu.roll(b, d, 0), 0.0)
        b = a * b_sh + b
        a = a * a_sh
        d *= 2
    hs = a * hc_ref[0:1, :] + b
    hc_ref[...] = jnp.broadcast_to(hs[bt - 1:bt, :], hc_ref.shape)
    o_ref[...] = (hs * jax.nn.gelu(gate_ref[...].astype(F32))).astype(o_ref.dtype)


def _lru(z, conv_w, conv_b, w_rg, b_rg, w_ig, b_ig, lam, bsz, tp, n_real, x_col, gate_col):
    n = z.shape[0]
    c = conv_w.shape[1]
    bt = UNIT
    nt = tp // bt
    row = lambda v: v.astype(F32).reshape(1, c)
    wspec = pl.BlockSpec(w_rg.shape, lambda b, i: (0, 0, 0))
    vspec = pl.BlockSpec((1, c), lambda b, i: (0, 0))
    return pl.pallas_call(
        functools.partial(_lru_body, bt=bt, n_real=n_real),
        out_shape=jax.ShapeDtypeStruct((n, c), BF16),
        grid=(bsz, nt),
        in_specs=[pl.BlockSpec((bt, c), lambda b, i: (b * nt + i, x_col)),
                  pl.BlockSpec((bt, c), lambda b, i: (b * nt + i, gate_col)),
                  pl.BlockSpec((8, c), lambda b, i: (0, 0)),
                  vspec, wspec, vspec, wspec, vspec, vspec],
        out_specs=pl.BlockSpec((bt, c), lambda b, i: (b * nt + i, 0)),
        scratch_shapes=[pltpu.VMEM((8, c), F32), pltpu.VMEM((8, c), F32)],
        compiler_params=_cparams(("arbitrary", "arbitrary")),
        name="rg_lru",
    )(z, z, jnp.pad(conv_w.astype(F32), ((0, 8 - SHORT_K), (0, 0))), row(conv_b),
      w_rg.astype(BF16), row(b_rg), w_ig.astype(BF16), row(b_ig), row(lam))


def _pick(n, candidates):
    for c in candidates:
        if n % c == 0:
            return c
    raise ValueError(f"no block size in {candidates} divides {n}")


def _pack_weights_body(a_ref, b_ref, o_ref, *, scaled_tiles, scale, head_tiles, gap):
    j = pl.program_id(0)
    tile = o_ref.shape[1]

    @pl.when(j < scaled_tiles)
    def _():
        o_ref[...] = (a_ref[...] * scale).astype(BF16)

    @pl.when((j >= scaled_tiles) & (j < head_tiles))
    def _():
        o_ref[...] = a_ref[...].astype(BF16)

    @pl.when(j >= head_tiles)
    def _():
        x = jnp.concatenate([a_ref[...], b_ref[...]], axis=1)
        o_ref[...] = x[:, gap:gap + tile].astype(BF16)


def _pack_inproj_weights(w_in, head, gap, scaled=0, scale=1.0):
    d, total = w_in.shape
    tile = 512
    n_out = total - gap
    return pl.pallas_call(
        functools.partial(_pack_weights_body, scaled_tiles=scaled // tile, scale=scale,
                          head_tiles=head // tile, gap=gap),
        out_shape=jax.ShapeDtypeStruct((d, n_out), BF16),
        grid=(n_out // tile,),
        in_specs=[pl.BlockSpec((d, tile), lambda j: (0, j)),
                  pl.BlockSpec((d, LANES), lambda j: (0, (tile // LANES) * (j + 1)))],
        out_specs=pl.BlockSpec((d, tile), lambda j: (0, j)),
        compiler_params=_cparams(("arbitrary",)),
        name="pack_inproj_weights",
    )(w_in, w_in)


def _small_weights(cols):
    k = cols.shape[1]
    return (jnp.pad(cols.T, ((0, 16 - k), (0, 0))).astype(BF16),
            jnp.pad(cols, ((0, 0), (0, LANES - k))).astype(BF16))


def _router_weights(w_group, b_group, w_expert, b_expert):
    w = jnp.concatenate([w_expert, w_group], axis=1).astype(F32)
    b = jnp.concatenate([b_expert, b_group]).astype(F32)
    k = w.shape[1]
    w = jnp.pad(w, ((0, 0), (0, LANES - k)))
    w_hi = w.astype(BF16)
    w_lo = (w - w_hi.astype(F32)).astype(BF16)
    return jnp.concatenate([w_hi, w_lo], axis=1), jnp.pad(b, (0, LANES - k)).reshape(1, LANES)


def kernel(x, meta_tokens, norm_mix_g, norm_ffn_g, norm_final_g, ab_w_in, ab_forget_b, ab_conv_w, ab_conv_b, ab_ln_g, ab_ln_b, ab_w_out, cd_w_in, cd_qkv_conv_w, cd_a_log, cd_dt_bias, cd_out_norm_g, cd_lru_conv_w, cd_lru_conv_b, cd_w_rg, cd_b_rg, cd_w_ig, cd_b_ig, cd_lru_lambda, cd_w_out, moe_w_group, moe_b_group, moe_w_expert, moe_b_expert, moe_w_gate, moe_w_up, moe_w_down):
    bsz, seq, d = x.shape
    depth = norm_mix_g.shape[0]
    n_real = N_META + seq
    tp = -(-(OFF + seq) // UNIT) * UNIT
    n = bsz * tp
    hw = N_HEADS * HEAD_DIM
    bm = _pick(n, (768, 512, 256))
    bn = 1024

    h = _embed(x.astype(F32).reshape(bsz * seq, d), meta_tokens.astype(F32), bsz, seq, tp)
    row = lambda v: v.astype(F32).reshape(1, -1)

    for layer in range(depth):
        i = layer // 2
        g_mix = row(norm_mix_g[layer])
        if layer % 2 == 0:
            w_in = ab_w_in[i]
            w_main = _pack_inproj_weights(w_in.astype(F32), 3 * hw, N_HEADS, scaled=hw,
                                          scale=LOG2E * HEAD_DIM ** -0.5)
            wst, wsc = _small_weights(w_in[:, 3 * hw:3 * hw + N_HEADS])
            z, _, f_slab = _norm_inproj(h, g_mix, w_main, wst, wsc, bm, bn)
            kx, vx = _fox_prep(z, f_slab, ab_forget_b[i], bsz, tp, n_real)
            ya = _fox_attn(z, kx, vx, bsz, tp)
            yb = _conformer(z, ab_conv_w[i], ab_conv_b[i], ab_ln_g[i], ab_ln_b[i], bsz, tp, 3, 4)
            w_out = ab_w_out[i].astype(BF16)
        else:
            w_in = cd_w_in[i]
            w_main = _pack_inproj_weights(w_in.astype(F32), 3 * hw, 2 * N_HEADS)
            wst, wsc = _small_weights(w_in[:, 3 * hw:3 * hw + 2 * N_HEADS])
            z, zt, zc = _norm_inproj(h, g_mix, w_main, wst, wsc, bm, bn)
            u, w, qd, kd, attn, eg = _gdn_pre(z, zc, zt, cd_qkv_conv_w[i], cd_a_log[i], cd_dt_bias[i],
                                              bsz, tp)
            ya = _gdn_scan(u, w, qd, kd, attn, eg, z, cd_out_norm_g[i], bsz, tp, 3)
            yb = _lru(z, cd_lru_conv_w[i], cd_lru_conv_b[i], cd_w_rg[i], cd_b_rg[i], cd_w_ig[i],
                      cd_b_ig[i], cd_lru_lambda[i], bsz, tp, n_real, 4, 5)
            w_out = cd_w_out[i].astype(BF16)
        g_ffn = row(norm_ffn_g[layer])
        w_r, b_r = _router_weights(moe_w_group[layer], moe_b_group[layer],
                                   moe_w_expert[layer], moe_b_expert[layer])
        h, route, counts, xp, er = _outproj_router(ya, yb, h, w_out, g_ffn, w_r, b_r, tp, n_real)
        h = _moe(h, xp, route, counts, er, moe_w_gate, moe_w_up, moe_w_down, layer,
                 row(norm_final_g), tp, n_real, final_norm=(layer == depth - 1))
    return h.reshape(bsz, seq, d).astype(x.dtype)
```

```python
import functools

import jax
import jax.numpy as jnp
from jax import lax
from jax.experimental import pallas as pl
from jax.experimental.pallas import tpu as pltpu

F32 = jnp.float32
BF16 = jnp.bfloat16
I32 = jnp.int32
U32 = jnp.uint32

EPS = 1e-6
N_META = 16
CHUNK = 64
FRONT = CHUNK - N_META
OFF = FRONT + N_META
LANES = 128
UNIT = 256
HEAD_DIM = 128
N_HEADS = 8
CONF_K = 31
SHORT_K = 4
LRU_C = 8.0
N_GROUPS = 4
EXPERTS_PER_GROUP = 8
N_EXPERTS = N_GROUPS * EXPERTS_PER_GROUP
MOE_BM = 256
NEG = -1e30
LOG2E = 1.4426950408889634
VMEM_LIMIT = 56 * 1024 * 1024


def _cparams(sem):
    return pltpu.CompilerParams(dimension_semantics=sem, vmem_limit_bytes=VMEM_LIMIT)


def _row_valid(pos0, rows, n_real):
    pos = pos0 + lax.broadcasted_iota(I32, (rows, 1), 0)
    return (pos >= FRONT) & (pos < FRONT + n_real)


def _sigmoid(x):
    return 1.0 / (1.0 + jnp.exp(-x))


def _softplus(x):
    return jnp.maximum(x, 0.0) + jnp.log1p(jnp.exp(-jnp.abs(x)))


def _silu(x):
    return x * _sigmoid(x)


def _embed_body(x_ref, meta_ref, h_hbm, zero_ref, sem, *, seq, tp, rows):
    b = pl.program_id(0)
    i = pl.program_id(1)
    base = b * tp
    tail = tp - OFF - seq
    frames = pltpu.make_async_copy(x_ref, h_hbm.at[pl.ds(base + OFF + i * rows, rows)], sem.at[0])
    frames.start()

    @pl.when(i == 0)
    def _():
        zero_ref[...] = jnp.zeros_like(zero_ref)
        copies = [
            pltpu.make_async_copy(meta_ref, h_hbm.at[pl.ds(base + FRONT, N_META)], sem.at[1]),
            pltpu.make_async_copy(zero_ref.at[pl.ds(0, FRONT)], h_hbm.at[pl.ds(base, FRONT)], sem.at[2]),
        ]
        if tail:
            copies.append(pltpu.make_async_copy(zero_ref.at[pl.ds(0, tail)],
                                                h_hbm.at[pl.ds(base + OFF + seq, tail)], sem.at[3]))
        for c in copies:
            c.start()
        for c in copies:
            c.wait()

    frames.wait()


def _embed(x2d, meta_tokens, bsz, seq, tp):
    d = x2d.shape[1]
    rows = _pick(seq, (1024, 512, 256, 128, 64, 32, 16, 8))
    per_seq = seq // rows
    return pl.pallas_call(
        functools.partial(_embed_body, seq=seq, tp=tp, rows=rows),
        out_shape=jax.ShapeDtypeStruct((bsz * tp, d), F32),
        grid=(bsz, per_seq),
        in_specs=[pl.BlockSpec((rows, d), lambda b, i: (b * per_seq + i, 0)),
                  pl.BlockSpec((N_META, d), lambda b, i: (0, 0))],
        out_specs=pl.BlockSpec(memory_space=pl.ANY),
        scratch_shapes=[pltpu.VMEM((max(FRONT, tp - OFF - seq), d), F32), pltpu.SemaphoreType.DMA((4,))],
        compiler_params=_cparams(("arbitrary", "arbitrary")),
        name="embed",
    )(x2d, meta_tokens)


def _norm_inproj_body(h_ref, g_ref, w_ref, wst_ref, wsc_ref, o_ref, ot_ref, oc_ref, xn_ref):
    @pl.when(pl.program_id(1) == 0)
    def _():
        x = h_ref[...]
        ms = jnp.mean(x * x, axis=-1, keepdims=True)
        xn = (x * lax.rsqrt(ms + EPS) * g_ref[...]).astype(BF16)
        xn_ref[...] = xn
        ot_ref[...] = lax.dot_general(wst_ref[...], xn, (((1,), (1,)), ((), ())),
                                      preferred_element_type=F32)
        oc_ref[...] = jnp.dot(xn, wsc_ref[...], preferred_element_type=F32)

    o_ref[...] = jnp.dot(xn_ref[...], w_ref[...], preferred_element_type=F32).astype(o_ref.dtype)


def _norm_inproj(h, g, w_main, w_small_t, w_small_c, bm, bn):
    n, d = h.shape
    nw = w_main.shape[1]
    return pl.pallas_call(
        _norm_inproj_body,
        out_shape=[jax.ShapeDtypeStruct((n, nw), BF16),
                   jax.ShapeDtypeStruct((16, n), F32),
                   jax.ShapeDtypeStruct((n, LANES), F32)],
        grid=(n // bm, nw // bn),
        in_specs=[pl.BlockSpec((bm, d), lambda i, j: (i, 0)),
                  pl.BlockSpec((1, d), lambda i, j: (0, 0)),
                  pl.BlockSpec((d, bn), lambda i, j: (0, j)),
                  pl.BlockSpec((16, d), lambda i, j: (0, 0)),
                  pl.BlockSpec((d, LANES), lambda i, j: (0, 0))],
        out_specs=[pl.BlockSpec((bm, bn), lambda i, j: (i, j)),
                   pl.BlockSpec((16, bm), lambda i, j: (0, i)),
                   pl.BlockSpec((bm, LANES), lambda i, j: (i, 0))],
        scratch_shapes=[pltpu.VMEM((bm, d), BF16)],
        compiler_params=_cparams(("arbitrary", "arbitrary")),
        name="norm_inproj",
    )(h, g, w_main, w_small_t, w_small_c)


def _split3(x):
    p1 = x.astype(BF16)
    r1 = x - p1.astype(F32)
    p2 = r1.astype(BF16)
    p3 = (r1 - p2.astype(F32)).astype(BF16)
    return p1, p2, p3


def _fox_prep_body(k_ref, v_ref, f_ref, fb_ref, kx_ref, vx_ref, carry_ref, *, rows, n_real):
    i = pl.program_id(1)

    @pl.when(i == 0)
    def _():
        carry_ref[...] = jnp.zeros_like(carry_ref)

    f = f_ref[...] + fb_ref[...]
    lf = jnp.minimum(f, 0.0) - jnp.log1p(jnp.exp(-jnp.abs(f)))
    pos = i * rows + lax.broadcasted_iota(I32, (rows, 1), 0)
    lf = jnp.where((pos >= FRONT) & (pos < FRONT + n_real), lf, 0.0)
    rr = lax.broadcasted_iota(I32, (rows, rows), 0)
    cc = lax.broadcasted_iota(I32, (rows, rows), 1)
    tri = jnp.where(rr >= cc, 1.0, 0.0).astype(BF16)
    c = carry_ref[0:1, :]
    for piece in _split3(lf):
        c = c + jnp.dot(tri, piece, preferred_element_type=F32)
    carry_ref[...] = jnp.broadcast_to(c[rows - 1:rows, :], carry_ref.shape)
    cs = jnp.where(pos < FRONT, NEG, -LOG2E * c)
    p1, p2, p3 = (p.astype(F32) for p in _split3(cs))
    lane = lax.broadcasted_iota(I32, (rows, LANES), 1)
    ones_col = jnp.where(lane == 0, 1.0, 0.0).astype(BF16)
    for h in range(N_HEADS):
        ext = jnp.where(lane == 0, p1[:, h:h + 1],
                        jnp.where(lane == 1, p2[:, h:h + 1], jnp.where(lane == 2, p3[:, h:h + 1], 0.0)))
        kx_ref[:, (2 * h) * HEAD_DIM:(2 * h + 1) * HEAD_DIM] = k_ref[:, h * HEAD_DIM:(h + 1) * HEAD_DIM]
        kx_ref[:, (2 * h + 1) * HEAD_DIM:(2 * h + 2) * HEAD_DIM] = ext.astype(BF16)
        vx_ref[:, (2 * h) * HEAD_DIM:(2 * h + 1) * HEAD_DIM] = v_ref[:, h * HEAD_DIM:(h + 1) * HEAD_DIM]
        vx_ref[:, (2 * h + 1) * HEAD_DIM:(2 * h + 2) * HEAD_DIM] = ones_col


def _fox_prep(z, f_slab, forget_b, bsz, tp, n_real):
    n = z.shape[0]
    hw = N_HEADS * HEAD_DIM
    rows = UNIT
    nt = tp // rows
    fb = jnp.pad(forget_b.astype(F32), (0, LANES - N_HEADS)).reshape(1, LANES)
    return pl.pallas_call(
        functools.partial(_fox_prep_body, rows=rows, n_real=n_real),
        out_shape=[jax.ShapeDtypeStruct((n, 2 * hw), BF16), jax.ShapeDtypeStruct((n, 2 * hw), BF16)],
        grid=(bsz, nt),
        in_specs=[pl.BlockSpec((rows, hw), lambda b, i: (b * nt + i, 1)),
                  pl.BlockSpec((rows, hw), lambda b, i: (b * nt + i, 2)),
                  pl.BlockSpec((rows, LANES), lambda b, i: (b * nt + i, 0)),
                  pl.BlockSpec((1, LANES), lambda b, i: (0, 0))],
        out_specs=[pl.BlockSpec((rows, 2 * hw), lambda b, i: (b * nt + i, 0)),
                   pl.BlockSpec((rows, 2 * hw), lambda b, i: (b * nt + i, 0))],
        scratch_shapes=[pltpu.VMEM((8, LANES), F32)],
        compiler_params=_cparams(("arbitrary", "arbitrary")),
        name="fox_prep",
    )(z, z, f_slab, fb)


def _fox_attn_body(q_ref, k_ref, v_ref, o_ref, *, blk, n_split):
    qi = pl.program_id(2)
    half = blk // n_split
    lane = lax.broadcasted_iota(I32, (blk, HEAD_DIM), 1)
    q = jnp.concatenate([q_ref[...], jnp.where(lane < 3, 1.0, 0.0).astype(BF16)], axis=1)
    qs = tuple(q[r * half:(r + 1) * half] for r in range(n_split))

    def step(js, carry, diag_last):
        starts = [pl.multiple_of(j * blk, blk) for j in js]
        ss = []
        for r, qh in enumerate(qs):
            srow = []
            for b, st in enumerate(starts):
                on_diag = diag_last and b == len(js) - 1
                ncol = (r + 1) * half if on_diag else blk
                s = lax.dot_general(qh, k_ref[pl.ds(st, ncol), :], (((1,), (1,)), ((), ())),
                                    preferred_element_type=F32)
                if on_diag:
                    row = r * half + lax.broadcasted_iota(I32, (half, ncol), 0)
                    col = lax.broadcasted_iota(I32, (half, ncol), 1)
                    s = jnp.where(col <= row, s, NEG)
                srow.append((s, st, ncol))
            ss.append(srow)
        out = []
        for r in range(n_split):
            m = carry[2 * r]
            m_new = m
            for s, _, _ in ss[r]:
                m_new = jnp.maximum(m_new, jnp.max(s, axis=1, keepdims=True))
            acc = jnp.exp2(m - m_new) * carry[2 * r + 1]
            for s, st, ncol in ss[r]:
                acc = acc + jnp.dot(jnp.exp2(s - m_new).astype(BF16), v_ref[pl.ds(st, ncol), :],
                                    preferred_element_type=F32)
            out += [m_new, acc]
        return tuple(out)

    init = (jnp.full((half, 1), NEG, F32), jnp.zeros((half, 2 * HEAD_DIM), F32)) * n_split
    carry = lax.fori_loop(0, qi // 2, lambda j, c: step((2 * j, 2 * j + 1), c, False), init)
    res = lax.cond(qi % 2 == 1, lambda c: step((qi - 1, qi), c, True),
                   lambda c: step((qi,), c, True), carry)
    for r in range(n_split):
        acc = res[2 * r + 1]
        o_ref[r * half:(r + 1) * half, :] = (acc[:, 0:HEAD_DIM] / acc[:, HEAD_DIM:HEAD_DIM + 1]
                                             ).astype(o_ref.dtype)


def _fox_attn(z, kx, vx, bsz, tp):
    n = z.shape[0]
    blk = _pick(tp, (768, 512, 256))
    nq = tp // blk
    return pl.pallas_call(
        functools.partial(_fox_attn_body, blk=blk, n_split=2),
        out_shape=jax.ShapeDtypeStruct((n, N_HEADS * HEAD_DIM), BF16),
        grid=(bsz, N_HEADS, nq),
        in_specs=[pl.BlockSpec((blk, HEAD_DIM), lambda b, h, i: (b * nq + i, h)),
                  pl.BlockSpec((tp, 2 * HEAD_DIM), lambda b, h, i: (b, h)),
                  pl.BlockSpec((tp, 2 * HEAD_DIM), lambda b, h, i: (b, h))],
        out_specs=pl.BlockSpec((blk, HEAD_DIM), lambda b, h, i: (b * nq + i, h)),
        compiler_params=_cparams(("arbitrary", "arbitrary", "arbitrary")),
        name="fox_attn",
    )(z, kx, vx)


def _shifted_taps(x_ext, halo, rows, n_taps):
    for r in range(min(8, n_taps)):
        rolled = x_ext if r == 0 else pltpu.roll(x_ext, r, 0)
        for q in range(halo // 8):
            s = 8 * q + r
            if s > n_taps - 1:
                continue
            yield n_taps - 1 - s, rolled[halo - 8 * q:halo - 8 * q + rows]


def _conformer_body(a_ref, b_ref, w_ref, cb_ref, lg_ref, lb_ref, o_ref, halo_ref, *, bt):
    halo = halo_ref.shape[0]

    @pl.when(pl.program_id(1) == 0)
    def _():
        halo_ref[...] = jnp.zeros_like(halo_ref)

    u = a_ref[...].astype(F32) * _sigmoid(b_ref[...].astype(F32))
    x_ext = jnp.concatenate([halo_ref[...], u], axis=0)
    halo_ref[...] = u[bt - halo:bt]
    acc = jnp.zeros_like(u)
    for j, tap in _shifted_taps(x_ext, halo, bt, CONF_K):
        acc = acc + w_ref[j:j + 1, :] * tap
    y = acc + cb_ref[...]
    mu = jnp.mean(y, axis=-1, keepdims=True)
    yc = y - mu
    var = jnp.mean(yc * yc, axis=-1, keepdims=True)
    yn = yc * lax.rsqrt(var + EPS) * lg_ref[...] + lb_ref[...]
    o_ref[...] = _silu(yn).astype(o_ref.dtype)


def _conformer(z, conv_w, conv_b, ln_g, ln_b, bsz, tp, col_a, col_b):
    n = z.shape[0]
    c = conv_w.shape[1]
    bt = UNIT
    nt = tp // bt
    w = jnp.pad(conv_w.astype(F32), ((0, 32 - CONF_K), (0, 0)))
    row = lambda v: v.astype(F32).reshape(1, c)
    return pl.pallas_call(
        functools.partial(_conformer_body, bt=bt),
        out_shape=jax.ShapeDtypeStruct((n, c), BF16),
        grid=(bsz, nt),
        in_specs=[pl.BlockSpec((bt, c), lambda b, i: (b * nt + i, col_a)),
                  pl.BlockSpec((bt, c), lambda b, i: (b * nt + i, col_b)),
                  pl.BlockSpec((32, c), lambda b, i: (0, 0)),
                  pl.BlockSpec((1, c), lambda b, i: (0, 0)),
                  pl.BlockSpec((1, c), lambda b, i: (0, 0)),
                  pl.BlockSpec((1, c), lambda b, i: (0, 0))],
        out_specs=pl.BlockSpec((bt, c), lambda b, i: (b * nt + i, 0)),
        scratch_shapes=[pltpu.VMEM((32, c), F32)],
        compiler_params=_cparams(("arbitrary", "arbitrary")),
        name="conformer_conv",
    )(z, z, w, row(conv_b), row(ln_g), row(ln_b))


def _outproj_router_body(ya_ref, yb_ref, h_ref, w_ref, g_ref, wr_ref, br_ref, ho_ref, r_ref, cnt_ref,
                         xp_ref, er_ref, *, bm, blocks_per_seq, n_real):
    kh = ya_ref.shape[1]
    pos0 = (pl.program_id(0) % blocks_per_seq) * bm
    rh = bm // 2
    parts = [slice(r * rh, (r + 1) * rh) for r in range(2)]
    ys = [jnp.dot(ya_ref[p, :], w_ref[0:kh, :], preferred_element_type=F32)
          + jnp.dot(yb_ref[p, :], w_ref[kh:2 * kh, :], preferred_element_type=F32) for p in parts]
    lgs = []
    for r, (p, y) in enumerate(zip(parts, ys)):
        hn = jnp.where(_row_valid(pos0 + r * rh, rh, n_real), h_ref[p, :] + y, 0.0)
        ho_ref[p, :] = hn
        ms = jnp.mean(hn * hn, axis=-1, keepdims=True)
        xn = hn * lax.rsqrt(ms + EPS) * g_ref[...]
        xp_ref[p, :] = _pack_pairs(xn)
        xh = xn.astype(BF16)
        xl = (xn - xh.astype(F32)).astype(BF16)
        t = jnp.dot(xh, wr_ref[...], preferred_element_type=F32)
        lgs.append(t[:, 0:LANES] + t[:, LANES:2 * LANES]
                   + jnp.dot(xl, wr_ref[:, 0:LANES], preferred_element_type=F32) + br_ref[...])
    lg = jnp.concatenate(lgs, axis=0)
    lane = lax.broadcasted_iota(I32, lg.shape, 1)
    lane_f = lane.astype(F32)
    big = float(LANES)

    is_grp = (lane >= N_EXPERTS) & (lane < N_EXPERTS + N_GROUPS)
    gl = jnp.where(is_grp, lg, -jnp.inf)
    gmax = jnp.max(gl, axis=1, keepdims=True)
    gidx = jnp.min(jnp.where(gl == gmax, lane_f, big), axis=1, keepdims=True) - N_EXPERTS
    g_prob = 1.0 / jnp.sum(jnp.where(is_grp, jnp.exp(lg - gmax), 0.0), axis=1, keepdims=True)

    lo = gidx * EXPERTS_PER_GROUP
    in_grp = (lane_f >= lo) & (lane_f < lo + EXPERTS_PER_GROUP)
    el = jnp.where(in_grp, lg, -jnp.inf)
    m1 = jnp.max(el, axis=1, keepdims=True)
    i1 = jnp.min(jnp.where(el == m1, lane_f, big), axis=1, keepdims=True)
    el2 = jnp.where(lane_f == i1, -jnp.inf, el)
    m2 = jnp.max(el2, axis=1, keepdims=True)
    i2 = jnp.min(jnp.where(el2 == m2, lane_f, big), axis=1, keepdims=True)
    e2 = jnp.exp(m2 - m1)
    w1 = g_prob / (1.0 + e2)
    w2 = g_prob * e2 / (1.0 + e2)

    @pl.when(pl.program_id(0) == 0)
    def _():
        cnt_ref[...] = jnp.zeros_like(cnt_ref)

    oh1 = lane_f == i1
    oh2 = lane_f == i2
    onehots = jnp.concatenate([jnp.where(oh1, 1.0, 0.0), jnp.where(oh2, 1.0, 0.0)], axis=1).astype(BF16)
    rr = lax.broadcasted_iota(I32, (bm, bm), 0)
    cc = lax.broadcasted_iota(I32, (bm, bm), 1)
    tri = jnp.where(rr >= cc, 1.0, 0.0).astype(BF16)
    csum = jnp.dot(tri, onehots, preferred_element_type=F32)
    c1 = csum[:, 0:LANES]
    c2 = csum[:, LANES:2 * LANES]
    tot1 = c1[bm - 1:bm, :]
    tot2 = c2[bm - 1:bm, :]
    before = cnt_ref[0:1, :]
    rank1 = jnp.sum(jnp.where(oh1, before + c1 - 1.0, 0.0), axis=1, keepdims=True)
    rank2 = jnp.sum(jnp.where(oh2, before + tot1 + c2 - 1.0, 0.0), axis=1, keepdims=True)
    cnt_ref[...] = jnp.broadcast_to(before + tot1 + tot2, cnt_ref.shape)

    r_ref[...] = jnp.where(lane == 0, i1, jnp.where(lane == 1, i2,
                           jnp.where(lane == 2, w1, jnp.where(lane == 3, w2, 0.0))))
    eye = rr == cc
    er_ref[0] = jnp.concatenate(
        [jnp.sum(jnp.where(eye, col, 0.0), axis=0, keepdims=True) for col in (i1, i2, rank1, rank2)], axis=0)


def _outproj_router(ya, yb, h, w_out, g_ffn, w_router, b_router, tp, n_real):
    n, d = h.shape
    half = ya.shape[1]
    bm = UNIT
    return pl.pallas_call(
        functools.partial(_outproj_router_body, bm=bm, blocks_per_seq=tp // bm, n_real=n_real),
        out_shape=[jax.ShapeDtypeStruct((n, d), F32), jax.ShapeDtypeStruct((n, LANES), F32),
                   jax.ShapeDtypeStruct((8, LANES), F32), jax.ShapeDtypeStruct((n, d // 2), U32),
                   jax.ShapeDtypeStruct((n // bm, 4, bm), F32)],
        grid=(n // bm,),
        in_specs=[pl.BlockSpec((bm, half), lambda i: (i, 0)),
                  pl.BlockSpec((bm, half), lambda i: (i, 0)),
                  pl.BlockSpec((bm, d), lambda i: (i, 0)),
                  pl.BlockSpec((2 * half, d), lambda i: (0, 0)),
                  pl.BlockSpec((1, d), lambda i: (0, 0)),
                  pl.BlockSpec((d, 2 * LANES), lambda i: (0, 0)),
                  pl.BlockSpec((1, LANES), lambda i: (0, 0))],
        out_specs=[pl.BlockSpec((bm, d), lambda i: (i, 0)),
                   pl.BlockSpec((bm, LANES), lambda i: (i, 0)),
                   pl.BlockSpec((8, LANES), lambda i: (0, 0)),
                   pl.BlockSpec((bm, d // 2), lambda i: (i, 0)),
                   pl.BlockSpec((1, 4, bm), lambda i: (i, 0, 0))],
        compiler_params=_cparams(("arbitrary",)),
        name="outproj_router",
    )(ya, yb, h, w_out, g_ffn, w_router, b_router)


def _pack_pairs(x):
    c = x.shape[1] // 2
    hi = lax.bitcast_convert_type(x[:, :c].astype(BF16).astype(F32), U32)
    lo = lax.bitcast_convert_type(x[:, c:].astype(BF16).astype(F32), U32)
    return hi | (lo >> 16)


def _unpack_pairs(p):
    hi = lax.bitcast_convert_type(p & jnp.uint32(0xFFFF0000), F32)
    lo = lax.bitcast_convert_type(p << 16, F32)
    return hi, lo


def _dispatch_body(meta_ref, dest_ref, xin_ref, xs_hbm, xp_ref, zero_ref, sem, zsem, *, tb):
    i = pl.program_id(0)
    slot = i % 2

    def drain(s):
        for _ in range(2):
            pltpu.make_async_copy(zero_ref, xs_hbm.at[pl.ds(0, tb)], sem.at[s]).wait()

    for g in range(tb // 8):
        xp_ref[slot, g] = xin_ref[g * 8:(g + 1) * 8, :]

    def issue(g, _):
        for u in range(8):
            for k in range(2):
                d = dest_ref[0, 0, k * tb + g * 8 + u]
                pltpu.make_async_copy(xp_ref.at[slot, g, pl.ds(u, 1)], xs_hbm.at[pl.ds(d, 1)],
                                      sem.at[slot]).start()
        return 0

    lax.fori_loop(0, tb // 8, issue, 0)

    @pl.when(i > 0)
    def _():
        drain(1 - slot)

    @pl.when(i == pl.num_programs(0) - 1)
    def _():
        drain(slot)

    @pl.when(i == 0)
    def _():
        zero_ref[...] = jnp.zeros_like(zero_ref)

        def per_expert(e, _):
            cnt = meta_ref[e]
            start = meta_ref[N_EXPERTS + e]
            padded = meta_ref[2 * N_EXPERTS + e]

            def pad_copy(r):
                return pltpu.make_async_copy(zero_ref.at[pl.ds(0, 1)], xs_hbm.at[pl.ds(start + r, 1)], zsem)

            def zissue(r, _):
                pad_copy(r).start()
                return 0

            def zwait(r, _):
                pad_copy(r).wait()
                return 0

            lax.fori_loop(cnt, padded, zissue, 0)
            lax.fori_loop(cnt, padded, zwait, 0)
            return 0

        lax.fori_loop(0, N_EXPERTS, per_expert, 0)

        def tail_copy(blk):
            return pltpu.make_async_copy(zero_ref, xs_hbm.at[pl.ds(blk * MOE_BM, MOE_BM)], zsem)

        def tissue(blk, _):
            tail_copy(blk).start()
            return 0

        def twait(blk, _):
            tail_copy(blk).wait()
            return 0

        n_used = meta_ref[3 * N_EXPERTS]
        lax.fori_loop(n_used, xs_hbm.shape[0] // MOE_BM, tissue, 0)
        lax.fori_loop(n_used, xs_hbm.shape[0] // MOE_BM, twait, 0)


def _dispatch(xp, dest3, meta, n_rows):
    n, dp = xp.shape
    tb = UNIT
    return pl.pallas_call(
        functools.partial(_dispatch_body, tb=tb),
        out_shape=jax.ShapeDtypeStruct((n_rows, dp), U32),
        grid_spec=pltpu.PrefetchScalarGridSpec(
            num_scalar_prefetch=1,
            grid=(n // tb,),
            in_specs=[pl.BlockSpec((1, 1, 2 * tb), lambda i, m: (i, 0, 0), memory_space=pltpu.SMEM),
                      pl.BlockSpec((tb, dp), lambda i, m: (i, 0))],
            out_specs=pl.BlockSpec(memory_space=pl.ANY),
            scratch_shapes=[pltpu.VMEM((2, tb // 8, 8, dp), U32), pltpu.VMEM((MOE_BM, dp), U32),
                            pltpu.SemaphoreType.DMA((2,)), pltpu.SemaphoreType.DMA(())]),
        compiler_params=_cparams(("arbitrary",)),
        name="moe_dispatch",
    )(meta, dest3, xp)


def _experts_body(be_ref, nxt_ref, nu_ref, xs_ref, wg_hbm, wu_hbm, wd_hbm, ys_ref,
                  wg_st, wu_st, wd_st, wg_bf, wu_bf, wd_bf, ord_ref, sem, *, layer):
    i = pl.program_id(0)
    n_used = nu_ref[0]

    def weight_copies(e, s):
        return (pltpu.make_async_copy(wg_hbm.at[layer, e], wg_st.at[s], sem.at[s, 0]),
                pltpu.make_async_copy(wu_hbm.at[layer, e], wu_st.at[s], sem.at[s, 1]),
                pltpu.make_async_copy(wd_hbm.at[layer, e], wd_st.at[s], sem.at[s, 2]))

    @pl.when(i == 0)
    def _():
        ord_ref[0] = 0
        for c in weight_copies(be_ref[0], 0):
            c.start()

    @pl.when(i < n_used)
    def _():
        e = be_ref[i]
        prev = be_ref[jnp.maximum(i - 1, 0)]

        @pl.when((i == 0) | (e != prev))
        def _():
            s = ord_ref[0] % 2
            for c in weight_copies(e, s):
                c.wait()
            nxt = nxt_ref[i]

            @pl.when(nxt != e)
            def _():
                for c in weight_copies(nxt, 1 - s):
                    c.start()

            wg_bf[...] = wg_st[s].astype(BF16)
            wu_bf[...] = wu_st[s].astype(BF16)
            wd_bf[...] = wd_st[s].astype(BF16)
            ord_ref[0] = ord_ref[0] + 1

        half = MOE_BM // 2
        kh = wg_bf.shape[0] // 2
        xns = []
        for r in range(2):
            hi, lo = _unpack_pairs(xs_ref[r * half:(r + 1) * half, :])
            xns.append((hi.astype(BF16), lo.astype(BF16)))

        def up(w_bf):
            return [jnp.dot(xa, w_bf[0:kh, :], preferred_element_type=F32)
                    + jnp.dot(xb, w_bf[kh:2 * kh, :], preferred_element_type=F32) for xa, xb in xns]

        hgs = up(wg_bf)
        hus = up(wu_bf)
        hids = [(_silu(hg) * hu).astype(BF16) for hg, hu in zip(hgs, hus)]
        for r in range(2):
            ys_ref[r * half:(r + 1) * half, :] = _pack_pairs(
                jnp.dot(hids[r], wd_bf[...], preferred_element_type=F32))

    @pl.when(i >= n_used)
    def _():
        ys_ref[...] = jnp.zeros_like(ys_ref)


def _experts(xs, w_gate, w_up, w_down, layer, block_expert, next_expert, n_used):
    r, dp = xs.shape
    d = 2 * dp
    de = w_gate.shape[3]
    nb = r // MOE_BM
    return pl.pallas_call(
        functools.partial(_experts_body, layer=layer),
        out_shape=jax.ShapeDtypeStruct((r, dp), U32),
        grid_spec=pltpu.PrefetchScalarGridSpec(
            num_scalar_prefetch=3,
            grid=(nb,),
            in_specs=[pl.BlockSpec((MOE_BM, dp), lambda i, be, nx, nu: (i, 0)),
                      pl.BlockSpec(memory_space=pl.ANY),
                      pl.BlockSpec(memory_space=pl.ANY),
                      pl.BlockSpec(memory_space=pl.ANY)],
            out_specs=pl.BlockSpec((MOE_BM, dp), lambda i, be, nx, nu: (i, 0)),
            scratch_shapes=[pltpu.VMEM((2, d, de), F32), pltpu.VMEM((2, d, de), F32),
                            pltpu.VMEM((2, de, d), F32),
                            pltpu.VMEM((d, de), BF16), pltpu.VMEM((d, de), BF16), pltpu.VMEM((de, d), BF16),
                            pltpu.SMEM((1,), I32), pltpu.SemaphoreType.DMA((2, 3))]),
        compiler_params=_cparams(("arbitrary",)),
        name="moe_experts",
    )(block_expert, next_expert, n_used, xs, w_gate, w_up, w_down)


def _combine_body(dest_ref, dnext_ref, h_ref, r_ref, gf_ref, ys_hbm, o_ref, buf, sem, *out_stage,
                  tb, blocks_per_seq, n_real, final_norm):
    i = pl.program_id(0)
    slot = i % 2

    def gather(d_ref, s):
        def issue(t, _):
            for k in range(2):
                d = d_ref[0, 0, k * tb + t]
                pltpu.make_async_copy(ys_hbm.at[pl.ds(d, 1)], buf.at[s, k, pl.ds(t, 1)], sem.at[s]).start()
            return 0

        lax.fori_loop(0, tb, issue, 0, unroll=8)

    @pl.when(i == 0)
    def _():
        gather(dest_ref, 0)

    @pl.when(i + 1 < pl.num_programs(0))
    def _():
        gather(dnext_ref, 1 - slot)

    for k in range(2):
        pltpu.make_async_copy(ys_hbm.at[pl.ds(0, tb)], buf.at[slot, k], sem.at[slot]).wait()

    r = r_ref[...]
    w1 = r[:, 2:3]
    w2 = r[:, 3:4]
    y1 = _unpack_pairs(buf[slot, 0])
    y2 = _unpack_pairs(buf[slot, 1])
    y = jnp.concatenate([w1 * y1[0] + w2 * y2[0], w1 * y1[1] + w2 * y2[1]], axis=1)
    pos0 = (i % blocks_per_seq) * tb
    hn = jnp.where(_row_valid(pos0, tb, n_real), h_ref[...] + y, 0.0)
    if not final_norm:
        o_ref[...] = hn
        return

    obuf, osem = out_stage
    seq = n_real - N_META
    ms = jnp.mean(hn * hn, axis=-1, keepdims=True)
    obuf[slot] = hn * lax.rsqrt(ms + EPS) * gf_ref[...]

    def frame_copy(step, s, start):
        b = step // blocks_per_seq
        j = step % blocks_per_seq
        row0 = b * seq + j * tb - OFF
        tail = OFF + seq - (blocks_per_seq - 1) * tb

        def go(src, dst):
            cp = pltpu.make_async_copy(src, dst, osem.at[s])
            cp.start() if start else cp.wait()

        @pl.when(j == 0)
        def _():
            go(obuf.at[s, pl.ds(OFF, tb - OFF)], o_ref.at[pl.ds(b * seq, tb - OFF)])

        @pl.when((j > 0) & (j < blocks_per_seq - 1))
        def _():
            go(obuf.at[s], o_ref.at[pl.ds(row0, tb)])

        @pl.when(j == blocks_per_seq - 1)
        def _():
            go(obuf.at[s, pl.ds(0, tail)], o_ref.at[pl.ds(row0, tail)])

    frame_copy(i, slot, True)

    @pl.when(i > 0)
    def _():
        frame_copy(i - 1, 1 - slot, False)

    @pl.when(i == pl.num_programs(0) - 1)
    def _():
        frame_copy(i, slot, False)


def _combine(h, route, dest3, ys, g_final, tp, n_real, final_norm):
    n, d = h.shape
    tb = UNIT
    last = n // tb - 1
    assert tp // tb >= 2
    if final_norm:
        out_shape = jax.ShapeDtypeStruct((n // tp * (n_real - N_META), d), F32)
        out_spec = pl.BlockSpec(memory_space=pl.ANY)
        out_stage = [pltpu.VMEM((2, tb, d), F32), pltpu.SemaphoreType.DMA((2,))]
    else:
        out_shape = jax.ShapeDtypeStruct((n, d), F32)
        out_spec = pl.BlockSpec((tb, d), lambda i: (i, 0))
        out_stage = []
    return pl.pallas_call(
        functools.partial(_combine_body, tb=tb, blocks_per_seq=tp // tb, n_real=n_real,
                          final_norm=final_norm),
        out_shape=out_shape,
        grid=(n // tb,),
        in_specs=[pl.BlockSpec((1, 1, 2 * tb), lambda i: (i, 0, 0), memory_space=pltpu.SMEM),
                  pl.BlockSpec((1, 1, 2 * tb), lambda i: (jnp.minimum(i + 1, last), 0, 0),
                               memory_space=pltpu.SMEM),
                  pl.BlockSpec((tb, d), lambda i: (i, 0)),
                  pl.BlockSpec((tb, LANES), lambda i: (i, 0)),
                  pl.BlockSpec((1, d), lambda i: (0, 0)),
                  pl.BlockSpec(memory_space=pl.ANY)],
        out_specs=out_spec,
        scratch_shapes=[pltpu.VMEM((2, 2, tb, d // 2), U32), pltpu.SemaphoreType.DMA((2,))] + out_stage,
        compiler_params=_cparams(("arbitrary",)),
        name="moe_combine",
    )(dest3, dest3, h, route, g_final, ys)


def _moe(h, xp, route, counts_f, er, w_gate, w_up, w_down, layer, g_final, tp, n_real, final_norm):
    n, d = h.shape
    n_pairs = 2 * n
    counts = counts_f[0, 0:N_EXPERTS].astype(I32)
    padded = (counts + MOE_BM - 1) // MOE_BM * MOE_BM
    p_ends = jnp.cumsum(padded)
    p_starts = p_ends - padded
    e_t = er[:, 0:2, :].astype(I32)
    start_of = jnp.sum(jnp.where(e_t[..., None] == jnp.arange(N_EXPERTS, dtype=I32), p_starts, 0), axis=-1)
    dest = start_of + er[:, 2:4, :].astype(I32)
    n_blocks = -(-(n_pairs + N_EXPERTS * (MOE_BM - 1)) // MOE_BM)
    block_row0 = jnp.arange(n_blocks, dtype=I32) * MOE_BM
    block_expert = jnp.minimum(jnp.sum((p_ends[None, :] <= block_row0[:, None]).astype(I32), axis=1),
                               N_EXPERTS - 1)
    n_used = (p_ends[-1:] // MOE_BM).astype(I32)
    meta = jnp.concatenate([counts, p_starts, padded, n_used]).astype(I32)
    dest3 = dest.reshape(n // UNIT, 1, 2 * UNIT)

    xs = _dispatch(xp, dest3, meta, n_blocks * MOE_BM)
    ids = jnp.arange(N_EXPERTS, dtype=I32)
    owner = jnp.where(counts > 0, ids, N_EXPERTS)
    first_from = lax.cummin(owner[::-1])[::-1]
    after = jnp.concatenate([first_from[1:], jnp.full((1,), N_EXPERTS, I32)])
    next_expert = jnp.take(jnp.where(after < N_EXPERTS, after, ids), block_expert)
    ys = _experts(xs, w_gate, w_up, w_down, layer, block_expert, next_expert, n_used)
    return _combine(h, route, dest3, ys, g_final, tp, n_real, final_norm)


def _short_conv(x, halo_ref, w_ref, first):
    rows = x.shape[0]

    @pl.when(first)
    def _():
        halo_ref[...] = jnp.zeros_like(halo_ref)

    x_ext = jnp.concatenate([halo_ref[...], x], axis=0)
    halo_ref[...] = x[rows - 8:rows]
    acc = jnp.zeros_like(x)
    for j, tap in _shifted_taps(x_ext, 8, rows, SHORT_K):
        acc = acc + w_ref[j:j + 1, :] * tap
    return acc


def _bdot(a, b):
    return jnp.dot(a.astype(BF16), b.astype(BF16), preferred_element_type=F32)


def _gdn_pre_body(z_ref, sc_ref, st_ref, cw_ref, al_r_ref, dt_r_ref, al_c_ref, dt_c_ref,
                  u_ref, w_ref, qd_ref, kd_ref, at_ref, eg_ref, halo_ref):
    rows = z_ref.shape[0]
    hw = N_HEADS * HEAD_DIM
    x = _silu(_short_conv(z_ref[...].astype(F32), halo_ref, cw_ref, pl.program_id(1) == 0))

    sc = sc_ref[...]
    g_cols = -jnp.exp(al_r_ref[...]) * _softplus(sc + dt_r_ref[...])
    beta_cols = _sigmoid(sc)
    st = st_ref[...]
    g_rows = -jnp.exp(al_c_ref[:, 0:1]) * _softplus(st + dt_c_ref[:, 0:1])

    ri = lax.broadcasted_iota(I32, (rows, rows), 0)
    ci = lax.broadcasted_iota(I32, (rows, rows), 1)
    same64 = (ri >> 6) == (ci >> 6)
    same32 = (ri >> 5) == (ci >> 5)
    same16 = (ri >> 4) == (ci >> 4)
    lower = ri >= ci
    strict = ri > ci
    lane = lax.broadcasted_iota(I32, (rows, LANES), 1)
    eg_slab = jnp.zeros((rows, LANES), F32)
    scale = HEAD_DIM ** -0.5

    heads = range(N_HEADS)
    in_chunk_lower = same64 & lower
    in_chunk_upper = same64 & (ri <= ci)
    a_mats, nmats, rhss = [], [], []
    for h in heads:
        sl = slice(h * HEAD_DIM, (h + 1) * HEAD_DIM)
        q = x[:, sl]
        k = x[:, hw + h * HEAD_DIM:hw + (h + 1) * HEAD_DIM]
        v = x[:, 2 * hw + h * HEAD_DIM:2 * hw + (h + 1) * HEAD_DIM]
        q = q * lax.rsqrt(jnp.sum(q * q, axis=-1, keepdims=True) + EPS)
        k = k * lax.rsqrt(jnp.sum(k * k, axis=-1, keepdims=True) + EPS)
        g_col = g_cols[:, h:h + 1]
        beta = beta_cols[:, N_HEADS + h:N_HEADS + h + 1]
        g_row = g_rows[h:h + 1, :]

        gc_col = jnp.sum(jnp.where(in_chunk_lower, g_row, 0.0), axis=1, keepdims=True)
        gc_row = jnp.sum(jnp.where(in_chunk_upper, g_col, 0.0), axis=0, keepdims=True)
        gtot_col = jnp.sum(jnp.where(same64, g_row, 0.0), axis=1, keepdims=True)
        decay = jnp.exp(jnp.where(in_chunk_lower, gc_col - gc_row, NEG))
        eg = jnp.exp(gc_col)
        ekd = jnp.exp(gtot_col - gc_col)

        kb = k * beta
        kbf = k.astype(BF16)
        kk = lax.dot_general(kb.astype(BF16), kbf, (((1,), (1,)), ((), ())), preferred_element_type=F32)
        qs = q * scale
        qk = lax.dot_general(qs.astype(BF16), kbf, (((1,), (1,)), ((), ())), preferred_element_type=F32)
        attn = qk * decay
        a_mats.append(jnp.where(strict, kk * decay, 0.0))
        rhss.append(jnp.concatenate([v * beta, kb * eg], axis=1))
        qd_ref[:, sl] = (qs * eg).astype(BF16)
        kd_ref[:, sl] = (k * ekd).astype(BF16)
        for c in range(rows // CHUNK):
            cs = slice(c * CHUNK, (c + 1) * CHUNK)
            at_ref[h, cs, :] = attn[cs, cs].astype(BF16)
        eg_slab = jnp.where(lane == h, eg, eg_slab)
    eg_ref[...] = eg_slab

    nmats = [jnp.where(same16, -a, 0.0) for a in a_mats]
    qqs = nmats
    for _ in range(3):
        qqs = [_bdot(qq, qq) for qq in qqs]
        prods = [_bdot(qq, nm) for qq, nm in zip(qqs, nmats)]
        nmats = [nm + qq + pr for nm, qq, pr in zip(nmats, qqs, prods)]
    for level_mask, inner_mask in ((same32, same16), (same64, same32)):
        sel = level_mask & jnp.logical_not(inner_mask)
        offs = [jnp.where(sel, a, 0.0) for a in a_mats]
        bmats = [off + _bdot(nm, off) for nm, off in zip(nmats, offs)]
        prods = [_bdot(bm, nm) for bm, nm in zip(bmats, nmats)]
        nmats = [nm - bm - pr for nm, bm, pr in zip(nmats, bmats, prods)]

    for h in heads:
        sl = slice(h * HEAD_DIM, (h + 1) * HEAD_DIM)
        uw = rhss[h] + _bdot(nmats[h], rhss[h])
        u_ref[:, sl] = uw[:, 0:HEAD_DIM]
        w_ref[:, sl] = uw[:, HEAD_DIM:2 * HEAD_DIM].astype(BF16)


def _gdn_pre(z, sc, st, conv_w, a_log, dt_bias, bsz, tp):
    n = z.shape[0]
    hw = N_HEADS * HEAD_DIM
    rows = UNIT
    nu = tp // rows
    pad_r = lambda v: jnp.pad(v.astype(F32), (0, LANES - N_HEADS)).reshape(1, LANES)
    pad_c = lambda v: jnp.broadcast_to(jnp.pad(v.astype(F32), (0, 16 - N_HEADS))[:, None], (16, LANES))
    outs = pl.pallas_call(
        _gdn_pre_body,
        out_shape=[jax.ShapeDtypeStruct((n, hw), F32),
                   jax.ShapeDtypeStruct((n, hw), BF16),
                   jax.ShapeDtypeStruct((n, hw), BF16),
                   jax.ShapeDtypeStruct((n, hw), BF16),
                   jax.ShapeDtypeStruct((N_HEADS, n, CHUNK), BF16),
                   jax.ShapeDtypeStruct((n, LANES), F32)],
        grid=(bsz, nu),
        in_specs=[pl.BlockSpec((rows, 3 * hw), lambda b, i: (b * nu + i, 0)),
                  pl.BlockSpec((rows, LANES), lambda b, i: (b * nu + i, 0)),
                  pl.BlockSpec((16, rows), lambda b, i: (0, b * nu + i)),
                  pl.BlockSpec((8, 3 * hw), lambda b, i: (0, 0)),
                  pl.BlockSpec((1, LANES), lambda b, i: (0, 0)),
                  pl.BlockSpec((1, LANES), lambda b, i: (0, 0)),
                  pl.BlockSpec((16, LANES), lambda b, i: (0, 0)),
                  pl.BlockSpec((16, LANES), lambda b, i: (0, 0))],
        out_specs=[pl.BlockSpec((rows, hw), lambda b, i: (b * nu + i, 0)),
                   pl.BlockSpec((rows, hw), lambda b, i: (b * nu + i, 0)),
                   pl.BlockSpec((rows, hw), lambda b, i: (b * nu + i, 0)),
                   pl.BlockSpec((rows, hw), lambda b, i: (b * nu + i, 0)),
                   pl.BlockSpec((N_HEADS, rows, CHUNK), lambda b, i: (0, b * nu + i, 0)),
                   pl.BlockSpec((rows, LANES), lambda b, i: (b * nu + i, 0))],
        scratch_shapes=[pltpu.VMEM((8, 3 * hw), F32)],
        compiler_params=_cparams(("arbitrary", "arbitrary")),
        name="gdn_pre",
    )(z, sc, st, jnp.pad(conv_w.astype(F32), ((0, 8 - SHORT_K), (0, 0))),
      pad_r(a_log), pad_r(dt_bias), pad_c(a_log), pad_c(dt_bias))
    return outs


def _gdn_scan_body(u_ref, w_ref, qd_ref, kd_ref, at_ref, eg_ref, z_ref, gn_ref, o_ref, s_ref, *, bsz):
    @pl.when(pl.program_id(0) == 0)
    def _():
        s_ref[...] = jnp.zeros_like(s_ref)

    units = [(b, h, slice(h * HEAD_DIM, (h + 1) * HEAD_DIM)) for b in range(bsz) for h in range(N_HEADS)]
    states = [s_ref[b * N_HEADS + h] for b, h, _ in units]
    rs = [jnp.dot(jnp.concatenate([w_ref[b, :, sl], qd_ref[b, :, sl]], axis=0), s.astype(BF16),
                  preferred_element_type=F32) for (b, h, sl), s in zip(units, states)]
    vbs = [(u_ref[b, :, sl] - r[0:CHUNK]).astype(BF16) for (b, h, sl), r in zip(units, rs)]
    intra = [jnp.dot(at_ref[h, b], vb, preferred_element_type=F32) for (b, h, sl), vb in zip(units, vbs)]
    outer = [lax.dot_general(kd_ref[b, :, sl], vb, (((0,), (0,)), ((), ())), preferred_element_type=F32)
             for (b, h, sl), vb in zip(units, vbs)]
    for (b, h, sl), s, r, a, kv in zip(units, states, rs, intra, outer):
        s_ref[b * N_HEADS + h] = s * eg_ref[b, CHUNK - 1:CHUNK, h:h + 1] + kv
        o = r[CHUNK:2 * CHUNK] + a
        on = o * lax.rsqrt(jnp.mean(o * o, axis=-1, keepdims=True) + EPS) * gn_ref[...]
        o_ref[b, :, sl] = (on * _silu(z_ref[b, :, sl].astype(F32))).astype(o_ref.dtype)


def _gdn_scan(u, w, qd, kd, attn, eg, z, out_norm_g, bsz, tp, z_col):
    hw = N_HEADS * HEAD_DIM
    nc = tp // CHUNK
    v3 = lambda a: a.reshape(bsz, tp, a.shape[-1])
    blk3 = pl.BlockSpec((bsz, CHUNK, hw), lambda c: (0, c, 0))
    return pl.pallas_call(
        functools.partial(_gdn_scan_body, bsz=bsz),
        out_shape=jax.ShapeDtypeStruct((bsz, tp, hw), BF16),
        grid=(nc,),
        in_specs=[blk3, blk3, blk3, blk3,
                  pl.BlockSpec((N_HEADS, bsz, CHUNK, CHUNK), lambda c: (0, 0, c, 0)),
                  pl.BlockSpec((bsz, CHUNK, LANES), lambda c: (0, c, 0)),
                  pl.BlockSpec((bsz, CHUNK, hw), lambda c: (0, c, z_col)),
                  pl.BlockSpec((1, HEAD_DIM), lambda c: (0, 0))],
        out_specs=blk3,
        scratch_shapes=[pltpu.VMEM((bsz * N_HEADS, HEAD_DIM, HEAD_DIM), F32)],
        compiler_params=_cparams(("arbitrary",)),
        name="gdn_scan",
    )(v3(u), v3(w), v3(qd), v3(kd), attn.reshape(N_HEADS, bsz, tp, CHUNK), v3(eg), v3(z),
      out_norm_g.astype(F32).reshape(1, HEAD_DIM)).reshape(bsz * tp, hw)


def _lru_body(x_ref, gate_ref, cw_ref, cb_ref, wr_ref, br_ref, wi_ref, bi_ref, lam_ref, o_ref,
              halo_ref, hc_ref, *, bt, n_real):
    first = pl.program_id(1) == 0

    @pl.when(first)
    def _():
        hc_ref[...] = jnp.zeros_like(hc_ref)

    x = _short_conv(x_ref[...].astype(F32), halo_ref, cw_ref, first) + cb_ref[...]
    nblk = wr_ref.shape[0]
    bd = wr_ref.shape[1]
    xb = x.astype(BF16)
    rg = jnp.concatenate([jnp.dot(xb[:, n * bd:(n + 1) * bd], wr_ref[n], preferred_element_type=F32)
                          for n in range(nblk)], axis=1)
    ig = jnp.concatenate([jnp.dot(xb[:, n * bd:(n + 1) * bd], wi_ref[n], preferred_element_type=F32)
                          for n in range(nblk)], axis=1)
    r = _sigmoid(rg + br_ref[...])
    ig = _sigmoid(ig + bi_ref[...])
    log_a = (-LRU_C * _softplus(-lam_ref[...])) * r
    a = jnp.exp(log_a)
    th = jnp.tanh(log_a)
    b = jnp.sqrt(-2.0 * th / (1.0 - th)) * (ig * x)
    b = jnp.where(_row_valid(pl.program_id(1) * bt, bt, n_real), b, 0.0)

    row = lax.broadcasted_iota(I32, (bt, 1), 0)
    d = 1
    while d < bt:
        keep = row >= d
        a_sh = jnp.where(keep, pltpu.roll(a, d, 0), 1.0)
        b_sh = jnp.where(keep, pltpu.roll(b, d, 0), 0.0)
        b = a * b_sh + b
        a = a * a_sh
        d *= 2
    hs = a * hc_ref[0:1, :] + b
    hc_ref[...] = jnp.broadcast_to(hs[bt - 1:bt, :], hc_ref.shape)
    o_ref[...] = (hs * jax.nn.gelu(gate_ref[...].astype(F32))).astype(o_ref.dtype)


def _lru(z, conv_w, conv_b, w_rg, b_rg, w_ig, b_ig, lam, bsz, tp, n_real, x_col, gate_col):
    n = z.shape[0]
    c = conv_w.shape[1]
    bt = UNIT
    nt = tp // bt
    row = lambda v: v.astype(F32).reshape(1, c)
    wspec = pl.BlockSpec(w_rg.shape, lambda b, i: (0, 0, 0))
    vspec = pl.BlockSpec((1, c), lambda b, i: (0, 0))
    return pl.pallas_call(
        functools.partial(_lru_body, bt=bt, n_real=n_real),
        out_shape=jax.ShapeDtypeStruct((n, c), BF16),
        grid=(bsz, nt),
        in_specs=[pl.BlockSpec((bt, c), lambda b, i: (b * nt + i, x_col)),
                  pl.BlockSpec((bt, c), lambda b, i: (b * nt + i, gate_col)),
                  pl.BlockSpec((8, c), lambda b, i: (0, 0)),
                  vspec, wspec, vspec, wspec, vspec, vspec],
        out_specs=pl.BlockSpec((bt, c), lambda b, i: (b * nt + i, 0)),
        scratch_shapes=[pltpu.VMEM((8, c), F32), pltpu.VMEM((8, c), F32)],
        compiler_params=_cparams(("arbitrary", "arbitrary")),
        name="rg_lru",
    )(z, z, jnp.pad(conv_w.astype(F32), ((0, 8 - SHORT_K), (0, 0))), row(conv_b),
      w_rg.astype(BF16), row(b_rg), w_ig.astype(BF16), row(b_ig), row(lam))


def _pick(n, candidates):
    for c in candidates:
        if n % c == 0:
            return c
    raise ValueError(f"no block size in {candidates} divides {n}")


def _pack_weights_body(a_ref, b_ref, o_ref, *, scaled_tiles, scale, head_tiles, gap):
    j = pl.program_id(0)
    tile = o_ref.shape[1]

    @pl.when(j < scaled_tiles)
    def _():
        o_ref[...] = (a_ref[...] * scale).astype(BF16)

    @pl.when((j >= scaled_tiles) & (j < head_tiles))
    def _():
        o_ref[...] = a_ref[...].astype(BF16)

    @pl.when(j >= head_tiles)
    def _():
        x = jnp.concatenate([a_ref[...], b_ref[...]], axis=1)
        o_ref[...] = x[:, gap:gap + tile].astype(BF16)


def _pack_inproj_weights(w_in, head, gap, scaled=0, scale=1.0):
    d, total = w_in.shape
    tile = 512
    n_out = total - gap
    return pl.pallas_call(
        functools.partial(_pack_weights_body, scaled_tiles=scaled // tile, scale=scale,
                          head_tiles=head // tile, gap=gap),
        out_shape=jax.ShapeDtypeStruct((d, n_out), BF16),
        grid=(n_out // tile,),
        in_specs=[pl.BlockSpec((d, tile), lambda j: (0, j)),
                  pl.BlockSpec((d, LANES), lambda j: (0, (tile // LANES) * (j + 1)))],
        out_specs=pl.BlockSpec((d, tile), lambda j: (0, j)),
        compiler_params=_cparams(("arbitrary",)),
        name="pack_inproj_weights",
    )(w_in, w_in)


def _small_weights(cols):
    k = cols.shape[1]
    return (jnp.pad(cols.T, ((0, 16 - k), (0, 0))).astype(BF16),
            jnp.pad(cols, ((0, 0), (0, LANES - k))).astype(BF16))


def _router_weights(w_group, b_group, w_expert, b_expert):
    w = jnp.concatenate([w_expert, w_group], axis=1).astype(F32)
    b = jnp.concatenate([b_expert, b_group]).astype(F32)
    k = w.shape[1]
    w = jnp.pad(w, ((0, 0), (0, LANES - k)))
    w_hi = w.astype(BF16)
    w_lo = (w - w_hi.astype(F32)).astype(BF16)
    return jnp.concatenate([w_hi, w_lo], axis=1), jnp.pad(b, (0, LANES - k)).reshape(1, LANES)


def kernel(x, meta_tokens, norm_mix_g, norm_ffn_g, norm_final_g, ab_w_in, ab_forget_b, ab_conv_w, ab_conv_b, ab_ln_g, ab_ln_b, ab_w_out, cd_w_in, cd_qkv_conv_w, cd_a_log, cd_dt_bias, cd_out_norm_g, cd_lru_conv_w, cd_lru_conv_b, cd_w_rg, cd_b_rg, cd_w_ig, cd_b_ig, cd_lru_lambda, cd_w_out, moe_w_group, moe_b_group, moe_w_expert, moe_b_expert, moe_w_gate, moe_w_up, moe_w_down):
    bsz, seq, d = x.shape
    depth = norm_mix_g.shape[0]
    n_real = N_META + seq
    tp = -(-(OFF + seq) // UNIT) * UNIT
    n = bsz * tp
    hw = N_HEADS * HEAD_DIM
    bm = _pick(n, (768, 512, 256))
    bn = 1024

    h = _embed(x.astype(F32).reshape(bsz * seq, d), meta_tokens.astype(F32), bsz, seq, tp)
    row = lambda v: v.astype(F32).reshape(1, -1)

    for layer in range(depth):
        i = layer // 2
        g_mix = row(norm_mix_g[layer])
        if layer % 2 == 0:
            w_in = ab_w_in[i]
            w_main = _pack_inproj_weights(w_in.astype(F32), 3 * hw, N_HEADS, scaled=hw,
                                          scale=LOG2E * HEAD_DIM ** -0.5)
            wst, wsc = _small_weights(w_in[:, 3 * hw:3 * hw + N_HEADS])
            z, _, f_slab = _norm_inproj(h, g_mix, w_main, wst, wsc, bm, bn)
            kx, vx = _fox_prep(z, f_slab, ab_forget_b[i], bsz, tp, n_real)
            ya = _fox_attn(z, kx, vx, bsz, tp)
            yb = _conformer(z, ab_conv_w[i], ab_conv_b[i], ab_ln_g[i], ab_ln_b[i], bsz, tp, 3, 4)
            w_out = ab_w_out[i].astype(BF16)
        else:
            w_in = cd_w_in[i]
            w_main = _pack_inproj_weights(w_in.astype(F32), 3 * hw, 2 * N_HEADS)
            wst, wsc = _small_weights(w_in[:, 3 * hw:3 * hw + 2 * N_HEADS])
            z, zt, zc = _norm_inproj(h, g_mix, w_main, wst, wsc, bm, bn)
            u, w, qd, kd, attn, eg = _gdn_pre(z, zc, zt, cd_qkv_conv_w[i], cd_a_log[i], cd_dt_bias[i],
                                              bsz, tp)
            ya = _gdn_scan(u, w, qd, kd, attn, eg, z, cd_out_norm_g[i], bsz, tp, 3)
            yb = _lru(z, cd_lru_conv_w[i], cd_lru_conv_b[i], cd_w_rg[i], cd_b_rg[i], cd_w_ig[i],
                      cd_b_ig[i], cd_lru_lambda[i], bsz, tp, n_real, 4, 5)
            w_out = cd_w_out[i].astype(BF16)
        g_ffn = row(norm_ffn_g[layer])
        w_r, b_r = _router_weights(moe_w_group[layer], moe_b_group[layer],
                                   moe_w_expert[layer], moe_b_expert[layer])
        h, route, counts, xp, er = _outproj_router(ya, yb, h, w_out, g_ffn, w_r, b_r, tp, n_real)
        h = _moe(h, xp, route, counts, er, moe_w_gate, moe_w_up, moe_w_down, layer,
                 row(norm_final_g), tp, n_real, final_norm=(layer == depth - 1))
    return h.reshape(bsz, seq, d).astype(x.dtype)
```

```python
import functools

import jax
import jax.numpy as jnp
from jax import lax
from jax.experimental import pallas as pl
from jax.experimental.pallas import tpu as pltpu

F32 = jnp.float32
BF16 = jnp.bfloat16
I32 = jnp.int32
U32 = jnp.uint32

EPS = 1e-6
N_META = 16
CHUNK = 64
FRONT = CHUNK - N_META
OFF = FRONT + N_META
LANES = 128
UNIT = 256
HEAD_DIM = 128
N_HEADS = 8
CONF_K = 31
SHORT_K = 4
LRU_C = 8.0
N_GROUPS = 4
EXPERTS_PER_GROUP = 8
N_EXPERTS = N_GROUPS * EXPERTS_PER_GROUP
MOE_BM = 256
NEG = -1e30
LOG2E = 1.4426950408889634
VMEM_LIMIT = 56 * 1024 * 1024


def _cparams(sem):
    return pltpu.CompilerParams(dimension_semantics=sem, vmem_limit_bytes=VMEM_LIMIT)


def _row_valid(pos0, rows, n_real):
    pos = pos0 + lax.broadcasted_iota(I32, (rows, 1), 0)
    return (pos >= FRONT) & (pos < FRONT + n_real)


def _sigmoid(x):
    return 1.0 / (1.0 + jnp.exp(-x))


def _softplus(x):
    return jnp.maximum(x, 0.0) + jnp.log1p(jnp.exp(-jnp.abs(x)))


def _silu(x):
    return x * _sigmoid(x)


def _embed_body(x_ref, meta_ref, h_hbm, zero_ref, sem, *, seq, tp, rows):
    b = pl.program_id(0)
    i = pl.program_id(1)
    base = b * tp
    tail = tp - OFF - seq
    frames = pltpu.make_async_copy(x_ref, h_hbm.at[pl.ds(base + OFF + i * rows, rows)], sem.at[0])
    frames.start()

    @pl.when(i == 0)
    def _():
        zero_ref[...] = jnp.zeros_like(zero_ref)
        copies = [
            pltpu.make_async_copy(meta_ref, h_hbm.at[pl.ds(base + FRONT, N_META)], sem.at[1]),
            pltpu.make_async_copy(zero_ref.at[pl.ds(0, FRONT)], h_hbm.at[pl.ds(base, FRONT)], sem.at[2]),
        ]
        if tail:
            copies.append(pltpu.make_async_copy(zero_ref.at[pl.ds(0, tail)],
                                                h_hbm.at[pl.ds(base + OFF + seq, tail)], sem.at[3]))
        for c in copies:
            c.start()
        for c in copies:
            c.wait()

    frames.wait()


def _embed(x2d, meta_tokens, bsz, seq, tp):
    d = x2d.shape[1]
    rows = _pick(seq, (1024, 512, 256, 128, 64, 32, 16, 8))
    per_seq = seq // rows
    return pl.pallas_call(
        functools.partial(_embed_body, seq=seq, tp=tp, rows=rows),
        out_shape=jax.ShapeDtypeStruct((bsz * tp, d), F32),
        grid=(bsz, per_seq),
        in_specs=[pl.BlockSpec((rows, d), lambda b, i: (b * per_seq + i, 0)),
                  pl.BlockSpec((N_META, d), lambda b, i: (0, 0))],
        out_specs=pl.BlockSpec(memory_space=pl.ANY),
        scratch_shapes=[pltpu.VMEM((max(FRONT, tp - OFF - seq), d), F32), pltpu.SemaphoreType.DMA((4,))],
        compiler_params=_cparams(("arbitrary", "arbitrary")),
        name="embed",
    )(x2d, meta_tokens)


def _norm_inproj_body(h_ref, g_ref, w_ref, wst_ref, wsc_ref, o_ref, ot_ref, oc_ref, xn_ref):
    @pl.when(pl.program_id(1) == 0)
    def _():
        x = h_ref[...]
        ms = jnp.mean(x * x, axis=-1, keepdims=True)
        xn = (x * lax.rsqrt(ms + EPS) * g_ref[...]).astype(BF16)
        xn_ref[...] = xn
        ot_ref[...] = lax.dot_general(wst_ref[...], xn, (((1,), (1,)), ((), ())),
                                      preferred_element_type=F32)
        oc_ref[...] = jnp.dot(xn, wsc_ref[...], preferred_element_type=F32)

    o_ref[...] = jnp.dot(xn_ref[...], w_ref[...], preferred_element_type=F32).astype(o_ref.dtype)


def _norm_inproj(h, g, w_main, w_small_t, w_small_c, bm, bn):
    n, d = h.shape
    nw = w_main.shape[1]
    return pl.pallas_call(
        _norm_inproj_body,
        out_shape=[jax.ShapeDtypeStruct((n, nw), BF16),
                   jax.ShapeDtypeStruct((16, n), F32),
                   jax.ShapeDtypeStruct((n, LANES), F32)],
        grid=(n // bm, nw // bn),
        in_specs=[pl.BlockSpec((bm, d), lambda i, j: (i, 0)),
                  pl.BlockSpec((1, d), lambda i, j: (0, 0)),
                  pl.BlockSpec((d, bn), lambda i, j: (0, j)),
                  pl.BlockSpec((16, d), lambda i, j: (0, 0)),
                  pl.BlockSpec((d, LANES), lambda i, j: (0, 0))],
        out_specs=[pl.BlockSpec((bm, bn), lambda i, j: (i, j)),
                   pl.BlockSpec((16, bm), lambda i, j: (0, i)),
                   pl.BlockSpec((bm, LANES), lambda i, j: (i, 0))],
        scratch_shapes=[pltpu.VMEM((bm, d), BF16)],
        compiler_params=_cparams(("arbitrary", "arbitrary")),
        name="norm_inproj",
    )(h, g, w_main, w_small_t, w_small_c)


def _split3(x):
    p1 = x.astype(BF16)
    r1 = x - p1.astype(F32)
    p2 = r1.astype(BF16)
    p3 = (r1 - p2.astype(F32)).astype(BF16)
    return p1, p2, p3


def _fox_prep_body(k_ref, v_ref, f_ref, fb_ref, kx_ref, vx_ref, carry_ref, *, rows, n_real):
    i = pl.program_id(1)

    @pl.when(i == 0)
    def _():
        carry_ref[...] = jnp.zeros_like(carry_ref)

    f = f_ref[...] + fb_ref[...]
    lf = jnp.minimum(f, 0.0) - jnp.log1p(jnp.exp(-jnp.abs(f)))
    pos = i * rows + lax.broadcasted_iota(I32, (rows, 1), 0)
    lf = jnp.where((pos >= FRONT) & (pos < FRONT + n_real), lf, 0.0)
    rr = lax.broadcasted_iota(I32, (rows, rows), 0)
    cc = lax.broadcasted_iota(I32, (rows, rows), 1)
    tri = jnp.where(rr >= cc, 1.0, 0.0).astype(BF16)
    c = carry_ref[0:1, :]
    for piece in _split3(lf):
        c = c + jnp.dot(tri, piece, preferred_element_type=F32)
    carry_ref[...] = jnp.broadcast_to(c[rows - 1:rows, :], carry_ref.shape)
    cs = jnp.where(pos < FRONT, NEG, -LOG2E * c)
    p1, p2, p3 = (p.astype(F32) for p in _split3(cs))
    lane = lax.broadcasted_iota(I32, (rows, LANES), 1)
    ones_col = jnp.where(lane == 0, 1.0, 0.0).astype(BF16)
    for h in range(N_HEADS):
        ext = jnp.where(lane == 0, p1[:, h:h + 1],
                        jnp.where(lane == 1, p2[:, h:h + 1], jnp.where(lane == 2, p3[:, h:h + 1], 0.0)))
        kx_ref[:, (2 * h) * HEAD_DIM:(2 * h + 1) * HEAD_DIM] = k_ref[:, h * HEAD_DIM:(h + 1) * HEAD_DIM]
        kx_ref[:, (2 * h + 1) * HEAD_DIM:(2 * h + 2) * HEAD_DIM] = ext.astype(BF16)
        vx_ref[:, (2 * h) * HEAD_DIM:(2 * h + 1) * HEAD_DIM] = v_ref[:, h * HEAD_DIM:(h + 1) * HEAD_DIM]
        vx_ref[:, (2 * h + 1) * HEAD_DIM:(2 * h + 2) * HEAD_DIM] = ones_col


def _fox_prep(z, f_slab, forget_b, bsz, tp, n_real):
    n = z.shape[0]
    hw = N_HEADS * HEAD_DIM
    rows = UNIT
    nt = tp // rows
    fb = jnp.pad(forget_b.astype(F32), (0, LANES - N_HEADS)).reshape(1, LANES)
    return pl.pallas_call(
        functools.partial(_fox_prep_body, rows=rows, n_real=n_real),
        out_shape=[jax.ShapeDtypeStruct((n, 2 * hw), BF16), jax.ShapeDtypeStruct((n, 2 * hw), BF16)],
        grid=(bsz, nt),
        in_specs=[pl.BlockSpec((rows, hw), lambda b, i: (b * nt + i, 1)),
                  pl.BlockSpec((rows, hw), lambda b, i: (b * nt + i, 2)),
                  pl.BlockSpec((rows, LANES), lambda b, i: (b * nt + i, 0)),
                  pl.BlockSpec((1, LANES), lambda b, i: (0, 0))],
        out_specs=[pl.BlockSpec((rows, 2 * hw), lambda b, i: (b * nt + i, 0)),
                   pl.BlockSpec((rows, 2 * hw), lambda b, i: (b * nt + i, 0))],
        scratch_shapes=[pltpu.VMEM((8, LANES), F32)],
        compiler_params=_cparams(("arbitrary", "arbitrary")),
        name="fox_prep",
    )(z, z, f_slab, fb)


def _fox_attn_body(q_ref, k_ref, v_ref, o_ref, *, blk, n_split, group):
    qi = pl.program_id(2)
    half = blk // n_split
    lane = lax.broadcasted_iota(I32, (blk, HEAD_DIM), 1)
    q = jnp.concatenate([q_ref[...], jnp.where(lane < 3, 1.0, 0.0).astype(BF16)], axis=1)
    qs = tuple(q[r * half:(r + 1) * half] for r in range(n_split))

    def step(js, carry, diag_last):
        starts = [pl.multiple_of(j * blk, blk) for j in js]
        ss = []
        for r, qh in enumerate(qs):
            srow = []
            for b, st in enumerate(starts):
                on_diag = diag_last and b == len(js) - 1
                ncol = (r + 1) * half if on_diag else blk
                s = lax.dot_general(qh, k_ref[pl.ds(st, ncol), :], (((1,), (1,)), ((), ())),
                                    preferred_element_type=F32)
                if on_diag:
                    row = r * half + lax.broadcasted_iota(I32, (half, ncol), 0)
                    col = lax.broadcasted_iota(I32, (half, ncol), 1)
                    s = jnp.where(col <= row, s, NEG)
                srow.append((s, st, ncol))
            ss.append(srow)
        out = []
        for r in range(n_split):
            m = carry[2 * r]
            m_new = m
            for s, _, _ in ss[r]:
                m_new = jnp.maximum(m_new, jnp.max(s, axis=1, keepdims=True))
            acc = jnp.exp2(m - m_new) * carry[2 * r + 1]
            for s, st, ncol in ss[r]:
                acc = acc + jnp.dot(jnp.exp2(s - m_new).astype(BF16), v_ref[pl.ds(st, ncol), :],
                                    preferred_element_type=F32)
            out += [m_new, acc]
        return tuple(out)

    init = (jnp.full((half, 1), NEG, F32), jnp.zeros((half, 2 * HEAD_DIM), F32)) * n_split
    carry = lax.fori_loop(0, qi // group,
                          lambda j, c: step(tuple(group * j + g for g in range(group)), c, False), init)
    tails = [functools.partial(lambda r, c: step(tuple(qi - r + g for g in range(r + 1)), c, True), r)
             for r in range(group)]
    res = lax.switch(qi % group, tails, carry)
    for r in range(n_split):
        acc = res[2 * r + 1]
        o_ref[r * half:(r + 1) * half, :] = (acc[:, 0:HEAD_DIM] / acc[:, HEAD_DIM:HEAD_DIM + 1]
                                             ).astype(o_ref.dtype)


def _fox_attn(z, kx, vx, bsz, tp):
    n = z.shape[0]
    blk = _pick(tp, (768, 512, 256))
    nq = tp // blk
    return pl.pallas_call(
        functools.partial(_fox_attn_body, blk=blk, n_split=2, group=4),
        out_shape=jax.ShapeDtypeStruct((n, N_HEADS * HEAD_DIM), BF16),
        grid=(bsz, N_HEADS, nq),
        in_specs=[pl.BlockSpec((blk, HEAD_DIM), lambda b, h, i: (b * nq + i, h)),
                  pl.BlockSpec((tp, 2 * HEAD_DIM), lambda b, h, i: (b, h)),
                  pl.BlockSpec((tp, 2 * HEAD_DIM), lambda b, h, i: (b, h))],
        out_specs=pl.BlockSpec((blk, HEAD_DIM), lambda b, h, i: (b * nq + i, h)),
        compiler_params=_cparams(("arbitrary", "arbitrary", "arbitrary")),
        name="fox_attn",
    )(z, kx, vx)


def _shifted_taps(x_ext, halo, rows, n_taps):
    for r in range(min(8, n_taps)):
        rolled = x_ext if r == 0 else pltpu.roll(x_ext, r, 0)
        for q in range(halo // 8):
            s = 8 * q + r
            if s > n_taps - 1:
                continue
            yield n_taps - 1 - s, rolled[halo - 8 * q:halo - 8 * q + rows]


def _conformer_body(a_ref, b_ref, w_ref, cb_ref, lg_ref, lb_ref, o_ref, halo_ref, *, bt):
    halo = halo_ref.shape[0]

    @pl.when(pl.program_id(1) == 0)
    def _():
        halo_ref[...] = jnp.zeros_like(halo_ref)

    u = a_ref[...].astype(F32) * _sigmoid(b_ref[...].astype(F32))
    x_ext = jnp.concatenate([halo_ref[...], u], axis=0)
    halo_ref[...] = u[bt - halo:bt]
    acc = jnp.zeros_like(u)
    for j, tap in _shifted_taps(x_ext, halo, bt, CONF_K):
        acc = acc + w_ref[j:j + 1, :] * tap
    y = acc + cb_ref[...]
    mu = jnp.mean(y, axis=-1, keepdims=True)
    yc = y - mu
    var = jnp.mean(yc * yc, axis=-1, keepdims=True)
    yn = yc * lax.rsqrt(var + EPS) * lg_ref[...] + lb_ref[...]
    o_ref[...] = _silu(yn).astype(o_ref.dtype)


def _conformer(z, conv_w, conv_b, ln_g, ln_b, bsz, tp, col_a, col_b):
    n = z.shape[0]
    c = conv_w.shape[1]
    bt = UNIT
    nt = tp // bt
    w = jnp.pad(conv_w.astype(F32), ((0, 32 - CONF_K), (0, 0)))
    row = lambda v: v.astype(F32).reshape(1, c)
    return pl.pallas_call(
        functools.partial(_conformer_body, bt=bt),
        out_shape=jax.ShapeDtypeStruct((n, c), BF16),
        grid=(bsz, nt),
        in_specs=[pl.BlockSpec((bt, c), lambda b, i: (b * nt + i, col_a)),
                  pl.BlockSpec((bt, c), lambda b, i: (b * nt + i, col_b)),
                  pl.BlockSpec((32, c), lambda b, i: (0, 0)),
                  pl.BlockSpec((1, c), lambda b, i: (0, 0)),
                  pl.BlockSpec((1, c), lambda b, i: (0, 0)),
                  pl.BlockSpec((1, c), lambda b, i: (0, 0))],
        out_specs=pl.BlockSpec((bt, c), lambda b, i: (b * nt + i, 0)),
        scratch_shapes=[pltpu.VMEM((32, c), F32)],
        compiler_params=_cparams(("arbitrary", "arbitrary")),
        name="conformer_conv",
    )(z, z, w, row(conv_b), row(ln_g), row(ln_b))


def _outproj_router_body(ya_ref, yb_ref, h_ref, w_ref, g_ref, wr_ref, br_ref, ho_ref, r_ref, cnt_ref,
                         xp_ref, er_ref, *, bm, blocks_per_seq, n_real):
    kh = ya_ref.shape[1]
    pos0 = (pl.program_id(0) % blocks_per_seq) * bm
    rh = bm // 2
    parts = [slice(r * rh, (r + 1) * rh) for r in range(2)]
    ys = [jnp.dot(ya_ref[p, :], w_ref[0:kh, :], preferred_element_type=F32)
          + jnp.dot(yb_ref[p, :], w_ref[kh:2 * kh, :], preferred_element_type=F32) for p in parts]
    lgs = []
    for r, (p, y) in enumerate(zip(parts, ys)):
        hn = jnp.where(_row_valid(pos0 + r * rh, rh, n_real), h_ref[p, :] + y, 0.0)
        ho_ref[p, :] = hn
        ms = jnp.mean(hn * hn, axis=-1, keepdims=True)
        xn = hn * lax.rsqrt(ms + EPS) * g_ref[...]
        xp_ref[p, :] = _pack_pairs(xn)
        xh = xn.astype(BF16)
        xl = (xn - xh.astype(F32)).astype(BF16)
        t = jnp.dot(xh, wr_ref[...], preferred_element_type=F32)
        lgs.append(t[:, 0:LANES] + t[:, LANES:2 * LANES]
                   + jnp.dot(xl, wr_ref[:, 0:LANES], preferred_element_type=F32) + br_ref[...])
    lg = jnp.concatenate(lgs, axis=0)
    lane = lax.broadcasted_iota(I32, lg.shape, 1)
    lane_f = lane.astype(F32)
    big = float(LANES)

    is_grp = (lane >= N_EXPERTS) & (lane < N_EXPERTS + N_GROUPS)
    gl = jnp.where(is_grp, lg, -jnp.inf)
    gmax = jnp.max(gl, axis=1, keepdims=True)
    gidx = jnp.min(jnp.where(gl == gmax, lane_f, big), axis=1, keepdims=True) - N_EXPERTS
    g_prob = 1.0 / jnp.sum(jnp.where(is_grp, jnp.exp(lg - gmax), 0.0), axis=1, keepdims=True)

    lo = gidx * EXPERTS_PER_GROUP
    in_grp = (lane_f >= lo) & (lane_f < lo + EXPERTS_PER_GROUP)
    el = jnp.where(in_grp, lg, -jnp.inf)
    m1 = jnp.max(el, axis=1, keepdims=True)
    i1 = jnp.min(jnp.where(el == m1, lane_f, big), axis=1, keepdims=True)
    el2 = jnp.where(lane_f == i1, -jnp.inf, el)
    m2 = jnp.max(el2, axis=1, keepdims=True)
    i2 = jnp.min(jnp.where(el2 == m2, lane_f, big), axis=1, keepdims=True)
    e2 = jnp.exp(m2 - m1)
    w1 = g_prob / (1.0 + e2)
    w2 = g_prob * e2 / (1.0 + e2)

    @pl.when(pl.program_id(0) == 0)
    def _():
        cnt_ref[...] = jnp.zeros_like(cnt_ref)

    oh1 = lane_f == i1
    oh2 = lane_f == i2
    onehots = jnp.concatenate([jnp.where(oh1, 1.0, 0.0), jnp.where(oh2, 1.0, 0.0)], axis=1).astype(BF16)
    rr = lax.broadcasted_iota(I32, (bm, bm), 0)
    cc = lax.broadcasted_iota(I32, (bm, bm), 1)
    tri = jnp.where(rr >= cc, 1.0, 0.0).astype(BF16)
    csum = jnp.dot(tri, onehots, preferred_element_type=F32)
    c1 = csum[:, 0:LANES]
    c2 = csum[:, LANES:2 * LANES]
    tot1 = c1[bm - 1:bm, :]
    tot2 = c2[bm - 1:bm, :]
    before = cnt_ref[0:1, :]
    rank1 = jnp.sum(jnp.where(oh1, before + c1 - 1.0, 0.0), axis=1, keepdims=True)
    rank2 = jnp.sum(jnp.where(oh2, before + tot1 + c2 - 1.0, 0.0), axis=1, keepdims=True)
    cnt_ref[...] = jnp.broadcast_to(before + tot1 + tot2, cnt_ref.shape)

    r_ref[...] = jnp.where(lane == 0, i1, jnp.where(lane == 1, i2,
                           jnp.where(lane == 2, w1, jnp.where(lane == 3, w2, 0.0))))
    eye = rr == cc
    er_ref[0] = jnp.concatenate(
        [jnp.sum(jnp.where(eye, col, 0.0), axis=0, keepdims=True) for col in (i1, i2, rank1, rank2)], axis=0)


def _outproj_router(ya, yb, h, w_out, g_ffn, w_router, b_router, tp, n_real):
    n, d = h.shape
    half = ya.shape[1]
    bm = UNIT
    return pl.pallas_call(
        functools.partial(_outproj_router_body, bm=bm, blocks_per_seq=tp // bm, n_real=n_real),
        out_shape=[jax.ShapeDtypeStruct((n, d), F32), jax.ShapeDtypeStruct((n, LANES), F32),
                   jax.ShapeDtypeStruct((8, LANES), F32), jax.ShapeDtypeStruct((n, d // 2), U32),
                   jax.ShapeDtypeStruct((n // bm, 4, bm), F32)],
        grid=(n // bm,),
        in_specs=[pl.BlockSpec((bm, half), lambda i: (i, 0)),
                  pl.BlockSpec((bm, half), lambda i: (i, 0)),
                  pl.BlockSpec((bm, d), lambda i: (i, 0)),
                  pl.BlockSpec((2 * half, d), lambda i: (0, 0)),
                  pl.BlockSpec((1, d), lambda i: (0, 0)),
                  pl.BlockSpec((d, 2 * LANES), lambda i: (0, 0)),
                  pl.BlockSpec((1, LANES), lambda i: (0, 0))],
        out_specs=[pl.BlockSpec((bm, d), lambda i: (i, 0)),
                   pl.BlockSpec((bm, LANES), lambda i: (i, 0)),
                   pl.BlockSpec((8, LANES), lambda i: (0, 0)),
                   pl.BlockSpec((bm, d // 2), lambda i: (i, 0)),
                   pl.BlockSpec((1, 4, bm), lambda i: (i, 0, 0))],
        compiler_params=_cparams(("arbitrary",)),
        name="outproj_router",
    )(ya, yb, h, w_out, g_ffn, w_router, b_router)


def _pack_pairs(x):
    c = x.shape[1] // 2
    hi = lax.bitcast_convert_type(x[:, :c].astype(BF16).astype(F32), U32)
    lo = lax.bitcast_convert_type(x[:, c:].astype(BF16).astype(F32), U32)
    return hi | (lo >> 16)


def _unpack_pairs(p):
    hi = lax.bitcast_convert_type(p & jnp.uint32(0xFFFF0000), F32)
    lo = lax.bitcast_convert_type(p << 16, F32)
    return hi, lo


def _dispatch_body(meta_ref, dest_ref, xin_ref, xs_hbm, xp_ref, zero_ref, sem, zsem, *, tb):
    i = pl.program_id(0)
    slot = i % 2

    def drain(s):
        for _ in range(2):
            pltpu.make_async_copy(zero_ref, xs_hbm.at[pl.ds(0, tb)], sem.at[s]).wait()

    for g in range(tb // 8):
        xp_ref[slot, g] = xin_ref[g * 8:(g + 1) * 8, :]

    def issue(g, _):
        for u in range(8):
            for k in range(2):
                d = dest_ref[0, 0, k * tb + g * 8 + u]
                pltpu.make_async_copy(xp_ref.at[slot, g, pl.ds(u, 1)], xs_hbm.at[pl.ds(d, 1)],
                                      sem.at[slot]).start()
        return 0

    lax.fori_loop(0, tb // 8, issue, 0)

    @pl.when(i > 0)
    def _():
        drain(1 - slot)

    @pl.when(i == pl.num_programs(0) - 1)
    def _():
        drain(slot)

    @pl.when(i == 0)
    def _():
        zero_ref[...] = jnp.zeros_like(zero_ref)

        def per_expert(e, _):
            cnt = meta_ref[e]
            start = meta_ref[N_EXPERTS + e]
            padded = meta_ref[2 * N_EXPERTS + e]

            def pad_copy(r):
                return pltpu.make_async_copy(zero_ref.at[pl.ds(0, 1)], xs_hbm.at[pl.ds(start + r, 1)], zsem)

            def zissue(r, _):
                pad_copy(r).start()
                return 0

            def zwait(r, _):
                pad_copy(r).wait()
                return 0

            lax.fori_loop(cnt, padded, zissue, 0)
            lax.fori_loop(cnt, padded, zwait, 0)
            return 0

        lax.fori_loop(0, N_EXPERTS, per_expert, 0)

        def tail_copy(blk):
            return pltpu.make_async_copy(zero_ref, xs_hbm.at[pl.ds(blk * MOE_BM, MOE_BM)], zsem)

        def tissue(blk, _):
            tail_copy(blk).start()
            return 0

        def twait(blk, _):
            tail_copy(blk).wait()
            return 0

        n_used = meta_ref[3 * N_EXPERTS]
        lax.fori_loop(n_used, xs_hbm.shape[0] // MOE_BM, tissue, 0)
        lax.fori_loop(n_used, xs_hbm.shape[0] // MOE_BM, twait, 0)


def _dispatch(xp, dest3, meta, n_rows):
    n, dp = xp.shape
    tb = UNIT
    return pl.pallas_call(
        functools.partial(_dispatch_body, tb=tb),
        out_shape=jax.ShapeDtypeStruct((n_rows, dp), U32),
        grid_spec=pltpu.PrefetchScalarGridSpec(
            num_scalar_prefetch=1,
            grid=(n // tb,),
            in_specs=[pl.BlockSpec((1, 1, 2 * tb), lambda i, m: (i, 0, 0), memory_space=pltpu.SMEM),
                      pl.BlockSpec((tb, dp), lambda i, m: (i, 0))],
            out_specs=pl.BlockSpec(memory_space=pl.ANY),
            scratch_shapes=[pltpu.VMEM((2, tb // 8, 8, dp), U32), pltpu.VMEM((MOE_BM, dp), U32),
                            pltpu.SemaphoreType.DMA((2,)), pltpu.SemaphoreType.DMA(())]),
        compiler_params=_cparams(("arbitrary",)),
        name="moe_dispatch",
    )(meta, dest3, xp)


def _experts_body(be_ref, nxt_ref, nu_ref, xs_ref, wg_hbm, wu_hbm, wd_hbm, ys_ref,
                  wg_st, wu_st, wd_st, wg_bf, wu_bf, wd_bf, ord_ref, sem, *, layer):
    i = pl.program_id(0)
    n_used = nu_ref[0]

    def weight_copies(e, s):
        return (pltpu.make_async_copy(wg_hbm.at[layer, e], wg_st.at[s], sem.at[s, 0]),
                pltpu.make_async_copy(wu_hbm.at[layer, e], wu_st.at[s], sem.at[s, 1]),
                pltpu.make_async_copy(wd_hbm.at[layer, e], wd_st.at[s], sem.at[s, 2]))

    @pl.when(i == 0)
    def _():
        ord_ref[0] = 0
        for c in weight_copies(be_ref[0], 0):
            c.start()

    @pl.when(i < n_used)
    def _():
        e = be_ref[i]
        prev = be_ref[jnp.maximum(i - 1, 0)]

        @pl.when((i == 0) | (e != prev))
        def _():
            s = ord_ref[0] % 2
            for c in weight_copies(e, s):
                c.wait()
            nxt = nxt_ref[i]

            @pl.when(nxt != e)
            def _():
                for c in weight_copies(nxt, 1 - s):
                    c.start()

            wg_bf[...] = wg_st[s].astype(BF16)
            wu_bf[...] = wu_st[s].astype(BF16)
            wd_bf[...] = wd_st[s].astype(BF16)
            ord_ref[0] = ord_ref[0] + 1

        half = MOE_BM // 2
        kh = wg_bf.shape[0] // 2
        xns = []
        for r in range(2):
            hi, lo = _unpack_pairs(xs_ref[r * half:(r + 1) * half, :])
            xns.append((hi.astype(BF16), lo.astype(BF16)))

        def up(w_bf):
            return [jnp.dot(xa, w_bf[0:kh, :], preferred_element_type=F32)
                    + jnp.dot(xb, w_bf[kh:2 * kh, :], preferred_element_type=F32) for xa, xb in xns]

        hgs = up(wg_bf)
        hus = up(wu_bf)
        hids = [(_silu(hg) * hu).astype(BF16) for hg, hu in zip(hgs, hus)]
        for r in range(2):
            ys_ref[r * half:(r + 1) * half, :] = _pack_pairs(
                jnp.dot(hids[r], wd_bf[...], preferred_element_type=F32))

    @pl.when(i >= n_used)
    def _():
        ys_ref[...] = jnp.zeros_like(ys_ref)


def _experts(xs, w_gate, w_up, w_down, layer, block_expert, next_expert, n_used):
    r, dp = xs.shape
    d = 2 * dp
    de = w_gate.shape[3]
    nb = r // MOE_BM
    return pl.pallas_call(
        functools.partial(_experts_body, layer=layer),
        out_shape=jax.ShapeDtypeStruct((r, dp), U32),
        grid_spec=pltpu.PrefetchScalarGridSpec(
            num_scalar_prefetch=3,
            grid=(nb,),
            in_specs=[pl.BlockSpec((MOE_BM, dp), lambda i, be, nx, nu: (i, 0)),
                      pl.BlockSpec(memory_space=pl.ANY),
                      pl.BlockSpec(memory_space=pl.ANY),
                      pl.BlockSpec(memory_space=pl.ANY)],
            out_specs=pl.BlockSpec((MOE_BM, dp), lambda i, be, nx, nu: (i, 0)),
            scratch_shapes=[pltpu.VMEM((2, d, de), F32), pltpu.VMEM((2, d, de), F32),
                            pltpu.VMEM((2, de, d), F32),
                            pltpu.VMEM((d, de), BF16), pltpu.VMEM((d, de), BF16), pltpu.VMEM((de, d), BF16),
                            pltpu.SMEM((1,), I32), pltpu.SemaphoreType.DMA((2, 3))]),
        compiler_params=_cparams(("arbitrary",)),
        name="moe_experts",
    )(block_expert, next_expert, n_used, xs, w_gate, w_up, w_down)


def _combine_body(dest_ref, dnext_ref, h_ref, r_ref, gf_ref, ys_hbm, o_ref, buf, sem, *out_stage,
                  tb, blocks_per_seq, n_real, final_norm):
    i = pl.program_id(0)
    slot = i % 2

    def gather(d_ref, s):
        def issue(t, _):
            for k in range(2):
                d = d_ref[0, 0, k * tb + t]
                pltpu.make_async_copy(ys_hbm.at[pl.ds(d, 1)], buf.at[s, k, pl.ds(t, 1)], sem.at[s]).start()
            return 0

        lax.fori_loop(0, tb, issue, 0, unroll=8)

    @pl.when(i == 0)
    def _():
        gather(dest_ref, 0)

    @pl.when(i + 1 < pl.num_programs(0))
    def _():
        gather(dnext_ref, 1 - slot)

    for k in range(2):
        pltpu.make_async_copy(ys_hbm.at[pl.ds(0, tb)], buf.at[slot, k], sem.at[slot]).wait()

    r = r_ref[...]
    w1 = r[:, 2:3]
    w2 = r[:, 3:4]
    y1 = _unpack_pairs(buf[slot, 0])
    y2 = _unpack_pairs(buf[slot, 1])
    y = jnp.concatenate([w1 * y1[0] + w2 * y2[0], w1 * y1[1] + w2 * y2[1]], axis=1)
    pos0 = (i % blocks_per_seq) * tb
    hn = jnp.where(_row_valid(pos0, tb, n_real), h_ref[...] + y, 0.0)
    if not final_norm:
        o_ref[...] = hn
        return

    obuf, osem = out_stage
    seq = n_real - N_META
    ms = jnp.mean(hn * hn, axis=-1, keepdims=True)
    obuf[slot] = hn * lax.rsqrt(ms + EPS) * gf_ref[...]

    def frame_copy(step, s, start):
        b = step // blocks_per_seq
        j = step % blocks_per_seq
        row0 = b * seq + j * tb - OFF
        tail = OFF + seq - (blocks_per_seq - 1) * tb

        def go(src, dst):
            cp = pltpu.make_async_copy(src, dst, osem.at[s])
            cp.start() if start else cp.wait()

        @pl.when(j == 0)
        def _():
            go(obuf.at[s, pl.ds(OFF, tb - OFF)], o_ref.at[pl.ds(b * seq, tb - OFF)])

        @pl.when((j > 0) & (j < blocks_per_seq - 1))
        def _():
            go(obuf.at[s], o_ref.at[pl.ds(row0, tb)])

        @pl.when(j == blocks_per_seq - 1)
        def _():
            go(obuf.at[s, pl.ds(0, tail)], o_ref.at[pl.ds(row0, tail)])

    frame_copy(i, slot, True)

    @pl.when(i > 0)
    def _():
        frame_copy(i - 1, 1 - slot, False)

    @pl.when(i == pl.num_programs(0) - 1)
    def _():
        frame_copy(i, slot, False)


def _combine(h, route, dest3, ys, g_final, tp, n_real, final_norm):
    n, d = h.shape
    tb = UNIT
    last = n // tb - 1
    assert tp // tb >= 2
    if final_norm:
        out_shape = jax.ShapeDtypeStruct((n // tp * (n_real - N_META), d), F32)
        out_spec = pl.BlockSpec(memory_space=pl.ANY)
        out_stage = [pltpu.VMEM((2, tb, d), F32), pltpu.SemaphoreType.DMA((2,))]
    else:
        out_shape = jax.ShapeDtypeStruct((n, d), F32)
        out_spec = pl.BlockSpec((tb, d), lambda i: (i, 0))
        out_stage = []
    return pl.pallas_call(
        functools.partial(_combine_body, tb=tb, blocks_per_seq=tp // tb, n_real=n_real,
                          final_norm=final_norm),
        out_shape=out_shape,
        grid=(n // tb,),
        in_specs=[pl.BlockSpec((1, 1, 2 * tb), lambda i: (i, 0, 0), memory_space=pltpu.SMEM),
                  pl.BlockSpec((1, 1, 2 * tb), lambda i: (jnp.minimum(i + 1, last), 0, 0),
                               memory_space=pltpu.SMEM),
                  pl.BlockSpec((tb, d), lambda i: (i, 0)),
                  pl.BlockSpec((tb, LANES), lambda i: (i, 0)),
                  pl.BlockSpec((1, d), lambda i: (0, 0)),
                  pl.BlockSpec(memory_space=pl.ANY)],
        out_specs=out_spec,
        scratch_shapes=[pltpu.VMEM((2, 2, tb, d // 2), U32), pltpu.SemaphoreType.DMA((2,))] + out_stage,
        compiler_params=_cparams(("arbitrary",)),
        name="moe_combine",
    )(dest3, dest3, h, route, g_final, ys)


def _moe(h, xp, route, counts_f, er, w_gate, w_up, w_down, layer, g_final, tp, n_real, final_norm):
    n, d = h.shape
    n_pairs = 2 * n
    counts = counts_f[0, 0:N_EXPERTS].astype(I32)
    padded = (counts + MOE_BM - 1) // MOE_BM * MOE_BM
    p_ends = jnp.cumsum(padded)
    p_starts = p_ends - padded
    e_t = er[:, 0:2, :].astype(I32)
    start_of = jnp.sum(jnp.where(e_t[..., None] == jnp.arange(N_EXPERTS, dtype=I32), p_starts, 0), axis=-1)
    dest = start_of + er[:, 2:4, :].astype(I32)
    n_blocks = -(-(n_pairs + N_EXPERTS * (MOE_BM - 1)) // MOE_BM)
    block_row0 = jnp.arange(n_blocks, dtype=I32) * MOE_BM
    block_expert = jnp.minimum(jnp.sum((p_ends[None, :] <= block_row0[:, None]).astype(I32), axis=1),
                               N_EXPERTS - 1)
    n_used = (p_ends[-1:] // MOE_BM).astype(I32)
    meta = jnp.concatenate([counts, p_starts, padded, n_used]).astype(I32)
    dest3 = dest.reshape(n // UNIT, 1, 2 * UNIT)

    xs = _dispatch(xp, dest3, meta, n_blocks * MOE_BM)
    ids = jnp.arange(N_EXPERTS, dtype=I32)
    owner = jnp.where(counts > 0, ids, N_EXPERTS)
    first_from = lax.cummin(owner[::-1])[::-1]
    after = jnp.concatenate([first_from[1:], jnp.full((1,), N_EXPERTS, I32)])
    next_expert = jnp.take(jnp.where(after < N_EXPERTS, after, ids), block_expert)
    ys = _experts(xs, w_gate, w_up, w_down, layer, block_expert, next_expert, n_used)
    return _combine(h, route, dest3, ys, g_final, tp, n_real, final_norm)


def _short_conv(x, halo_ref, w_ref, first):
    rows = x.shape[0]

    @pl.when(first)
    def _():
        halo_ref[...] = jnp.zeros_like(halo_ref)

    x_ext = jnp.concatenate([halo_ref[...], x], axis=0)
    halo_ref[...] = x[rows - 8:rows]
    acc = jnp.zeros_like(x)
    for j, tap in _shifted_taps(x_ext, 8, rows, SHORT_K):
        acc = acc + w_ref[j:j + 1, :] * tap
    return acc


def _bdot(a, b):
    return jnp.dot(a.astype(BF16), b.astype(BF16), preferred_element_type=F32)


def _gdn_pre_body(z_ref, sc_ref, st_ref, cw_ref, al_r_ref, dt_r_ref, al_c_ref, dt_c_ref,
                  u_ref, w_ref, qd_ref, kd_ref, at_ref, eg_ref, halo_ref):
    rows = z_ref.shape[0]
    hw = N_HEADS * HEAD_DIM
    x = _silu(_short_conv(z_ref[...].astype(F32), halo_ref, cw_ref, pl.program_id(1) == 0))

    sc = sc_ref[...]
    g_cols = -jnp.exp(al_r_ref[...]) * _softplus(sc + dt_r_ref[...])
    beta_cols = _sigmoid(sc)
    st = st_ref[...]
    g_rows = -jnp.exp(al_c_ref[:, 0:1]) * _softplus(st + dt_c_ref[:, 0:1])

    ri = lax.broadcasted_iota(I32, (rows, rows), 0)
    ci = lax.broadcasted_iota(I32, (rows, rows), 1)
    same64 = (ri >> 6) == (ci >> 6)
    same32 = (ri >> 5) == (ci >> 5)
    same16 = (ri >> 4) == (ci >> 4)
    lower = ri >= ci
    strict = ri > ci
    lane = lax.broadcasted_iota(I32, (rows, LANES), 1)
    eg_slab = jnp.zeros((rows, LANES), F32)
    scale = HEAD_DIM ** -0.5

    heads = range(N_HEADS)
    in_chunk_lower = same64 & lower
    in_chunk_upper = same64 & (ri <= ci)
    a_mats, nmats, rhss = [], [], []
    for h in heads:
        sl = slice(h * HEAD_DIM, (h + 1) * HEAD_DIM)
        q = x[:, sl]
        k = x[:, hw + h * HEAD_DIM:hw + (h + 1) * HEAD_DIM]
        v = x[:, 2 * hw + h * HEAD_DIM:2 * hw + (h + 1) * HEAD_DIM]
        q = q * lax.rsqrt(jnp.sum(q * q, axis=-1, keepdims=True) + EPS)
        k = k * lax.rsqrt(jnp.sum(k * k, axis=-1, keepdims=True) + EPS)
        g_col = g_cols[:, h:h + 1]
        beta = beta_cols[:, N_HEADS + h:N_HEADS + h + 1]
        g_row = g_rows[h:h + 1, :]

        gc_col = jnp.sum(jnp.where(in_chunk_lower, g_row, 0.0), axis=1, keepdims=True)
        gc_row = jnp.sum(jnp.where(in_chunk_upper, g_col, 0.0), axis=0, keepdims=True)
        gtot_col = jnp.sum(jnp.where(same64, g_row, 0.0), axis=1, keepdims=True)
        decay = jnp.exp(jnp.where(in_chunk_lower, gc_col - gc_row, NEG))
        eg = jnp.exp(gc_col)
        ekd = jnp.exp(gtot_col - gc_col)

        kb = k * beta
        kbf = k.astype(BF16)
        kk = lax.dot_general(kb.astype(BF16), kbf, (((1,), (1,)), ((), ())), preferred_element_type=F32)
        qs = q * scale
        qk = lax.dot_general(qs.astype(BF16), kbf, (((1,), (1,)), ((), ())), preferred_element_type=F32)
        attn = qk * decay
        a_mats.append(jnp.where(strict, kk * decay, 0.0))
        rhss.append(jnp.concatenate([v * beta, kb * eg], axis=1))
        qd_ref[:, sl] = (qs * eg).astype(BF16)
        kd_ref[:, sl] = (k * ekd).astype(BF16)
        for c in range(rows // CHUNK):
            cs = slice(c * CHUNK, (c + 1) * CHUNK)
            at_ref[h, cs, :] = attn[cs, cs].astype(BF16)
        eg_slab = jnp.where(lane == h, eg, eg_slab)
    eg_ref[...] = eg_slab

    nmats = [jnp.where(same16, -a, 0.0) for a in a_mats]
    qqs = nmats
    for _ in range(3):
        qqs = [_bdot(qq, qq) for qq in qqs]
        prods = [_bdot(qq, nm) for qq, nm in zip(qqs, nmats)]
        nmats = [nm + qq + pr for nm, qq, pr in zip(nmats, qqs, prods)]
    for level_mask, inner_mask in ((same32, same16), (same64, same32)):
        sel = level_mask & jnp.logical_not(inner_mask)
        offs = [jnp.where(sel, a, 0.0) for a in a_mats]
        bmats = [off + _bdot(nm, off) for nm, off in zip(nmats, offs)]
        prods = [_bdot(bm, nm) for bm, nm in zip(bmats, nmats)]
        nmats = [nm - bm - pr for nm, bm, pr in zip(nmats, bmats, prods)]

    for h in heads:
        sl = slice(h * HEAD_DIM, (h + 1) * HEAD_DIM)
        uw = rhss[h] + _bdot(nmats[h], rhss[h])
        u_ref[:, sl] = uw[:, 0:HEAD_DIM]
        w_ref[:, sl] = uw[:, HEAD_DIM:2 * HEAD_DIM].astype(BF16)


def _gdn_pre(z, sc, st, conv_w, a_log, dt_bias, bsz, tp):
    n = z.shape[0]
    hw = N_HEADS * HEAD_DIM
    rows = UNIT
    nu = tp // rows
    pad_r = lambda v: jnp.pad(v.astype(F32), (0, LANES - N_HEADS)).reshape(1, LANES)
    pad_c = lambda v: jnp.broadcast_to(jnp.pad(v.astype(F32), (0, 16 - N_HEADS))[:, None], (16, LANES))
    outs = pl.pallas_call(
        _gdn_pre_body,
        out_shape=[jax.ShapeDtypeStruct((n, hw), F32),
                   jax.ShapeDtypeStruct((n, hw), BF16),
                   jax.ShapeDtypeStruct((n, hw), BF16),
                   jax.ShapeDtypeStruct((n, hw), BF16),
                   jax.ShapeDtypeStruct((N_HEADS, n, CHUNK), BF16),
                   jax.ShapeDtypeStruct((n, LANES), F32)],
        grid=(bsz, nu),
        in_specs=[pl.BlockSpec((rows, 3 * hw), lambda b, i: (b * nu + i, 0)),
                  pl.BlockSpec((rows, LANES), lambda b, i: (b * nu + i, 0)),
                  pl.BlockSpec((16, rows), lambda b, i: (0, b * nu + i)),
                  pl.BlockSpec((8, 3 * hw), lambda b, i: (0, 0)),
                  pl.BlockSpec((1, LANES), lambda b, i: (0, 0)),
                  pl.BlockSpec((1, LANES), lambda b, i: (0, 0)),
                  pl.BlockSpec((16, LANES), lambda b, i: (0, 0)),
                  pl.BlockSpec((16, LANES), lambda b, i: (0, 0))],
        out_specs=[pl.BlockSpec((rows, hw), lambda b, i: (b * nu + i, 0)),
                   pl.BlockSpec((rows, hw), lambda b, i: (b * nu + i, 0)),
                   pl.BlockSpec((rows, hw), lambda b, i: (b * nu + i, 0)),
                   pl.BlockSpec((rows, hw), lambda b, i: (b * nu + i, 0)),
                   pl.BlockSpec((N_HEADS, rows, CHUNK), lambda b, i: (0, b * nu + i, 0)),
                   pl.BlockSpec((rows, LANES), lambda b, i: (b * nu + i, 0))],
        scratch_shapes=[pltpu.VMEM((8, 3 * hw), F32)],
        compiler_params=_cparams(("arbitrary", "arbitrary")),
        name="gdn_pre",
    )(z, sc, st, jnp.pad(conv_w.astype(F32), ((0, 8 - SHORT_K), (0, 0))),
      pad_r(a_log), pad_r(dt_bias), pad_c(a_log), pad_c(dt_bias))
    return outs


def _gdn_scan_body(u_ref, w_ref, qd_ref, kd_ref, at_ref, eg_ref, z_ref, gn_ref, o_ref, s_ref, *, bsz):
    @pl.when(pl.program_id(0) == 0)
    def _():
        s_ref[...] = jnp.zeros_like(s_ref)

    units = [(b, h, slice(h * HEAD_DIM, (h + 1) * HEAD_DIM)) for b in range(bsz) for h in range(N_HEADS)]
    states = [s_ref[b * N_HEADS + h] for b, h, _ in units]
    rs = [jnp.dot(jnp.concatenate([w_ref[b, :, sl], qd_ref[b, :, sl]], axis=0), s.astype(BF16),
                  preferred_element_type=F32) for (b, h, sl), s in zip(units, states)]
    vbs = [(u_ref[b, :, sl] - r[0:CHUNK]).astype(BF16) for (b, h, sl), r in zip(units, rs)]
    intra = [jnp.dot(at_ref[h, b], vb, preferred_element_type=F32) for (b, h, sl), vb in zip(units, vbs)]
    outer = [lax.dot_general(kd_ref[b, :, sl], vb, (((0,), (0,)), ((), ())), preferred_element_type=F32)
             for (b, h, sl), vb in zip(units, vbs)]
    for (b, h, sl), s, r, a, kv in zip(units, states, rs, intra, outer):
        s_ref[b * N_HEADS + h] = s * eg_ref[b, CHUNK - 1:CHUNK, h:h + 1] + kv
        o = r[CHUNK:2 * CHUNK] + a
        on = o * lax.rsqrt(jnp.mean(o * o, axis=-1, keepdims=True) + EPS) * gn_ref[...]
        o_ref[b, :, sl] = (on * _silu(z_ref[b, :, sl].astype(F32))).astype(o_ref.dtype)


def _gdn_scan(u, w, qd, kd, attn, eg, z, out_norm_g, bsz, tp, z_col):
    hw = N_HEADS * HEAD_DIM
    nc = tp // CHUNK
    v3 = lambda a: a.reshape(bsz, tp, a.shape[-1])
    blk3 = pl.BlockSpec((bsz, CHUNK, hw), lambda c: (0, c, 0))
    return pl.pallas_call(
        functools.partial(_gdn_scan_body, bsz=bsz),
        out_shape=jax.ShapeDtypeStruct((bsz, tp, hw), BF16),
        grid=(nc,),
        in_specs=[blk3, blk3, blk3, blk3,
                  pl.BlockSpec((N_HEADS, bsz, CHUNK, CHUNK), lambda c: (0, 0, c, 0)),
                  pl.BlockSpec((bsz, CHUNK, LANES), lambda c: (0, c, 0)),
                  pl.BlockSpec((bsz, CHUNK, hw), lambda c: (0, c, z_col)),
                  pl.BlockSpec((1, HEAD_DIM), lambda c: (0, 0))],
        out_specs=blk3,
        scratch_shapes=[pltpu.VMEM((bsz * N_HEADS, HEAD_DIM, HEAD_DIM), F32)],
        compiler_params=_cparams(("arbitrary",)),
        name="gdn_scan",
    )(v3(u), v3(w), v3(qd), v3(kd), attn.reshape(N_HEADS, bsz, tp, CHUNK), v3(eg), v3(z),
      out_norm_g.astype(F32).reshape(1, HEAD_DIM)).reshape(bsz * tp, hw)


def _lru_body(x_ref, gate_ref, cw_ref, cb_ref, wr_ref, br_ref, wi_ref, bi_ref, lam_ref, o_ref,
              halo_ref, hc_ref, *, bt, n_real):
    first = pl.program_id(1) == 0

    @pl.when(first)
    def _():
        hc_ref[...] = jnp.zeros_like(hc_ref)

    x = _short_conv(x_ref[...].astype(F32), halo_ref, cw_ref, first) + cb_ref[...]
    nblk = wr_ref.shape[0]
    bd = wr_ref.shape[1]
    xb = x.astype(BF16)
    rg = jnp.concatenate([jnp.dot(xb[:, n * bd:(n + 1) * bd], wr_ref[n], preferred_element_type=F32)
                          for n in range(nblk)], axis=1)
    ig = jnp.concatenate([jnp.dot(xb[:, n * bd:(n + 1) * bd], wi_ref[n], preferred_element_type=F32)
                          for n in range(nblk)], axis=1)
    r = _sigmoid(rg + br_ref[...])
    ig = _sigmoid(ig + bi_ref[...])
    log_a = (-LRU_C * _softplus(-lam_ref[...])) * r
    a = jnp.exp(log_a)
    th = jnp.tanh(log_a)
    b = jnp.sqrt(-2.0 * th / (1.0 - th)) * (ig * x)
    b = jnp.where(_row_valid(pl.program_id(1) * bt, bt, n_real), b, 0.0)

    row = lax.broadcasted_iota(I32, (bt, 1), 0)
    d = 1
    while d < bt:
        keep = row >= d
        a_sh = jnp.where(keep, pltpu.roll(a, d, 0), 1.0)
        b_sh = jnp.where(keep, pltpu.roll(b, d, 0), 0.0)
        b = a * b_sh + b
        a = a * a_sh
        d *= 2
    hs = a * hc_ref[0:1, :] + b
    hc_ref[...] = jnp.broadcast_to(hs[bt - 1:bt, :], hc_ref.shape)
    o_ref[...] = (hs * jax.nn.gelu(gate_ref[...].astype(F32))).astype(o_ref.dtype)


def _lru(z, conv_w, conv_b, w_rg, b_rg, w_ig, b_ig, lam, bsz, tp, n_real, x_col, gate_col):
    n = z.shape[0]
    c = conv_w.shape[1]
    bt = UNIT
    nt = tp // bt
    row = lambda v: v.astype(F32).reshape(1, c)
    wspec = pl.BlockSpec(w_rg.shape, lambda b, i: (0, 0, 0))
    vspec = pl.BlockSpec((1, c), lambda b, i: (0, 0))
    return pl.pallas_call(
        functools.partial(_lru_body, bt=bt, n_real=n_real),
        out_shape=jax.ShapeDtypeStruct((n, c), BF16),
        grid=(bsz, nt),
        in_specs=[pl.BlockSpec((bt, c), lambda b, i: (b * nt + i, x_col)),
                  pl.BlockSpec((bt, c), lambda b, i: (b * nt + i, gate_col)),
                  pl.BlockSpec((8, c), lambda b, i: (0, 0)),
                  vspec, wspec, vspec, wspec, vspec, vspec],
        out_specs=pl.BlockSpec((bt, c), lambda b, i: (b * nt + i, 0)),
        scratch_shapes=[pltpu.VMEM((8, c), F32), pltpu.VMEM((8, c), F32)],
        compiler_params=_cparams(("arbitrary", "arbitrary")),
        name="rg_lru",
    )(z, z, jnp.pad(conv_w.astype(F32), ((0, 8 - SHORT_K), (0, 0))), row(conv_b),
      w_rg.astype(BF16), row(b_rg), w_ig.astype(BF16), row(b_ig), row(lam))


def _pick(n, candidates):
    for c in candidates:
        if n % c == 0:
            return c
    raise ValueError(f"no block size in {candidates} divides {n}")


def _pack_weights_body(a_ref, b_ref, o_ref, *, scaled_tiles, scale, head_tiles, gap):
    j = pl.program_id(0)
    tile = o_ref.shape[1]

    @pl.when(j < scaled_tiles)
    def _():
        o_ref[...] = (a_ref[...] * scale).astype(BF16)

    @pl.when((j >= scaled_tiles) & (j < head_tiles))
    def _():
        o_ref[...] = a_ref[...].astype(BF16)

    @pl.when(j >= head_tiles)
    def _():
        x = jnp.concatenate([a_ref[...], b_ref[...]], axis=1)
        o_ref[...] = x[:, gap:gap + tile].astype(BF16)


def _pack_inproj_weights(w_in, head, gap, scaled=0, scale=1.0):
    d, total = w_in.shape
    tile = 512
    n_out = total - gap
    return pl.pallas_call(
        functools.partial(_pack_weights_body, scaled_tiles=scaled // tile, scale=scale,
                          head_tiles=head // tile, gap=gap),
        out_shape=jax.ShapeDtypeStruct((d, n_out), BF16),
        grid=(n_out // tile,),
        in_specs=[pl.BlockSpec((d, tile), lambda j: (0, j)),
                  pl.BlockSpec((d, LANES), lambda j: (0, (tile // LANES) * (j + 1)))],
        out_specs=pl.BlockSpec((d, tile), lambda j: (0, j)),
        compiler_params=_cparams(("arbitrary",)),
        name="pack_inproj_weights",
    )(w_in, w_in)


def _small_weights(cols):
    k = cols.shape[1]
    return (jnp.pad(cols.T, ((0, 16 - k), (0, 0))).astype(BF16),
            jnp.pad(cols, ((0, 0), (0, LANES - k))).astype(BF16))


def _router_weights(w_group, b_group, w_expert, b_expert):
    w = jnp.concatenate([w_expert, w_group], axis=1).astype(F32)
    b = jnp.concatenate([b_expert, b_group]).astype(F32)
    k = w.shape[1]
    w = jnp.pad(w, ((0, 0), (0, LANES - k)))
    w_hi = w.astype(BF16)
    w_lo = (w - w_hi.astype(F32)).astype(BF16)
    return jnp.concatenate([w_hi, w_lo], axis=1), jnp.pad(b, (0, LANES - k)).reshape(1, LANES)


def kernel(x, meta_tokens, norm_mix_g, norm_ffn_g, norm_final_g, ab_w_in, ab_forget_b, ab_conv_w, ab_conv_b, ab_ln_g, ab_ln_b, ab_w_out, cd_w_in, cd_qkv_conv_w, cd_a_log, cd_dt_bias, cd_out_norm_g, cd_lru_conv_w, cd_lru_conv_b, cd_w_rg, cd_b_rg, cd_w_ig, cd_b_ig, cd_lru_lambda, cd_w_out, moe_w_group, moe_b_group, moe_w_expert, moe_b_expert, moe_w_gate, moe_w_up, moe_w_down):
    bsz, seq, d = x.shape
    depth = norm_mix_g.shape[0]
    n_real = N_META + seq
    tp = -(-(OFF + seq) // UNIT) * UNIT
    n = bsz * tp
    hw = N_HEADS * HEAD_DIM
    bm = _pick(n, (768, 512, 256))
    bn = 1024

    h = _embed(x.astype(F32).reshape(bsz * seq, d), meta_tokens.astype(F32), bsz, seq, tp)
    row = lambda v: v.astype(F32).reshape(1, -1)

    for layer in range(depth):
        i = layer // 2
        g_mix = row(norm_mix_g[layer])
        if layer % 2 == 0:
            w_in = ab_w_in[i]
            w_main = _pack_inproj_weights(w_in.astype(F32), 3 * hw, N_HEADS, scaled=hw,
                                          scale=LOG2E * HEAD_DIM ** -0.5)
            wst, wsc = _small_weights(w_in[:, 3 * hw:3 * hw + N_HEADS])
            z, _, f_slab = _norm_inproj(h, g_mix, w_main, wst, wsc, bm, bn)
            kx, vx = _fox_prep(z, f_slab, ab_forget_b[i], bsz, tp, n_real)
            ya = _fox_attn(z, kx, vx, bsz, tp)
            yb = _conformer(z, ab_conv_w[i], ab_conv_b[i], ab_ln_g[i], ab_ln_b[i], bsz, tp, 3, 4)
            w_out = ab_w_out[i].astype(BF16)
        else:
            w_in = cd_w_in[i]
            w_main = _pack_inproj_weights(w_in.astype(F32), 3 * hw, 2 * N_HEADS)
            wst, wsc = _small_weights(w_in[:, 3 * hw:3 * hw + 2 * N_HEADS])
            z, zt, zc = _norm_inproj(h, g_mix, w_main, wst, wsc, bm, bn)
            u, w, qd, kd, attn, eg = _gdn_pre(z, zc, zt, cd_qkv_conv_w[i], cd_a_log[i], cd_dt_bias[i],
                                              bsz, tp)
            ya = _gdn_scan(u, w, qd, kd, attn, eg, z, cd_out_norm_g[i], bsz, tp, 3)
            yb = _lru(z, cd_lru_conv_w[i], cd_lru_conv_b[i], cd_w_rg[i], cd_b_rg[i], cd_w_ig[i],
                      cd_b_ig[i], cd_lru_lambda[i], bsz, tp, n_real, 4, 5)
            w_out = cd_w_out[i].astype(BF16)
        g_ffn = row(norm_ffn_g[layer])
        w_r, b_r = _router_weights(moe_w_group[layer], moe_b_group[layer],
                                   moe_w_expert[layer], moe_b_expert[layer])
        h, route, counts, xp, er = _outproj_router(ya, yb, h, w_out, g_ffn, w_r, b_r, tp, n_real)
        h = _moe(h, xp, route, counts, er, moe_w_gate, moe_w_up, moe_w_down, layer,
                 row(norm_final_g), tp, n_real, final_norm=(layer == depth - 1))
    return h.reshape(bsz, seq, d).astype(x.dtype)
```

```python
import functools

import jax
import jax.numpy as jnp
from jax import lax
from jax.experimental import pallas as pl
from jax.experimental.pallas import tpu as pltpu

F32 = jnp.float32
BF16 = jnp.bfloat16
I32 = jnp.int32
U32 = jnp.uint32

EPS = 1e-6
N_META = 16
CHUNK = 64
FRONT = CHUNK - N_META
OFF = FRONT + N_META
LANES = 128
UNIT = 256
HEAD_DIM = 128
N_HEADS = 8
CONF_K = 31
SHORT_K = 4
LRU_C = 8.0
N_GROUPS = 4
EXPERTS_PER_GROUP = 8
N_EXPERTS = N_GROUPS * EXPERTS_PER_GROUP
MOE_BM = 256
NEG = -1e30
LOG2E = 1.4426950408889634
N_BIAS_PIECES = 3
VMEM_LIMIT = 56 * 1024 * 1024


def _cparams(sem):
    return pltpu.CompilerParams(dimension_semantics=sem, vmem_limit_bytes=VMEM_LIMIT)


def _row_valid(pos0, rows, n_real):
    pos = pos0 + lax.broadcasted_iota(I32, (rows, 1), 0)
    return (pos >= FRONT) & (pos < FRONT + n_real)


def _sigmoid(x):
    return 1.0 / (1.0 + jnp.exp(-x))


def _softplus(x):
    return jnp.maximum(x, 0.0) + jnp.log1p(jnp.exp(-jnp.abs(x)))


def _silu(x):
    return x * _sigmoid(x)


def _embed_body(x_ref, meta_ref, h_hbm, zero_ref, sem, *, seq, tp, rows):
    b = pl.program_id(0)
    i = pl.program_id(1)
    base = b * tp
    tail = tp - OFF - seq
    frames = pltpu.make_async_copy(x_ref, h_hbm.at[pl.ds(base + OFF + i * rows, rows)], sem.at[0])
    frames.start()

    @pl.when(i == 0)
    def _():
        zero_ref[...] = jnp.zeros_like(zero_ref)
        copies = [
            pltpu.make_async_copy(meta_ref, h_hbm.at[pl.ds(base + FRONT, N_META)], sem.at[1]),
            pltpu.make_async_copy(zero_ref.at[pl.ds(0, FRONT)], h_hbm.at[pl.ds(base, FRONT)], sem.at[2]),
        ]
        if tail:
            copies.append(pltpu.make_async_copy(zero_ref.at[pl.ds(0, tail)],
                                                h_hbm.at[pl.ds(base + OFF + seq, tail)], sem.at[3]))
        for c in copies:
            c.start()
        for c in copies:
            c.wait()

    frames.wait()


def _embed(x2d, meta_tokens, bsz, seq, tp):
    d = x2d.shape[1]
    rows = _pick(seq, (1024, 512, 256, 128, 64, 32, 16, 8))
    per_seq = seq // rows
    return pl.pallas_call(
        functools.partial(_embed_body, seq=seq, tp=tp, rows=rows),
        out_shape=jax.ShapeDtypeStruct((bsz * tp, d), F32),
        grid=(bsz, per_seq),
        in_specs=[pl.BlockSpec((rows, d), lambda b, i: (b * per_seq + i, 0)),
                  pl.BlockSpec((N_META, d), lambda b, i: (0, 0))],
        out_specs=pl.BlockSpec(memory_space=pl.ANY),
        scratch_shapes=[pltpu.VMEM((max(FRONT, tp - OFF - seq), d), F32), pltpu.SemaphoreType.DMA((4,))],
        compiler_params=_cparams(("arbitrary", "arbitrary")),
        name="embed",
    )(x2d, meta_tokens)


def _norm_inproj_body(h_ref, g_ref, w_ref, wst_ref, wsc_ref, o_ref, ot_ref, oc_ref, xn_ref):
    @pl.when(pl.program_id(1) == 0)
    def _():
        x = h_ref[...]
        ms = jnp.mean(x * x, axis=-1, keepdims=True)
        xn = (x * lax.rsqrt(ms + EPS) * g_ref[...]).astype(BF16)
        xn_ref[...] = xn
        ot_ref[...] = lax.dot_general(wst_ref[...], xn, (((1,), (1,)), ((), ())),
                                      preferred_element_type=F32)
        oc_ref[...] = jnp.dot(xn, wsc_ref[...], preferred_element_type=F32)

    o_ref[...] = jnp.dot(xn_ref[...], w_ref[...], preferred_element_type=F32).astype(o_ref.dtype)


def _norm_inproj(h, g, w_main, w_small_t, w_small_c, bm, bn):
    n, d = h.shape
    nw = w_main.shape[1]
    return pl.pallas_call(
        _norm_inproj_body,
        out_shape=[jax.ShapeDtypeStruct((n, nw), BF16),
                   jax.ShapeDtypeStruct((16, n), F32),
                   jax.ShapeDtypeStruct((n, LANES), F32)],
        grid=(n // bm, nw // bn),
        in_specs=[pl.BlockSpec((bm, d), lambda i, j: (i, 0)),
                  pl.BlockSpec((1, d), lambda i, j: (0, 0)),
                  pl.BlockSpec((d, bn), lambda i, j: (0, j)),
                  pl.BlockSpec((16, d), lambda i, j: (0, 0)),
                  pl.BlockSpec((d, LANES), lambda i, j: (0, 0))],
        out_specs=[pl.BlockSpec((bm, bn), lambda i, j: (i, j)),
                   pl.BlockSpec((16, bm), lambda i, j: (0, i)),
                   pl.BlockSpec((bm, LANES), lambda i, j: (i, 0))],
        scratch_shapes=[pltpu.VMEM((bm, d), BF16)],
        compiler_params=_cparams(("arbitrary", "arbitrary")),
        name="norm_inproj",
    )(h, g, w_main, w_small_t, w_small_c)


def _split3(x):
    p1 = x.astype(BF16)
    r1 = x - p1.astype(F32)
    p2 = r1.astype(BF16)
    p3 = (r1 - p2.astype(F32)).astype(BF16)
    return p1, p2, p3


def _fox_prep_body(f_ref, fb_ref, ke_ref, carry_ref, *, rows, n_real):
    i = pl.program_id(1)

    @pl.when(i == 0)
    def _():
        carry_ref[...] = jnp.zeros_like(carry_ref)

    f = f_ref[...] + fb_ref[...]
    lf = jnp.minimum(f, 0.0) - jnp.log1p(jnp.exp(-jnp.abs(f)))
    pos = i * rows + lax.broadcasted_iota(I32, (rows, 1), 0)
    lf = jnp.where((pos >= FRONT) & (pos < FRONT + n_real), lf, 0.0)
    rr = lax.broadcasted_iota(I32, (rows, rows), 0)
    cc = lax.broadcasted_iota(I32, (rows, rows), 1)
    tri = jnp.where(rr >= cc, 1.0, 0.0).astype(BF16)
    c = carry_ref[0:1, :]
    for piece in _split3(lf):
        c = c + jnp.dot(tri, piece, preferred_element_type=F32)
    carry_ref[...] = jnp.broadcast_to(c[rows - 1:rows, :], carry_ref.shape)
    cs = jnp.where(pos < FRONT, NEG, -LOG2E * c)
    p1, p2, p3 = (p.astype(F32) for p in _split3(cs))
    lane = lax.broadcasted_iota(I32, (rows, LANES), 1)
    for h in range(N_HEADS):
        ext = jnp.where(lane == 0, p1[:, h:h + 1],
                        jnp.where(lane == 1, p2[:, h:h + 1], jnp.where(lane == 2, p3[:, h:h + 1], 0.0)))
        ke_ref[:, h * HEAD_DIM:(h + 1) * HEAD_DIM] = ext.astype(BF16)


def _fox_prep(f_slab, forget_b, bsz, tp, n_real):
    n = f_slab.shape[0]
    hw = N_HEADS * HEAD_DIM
    rows = UNIT
    nt = tp // rows
    fb = jnp.pad(forget_b.astype(F32), (0, LANES - N_HEADS)).reshape(1, LANES)
    return pl.pallas_call(
        functools.partial(_fox_prep_body, rows=rows, n_real=n_real),
        out_shape=jax.ShapeDtypeStruct((n, hw), BF16),
        grid=(bsz, nt),
        in_specs=[pl.BlockSpec((rows, LANES), lambda b, i: (b * nt + i, 0)),
                  pl.BlockSpec((1, LANES), lambda b, i: (0, 0))],
        out_specs=pl.BlockSpec((rows, hw), lambda b, i: (b * nt + i, 0)),
        scratch_shapes=[pltpu.VMEM((8, LANES), F32)],
        compiler_params=_cparams(("arbitrary", "arbitrary")),
        name="fox_prep",
    )(f_slab, fb)


def _fox_attn_body(q_ref, k_ref, ke_ref, v_ref, ve_ref, o_ref, *, blk, n_split, group):
    qi = pl.program_id(2)
    half = blk // n_split
    lane = lax.broadcasted_iota(I32, (blk, HEAD_DIM), 1)
    q = jnp.concatenate([q_ref[...], jnp.where(lane < N_BIAS_PIECES, 1.0, 0.0).astype(BF16)], axis=1)
    qs = tuple(q[r * half:(r + 1) * half] for r in range(n_split))

    def step(js, carry, diag_last):
        starts = [pl.multiple_of(j * blk, blk) for j in js]
        ss = []
        for r, qh in enumerate(qs):
            srow = []
            for b, st in enumerate(starts):
                on_diag = diag_last and b == len(js) - 1
                ncol = (r + 1) * half if on_diag else blk
                k = jnp.concatenate([k_ref[pl.ds(st, ncol), :], ke_ref[pl.ds(st, ncol), :]], axis=1)
                s = lax.dot_general(qh, k, (((1,), (1,)), ((), ())), preferred_element_type=F32)
                if on_diag:
                    row = r * half + lax.broadcasted_iota(I32, (half, ncol), 0)
                    col = lax.broadcasted_iota(I32, (half, ncol), 1)
                    s = jnp.where(col <= row, s, NEG)
                srow.append((s, st, ncol))
            ss.append(srow)
        out = []
        for r in range(n_split):
            m = carry[2 * r]
            m_new = m
            for s, _, _ in ss[r]:
                m_new = jnp.maximum(m_new, jnp.max(s, axis=1, keepdims=True))
            acc = jnp.exp2(m - m_new) * carry[2 * r + 1]
            for s, st, ncol in ss[r]:
                v = jnp.concatenate([v_ref[pl.ds(st, ncol), :], ve_ref[0:ncol, :]], axis=1)
                acc = acc + jnp.dot(jnp.exp2(s - m_new).astype(BF16), v, preferred_element_type=F32)
            out += [m_new, acc]
        return tuple(out)

    init = (jnp.full((half, 1), NEG, F32), jnp.zeros((half, 2 * HEAD_DIM), F32)) * n_split
    carry = lax.fori_loop(0, qi // group,
                          lambda j, c: step(tuple(group * j + g for g in range(group)), c, False), init)
    tails = [functools.partial(lambda r, c: step(tuple(qi - r + g for g in range(r + 1)), c, True), r)
             for r in range(group)]
    res = lax.switch(qi % group, tails, carry)
    for r in range(n_split):
        acc = res[2 * r + 1]
        o_ref[r * half:(r + 1) * half, :] = (acc[:, 0:HEAD_DIM] / acc[:, HEAD_DIM:HEAD_DIM + 1]
                                             ).astype(o_ref.dtype)


def _fox_attn(z, ke, bsz, tp):
    n = z.shape[0]
    blk = _pick(tp, (768, 512, 256))
    nq = tp // blk
    v_ext = jnp.zeros((blk, HEAD_DIM), BF16).at[:, 0].set(1.0)
    return pl.pallas_call(
        functools.partial(_fox_attn_body, blk=blk, n_split=2, group=4),
        out_shape=jax.ShapeDtypeStruct((n, N_HEADS * HEAD_DIM), BF16),
        grid=(bsz, N_HEADS, nq),
        in_specs=[pl.BlockSpec((blk, HEAD_DIM), lambda b, h, i: (b * nq + i, h)),
                  pl.BlockSpec((tp, HEAD_DIM), lambda b, h, i: (b, N_HEADS + h)),
                  pl.BlockSpec((tp, HEAD_DIM), lambda b, h, i: (b, h)),
                  pl.BlockSpec((tp, HEAD_DIM), lambda b, h, i: (b, 2 * N_HEADS + h)),
                  pl.BlockSpec((blk, HEAD_DIM), lambda b, h, i: (0, 0))],
        out_specs=pl.BlockSpec((blk, HEAD_DIM), lambda b, h, i: (b * nq + i, h)),
        compiler_params=_cparams(("arbitrary", "arbitrary", "arbitrary")),
        name="fox_attn",
    )(z, z, ke, z, v_ext)


def _shifted_taps(x_ext, halo, rows, n_taps):
    for r in range(min(8, n_taps)):
        rolled = x_ext if r == 0 else pltpu.roll(x_ext, r, 0)
        for q in range(halo // 8):
            s = 8 * q + r
            if s > n_taps - 1:
                continue
            yield n_taps - 1 - s, rolled[halo - 8 * q:halo - 8 * q + rows]


def _conformer_body(a_ref, b_ref, w_ref, cb_ref, lg_ref, lb_ref, o_ref, halo_ref, *, bt):
    halo = halo_ref.shape[0]

    @pl.when(pl.program_id(1) == 0)
    def _():
        halo_ref[...] = jnp.zeros_like(halo_ref)

    u = a_ref[...].astype(F32) * _sigmoid(b_ref[...].astype(F32))
    x_ext = jnp.concatenate([halo_ref[...], u], axis=0)
    halo_ref[...] = u[bt - halo:bt]
    acc = jnp.zeros_like(u)
    for j, tap in _shifted_taps(x_ext, halo, bt, CONF_K):
        acc = acc + w_ref[j:j + 1, :] * tap
    y = acc + cb_ref[...]
    mu = jnp.mean(y, axis=-1, keepdims=True)
    yc = y - mu
    var = jnp.mean(yc * yc, axis=-1, keepdims=True)
    yn = yc * lax.rsqrt(var + EPS) * lg_ref[...] + lb_ref[...]
    o_ref[...] = _silu(yn).astype(o_ref.dtype)


def _conformer(z, conv_w, conv_b, ln_g, ln_b, bsz, tp, col_a, col_b):
    n = z.shape[0]
    c = conv_w.shape[1]
    bt = UNIT
    nt = tp // bt
    w = jnp.pad(conv_w.astype(F32), ((0, 32 - CONF_K), (0, 0)))
    row = lambda v: v.astype(F32).reshape(1, c)
    return pl.pallas_call(
        functools.partial(_conformer_body, bt=bt),
        out_shape=jax.ShapeDtypeStruct((n, c), BF16),
        grid=(bsz, nt),
        in_specs=[pl.BlockSpec((bt, c), lambda b, i: (b * nt + i, col_a)),
                  pl.BlockSpec((bt, c), lambda b, i: (b * nt + i, col_b)),
                  pl.BlockSpec((32, c), lambda b, i: (0, 0)),
                  pl.BlockSpec((1, c), lambda b, i: (0, 0)),
                  pl.BlockSpec((1, c), lambda b, i: (0, 0)),
                  pl.BlockSpec((1, c), lambda b, i: (0, 0))],
        out_specs=pl.BlockSpec((bt, c), lambda b, i: (b * nt + i, 0)),
        scratch_shapes=[pltpu.VMEM((32, c), F32)],
        compiler_params=_cparams(("arbitrary", "arbitrary")),
        name="conformer_conv",
    )(z, z, w, row(conv_b), row(ln_g), row(ln_b))


def _outproj_router_body(ya_ref, yb_ref, h_ref, w_ref, g_ref, wr_ref, br_ref, ho_ref, r_ref, cnt_ref,
                         xp_ref, er_ref, *, bm, blocks_per_seq, n_real):
    kh = ya_ref.shape[1]
    pos0 = (pl.program_id(0) % blocks_per_seq) * bm
    rh = bm // 2
    parts = [slice(r * rh, (r + 1) * rh) for r in range(2)]
    ys = [jnp.dot(ya_ref[p, :], w_ref[0:kh, :], preferred_element_type=F32)
          + jnp.dot(yb_ref[p, :], w_ref[kh:2 * kh, :], preferred_element_type=F32) for p in parts]
    lgs = []
    for r, (p, y) in enumerate(zip(parts, ys)):
        hn = jnp.where(_row_valid(pos0 + r * rh, rh, n_real), h_ref[p, :] + y, 0.0)
        ho_ref[p, :] = hn
        ms = jnp.mean(hn * hn, axis=-1, keepdims=True)
        xn = hn * lax.rsqrt(ms + EPS) * g_ref[...]
        xp_ref[p, :] = _pack_pairs(xn)
        xh = xn.astype(BF16)
        xl = (xn - xh.astype(F32)).astype(BF16)
        t = jnp.dot(xh, wr_ref[...], preferred_element_type=F32)
        lgs.append(t[:, 0:LANES] + t[:, LANES:2 * LANES]
                   + jnp.dot(xl, wr_ref[:, 0:LANES], preferred_element_type=F32) + br_ref[...])
    lg = jnp.concatenate(lgs, axis=0)
    lane = lax.broadcasted_iota(I32, lg.shape, 1)
    lane_f = lane.astype(F32)
    big = float(LANES)

    is_grp = (lane >= N_EXPERTS) & (lane < N_EXPERTS + N_GROUPS)
    gl = jnp.where(is_grp, lg, -jnp.inf)
    gmax = jnp.max(gl, axis=1, keepdims=True)
    gidx = jnp.min(jnp.where(gl == gmax, lane_f, big), axis=1, keepdims=True) - N_EXPERTS
    g_prob = 1.0 / jnp.sum(jnp.where(is_grp, jnp.exp(lg - gmax), 0.0), axis=1, keepdims=True)

    lo = gidx * EXPERTS_PER_GROUP
    in_grp = (lane_f >= lo) & (lane_f < lo + EXPERTS_PER_GROUP)
    el = jnp.where(in_grp, lg, -jnp.inf)
    m1 = jnp.max(el, axis=1, keepdims=True)
    i1 = jnp.min(jnp.where(el == m1, lane_f, big), axis=1, keepdims=True)
    el2 = jnp.where(lane_f == i1, -jnp.inf, el)
    m2 = jnp.max(el2, axis=1, keepdims=True)
    i2 = jnp.min(jnp.where(el2 == m2, lane_f, big), axis=1, keepdims=True)
    e2 = jnp.exp(m2 - m1)
    w1 = g_prob / (1.0 + e2)
    w2 = g_prob * e2 / (1.0 + e2)

    @pl.when(pl.program_id(0) == 0)
    def _():
        cnt_ref[...] = jnp.zeros_like(cnt_ref)

    oh1 = lane_f == i1
    oh2 = lane_f == i2
    onehots = jnp.concatenate([jnp.where(oh1, 1.0, 0.0), jnp.where(oh2, 1.0, 0.0)], axis=1).astype(BF16)
    rr = lax.broadcasted_iota(I32, (bm, bm), 0)
    cc = lax.broadcasted_iota(I32, (bm, bm), 1)
    tri = jnp.where(rr >= cc, 1.0, 0.0).astype(BF16)
    csum = jnp.dot(tri, onehots, preferred_element_type=F32)
    c1 = csum[:, 0:LANES]
    c2 = csum[:, LANES:2 * LANES]
    tot1 = c1[bm - 1:bm, :]
    tot2 = c2[bm - 1:bm, :]
    before = cnt_ref[0:1, :]
    rank1 = jnp.sum(jnp.where(oh1, before + c1 - 1.0, 0.0), axis=1, keepdims=True)
    rank2 = jnp.sum(jnp.where(oh2, before + tot1 + c2 - 1.0, 0.0), axis=1, keepdims=True)
    cnt_ref[...] = jnp.broadcast_to(before + tot1 + tot2, cnt_ref.shape)

    r_ref[...] = jnp.where(lane == 0, i1, jnp.where(lane == 1, i2,
                           jnp.where(lane == 2, w1, jnp.where(lane == 3, w2, 0.0))))
    eye = rr == cc
    er_ref[0] = jnp.concatenate(
        [jnp.sum(jnp.where(eye, col, 0.0), axis=0, keepdims=True) for col in (i1, i2, rank1, rank2)], axis=0)


def _outproj_router(ya, yb, h, w_out, g_ffn, w_router, b_router, tp, n_real):
    n, d = h.shape
    half = ya.shape[1]
    bm = UNIT
    return pl.pallas_call(
        functools.partial(_outproj_router_body, bm=bm, blocks_per_seq=tp // bm, n_real=n_real),
        out_shape=[jax.ShapeDtypeStruct((n, d), F32), jax.ShapeDtypeStruct((n, LANES), F32),
                   jax.ShapeDtypeStruct((8, LANES), F32), jax.ShapeDtypeStruct((n, d // 2), U32),
                   jax.ShapeDtypeStruct((n // bm, 4, bm), F32)],
        grid=(n // bm,),
        in_specs=[pl.BlockSpec((bm, half), lambda i: (i, 0)),
                  pl.BlockSpec((bm, half), lambda i: (i, 0)),
                  pl.BlockSpec((bm, d), lambda i: (i, 0)),
                  pl.BlockSpec((2 * half, d), lambda i: (0, 0)),
                  pl.BlockSpec((1, d), lambda i: (0, 0)),
                  pl.BlockSpec((d, 2 * LANES), lambda i: (0, 0)),
                  pl.BlockSpec((1, LANES), lambda i: (0, 0))],
        out_specs=[pl.BlockSpec((bm, d), lambda i: (i, 0)),
                   pl.BlockSpec((bm, LANES), lambda i: (i, 0)),
                   pl.BlockSpec((8, LANES), lambda i: (0, 0)),
                   pl.BlockSpec((bm, d // 2), lambda i: (i, 0)),
                   pl.BlockSpec((1, 4, bm), lambda i: (i, 0, 0))],
        compiler_params=_cparams(("arbitrary",)),
        name="outproj_router",
    )(ya, yb, h, w_out, g_ffn, w_router, b_router)


def _pack_pairs(x):
    c = x.shape[1] // 2
    hi = lax.bitcast_convert_type(x[:, :c].astype(BF16).astype(F32), U32)
    lo = lax.bitcast_convert_type(x[:, c:].astype(BF16).astype(F32), U32)
    return hi | (lo >> 16)


def _unpack_pairs(p):
    hi = lax.bitcast_convert_type(p & jnp.uint32(0xFFFF0000), F32)
    lo = lax.bitcast_convert_type(p << 16, F32)
    return hi, lo


def _dispatch_body(meta_ref, dest_ref, xin_ref, xs_hbm, xp_ref, zero_ref, sem, zsem, *, tb):
    i = pl.program_id(0)
    slot = i % 2

    def drain(s):
        for _ in range(2):
            pltpu.make_async_copy(zero_ref, xs_hbm.at[pl.ds(0, tb)], sem.at[s]).wait()

    for g in range(tb // 8):
        xp_ref[slot, g] = xin_ref[g * 8:(g + 1) * 8, :]

    def issue(g, _):
        for u in range(8):
            for k in range(2):
                d = dest_ref[0, 0, k * tb + g * 8 + u]
                pltpu.make_async_copy(xp_ref.at[slot, g, pl.ds(u, 1)], xs_hbm.at[pl.ds(d, 1)],
                                      sem.at[slot]).start()
        return 0

    lax.fori_loop(0, tb // 8, issue, 0)

    @pl.when(i > 0)
    def _():
        drain(1 - slot)

    @pl.when(i == pl.num_programs(0) - 1)
    def _():
        drain(slot)

    @pl.when(i == 0)
    def _():
        zero_ref[...] = jnp.zeros_like(zero_ref)

        def per_expert(e, _):
            cnt = meta_ref[e]
            start = meta_ref[N_EXPERTS + e]
            padded = meta_ref[2 * N_EXPERTS + e]

            def pad_copy(r):
                return pltpu.make_async_copy(zero_ref.at[pl.ds(0, 1)], xs_hbm.at[pl.ds(start + r, 1)], zsem)

            def zissue(r, _):
                pad_copy(r).start()
                return 0

            def zwait(r, _):
                pad_copy(r).wait()
                return 0

            lax.fori_loop(cnt, padded, zissue, 0)
            lax.fori_loop(cnt, padded, zwait, 0)
            return 0

        lax.fori_loop(0, N_EXPERTS, per_expert, 0)

        def tail_copy(blk):
            return pltpu.make_async_copy(zero_ref, xs_hbm.at[pl.ds(blk * MOE_BM, MOE_BM)], zsem)

        def tissue(blk, _):
            tail_copy(blk).start()
            return 0

        def twait(blk, _):
            tail_copy(blk).wait()
            return 0

        n_used = meta_ref[3 * N_EXPERTS]
        lax.fori_loop(n_used, xs_hbm.shape[0] // MOE_BM, tissue, 0)
        lax.fori_loop(n_used, xs_hbm.shape[0] // MOE_BM, twait, 0)


def _dispatch(xp, dest3, meta, n_rows):
    n, dp = xp.shape
    tb = UNIT
    return pl.pallas_call(
        functools.partial(_dispatch_body, tb=tb),
        out_shape=jax.ShapeDtypeStruct((n_rows, dp), U32),
        grid_spec=pltpu.PrefetchScalarGridSpec(
            num_scalar_prefetch=1,
            grid=(n // tb,),
            in_specs=[pl.BlockSpec((1, 1, 2 * tb), lambda i, m: (i, 0, 0), memory_space=pltpu.SMEM),
                      pl.BlockSpec((tb, dp), lambda i, m: (i, 0))],
            out_specs=pl.BlockSpec(memory_space=pl.ANY),
            scratch_shapes=[pltpu.VMEM((2, tb // 8, 8, dp), U32), pltpu.VMEM((MOE_BM, dp), U32),
                            pltpu.SemaphoreType.DMA((2,)), pltpu.SemaphoreType.DMA(())]),
        compiler_params=_cparams(("arbitrary",)),
        name="moe_dispatch",
    )(meta, dest3, xp)


def _experts_body(be_ref, nxt_ref, nu_ref, xs_ref, wg_hbm, wu_hbm, wd_hbm, ys_ref,
                  wg_st, wu_st, wd_st, wg_bf, wu_bf, wd_bf, ord_ref, sem, *, layer):
    i = pl.program_id(0)
    n_used = nu_ref[0]

    def weight_copies(e, s):
        return (pltpu.make_async_copy(wg_hbm.at[layer, e], wg_st.at[s], sem.at[s, 0]),
                pltpu.make_async_copy(wu_hbm.at[layer, e], wu_st.at[s], sem.at[s, 1]),
                pltpu.make_async_copy(wd_hbm.at[layer, e], wd_st.at[s], sem.at[s, 2]))

    @pl.when(i == 0)
    def _():
        ord_ref[0] = 0
        for c in weight_copies(be_ref[0], 0):
            c.start()

    @pl.when(i < n_used)
    def _():
        e = be_ref[i]
        prev = be_ref[jnp.maximum(i - 1, 0)]

        @pl.when((i == 0) | (e != prev))
        def _():
            s = ord_ref[0] % 2
            for c in weight_copies(e, s):
                c.wait()
            nxt = nxt_ref[i]

            @pl.when(nxt != e)
            def _():
                for c in weight_copies(nxt, 1 - s):
                    c.start()

            wg_bf[...] = wg_st[s].astype(BF16)
            wu_bf[...] = wu_st[s].astype(BF16)
            wd_bf[...] = wd_st[s].astype(BF16)
            ord_ref[0] = ord_ref[0] + 1

        half = MOE_BM // 2
        kh = wg_bf.shape[0] // 2
        xns = []
        for r in range(2):
            hi, lo = _unpack_pairs(xs_ref[r * half:(r + 1) * half, :])
            xns.append((hi.astype(BF16), lo.astype(BF16)))

        def up(w_bf):
            return [jnp.dot(xa, w_bf[0:kh, :], preferred_element_type=F32)
                    + jnp.dot(xb, w_bf[kh:2 * kh, :], preferred_element_type=F32) for xa, xb in xns]

        hgs = up(wg_bf)
        hus = up(wu_bf)
        hids = [(_silu(hg) * hu).astype(BF16) for hg, hu in zip(hgs, hus)]
        for r in range(2):
            ys_ref[r * half:(r + 1) * half, :] = _pack_pairs(
                jnp.dot(hids[r], wd_bf[...], preferred_element_type=F32))

    @pl.when(i >= n_used)
    def _():
        ys_ref[...] = jnp.zeros_like(ys_ref)


def _experts(xs, w_gate, w_up, w_down, layer, block_expert, next_expert, n_used):
    r, dp = xs.shape
    d = 2 * dp
    de = w_gate.shape[3]
    nb = r // MOE_BM
    return pl.pallas_call(
        functools.partial(_experts_body, layer=layer),
        out_shape=jax.ShapeDtypeStruct((r, dp), U32),
        grid_spec=pltpu.PrefetchScalarGridSpec(
            num_scalar_prefetch=3,
            grid=(nb,),
            in_specs=[pl.BlockSpec((MOE_BM, dp), lambda i, be, nx, nu: (i, 0)),
                      pl.BlockSpec(memory_space=pl.ANY),
                      pl.BlockSpec(memory_space=pl.ANY),
                      pl.BlockSpec(memory_space=pl.ANY)],
            out_specs=pl.BlockSpec((MOE_BM, dp), lambda i, be, nx, nu: (i, 0)),
            scratch_shapes=[pltpu.VMEM((2, d, de), F32), pltpu.VMEM((2, d, de), F32),
                            pltpu.VMEM((2, de, d), F32),
                            pltpu.VMEM((d, de), BF16), pltpu.VMEM((d, de), BF16), pltpu.VMEM((de, d), BF16),
                            pltpu.SMEM((1,), I32), pltpu.SemaphoreType.DMA((2, 3))]),
        compiler_params=_cparams(("arbitrary",)),
        name="moe_experts",
    )(block_expert, next_expert, n_used, xs, w_gate, w_up, w_down)


def _combine_body(dest_ref, dnext_ref, h_ref, r_ref, gf_ref, ys_hbm, o_ref, buf, sem, *out_stage,
                  tb, blocks_per_seq, n_real, final_norm):
    i = pl.program_id(0)
    slot = i % 2

    def gather(d_ref, s):
        def issue(t, _):
            for k in range(2):
                d = d_ref[0, 0, k * tb + t]
                pltpu.make_async_copy(ys_hbm.at[pl.ds(d, 1)], buf.at[s, k, pl.ds(t, 1)], sem.at[s]).start()
            return 0

        lax.fori_loop(0, tb, issue, 0, unroll=8)

    @pl.when(i == 0)
    def _():
        gather(dest_ref, 0)

    @pl.when(i + 1 < pl.num_programs(0))
    def _():
        gather(dnext_ref, 1 - slot)

    for k in range(2):
        pltpu.make_async_copy(ys_hbm.at[pl.ds(0, tb)], buf.at[slot, k], sem.at[slot]).wait()

    r = r_ref[...]
    w1 = r[:, 2:3]
    w2 = r[:, 3:4]
    y1 = _unpack_pairs(buf[slot, 0])
    y2 = _unpack_pairs(buf[slot, 1])
    y = jnp.concatenate([w1 * y1[0] + w2 * y2[0], w1 * y1[1] + w2 * y2[1]], axis=1)
    pos0 = (i % blocks_per_seq) * tb
    hn = jnp.where(_row_valid(pos0, tb, n_real), h_ref[...] + y, 0.0)
    if not final_norm:
        o_ref[...] = hn
        return

    obuf, osem = out_stage
    seq = n_real - N_META
    ms = jnp.mean(hn * hn, axis=-1, keepdims=True)
    obuf[slot] = hn * lax.rsqrt(ms + EPS) * gf_ref[...]

    def frame_copy(step, s, start):
        b = step // blocks_per_seq
        j = step % blocks_per_seq
        row0 = b * seq + j * tb - OFF
        tail = OFF + seq - (blocks_per_seq - 1) * tb

        def go(src, dst):
            cp = pltpu.make_async_copy(src, dst, osem.at[s])
            cp.start() if start else cp.wait()

        @pl.when(j == 0)
        def _():
            go(obuf.at[s, pl.ds(OFF, tb - OFF)], o_ref.at[pl.ds(b * seq, tb - OFF)])

        @pl.when((j > 0) & (j < blocks_per_seq - 1))
        def _():
            go(obuf.at[s], o_ref.at[pl.ds(row0, tb)])

        @pl.when(j == blocks_per_seq - 1)
        def _():
            go(obuf.at[s, pl.ds(0, tail)], o_ref.at[pl.ds(row0, tail)])

    frame_copy(i, slot, True)

    @pl.when(i > 0)
    def _():
        frame_copy(i - 1, 1 - slot, False)

    @pl.when(i == pl.num_programs(0) - 1)
    def _():
        frame_copy(i, slot, False)


def _combine(h, route, dest3, ys, g_final, tp, n_real, final_norm):
    n, d = h.shape
    tb = UNIT
    last = n // tb - 1
    assert tp // tb >= 2
    if final_norm:
        out_shape = jax.ShapeDtypeStruct((n // tp * (n_real - N_META), d), F32)
        out_spec = pl.BlockSpec(memory_space=pl.ANY)
        out_stage = [pltpu.VMEM((2, tb, d), F32), pltpu.SemaphoreType.DMA((2,))]
    else:
        out_shape = jax.ShapeDtypeStruct((n, d), F32)
        out_spec = pl.BlockSpec((tb, d), lambda i: (i, 0))
        out_stage = []
    return pl.pallas_call(
        functools.partial(_combine_body, tb=tb, blocks_per_seq=tp // tb, n_real=n_real,
                          final_norm=final_norm),
        out_shape=out_shape,
        grid=(n // tb,),
        in_specs=[pl.BlockSpec((1, 1, 2 * tb), lambda i: (i, 0, 0), memory_space=pltpu.SMEM),
                  pl.BlockSpec((1, 1, 2 * tb), lambda i: (jnp.minimum(i + 1, last), 0, 0),
                               memory_space=pltpu.SMEM),
                  pl.BlockSpec((tb, d), lambda i: (i, 0)),
                  pl.BlockSpec((tb, LANES), lambda i: (i, 0)),
                  pl.BlockSpec((1, d), lambda i: (0, 0)),
                  pl.BlockSpec(memory_space=pl.ANY)],
        out_specs=out_spec,
        scratch_shapes=[pltpu.VMEM((2, 2, tb, d // 2), U32), pltpu.SemaphoreType.DMA((2,))] + out_stage,
        compiler_params=_cparams(("arbitrary",)),
        name="moe_combine",
    )(dest3, dest3, h, route, g_final, ys)


def _moe(h, xp, route, counts_f, er, w_gate, w_up, w_down, layer, g_final, tp, n_real, final_norm):
    n, d = h.shape
    n_pairs = 2 * n
    counts = counts_f[0, 0:N_EXPERTS].astype(I32)
    padded = (counts + MOE_BM - 1) // MOE_BM * MOE_BM
    p_ends = jnp.cumsum(padded)
    p_starts = p_ends - padded
    e_t = er[:, 0:2, :].astype(I32)
    start_of = jnp.sum(jnp.where(e_t[..., None] == jnp.arange(N_EXPERTS, dtype=I32), p_starts, 0), axis=-1)
    dest = start_of + er[:, 2:4, :].astype(I32)
    n_blocks = -(-(n_pairs + N_EXPERTS * (MOE_BM - 1)) // MOE_BM)
    block_row0 = jnp.arange(n_blocks, dtype=I32) * MOE_BM
    block_expert = jnp.minimum(jnp.sum((p_ends[None, :] <= block_row0[:, None]).astype(I32), axis=1),
                               N_EXPERTS - 1)
    n_used = (p_ends[-1:] // MOE_BM).astype(I32)
    meta = jnp.concatenate([counts, p_starts, padded, n_used]).astype(I32)
    dest3 = dest.reshape(n // UNIT, 1, 2 * UNIT)

    xs = _dispatch(xp, dest3, meta, n_blocks * MOE_BM)
    ids = jnp.arange(N_EXPERTS, dtype=I32)
    owner = jnp.where(counts > 0, ids, N_EXPERTS)
    first_from = lax.cummin(owner[::-1])[::-1]
    after = jnp.concatenate([first_from[1:], jnp.full((1,), N_EXPERTS, I32)])
    next_expert = jnp.take(jnp.where(after < N_EXPERTS, after, ids), block_expert)
    ys = _experts(xs, w_gate, w_up, w_down, layer, block_expert, next_expert, n_used)
    return _combine(h, route, dest3, ys, g_final, tp, n_real, final_norm)


def _short_conv(x, halo_ref, w_ref, first):
    rows = x.shape[0]

    @pl.when(first)
    def _():
        halo_ref[...] = jnp.zeros_like(halo_ref)

    x_ext = jnp.concatenate([halo_ref[...], x], axis=0)
    halo_ref[...] = x[rows - 8:rows]
    acc = jnp.zeros_like(x)
    for j, tap in _shifted_taps(x_ext, 8, rows, SHORT_K):
        acc = acc + w_ref[j:j + 1, :] * tap
    return acc


def _bdot(a, b):
    return jnp.dot(a.astype(BF16), b.astype(BF16), preferred_element_type=F32)


def _gdn_pre_body(z_ref, sc_ref, st_ref, cw_ref, al_r_ref, dt_r_ref, al_c_ref, dt_c_ref,
                  u_ref, w_ref, qd_ref, kd_ref, at_ref, eg_ref, halo_ref):
    rows = z_ref.shape[0]
    hw = N_HEADS * HEAD_DIM
    x = _silu(_short_conv(z_ref[...].astype(F32), halo_ref, cw_ref, pl.program_id(1) == 0))

    sc = sc_ref[...]
    g_cols = -jnp.exp(al_r_ref[...]) * _softplus(sc + dt_r_ref[...])
    beta_cols = _sigmoid(sc)
    st = st_ref[...]
    g_rows = -jnp.exp(al_c_ref[:, 0:1]) * _softplus(st + dt_c_ref[:, 0:1])

    ri = lax.broadcasted_iota(I32, (rows, rows), 0)
    ci = lax.broadcasted_iota(I32, (rows, rows), 1)
    same64 = (ri >> 6) == (ci >> 6)
    same32 = (ri >> 5) == (ci >> 5)
    same16 = (ri >> 4) == (ci >> 4)
    lower = ri >= ci
    strict = ri > ci
    lane = lax.broadcasted_iota(I32, (rows, LANES), 1)
    eg_slab = jnp.zeros((rows, LANES), F32)
    scale = HEAD_DIM ** -0.5

    heads = range(N_HEADS)
    in_chunk_lower = same64 & lower
    in_chunk_upper = same64 & (ri <= ci)
    a_mats, nmats, rhss = [], [], []
    for h in heads:
        sl = slice(h * HEAD_DIM, (h + 1) * HEAD_DIM)
        q = x[:, sl]
        k = x[:, hw + h * HEAD_DIM:hw + (h + 1) * HEAD_DIM]
        v = x[:, 2 * hw + h * HEAD_DIM:2 * hw + (h + 1) * HEAD_DIM]
        q = q * lax.rsqrt(jnp.sum(q * q, axis=-1, keepdims=True) + EPS)
        k = k * lax.rsqrt(jnp.sum(k * k, axis=-1, keepdims=True) + EPS)
        g_col = g_cols[:, h:h + 1]
        beta = beta_cols[:, N_HEADS + h:N_HEADS + h + 1]
        g_row = g_rows[h:h + 1, :]

        gc_col = jnp.sum(jnp.where(in_chunk_lower, g_row, 0.0), axis=1, keepdims=True)
        gc_row = jnp.sum(jnp.where(in_chunk_upper, g_col, 0.0), axis=0, keepdims=True)
        gtot_col = jnp.sum(jnp.where(same64, g_row, 0.0), axis=1, keepdims=True)
        decay = jnp.exp(jnp.where(in_chunk_lower, gc_col - gc_row, NEG))
        eg = jnp.exp(gc_col)
        ekd = jnp.exp(gtot_col - gc_col)

        kb = k * beta
        kbf = k.astype(BF16)
        kk = lax.dot_general(kb.astype(BF16), kbf, (((1,), (1,)), ((), ())), preferred_element_type=F32)
        qs = q * scale
        qk = lax.dot_general(qs.astype(BF16), kbf, (((1,), (1,)), ((), ())), preferred_element_type=F32)
        attn = qk * decay
        a_mats.append(jnp.where(strict, kk * decay, 0.0))
        rhss.append(jnp.concatenate([v * beta, kb * eg], axis=1))
        qd_ref[:, sl] = (qs * eg).astype(BF16)
        kd_ref[:, sl] = (k * ekd).astype(BF16)
        for c in range(rows // CHUNK):
            cs = slice(c * CHUNK, (c + 1) * CHUNK)
            at_ref[h, cs, :] = attn[cs, cs].astype(BF16)
        eg_slab = jnp.where(lane == h, eg, eg_slab)
    eg_ref[...] = eg_slab

    nmats = [jnp.where(same16, -a, 0.0) for a in a_mats]
    qqs = nmats
    for _ in range(3):
        qqs = [_bdot(qq, qq) for qq in qqs]
        prods = [_bdot(qq, nm) for qq, nm in zip(qqs, nmats)]
        nmats = [nm + qq + pr for nm, qq, pr in zip(nmats, qqs, prods)]
    for level_mask, inner_mask in ((same32, same16), (same64, same32)):
        sel = level_mask & jnp.logical_not(inner_mask)
        offs = [jnp.where(sel, a, 0.0) for a in a_mats]
        bmats = [off + _bdot(nm, off) for nm, off in zip(nmats, offs)]
        prods = [_bdot(bm, nm) for bm, nm in zip(bmats, nmats)]
        nmats = [nm - bm - pr for nm, bm, pr in zip(nmats, bmats, prods)]

    for h in heads:
        sl = slice(h * HEAD_DIM, (h + 1) * HEAD_DIM)
        uw = rhss[h] + _bdot(nmats[h], rhss[h])
        u_ref[:, sl] = uw[:, 0:HEAD_DIM]
        w_ref[:, sl] = uw[:, HEAD_DIM:2 * HEAD_DIM].astype(BF16)


def _gdn_pre(z, sc, st, conv_w, a_log, dt_bias, bsz, tp):
    n = z.shape[0]
    hw = N_HEADS * HEAD_DIM
    rows = UNIT
    nu = tp // rows
    pad_r = lambda v: jnp.pad(v.astype(F32), (0, LANES - N_HEADS)).reshape(1, LANES)
    pad_c = lambda v: jnp.broadcast_to(jnp.pad(v.astype(F32), (0, 16 - N_HEADS))[:, None], (16, LANES))
    outs = pl.pallas_call(
        _gdn_pre_body,
        out_shape=[jax.ShapeDtypeStruct((n, hw), F32),
                   jax.ShapeDtypeStruct((n, hw), BF16),
                   jax.ShapeDtypeStruct((n, hw), BF16),
                   jax.ShapeDtypeStruct((n, hw), BF16),
                   jax.ShapeDtypeStruct((N_HEADS, n, CHUNK), BF16),
                   jax.ShapeDtypeStruct((n, LANES), F32)],
        grid=(bsz, nu),
        in_specs=[pl.BlockSpec((rows, 3 * hw), lambda b, i: (b * nu + i, 0)),
                  pl.BlockSpec((rows, LANES), lambda b, i: (b * nu + i, 0)),
                  pl.BlockSpec((16, rows), lambda b, i: (0, b * nu + i)),
                  pl.BlockSpec((8, 3 * hw), lambda b, i: (0, 0)),
                  pl.BlockSpec((1, LANES), lambda b, i: (0, 0)),
                  pl.BlockSpec((1, LANES), lambda b, i: (0, 0)),
                  pl.BlockSpec((16, LANES), lambda b, i: (0, 0)),
                  pl.BlockSpec((16, LANES), lambda b, i: (0, 0))],
        out_specs=[pl.BlockSpec((rows, hw), lambda b, i: (b * nu + i, 0)),
                   pl.BlockSpec((rows, hw), lambda b, i: (b * nu + i, 0)),
                   pl.BlockSpec((rows, hw), lambda b, i: (b * nu + i, 0)),
                   pl.BlockSpec((rows, hw), lambda b, i: (b * nu + i, 0)),
                   pl.BlockSpec((N_HEADS, rows, CHUNK), lambda b, i: (0, b * nu + i, 0)),
                   pl.BlockSpec((rows, LANES), lambda b, i: (b * nu + i, 0))],
        scratch_shapes=[pltpu.VMEM((8, 3 * hw), F32)],
        compiler_params=_cparams(("arbitrary", "arbitrary")),
        name="gdn_pre",
    )(z, sc, st, jnp.pad(conv_w.astype(F32), ((0, 8 - SHORT_K), (0, 0))),
      pad_r(a_log), pad_r(dt_bias), pad_c(a_log), pad_c(dt_bias))
    return outs


def _gdn_scan_body(u_ref, w_ref, qd_ref, kd_ref, at_ref, eg_ref, z_ref, gn_ref, o_ref, s_ref, *, bsz):
    @pl.when(pl.program_id(0) == 0)
    def _():
        s_ref[...] = jnp.zeros_like(s_ref)

    units = [(b, h, slice(h * HEAD_DIM, (h + 1) * HEAD_DIM)) for b in range(bsz) for h in range(N_HEADS)]
    states = [s_ref[b * N_HEADS + h] for b, h, _ in units]
    rs = [jnp.dot(jnp.concatenate([w_ref[b, :, sl], qd_ref[b, :, sl]], axis=0), s.astype(BF16),
                  preferred_element_type=F32) for (b, h, sl), s in zip(units, states)]
    vbs = [(u_ref[b, :, sl] - r[0:CHUNK]).astype(BF16) for (b, h, sl), r in zip(units, rs)]
    intra = [jnp.dot(at_ref[h, b], vb, preferred_element_type=F32) for (b, h, sl), vb in zip(units, vbs)]
    outer = [lax.dot_general(kd_ref[b, :, sl], vb, (((0,), (0,)), ((), ())), preferred_element_type=F32)
             for (b, h, sl), vb in zip(units, vbs)]
    for (b, h, sl), s, r, a, kv in zip(units, states, rs, intra, outer):
        s_ref[b * N_HEADS + h] = s * eg_ref[b, CHUNK - 1:CHUNK, h:h + 1] + kv
        o = r[CHUNK:2 * CHUNK] + a
        on = o * lax.rsqrt(jnp.mean(o * o, axis=-1, keepdims=True) + EPS) * gn_ref[...]
        o_ref[b, :, sl] = (on * _silu(z_ref[b, :, sl].astype(F32))).astype(o_ref.dtype)


def _gdn_scan(u, w, qd, kd, attn, eg, z, out_norm_g, bsz, tp, z_col):
    hw = N_HEADS * HEAD_DIM
    nc = tp // CHUNK
    v3 = lambda a: a.reshape(bsz, tp, a.shape[-1])
    blk3 = pl.BlockSpec((bsz, CHUNK, hw), lambda c: (0, c, 0))
    return pl.pallas_call(
        functools.partial(_gdn_scan_body, bsz=bsz),
        out_shape=jax.ShapeDtypeStruct((bsz, tp, hw), BF16),
        grid=(nc,),
        in_specs=[blk3, blk3, blk3, blk3,
                  pl.BlockSpec((N_HEADS, bsz, CHUNK, CHUNK), lambda c: (0, 0, c, 0)),
                  pl.BlockSpec((bsz, CHUNK, LANES), lambda c: (0, c, 0)),
                  pl.BlockSpec((bsz, CHUNK, hw), lambda c: (0, c, z_col)),
                  pl.BlockSpec((1, HEAD_DIM), lambda c: (0, 0))],
        out_specs=blk3,
        scratch_shapes=[pltpu.VMEM((bsz * N_HEADS, HEAD_DIM, HEAD_DIM), F32)],
        compiler_params=_cparams(("arbitrary",)),
        name="gdn_scan",
    )(v3(u), v3(w), v3(qd), v3(kd), attn.reshape(N_HEADS, bsz, tp, CHUNK), v3(eg), v3(z),
      out_norm_g.astype(F32).reshape(1, HEAD_DIM)).reshape(bsz * tp, hw)


def _lru_body(x_ref, gate_ref, cw_ref, cb_ref, wr_ref, br_ref, wi_ref, bi_ref, lam_ref, o_ref,
              halo_ref, hc_ref, *, bt, n_real):
    first = pl.program_id(1) == 0

    @pl.when(first)
    def _():
        hc_ref[...] = jnp.zeros_like(hc_ref)

    x = _short_conv(x_ref[...].astype(F32), halo_ref, cw_ref, first) + cb_ref[...]
    nblk = wr_ref.shape[0]
    bd = wr_ref.shape[1]
    xb = x.astype(BF16)
    rg = jnp.concatenate([jnp.dot(xb[:, n * bd:(n + 1) * bd], wr_ref[n], preferred_element_type=F32)
                          for n in range(nblk)], axis=1)
    ig = jnp.concatenate([jnp.dot(xb[:, n * bd:(n + 1) * bd], wi_ref[n], preferred_element_type=F32)
                          for n in range(nblk)], axis=1)
    r = _sigmoid(rg + br_ref[...])
    ig = _sigmoid(ig + bi_ref[...])
    log_a = (-LRU_C * _softplus(-lam_ref[...])) * r
    a = jnp.exp(log_a)
    th = jnp.tanh(log_a)
    b = jnp.sqrt(-2.0 * th / (1.0 - th)) * (ig * x)
    b = jnp.where(_row_valid(pl.program_id(1) * bt, bt, n_real), b, 0.0)

    row = lax.broadcasted_iota(I32, (bt, 1), 0)
    d = 1
    while d < bt:
        keep = row >= d
        a_sh = jnp.where(keep, pltpu.roll(a, d, 0), 1.0)
        b_sh = jnp.where(keep, pltpu.roll(b, d, 0), 0.0)
        b = a * b_sh + b
        a = a * a_sh
        d *= 2
    hs = a * hc_ref[0:1, :] + b
    hc_ref[...] = jnp.broadcast_to(hs[bt - 1:bt, :], hc_ref.shape)
    o_ref[...] = (hs * jax.nn.gelu(gate_ref[...].astype(F32))).astype(o_ref.dtype)


def _lru(z, conv_w, conv_b, w_rg, b_rg, w_ig, b_ig, lam, bsz, tp, n_real, x_col, gate_col):
    n = z.shape[0]
    c = conv_w.shape[1]
    bt = UNIT
    nt = tp // bt
    row = lambda v: v.astype(F32).reshape(1, c)
    wspec = pl.BlockSpec(w_rg.shape, lambda b, i: (0, 0, 0))
    vspec = pl.BlockSpec((1, c), lambda b, i: (0, 0))
    return pl.pallas_call(
        functools.partial(_lru_body, bt=bt, n_real=n_real),
        out_shape=jax.ShapeDtypeStruct((n, c), BF16),
        grid=(bsz, nt),
        in_specs=[pl.BlockSpec((bt, c), lambda b, i: (b * nt + i, x_col)),
                  pl.BlockSpec((bt, c), lambda b, i: (b * nt + i, gate_col)),
                  pl.BlockSpec((8, c), lambda b, i: (0, 0)),
                  vspec, wspec, vspec, wspec, vspec, vspec],
        out_specs=pl.BlockSpec((bt, c), lambda b, i: (b * nt + i, 0)),
        scratch_shapes=[pltpu.VMEM((8, c), F32), pltpu.VMEM((8, c), F32)],
        compiler_params=_cparams(("arbitrary", "arbitrary")),
        name="rg_lru",
    )(z, z, jnp.pad(conv_w.astype(F32), ((0, 8 - SHORT_K), (0, 0))), row(conv_b),
      w_rg.astype(BF16), row(b_rg), w_ig.astype(BF16), row(b_ig), row(lam))


def _pick(n, candidates):
    for c in candidates:
        if n % c == 0:
            return c
    raise ValueError(f"no block size in {candidates} divides {n}")


def _pack_weights_body(a_ref, b_ref, o_ref, *, scaled_tiles, scale, head_tiles, gap):
    j = pl.program_id(0)
    tile = o_ref.shape[1]

    @pl.when(j < scaled_tiles)
    def _():
        o_ref[...] = (a_ref[0] * scale).astype(BF16)

    @pl.when((j >= scaled_tiles) & (j < head_tiles))
    def _():
        o_ref[...] = a_ref[0].astype(BF16)

    @pl.when(j >= head_tiles)
    def _():
        x = jnp.concatenate([a_ref[0], b_ref[0]], axis=1)
        o_ref[...] = x[:, gap:gap + tile].astype(BF16)


def _pack_inproj_weights(w_all, idx, head, gap, scaled=0, scale=1.0):
    _, d, total = w_all.shape
    tile = 512
    n_out = total - gap
    return pl.pallas_call(
        functools.partial(_pack_weights_body, scaled_tiles=scaled // tile, scale=scale,
                          head_tiles=head // tile, gap=gap),
        out_shape=jax.ShapeDtypeStruct((d, n_out), BF16),
        grid=(n_out // tile,),
        in_specs=[pl.BlockSpec((1, d, tile), lambda j: (idx, 0, j)),
                  pl.BlockSpec((1, d, LANES), lambda j: (idx, 0, (tile // LANES) * (j + 1)))],
        out_specs=pl.BlockSpec((d, tile), lambda j: (0, j)),
        compiler_params=_cparams(("arbitrary",)),
        name="pack_inproj_weights",
    )(w_all, w_all)


def _small_weights(cols):
    k = cols.shape[1]
    return (jnp.pad(cols.T, ((0, 16 - k), (0, 0))).astype(BF16),
            jnp.pad(cols, ((0, 0), (0, LANES - k))).astype(BF16))


def _router_weights(w_group, b_group, w_expert, b_expert):
    w = jnp.concatenate([w_expert, w_group], axis=1).astype(F32)
    b = jnp.concatenate([b_expert, b_group]).astype(F32)
    k = w.shape[1]
    w = jnp.pad(w, ((0, 0), (0, LANES - k)))
    w_hi = w.astype(BF16)
    w_lo = (w - w_hi.astype(F32)).astype(BF16)
    return jnp.concatenate([w_hi, w_lo], axis=1), jnp.pad(b, (0, LANES - k)).reshape(1, LANES)


def kernel(x, meta_tokens, norm_mix_g, norm_ffn_g, norm_final_g, ab_w_in, ab_forget_b, ab_conv_w, ab_conv_b, ab_ln_g, ab_ln_b, ab_w_out, cd_w_in, cd_qkv_conv_w, cd_a_log, cd_dt_bias, cd_out_norm_g, cd_lru_conv_w, cd_lru_conv_b, cd_w_rg, cd_b_rg, cd_w_ig, cd_b_ig, cd_lru_lambda, cd_w_out, moe_w_group, moe_b_group, moe_w_expert, moe_b_expert, moe_w_gate, moe_w_up, moe_w_down):
    bsz, seq, d = x.shape
    depth = norm_mix_g.shape[0]
    n_real = N_META + seq
    tp = -(-(OFF + seq) // UNIT) * UNIT
    n = bsz * tp
    hw = N_HEADS * HEAD_DIM
    bm = _pick(n, (768, 512, 256))
    bn = 1024

    h = _embed(x.astype(F32).reshape(bsz * seq, d), meta_tokens.astype(F32), bsz, seq, tp)
    row = lambda v: v.astype(F32).reshape(1, -1)

    for layer in range(depth):
        i = layer // 2
        g_mix = row(norm_mix_g[layer])
        if layer % 2 == 0:
            w_in = ab_w_in[i]
            w_main = _pack_inproj_weights(ab_w_in.astype(F32), i, 3 * hw, N_HEADS, scaled=hw,
                                          scale=LOG2E * HEAD_DIM ** -0.5)
            wst, wsc = _small_weights(w_in[:, 3 * hw:3 * hw + N_HEADS])
            z, _, f_slab = _norm_inproj(h, g_mix, w_main, wst, wsc, bm, bn)
            ke = _fox_prep(f_slab, ab_forget_b[i], bsz, tp, n_real)
            ya = _fox_attn(z, ke, bsz, tp)
            yb = _conformer(z, ab_conv_w[i], ab_conv_b[i], ab_ln_g[i], ab_ln_b[i], bsz, tp, 3, 4)
            w_out = ab_w_out[i].astype(BF16)
        else:
            w_in = cd_w_in[i]
            w_main = _pack_inproj_weights(cd_w_in.astype(F32), i, 3 * hw, 2 * N_HEADS)
            wst, wsc = _small_weights(w_in[:, 3 * hw:3 * hw + 2 * N_HEADS])
            z, zt, zc = _norm_inproj(h, g_mix, w_main, wst, wsc, bm, bn)
            u, w, qd, kd, attn, eg = _gdn_pre(z, zc, zt, cd_qkv_conv_w[i], cd_a_log[i], cd_dt_bias[i],
                                              bsz, tp)
            ya = _gdn_scan(u, w, qd, kd, attn, eg, z, cd_out_norm_g[i], bsz, tp, 3)
            yb = _lru(z, cd_lru_conv_w[i], cd_lru_conv_b[i], cd_w_rg[i], cd_b_rg[i], cd_w_ig[i],
                      cd_b_ig[i], cd_lru_lambda[i], bsz, tp, n_real, 4, 5)
            w_out = cd_w_out[i].astype(BF16)
        g_ffn = row(norm_ffn_g[layer])
        w_r, b_r = _router_weights(moe_w_group[layer], moe_b_group[layer],
                                   moe_w_expert[layer], moe_b_expert[layer])
        h, route, counts, xp, er = _outproj_router(ya, yb, h, w_out, g_ffn, w_r, b_r, tp, n_real)
        h = _moe(h, xp, route, counts, er, moe_w_gate, moe_w_up, moe_w_down, layer,
                 row(norm_final_g), tp, n_real, final_norm=(layer == depth - 1))
    return h.reshape(bsz, seq, d).astype(x.dtype)
```

```python
import functools

import jax
import jax.numpy as jnp
from jax import lax
from jax.experimental import pallas as pl
from jax.experimental.pallas import tpu as pltpu

F32 = jnp.float32
BF16 = jnp.bfloat16
I32 = jnp.int32
U32 = jnp.uint32

EPS = 1e-6
N_META = 16
CHUNK = 64
FRONT = CHUNK - N_META
OFF = FRONT + N_META
LANES = 128
UNIT = 256
HEAD_DIM = 128
N_HEADS = 8
CONF_K = 31
SHORT_K = 4
LRU_C = 8.0
N_GROUPS = 4
EXPERTS_PER_GROUP = 8
N_EXPERTS = N_GROUPS * EXPERTS_PER_GROUP
MOE_BM = 256
NEG = -1e30
LOG2E = 1.4426950408889634
N_BIAS_PIECES = 3
VMEM_LIMIT = 56 * 1024 * 1024


def _cparams(sem):
    return pltpu.CompilerParams(dimension_semantics=sem, vmem_limit_bytes=VMEM_LIMIT)


def _row_valid(pos0, rows, n_real):
    pos = pos0 + lax.broadcasted_iota(I32, (rows, 1), 0)
    return (pos >= FRONT) & (pos < FRONT + n_real)


def _sigmoid(x):
    return 1.0 / (1.0 + jnp.exp(-x))


def _softplus(x):
    return jnp.maximum(x, 0.0) + jnp.log1p(jnp.exp(-jnp.abs(x)))


def _silu(x):
    return x * _sigmoid(x)


def _embed_body(x_ref, meta_ref, h_hbm, zero_ref, sem, *, seq, tp, rows):
    b = pl.program_id(0)
    i = pl.program_id(1)
    base = b * tp
    tail = tp - OFF - seq
    frames = pltpu.make_async_copy(x_ref, h_hbm.at[pl.ds(base + OFF + i * rows, rows)], sem.at[0])
    frames.start()

    @pl.when(i == 0)
    def _():
        zero_ref[...] = jnp.zeros_like(zero_ref)
        copies = [
            pltpu.make_async_copy(meta_ref, h_hbm.at[pl.ds(base + FRONT, N_META)], sem.at[1]),
            pltpu.make_async_copy(zero_ref.at[pl.ds(0, FRONT)], h_hbm.at[pl.ds(base, FRONT)], sem.at[2]),
        ]
        if tail:
            copies.append(pltpu.make_async_copy(zero_ref.at[pl.ds(0, tail)],
                                                h_hbm.at[pl.ds(base + OFF + seq, tail)], sem.at[3]))
        for c in copies:
            c.start()
        for c in copies:
            c.wait()

    frames.wait()


def _embed(x2d, meta_tokens, bsz, seq, tp):
    d = x2d.shape[1]
    rows = _pick(seq, (1024, 512, 256, 128, 64, 32, 16, 8))
    per_seq = seq // rows
    return pl.pallas_call(
        functools.partial(_embed_body, seq=seq, tp=tp, rows=rows),
        out_shape=jax.ShapeDtypeStruct((bsz * tp, d), F32),
        grid=(bsz, per_seq),
        in_specs=[pl.BlockSpec((rows, d), lambda b, i: (b * per_seq + i, 0)),
                  pl.BlockSpec((N_META, d), lambda b, i: (0, 0))],
        out_specs=pl.BlockSpec(memory_space=pl.ANY),
        scratch_shapes=[pltpu.VMEM((max(FRONT, tp - OFF - seq), d), F32), pltpu.SemaphoreType.DMA((4,))],
        compiler_params=_cparams(("arbitrary", "arbitrary")),
        name="embed",
    )(x2d, meta_tokens)


def _norm_inproj_body(h_ref, g_ref, w_ref, wst_ref, wsc_ref, o_ref, ot_ref, oc_ref, xn_ref):
    @pl.when(pl.program_id(1) == 0)
    def _():
        x = h_ref[...]
        ms = jnp.mean(x * x, axis=-1, keepdims=True)
        xn = (x * lax.rsqrt(ms + EPS) * g_ref[...]).astype(BF16)
        xn_ref[...] = xn
        ot_ref[...] = lax.dot_general(wst_ref[...], xn, (((1,), (1,)), ((), ())),
                                      preferred_element_type=F32)
        oc_ref[...] = jnp.dot(xn, wsc_ref[...], preferred_element_type=F32)

    o_ref[...] = jnp.dot(xn_ref[...], w_ref[...], preferred_element_type=F32).astype(o_ref.dtype)


def _norm_inproj(h, g, w_main, w_small_t, w_small_c, bm, bn):
    n, d = h.shape
    nw = w_main.shape[1]
    return pl.pallas_call(
        _norm_inproj_body,
        out_shape=[jax.ShapeDtypeStruct((n, nw), BF16),
                   jax.ShapeDtypeStruct((16, n), F32),
                   jax.ShapeDtypeStruct((n, LANES), F32)],
        grid=(n // bm, nw // bn),
        in_specs=[pl.BlockSpec((bm, d), lambda i, j: (i, 0)),
                  pl.BlockSpec((1, d), lambda i, j: (0, 0)),
                  pl.BlockSpec((d, bn), lambda i, j: (0, j)),
                  pl.BlockSpec((16, d), lambda i, j: (0, 0)),
                  pl.BlockSpec((d, LANES), lambda i, j: (0, 0))],
        out_specs=[pl.BlockSpec((bm, bn), lambda i, j: (i, j)),
                   pl.BlockSpec((16, bm), lambda i, j: (0, i)),
                   pl.BlockSpec((bm, LANES), lambda i, j: (i, 0))],
        scratch_shapes=[pltpu.VMEM((bm, d), BF16)],
        compiler_params=_cparams(("arbitrary", "arbitrary")),
        name="norm_inproj",
    )(h, g, w_main, w_small_t, w_small_c)


def _split3(x):
    p1 = x.astype(BF16)
    r1 = x - p1.astype(F32)
    p2 = r1.astype(BF16)
    p3 = (r1 - p2.astype(F32)).astype(BF16)
    return p1, p2, p3


def _fox_prep_body(f_ref, fb_ref, ke_ref, carry_ref, *, rows, n_real):
    i = pl.program_id(1)

    @pl.when(i == 0)
    def _():
        carry_ref[...] = jnp.zeros_like(carry_ref)

    f = f_ref[...] + fb_ref[...]
    lf = jnp.minimum(f, 0.0) - jnp.log1p(jnp.exp(-jnp.abs(f)))
    pos = i * rows + lax.broadcasted_iota(I32, (rows, 1), 0)
    lf = jnp.where((pos >= FRONT) & (pos < FRONT + n_real), lf, 0.0)
    rr = lax.broadcasted_iota(I32, (rows, rows), 0)
    cc = lax.broadcasted_iota(I32, (rows, rows), 1)
    tri = jnp.where(rr >= cc, 1.0, 0.0).astype(BF16)
    c = carry_ref[0:1, :]
    for piece in _split3(lf):
        c = c + jnp.dot(tri, piece, preferred_element_type=F32)
    carry_ref[...] = jnp.broadcast_to(c[rows - 1:rows, :], carry_ref.shape)
    cs = jnp.where(pos < FRONT, NEG, -LOG2E * c)
    p1, p2, p3 = (p.astype(F32) for p in _split3(cs))
    lane = lax.broadcasted_iota(I32, (rows, LANES), 1)
    for h in range(N_HEADS):
        ext = jnp.where(lane == 0, p1[:, h:h + 1],
                        jnp.where(lane == 1, p2[:, h:h + 1], jnp.where(lane == 2, p3[:, h:h + 1], 0.0)))
        ke_ref[:, h * HEAD_DIM:(h + 1) * HEAD_DIM] = ext.astype(BF16)


def _fox_prep(f_slab, forget_b, bsz, tp, n_real):
    n = f_slab.shape[0]
    hw = N_HEADS * HEAD_DIM
    rows = UNIT
    nt = tp // rows
    fb = jnp.pad(forget_b.astype(F32), (0, LANES - N_HEADS)).reshape(1, LANES)
    return pl.pallas_call(
        functools.partial(_fox_prep_body, rows=rows, n_real=n_real),
        out_shape=jax.ShapeDtypeStruct((n, hw), BF16),
        grid=(bsz, nt),
        in_specs=[pl.BlockSpec((rows, LANES), lambda b, i: (b * nt + i, 0)),
                  pl.BlockSpec((1, LANES), lambda b, i: (0, 0))],
        out_specs=pl.BlockSpec((rows, hw), lambda b, i: (b * nt + i, 0)),
        scratch_shapes=[pltpu.VMEM((8, LANES), F32)],
        compiler_params=_cparams(("arbitrary", "arbitrary")),
        name="fox_prep",
    )(f_slab, fb)


def _fox_attn_body(q_ref, k_ref, ke_ref, v_ref, ve_ref, o_ref, *, blk, n_split, group):
    qi = pl.program_id(2)
    half = blk // n_split
    lane = lax.broadcasted_iota(I32, (blk, HEAD_DIM), 1)
    q = jnp.concatenate([q_ref[...], jnp.where(lane < N_BIAS_PIECES, 1.0, 0.0).astype(BF16)], axis=1)
    qs = tuple(q[r * half:(r + 1) * half] for r in range(n_split))

    def step(js, carry, diag_last):
        starts = [pl.multiple_of(j * blk, blk) for j in js]
        ss = []
        for r, qh in enumerate(qs):
            srow = []
            for b, st in enumerate(starts):
                on_diag = diag_last and b == len(js) - 1
                ncol = (r + 1) * half if on_diag else blk
                k = jnp.concatenate([k_ref[pl.ds(st, ncol), :], ke_ref[pl.ds(st, ncol), :]], axis=1)
                s = lax.dot_general(qh, k, (((1,), (1,)), ((), ())), preferred_element_type=F32)
                if on_diag:
                    row = r * half + lax.broadcasted_iota(I32, (half, ncol), 0)
                    col = lax.broadcasted_iota(I32, (half, ncol), 1)
                    s = jnp.where(col <= row, s, NEG)
                srow.append((s, st, ncol))
            ss.append(srow)
        out = []
        for r in range(n_split):
            m = carry[2 * r]
            m_new = m
            for s, _, _ in ss[r]:
                m_new = jnp.maximum(m_new, jnp.max(s, axis=1, keepdims=True))
            acc = jnp.exp2(m - m_new) * carry[2 * r + 1]
            for s, st, ncol in ss[r]:
                v = jnp.concatenate([v_ref[pl.ds(st, ncol), :], ve_ref[0:ncol, :]], axis=1)
                acc = acc + jnp.dot(jnp.exp2(s - m_new).astype(BF16), v, preferred_element_type=F32)
            out += [m_new, acc]
        return tuple(out)

    init = (jnp.full((half, 1), NEG, F32), jnp.zeros((half, 2 * HEAD_DIM), F32)) * n_split
    carry = lax.fori_loop(0, qi // group,
                          lambda j, c: step(tuple(group * j + g for g in range(group)), c, False), init)
    tails = [functools.partial(lambda r, c: step(tuple(qi - r + g for g in range(r + 1)), c, True), r)
             for r in range(group)]
    res = lax.switch(qi % group, tails, carry)
    for r in range(n_split):
        acc = res[2 * r + 1]
        o_ref[r * half:(r + 1) * half, :] = (acc[:, 0:HEAD_DIM] / acc[:, HEAD_DIM:HEAD_DIM + 1]
                                             ).astype(o_ref.dtype)


def _fox_attn(z, ke, bsz, tp):
    n = z.shape[0]
    blk = _pick(tp, (768, 512, 256))
    nq = tp // blk
    v_ext = jnp.zeros((blk, HEAD_DIM), BF16).at[:, 0].set(1.0)
    return pl.pallas_call(
        functools.partial(_fox_attn_body, blk=blk, n_split=2, group=4),
        out_shape=jax.ShapeDtypeStruct((n, N_HEADS * HEAD_DIM), BF16),
        grid=(bsz, N_HEADS, nq),
        in_specs=[pl.BlockSpec((blk, HEAD_DIM), lambda b, h, i: (b * nq + i, h)),
                  pl.BlockSpec((tp, HEAD_DIM), lambda b, h, i: (b, N_HEADS + h)),
                  pl.BlockSpec((tp, HEAD_DIM), lambda b, h, i: (b, h)),
                  pl.BlockSpec((tp, HEAD_DIM), lambda b, h, i: (b, 2 * N_HEADS + h)),
                  pl.BlockSpec((blk, HEAD_DIM), lambda b, h, i: (0, 0))],
        out_specs=pl.BlockSpec((blk, HEAD_DIM), lambda b, h, i: (b * nq + i, h)),
        compiler_params=_cparams(("arbitrary", "arbitrary", "arbitrary")),
        name="fox_attn",
    )(z, z, ke, z, v_ext)


def _shifted_taps(x_ext, halo, rows, n_taps):
    for r in range(min(8, n_taps)):
        rolled = x_ext if r == 0 else pltpu.roll(x_ext, r, 0)
        for q in range(halo // 8):
            s = 8 * q + r
            if s > n_taps - 1:
                continue
            yield n_taps - 1 - s, rolled[halo - 8 * q:halo - 8 * q + rows]


def _conformer_body(a_ref, b_ref, w_ref, cb_ref, lg_ref, lb_ref, o_ref, halo_ref, *, bt):
    halo = halo_ref.shape[0]

    @pl.when(pl.program_id(1) == 0)
    def _():
        halo_ref[...] = jnp.zeros_like(halo_ref)

    u = a_ref[...].astype(F32) * _sigmoid(b_ref[...].astype(F32))
    x_ext = jnp.concatenate([halo_ref[...], u], axis=0)
    halo_ref[...] = u[bt - halo:bt]
    acc = jnp.zeros_like(u)
    for j, tap in _shifted_taps(x_ext, halo, bt, CONF_K):
        acc = acc + w_ref[j:j + 1, :] * tap
    y = acc + cb_ref[...]
    mu = jnp.mean(y, axis=-1, keepdims=True)
    yc = y - mu
    var = jnp.mean(yc * yc, axis=-1, keepdims=True)
    yn = yc * lax.rsqrt(var + EPS) * lg_ref[...] + lb_ref[...]
    o_ref[...] = _silu(yn).astype(o_ref.dtype)


def _conformer(z, conv_w, conv_b, ln_g, ln_b, bsz, tp, col_a, col_b):
    n = z.shape[0]
    c = conv_w.shape[1]
    bt = UNIT
    nt = tp // bt
    w = jnp.pad(conv_w.astype(F32), ((0, 32 - CONF_K), (0, 0)))
    row = lambda v: v.astype(F32).reshape(1, c)
    return pl.pallas_call(
        functools.partial(_conformer_body, bt=bt),
        out_shape=jax.ShapeDtypeStruct((n, c), BF16),
        grid=(bsz, nt),
        in_specs=[pl.BlockSpec((bt, c), lambda b, i: (b * nt + i, col_a)),
                  pl.BlockSpec((bt, c), lambda b, i: (b * nt + i, col_b)),
                  pl.BlockSpec((32, c), lambda b, i: (0, 0)),
                  pl.BlockSpec((1, c), lambda b, i: (0, 0)),
                  pl.BlockSpec((1, c), lambda b, i: (0, 0)),
                  pl.BlockSpec((1, c), lambda b, i: (0, 0))],
        out_specs=pl.BlockSpec((bt, c), lambda b, i: (b * nt + i, 0)),
        scratch_shapes=[pltpu.VMEM((32, c), F32)],
        compiler_params=_cparams(("arbitrary", "arbitrary")),
        name="conformer_conv",
    )(z, z, w, row(conv_b), row(ln_g), row(ln_b))


def _outproj_router_body(ya_ref, yb_ref, h_ref, w_ref, g_ref, wr_ref, br_ref, ho_ref, r_ref, cnt_ref,
                         xp_ref, er_ref, *, bm, blocks_per_seq, n_real):
    kh = ya_ref.shape[1]
    pos0 = (pl.program_id(0) % blocks_per_seq) * bm
    rh = bm // 2
    parts = [slice(r * rh, (r + 1) * rh) for r in range(2)]
    ys = [jnp.dot(ya_ref[p, :], w_ref[0:kh, :], preferred_element_type=F32)
          + jnp.dot(yb_ref[p, :], w_ref[kh:2 * kh, :], preferred_element_type=F32) for p in parts]
    lgs = []
    for r, (p, y) in enumerate(zip(parts, ys)):
        hn = jnp.where(_row_valid(pos0 + r * rh, rh, n_real), h_ref[p, :] + y, 0.0)
        ho_ref[p, :] = hn
        ms = jnp.mean(hn * hn, axis=-1, keepdims=True)
        xn = hn * lax.rsqrt(ms + EPS) * g_ref[...]
        xp_ref[p, :] = _pack_pairs(xn)
        xh = xn.astype(BF16)
        xl = (xn - xh.astype(F32)).astype(BF16)
        t = jnp.dot(xh, wr_ref[...], preferred_element_type=F32)
        lgs.append(t[:, 0:LANES] + t[:, LANES:2 * LANES]
                   + jnp.dot(xl, wr_ref[:, 0:LANES], preferred_element_type=F32) + br_ref[...])
    lg = jnp.concatenate(lgs, axis=0)
    lane = lax.broadcasted_iota(I32, lg.shape, 1)
    lane_f = lane.astype(F32)
    big = float(LANES)

    is_grp = (lane >= N_EXPERTS) & (lane < N_EXPERTS + N_GROUPS)
    gl = jnp.where(is_grp, lg, -jnp.inf)
    gmax = jnp.max(gl, axis=1, keepdims=True)
    gidx = jnp.min(jnp.where(gl == gmax, lane_f, big), axis=1, keepdims=True) - N_EXPERTS
    g_prob = 1.0 / jnp.sum(jnp.where(is_grp, jnp.exp(lg - gmax), 0.0), axis=1, keepdims=True)

    lo = gidx * EXPERTS_PER_GROUP
    in_grp = (lane_f >= lo) & (lane_f < lo + EXPERTS_PER_GROUP)
    el = jnp.where(in_grp, lg, -jnp.inf)
    m1 = jnp.max(el, axis=1, keepdims=True)
    i1 = jnp.min(jnp.where(el == m1, lane_f, big), axis=1, keepdims=True)
    el2 = jnp.where(lane_f == i1, -jnp.inf, el)
    m2 = jnp.max(el2, axis=1, keepdims=True)
    i2 = jnp.min(jnp.where(el2 == m2, lane_f, big), axis=1, keepdims=True)
    e2 = jnp.exp(m2 - m1)
    w1 = g_prob / (1.0 + e2)
    w2 = g_prob * e2 / (1.0 + e2)

    @pl.when(pl.program_id(0) == 0)
    def _():
        cnt_ref[...] = jnp.zeros_like(cnt_ref)

    oh1 = lane_f == i1
    oh2 = lane_f == i2
    onehots = jnp.concatenate([jnp.where(oh1, 1.0, 0.0), jnp.where(oh2, 1.0, 0.0)], axis=1).astype(BF16)
    rr = lax.broadcasted_iota(I32, (bm, bm), 0)
    cc = lax.broadcasted_iota(I32, (bm, bm), 1)
    tri = jnp.where(rr >= cc, 1.0, 0.0).astype(BF16)
    csum = jnp.dot(tri, onehots, preferred_element_type=F32)
    c1 = csum[:, 0:LANES]
    c2 = csum[:, LANES:2 * LANES]
    tot1 = c1[bm - 1:bm, :]
    tot2 = c2[bm - 1:bm, :]
    before = cnt_ref[0:1, :]
    rank1 = jnp.sum(jnp.where(oh1, before + c1 - 1.0, 0.0), axis=1, keepdims=True)
    rank2 = jnp.sum(jnp.where(oh2, before + tot1 + c2 - 1.0, 0.0), axis=1, keepdims=True)
    cnt_ref[...] = jnp.broadcast_to(before + tot1 + tot2, cnt_ref.shape)

    r_ref[...] = jnp.where(lane == 0, i1, jnp.where(lane == 1, i2,
                           jnp.where(lane == 2, w1, jnp.where(lane == 3, w2, 0.0))))
    eye = rr == cc
    er_ref[0] = jnp.concatenate(
        [jnp.sum(jnp.where(eye, col, 0.0), axis=0, keepdims=True) for col in (i1, i2, rank1, rank2)], axis=0)


def _outproj_router(ya, yb, h, w_out, g_ffn, w_router, b_router, tp, n_real):
    n, d = h.shape
    half = ya.shape[1]
    bm = UNIT
    return pl.pallas_call(
        functools.partial(_outproj_router_body, bm=bm, blocks_per_seq=tp // bm, n_real=n_real),
        out_shape=[jax.ShapeDtypeStruct((n, d), F32), jax.ShapeDtypeStruct((n, LANES), F32),
                   jax.ShapeDtypeStruct((8, LANES), F32), jax.ShapeDtypeStruct((n, d // 2), U32),
                   jax.ShapeDtypeStruct((n // bm, 4, bm), F32)],
        grid=(n // bm,),
        in_specs=[pl.BlockSpec((bm, half), lambda i: (i, 0)),
                  pl.BlockSpec((bm, half), lambda i: (i, 0)),
                  pl.BlockSpec((bm, d), lambda i: (i, 0)),
                  pl.BlockSpec((2 * half, d), lambda i: (0, 0)),
                  pl.BlockSpec((1, d), lambda i: (0, 0)),
                  pl.BlockSpec((d, 2 * LANES), lambda i: (0, 0)),
                  pl.BlockSpec((1, LANES), lambda i: (0, 0))],
        out_specs=[pl.BlockSpec((bm, d), lambda i: (i, 0)),
                   pl.BlockSpec((bm, LANES), lambda i: (i, 0)),
                   pl.BlockSpec((8, LANES), lambda i: (0, 0)),
                   pl.BlockSpec((bm, d // 2), lambda i: (i, 0)),
                   pl.BlockSpec((1, 4, bm), lambda i: (i, 0, 0))],
        compiler_params=_cparams(("arbitrary",)),
        name="outproj_router",
    )(ya, yb, h, w_out, g_ffn, w_router, b_router)


def _pack_pairs(x):
    c = x.shape[1] // 2
    hi = lax.bitcast_convert_type(x[:, :c].astype(BF16).astype(F32), U32)
    lo = lax.bitcast_convert_type(x[:, c:].astype(BF16).astype(F32), U32)
    return hi | (lo >> 16)


def _unpack_pairs(p):
    hi = lax.bitcast_convert_type(p & jnp.uint32(0xFFFF0000), F32)
    lo = lax.bitcast_convert_type(p << 16, F32)
    return hi, lo


def _dispatch_body(meta_ref, dest_ref, xin_ref, xs_hbm, xp_ref, zero_ref, sem, zsem, *, tb):
    i = pl.program_id(0)
    slot = i % 2

    def drain(s):
        for _ in range(2):
            pltpu.make_async_copy(zero_ref, xs_hbm.at[pl.ds(0, tb)], sem.at[s]).wait()

    for g in range(tb // 8):
        xp_ref[slot, g] = xin_ref[g * 8:(g + 1) * 8, :]

    def issue(g, _):
        for u in range(8):
            for k in range(2):
                d = dest_ref[0, 0, k * tb + g * 8 + u]
                pltpu.make_async_copy(xp_ref.at[slot, g, pl.ds(u, 1)], xs_hbm.at[pl.ds(d, 1)],
                                      sem.at[slot]).start()
        return 0

    lax.fori_loop(0, tb // 8, issue, 0)

    @pl.when(i > 0)
    def _():
        drain(1 - slot)

    @pl.when(i == pl.num_programs(0) - 1)
    def _():
        drain(slot)

    @pl.when(i == 0)
    def _():
        zero_ref[...] = jnp.zeros_like(zero_ref)

        def per_expert(e, _):
            cnt = meta_ref[e]
            start = meta_ref[N_EXPERTS + e]
            padded = meta_ref[2 * N_EXPERTS + e]

            def pad_copy(r):
                return pltpu.make_async_copy(zero_ref.at[pl.ds(0, 1)], xs_hbm.at[pl.ds(start + r, 1)], zsem)

            def zissue(r, _):
                pad_copy(r).start()
                return 0

            def zwait(r, _):
                pad_copy(r).wait()
                return 0

            lax.fori_loop(cnt, padded, zissue, 0)
            lax.fori_loop(cnt, padded, zwait, 0)
            return 0

        lax.fori_loop(0, N_EXPERTS, per_expert, 0)

        def tail_copy(blk):
            return pltpu.make_async_copy(zero_ref, xs_hbm.at[pl.ds(blk * MOE_BM, MOE_BM)], zsem)

        def tissue(blk, _):
            tail_copy(blk).start()
            return 0

        def twait(blk, _):
            tail_copy(blk).wait()
            return 0

        n_used = meta_ref[3 * N_EXPERTS]
        lax.fori_loop(n_used, xs_hbm.shape[0] // MOE_BM, tissue, 0)
        lax.fori_loop(n_used, xs_hbm.shape[0] // MOE_BM, twait, 0)


def _dispatch(xp, dest3, meta, n_rows):
    n, dp = xp.shape
    tb = UNIT
    return pl.pallas_call(
        functools.partial(_dispatch_body, tb=tb),
        out_shape=jax.ShapeDtypeStruct((n_rows, dp), U32),
        grid_spec=pltpu.PrefetchScalarGridSpec(
            num_scalar_prefetch=1,
            grid=(n // tb,),
            in_specs=[pl.BlockSpec((1, 1, 2 * tb), lambda i, m: (i, 0, 0), memory_space=pltpu.SMEM),
                      pl.BlockSpec((tb, dp), lambda i, m: (i, 0))],
            out_specs=pl.BlockSpec(memory_space=pl.ANY),
            scratch_shapes=[pltpu.VMEM((2, tb // 8, 8, dp), U32), pltpu.VMEM((MOE_BM, dp), U32),
                            pltpu.SemaphoreType.DMA((2,)), pltpu.SemaphoreType.DMA(())]),
        compiler_params=_cparams(("arbitrary",)),
        name="moe_dispatch",
    )(meta, dest3, xp)


def _experts_body(be_ref, nxt_ref, nu_ref, xs_ref, wg_hbm, wu_hbm, wd_hbm, ys_ref,
                  wg_st, wu_st, wd_st, wg_bf, wu_bf, wd_bf, ord_ref, sem, *, layer):
    i = pl.program_id(0)
    n_used = nu_ref[0]

    def weight_copies(e, s):
        return (pltpu.make_async_copy(wg_hbm.at[layer, e], wg_st.at[s], sem.at[s, 0]),
                pltpu.make_async_copy(wu_hbm.at[layer, e], wu_st.at[s], sem.at[s, 1]),
                pltpu.make_async_copy(wd_hbm.at[layer, e], wd_st.at[s], sem.at[s, 2]))

    @pl.when(i == 0)
    def _():
        ord_ref[0] = 0
        for c in weight_copies(be_ref[0], 0):
            c.start()

    @pl.when(i < n_used)
    def _():
        e = be_ref[i]
        prev = be_ref[jnp.maximum(i - 1, 0)]

        @pl.when((i == 0) | (e != prev))
        def _():
            s = ord_ref[0] % 2
            for c in weight_copies(e, s):
                c.wait()
            nxt = nxt_ref[i]

            @pl.when(nxt != e)
            def _():
                for c in weight_copies(nxt, 1 - s):
                    c.start()

            wg_bf[...] = wg_st[s].astype(BF16)
            wu_bf[...] = wu_st[s].astype(BF16)
            wd_bf[...] = wd_st[s].astype(BF16)
            ord_ref[0] = ord_ref[0] + 1

        half = MOE_BM // 2
        kh = wg_bf.shape[0] // 2
        xns = []
        for r in range(2):
            hi, lo = _unpack_pairs(xs_ref[r * half:(r + 1) * half, :])
            xns.append((hi.astype(BF16), lo.astype(BF16)))

        def up(w_bf):
            return [jnp.dot(xa, w_bf[0:kh, :], preferred_element_type=F32)
                    + jnp.dot(xb, w_bf[kh:2 * kh, :], preferred_element_type=F32) for xa, xb in xns]

        hgs = up(wg_bf)
        hus = up(wu_bf)
        hids = [(_silu(hg) * hu).astype(BF16) for hg, hu in zip(hgs, hus)]
        for r in range(2):
            ys_ref[r * half:(r + 1) * half, :] = _pack_pairs(
                jnp.dot(hids[r], wd_bf[...], preferred_element_type=F32))

    @pl.when(i >= n_used)
    def _():
        ys_ref[...] = jnp.zeros_like(ys_ref)


def _experts(xs, w_gate, w_up, w_down, layer, block_expert, next_expert, n_used):
    r, dp = xs.shape
    d = 2 * dp
    de = w_gate.shape[3]
    nb = r // MOE_BM
    return pl.pallas_call(
        functools.partial(_experts_body, layer=layer),
        out_shape=jax.ShapeDtypeStruct((r, dp), U32),
        grid_spec=pltpu.PrefetchScalarGridSpec(
            num_scalar_prefetch=3,
            grid=(nb,),
            in_specs=[pl.BlockSpec((MOE_BM, dp), lambda i, be, nx, nu: (i, 0)),
                      pl.BlockSpec(memory_space=pl.ANY),
                      pl.BlockSpec(memory_space=pl.ANY),
                      pl.BlockSpec(memory_space=pl.ANY)],
            out_specs=pl.BlockSpec((MOE_BM, dp), lambda i, be, nx, nu: (i, 0)),
            scratch_shapes=[pltpu.VMEM((2, d, de), F32), pltpu.VMEM((2, d, de), F32),
                            pltpu.VMEM((2, de, d), F32),
                            pltpu.VMEM((d, de), BF16), pltpu.VMEM((d, de), BF16), pltpu.VMEM((de, d), BF16),
                            pltpu.SMEM((1,), I32), pltpu.SemaphoreType.DMA((2, 3))]),
        compiler_params=_cparams(("arbitrary",)),
        name="moe_experts",
    )(block_expert, next_expert, n_used, xs, w_gate, w_up, w_down)


def _combine_body(dest_ref, dnext_ref, h_ref, r_ref, gf_ref, ys_hbm, o_ref, buf, sem, *out_stage,
                  tb, blocks_per_seq, n_real, final_norm):
    i = pl.program_id(0)
    slot = i % 2

    def gather(d_ref, s):
        def issue(t, _):
            for k in range(2):
                d = d_ref[0, 0, k * tb + t]
                pltpu.make_async_copy(ys_hbm.at[pl.ds(d, 1)], buf.at[s, k, pl.ds(t, 1)], sem.at[s]).start()
            return 0

        lax.fori_loop(0, tb, issue, 0, unroll=8)

    @pl.when(i == 0)
    def _():
        gather(dest_ref, 0)

    @pl.when(i + 1 < pl.num_programs(0))
    def _():
        gather(dnext_ref, 1 - slot)

    for k in range(2):
        pltpu.make_async_copy(ys_hbm.at[pl.ds(0, tb)], buf.at[slot, k], sem.at[slot]).wait()

    r = r_ref[...]
    w1 = r[:, 2:3]
    w2 = r[:, 3:4]
    y1 = _unpack_pairs(buf[slot, 0])
    y2 = _unpack_pairs(buf[slot, 1])
    y = jnp.concatenate([w1 * y1[0] + w2 * y2[0], w1 * y1[1] + w2 * y2[1]], axis=1)
    pos0 = (i % blocks_per_seq) * tb
    hn = jnp.where(_row_valid(pos0, tb, n_real), h_ref[...] + y, 0.0)
    if not final_norm:
        o_ref[...] = hn
        return

    obuf, osem = out_stage
    seq = n_real - N_META
    ms = jnp.mean(hn * hn, axis=-1, keepdims=True)
    obuf[slot] = hn * lax.rsqrt(ms + EPS) * gf_ref[...]

    def frame_copy(step, s, start):
        b = step // blocks_per_seq
        j = step % blocks_per_seq
        row0 = b * seq + j * tb - OFF
        tail = OFF + seq - (blocks_per_seq - 1) * tb

        def go(src, dst):
            cp = pltpu.make_async_copy(src, dst, osem.at[s])
            cp.start() if start else cp.wait()

        @pl.when(j == 0)
        def _():
            go(obuf.at[s, pl.ds(OFF, tb - OFF)], o_ref.at[pl.ds(b * seq, tb - OFF)])

        @pl.when((j > 0) & (j < blocks_per_seq - 1))
        def _():
            go(obuf.at[s], o_ref.at[pl.ds(row0, tb)])

        @pl.when(j == blocks_per_seq - 1)
        def _():
            go(obuf.at[s, pl.ds(0, tail)], o_ref.at[pl.ds(row0, tail)])

    frame_copy(i, slot, True)

    @pl.when(i > 0)
    def _():
        frame_copy(i - 1, 1 - slot, False)

    @pl.when(i == pl.num_programs(0) - 1)
    def _():
        frame_copy(i, slot, False)


def _combine(h, route, dest3, ys, g_final, tp, n_real, final_norm):
    n, d = h.shape
    tb = UNIT
    last = n // tb - 1
    assert tp // tb >= 2
    if final_norm:
        out_shape = jax.ShapeDtypeStruct((n // tp * (n_real - N_META), d), F32)
        out_spec = pl.BlockSpec(memory_space=pl.ANY)
        out_stage = [pltpu.VMEM((2, tb, d), F32), pltpu.SemaphoreType.DMA((2,))]
    else:
        out_shape = jax.ShapeDtypeStruct((n, d), F32)
        out_spec = pl.BlockSpec((tb, d), lambda i: (i, 0))
        out_stage = []
    return pl.pallas_call(
        functools.partial(_combine_body, tb=tb, blocks_per_seq=tp // tb, n_real=n_real,
                          final_norm=final_norm),
        out_shape=out_shape,
        grid=(n // tb,),
        in_specs=[pl.BlockSpec((1, 1, 2 * tb), lambda i: (i, 0, 0), memory_space=pltpu.SMEM),
                  pl.BlockSpec((1, 1, 2 * tb), lambda i: (jnp.minimum(i + 1, last), 0, 0),
                               memory_space=pltpu.SMEM),
                  pl.BlockSpec((tb, d), lambda i: (i, 0)),
                  pl.BlockSpec((tb, LANES), lambda i: (i, 0)),
                  pl.BlockSpec((1, d), lambda i: (0, 0)),
                  pl.BlockSpec(memory_space=pl.ANY)],
        out_specs=out_spec,
        scratch_shapes=[pltpu.VMEM((2, 2, tb, d // 2), U32), pltpu.SemaphoreType.DMA((2,))] + out_stage,
        compiler_params=_cparams(("arbitrary",)),
        name="moe_combine",
    )(dest3, dest3, h, route, g_final, ys)


def _moe(h, xp, route, counts_f, er, w_gate, w_up, w_down, layer, g_final, tp, n_real, final_norm):
    n, d = h.shape
    n_pairs = 2 * n
    counts = counts_f[0, 0:N_EXPERTS].astype(I32)
    padded = (counts + MOE_BM - 1) // MOE_BM * MOE_BM
    p_ends = jnp.cumsum(padded)
    p_starts = p_ends - padded
    e_t = er[:, 0:2, :].astype(I32)
    dest = jnp.take(p_starts, e_t) + er[:, 2:4, :].astype(I32)
    n_blocks = -(-(n_pairs + N_EXPERTS * (MOE_BM - 1)) // MOE_BM)
    block_row0 = jnp.arange(n_blocks, dtype=I32) * MOE_BM
    block_expert = jnp.minimum(jnp.sum((p_ends[None, :] <= block_row0[:, None]).astype(I32), axis=1),
                               N_EXPERTS - 1)
    n_used = (p_ends[-1:] // MOE_BM).astype(I32)
    meta = jnp.concatenate([counts, p_starts, padded, n_used]).astype(I32)
    dest3 = dest.reshape(n // UNIT, 1, 2 * UNIT)

    xs = _dispatch(xp, dest3, meta, n_blocks * MOE_BM)
    ids = jnp.arange(N_EXPERTS, dtype=I32)
    owner = jnp.where(counts > 0, ids, N_EXPERTS)
    first_from = lax.cummin(owner[::-1])[::-1]
    after = jnp.concatenate([first_from[1:], jnp.full((1,), N_EXPERTS, I32)])
    next_expert = jnp.take(jnp.where(after < N_EXPERTS, after, ids), block_expert)
    ys = _experts(xs, w_gate, w_up, w_down, layer, block_expert, next_expert, n_used)
    return _combine(h, route, dest3, ys, g_final, tp, n_real, final_norm)


def _short_conv(x, halo_ref, w_ref, first):
    rows = x.shape[0]

    @pl.when(first)
    def _():
        halo_ref[...] = jnp.zeros_like(halo_ref)

    x_ext = jnp.concatenate([halo_ref[...], x], axis=0)
    halo_ref[...] = x[rows - 8:rows]
    acc = jnp.zeros_like(x)
    for j, tap in _shifted_taps(x_ext, 8, rows, SHORT_K):
        acc = acc + w_ref[j:j + 1, :] * tap
    return acc


def _bdot(a, b):
    return jnp.dot(a.astype(BF16), b.astype(BF16), preferred_element_type=F32)


def _gdn_pre_body(z_ref, sc_ref, st_ref, cw_ref, al_r_ref, dt_r_ref, al_c_ref, dt_c_ref,
                  u_ref, w_ref, qd_ref, kd_ref, at_ref, eg_ref, halo_ref):
    rows = z_ref.shape[0]
    hw = N_HEADS * HEAD_DIM
    x = _silu(_short_conv(z_ref[...].astype(F32), halo_ref, cw_ref, pl.program_id(1) == 0))

    sc = sc_ref[...]
    g_cols = -jnp.exp(al_r_ref[...]) * _softplus(sc + dt_r_ref[...])
    beta_cols = _sigmoid(sc)
    st = st_ref[...]
    g_rows = -jnp.exp(al_c_ref[:, 0:1]) * _softplus(st + dt_c_ref[:, 0:1])

    ri = lax.broadcasted_iota(I32, (rows, rows), 0)
    ci = lax.broadcasted_iota(I32, (rows, rows), 1)
    same64 = (ri >> 6) == (ci >> 6)
    same32 = (ri >> 5) == (ci >> 5)
    same16 = (ri >> 4) == (ci >> 4)
    lower = ri >= ci
    strict = ri > ci
    lane = lax.broadcasted_iota(I32, (rows, LANES), 1)
    eg_slab = jnp.zeros((rows, LANES), F32)
    scale = HEAD_DIM ** -0.5

    heads = range(N_HEADS)
    in_chunk_lower = same64 & lower
    in_chunk_upper = same64 & (ri <= ci)
    a_mats, nmats, rhss = [], [], []
    for h in heads:
        sl = slice(h * HEAD_DIM, (h + 1) * HEAD_DIM)
        q = x[:, sl]
        k = x[:, hw + h * HEAD_DIM:hw + (h + 1) * HEAD_DIM]
        v = x[:, 2 * hw + h * HEAD_DIM:2 * hw + (h + 1) * HEAD_DIM]
        q = q * lax.rsqrt(jnp.sum(q * q, axis=-1, keepdims=True) + EPS)
        k = k * lax.rsqrt(jnp.sum(k * k, axis=-1, keepdims=True) + EPS)
        g_col = g_cols[:, h:h + 1]
        beta = beta_cols[:, N_HEADS + h:N_HEADS + h + 1]
        g_row = g_rows[h:h + 1, :]

        gc_col = jnp.sum(jnp.where(in_chunk_lower, g_row, 0.0), axis=1, keepdims=True)
        gc_row = jnp.sum(jnp.where(in_chunk_upper, g_col, 0.0), axis=0, keepdims=True)
        gtot_col = jnp.sum(jnp.where(same64, g_row, 0.0), axis=1, keepdims=True)
        decay = jnp.exp(jnp.where(in_chunk_lower, gc_col - gc_row, NEG))
        eg = jnp.exp(gc_col)
        ekd = jnp.exp(gtot_col - gc_col)

        kb = k * beta
        kbf = k.astype(BF16)
        kk = lax.dot_general(kb.astype(BF16), kbf, (((1,), (1,)), ((), ())), preferred_element_type=F32)
        qs = q * scale
        qk = lax.dot_general(qs.astype(BF16), kbf, (((1,), (1,)), ((), ())), preferred_element_type=F32)
        attn = qk * decay
        a_mats.append(jnp.where(strict, kk * decay, 0.0))
        rhss.append(jnp.concatenate([v * beta, kb * eg], axis=1))
        qd_ref[:, sl] = (qs * eg).astype(BF16)
        kd_ref[:, sl] = (k * ekd).astype(BF16)
        for c in range(rows // CHUNK):
            cs = slice(c * CHUNK, (c + 1) * CHUNK)
            at_ref[h, cs, :] = attn[cs, cs].astype(BF16)
        eg_slab = jnp.where(lane == h, eg, eg_slab)
    eg_ref[...] = eg_slab

    nmats = [jnp.where(same16, -a, 0.0) for a in a_mats]
    qqs = nmats
    for _ in range(3):
        qqs = [_bdot(qq, qq) for qq in qqs]
        prods = [_bdot(qq, nm) for qq, nm in zip(qqs, nmats)]
        nmats = [nm + qq + pr for nm, qq, pr in zip(nmats, qqs, prods)]
    for level_mask, inner_mask in ((same32, same16), (same64, same32)):
        sel = level_mask & jnp.logical_not(inner_mask)
        offs = [jnp.where(sel, a, 0.0) for a in a_mats]
        bmats = [off + _bdot(nm, off) for nm, off in zip(nmats, offs)]
        prods = [_bdot(bm, nm) for bm, nm in zip(bmats, nmats)]
        nmats = [nm - bm - pr for nm, bm, pr in zip(nmats, bmats, prods)]

    for h in heads:
        sl = slice(h * HEAD_DIM, (h + 1) * HEAD_DIM)
        uw = rhss[h] + _bdot(nmats[h], rhss[h])
        u_ref[:, sl] = uw[:, 0:HEAD_DIM]
        w_ref[:, sl] = uw[:, HEAD_DIM:2 * HEAD_DIM].astype(BF16)


def _gdn_pre(z, sc, st, conv_w, a_log, dt_bias, bsz, tp):
    n = z.shape[0]
    hw = N_HEADS * HEAD_DIM
    rows = UNIT
    nu = tp // rows
    pad_r = lambda v: jnp.pad(v.astype(F32), (0, LANES - N_HEADS)).reshape(1, LANES)
    pad_c = lambda v: jnp.broadcast_to(jnp.pad(v.astype(F32), (0, 16 - N_HEADS))[:, None], (16, LANES))
    outs = pl.pallas_call(
        _gdn_pre_body,
        out_shape=[jax.ShapeDtypeStruct((n, hw), F32),
                   jax.ShapeDtypeStruct((n, hw), BF16),
                   jax.ShapeDtypeStruct((n, hw), BF16),
                   jax.ShapeDtypeStruct((n, hw), BF16),
                   jax.ShapeDtypeStruct((N_HEADS, n, CHUNK), BF16),
                   jax.ShapeDtypeStruct((n, LANES), F32)],
        grid=(bsz, nu),
        in_specs=[pl.BlockSpec((rows, 3 * hw), lambda b, i: (b * nu + i, 0)),
                  pl.BlockSpec((rows, LANES), lambda b, i: (b * nu + i, 0)),
                  pl.BlockSpec((16, rows), lambda b, i: (0, b * nu + i)),
                  pl.BlockSpec((8, 3 * hw), lambda b, i: (0, 0)),
                  pl.BlockSpec((1, LANES), lambda b, i: (0, 0)),
                  pl.BlockSpec((1, LANES), lambda b, i: (0, 0)),
                  pl.BlockSpec((16, LANES), lambda b, i: (0, 0)),
                  pl.BlockSpec((16, LANES), lambda b, i: (0, 0))],
        out_specs=[pl.BlockSpec((rows, hw), lambda b, i: (b * nu + i, 0)),
                   pl.BlockSpec((rows, hw), lambda b, i: (b * nu + i, 0)),
                   pl.BlockSpec((rows, hw), lambda b, i: (b * nu + i, 0)),
                   pl.BlockSpec((rows, hw), lambda b, i: (b * nu + i, 0)),
                   pl.BlockSpec((N_HEADS, rows, CHUNK), lambda b, i: (0, b * nu + i, 0)),
                   pl.BlockSpec((rows, LANES), lambda b, i: (b * nu + i, 0))],
        scratch_shapes=[pltpu.VMEM((8, 3 * hw), F32)],
        compiler_params=_cparams(("arbitrary", "arbitrary")),
        name="gdn_pre",
    )(z, sc, st, jnp.pad(conv_w.astype(F32), ((0, 8 - SHORT_K), (0, 0))),
      pad_r(a_log), pad_r(dt_bias), pad_c(a_log), pad_c(dt_bias))
    return outs


def _gdn_scan_body(u_ref, w_ref, qd_ref, kd_ref, at_ref, eg_ref, z_ref, gn_ref, o_ref, s_ref, *, bsz):
    @pl.when(pl.program_id(0) == 0)
    def _():
        s_ref[...] = jnp.zeros_like(s_ref)

    units = [(b, h, slice(h * HEAD_DIM, (h + 1) * HEAD_DIM)) for b in range(bsz) for h in range(N_HEADS)]
    states = [s_ref[b * N_HEADS + h] for b, h, _ in units]
    rs = [jnp.dot(jnp.concatenate([w_ref[b, :, sl], qd_ref[b, :, sl]], axis=0), s.astype(BF16),
                  preferred_element_type=F32) for (b, h, sl), s in zip(units, states)]
    vbs = [(u_ref[b, :, sl] - r[0:CHUNK]).astype(BF16) for (b, h, sl), r in zip(units, rs)]
    intra = [jnp.dot(at_ref[h, b], vb, preferred_element_type=F32) for (b, h, sl), vb in zip(units, vbs)]
    outer = [lax.dot_general(kd_ref[b, :, sl], vb, (((0,), (0,)), ((), ())), preferred_element_type=F32)
             for (b, h, sl), vb in zip(units, vbs)]
    for (b, h, sl), s, r, a, kv in zip(units, states, rs, intra, outer):
        s_ref[b * N_HEADS + h] = s * eg_ref[b, CHUNK - 1:CHUNK, h:h + 1] + kv
        o = r[CHUNK:2 * CHUNK] + a
        on = o * lax.rsqrt(jnp.mean(o * o, axis=-1, keepdims=True) + EPS) * gn_ref[...]
        o_ref[b, :, sl] = (on * _silu(z_ref[b, :, sl].astype(F32))).astype(o_ref.dtype)


def _gdn_scan(u, w, qd, kd, attn, eg, z, out_norm_g, bsz, tp, z_col):
    hw = N_HEADS * HEAD_DIM
    nc = tp // CHUNK
    v3 = lambda a: a.reshape(bsz, tp, a.shape[-1])
    blk3 = pl.BlockSpec((bsz, CHUNK, hw), lambda c: (0, c, 0))
    return pl.pallas_call(
        functools.partial(_gdn_scan_body, bsz=bsz),
        out_shape=jax.ShapeDtypeStruct((bsz, tp, hw), BF16),
        grid=(nc,),
        in_specs=[blk3, blk3, blk3, blk3,
                  pl.BlockSpec((N_HEADS, bsz, CHUNK, CHUNK), lambda c: (0, 0, c, 0)),
                  pl.BlockSpec((bsz, CHUNK, LANES), lambda c: (0, c, 0)),
                  pl.BlockSpec((bsz, CHUNK, hw), lambda c: (0, c, z_col)),
                  pl.BlockSpec((1, HEAD_DIM), lambda c: (0, 0))],
        out_specs=blk3,
        scratch_shapes=[pltpu.VMEM((bsz * N_HEADS, HEAD_DIM, HEAD_DIM), F32)],
        compiler_params=_cparams(("arbitrary",)),
        name="gdn_scan",
    )(v3(u), v3(w), v3(qd), v3(kd), attn.reshape(N_HEADS, bsz, tp, CHUNK), v3(eg), v3(z),
      out_norm_g.astype(F32).reshape(1, HEAD_DIM)).reshape(bsz * tp, hw)


def _lru_body(x_ref, gate_ref, cw_ref, cb_ref, wr_ref, br_ref, wi_ref, bi_ref, lam_ref, o_ref,
              halo_ref, hc_ref, *, bt, n_real):
    first = pl.program_id(1) == 0

    @pl.when(first)
    def _():
        hc_ref[...] = jnp.zeros_like(hc_ref)

    x = _short_conv(x_ref[...].astype(F32), halo_ref, cw_ref, first) + cb_ref[...]
    nblk = wr_ref.shape[0]
    bd = wr_ref.shape[1]
    xb = x.astype(BF16)
    rg = jnp.concatenate([jnp.dot(xb[:, n * bd:(n + 1) * bd], wr_ref[n], preferred_element_type=F32)
                          for n in range(nblk)], axis=1)
    ig = jnp.concatenate([jnp.dot(xb[:, n * bd:(n + 1) * bd], wi_ref[n], preferred_element_type=F32)
                          for n in range(nblk)], axis=1)
    r = _sigmoid(rg + br_ref[...])
    ig = _sigmoid(ig + bi_ref[...])
    log_a = (-LRU_C * _softplus(-lam_ref[...])) * r
    a = jnp.exp(log_a)
    th = jnp.tanh(log_a)
    b = jnp.sqrt(-2.0 * th / (1.0 - th)) * (ig * x)
    b = jnp.where(_row_valid(pl.program_id(1) * bt, bt, n_real), b, 0.0)

    sub = lax.broadcasted_iota(I32, (bt, 1), 0) & 7
    d = 1
    while d < 8:
        keep = sub >= d
        a_sh = jnp.where(keep, pltpu.roll(a, d, 0), 1.0)
        b_sh = jnp.where(keep, pltpu.roll(b, d, 0), 0.0)
        b = a * b_sh + b
        a = a * a_sh
        d *= 2
    carry = hc_ref[0:1, :]
    tiles = []
    for g in range(bt // 8):
        tiles.append(a[8 * g:8 * g + 8] * carry + b[8 * g:8 * g + 8])
        carry = tiles[-1][7:8, :]
    hc_ref[...] = jnp.broadcast_to(carry, hc_ref.shape)
    hs = jnp.concatenate(tiles, axis=0)
    o_ref[...] = (hs * jax.nn.gelu(gate_ref[...].astype(F32))).astype(o_ref.dtype)


def _lru(z, conv_w, conv_b, w_rg, b_rg, w_ig, b_ig, lam, bsz, tp, n_real, x_col, gate_col):
    n = z.shape[0]
    c = conv_w.shape[1]
    bt = UNIT
    nt = tp // bt
    row = lambda v: v.astype(F32).reshape(1, c)
    wspec = pl.BlockSpec(w_rg.shape, lambda b, i: (0, 0, 0))
    vspec = pl.BlockSpec((1, c), lambda b, i: (0, 0))
    return pl.pallas_call(
        functools.partial(_lru_body, bt=bt, n_real=n_real),
        out_shape=jax.ShapeDtypeStruct((n, c), BF16),
        grid=(bsz, nt),
        in_specs=[pl.BlockSpec((bt, c), lambda b, i: (b * nt + i, x_col)),
                  pl.BlockSpec((bt, c), lambda b, i: (b * nt + i, gate_col)),
                  pl.BlockSpec((8, c), lambda b, i: (0, 0)),
                  vspec, wspec, vspec, wspec, vspec, vspec],
        out_specs=pl.BlockSpec((bt, c), lambda b, i: (b * nt + i, 0)),
        scratch_shapes=[pltpu.VMEM((8, c), F32), pltpu.VMEM((8, c), F32)],
        compiler_params=_cparams(("arbitrary", "arbitrary")),
        name="rg_lru",
    )(z, z, jnp.pad(conv_w.astype(F32), ((0, 8 - SHORT_K), (0, 0))), row(conv_b),
      w_rg.astype(BF16), row(b_rg), w_ig.astype(BF16), row(b_ig), row(lam))


def _pick(n, candidates):
    for c in candidates:
        if n % c == 0:
            return c
    raise ValueError(f"no block size in {candidates} divides {n}")


def _pack_weights_body(a_ref, b_ref, o_ref, oc_ref, ot_ref, *, scaled_tiles, scale, head_tiles, gap):
    j = pl.program_id(0)
    tile = o_ref.shape[1]

    @pl.when(j < scaled_tiles)
    def _():
        o_ref[...] = (a_ref[0] * scale).astype(BF16)

    @pl.when((j >= scaled_tiles) & (j < head_tiles))
    def _():
        o_ref[...] = a_ref[0].astype(BF16)

    @pl.when(j >= head_tiles)
    def _():
        x = jnp.concatenate([a_ref[0], b_ref[0]], axis=1)
        o_ref[...] = x[:, gap:gap + tile].astype(BF16)

    @pl.when(j == head_tiles)
    def _():
        lane = lax.broadcasted_iota(I32, (a_ref.shape[1], LANES), 1)
        narrow = jnp.where(lane < gap, a_ref[0, :, 0:LANES], 0.0)
        oc_ref[...] = narrow.astype(BF16)
        ot_ref[...] = narrow.T[0:ot_ref.shape[0], :].astype(BF16)


def _pack_inproj_weights(w_all, idx, head, gap, scaled=0, scale=1.0):
    _, d, total = w_all.shape
    tile = 512
    n_out = total - gap
    return pl.pallas_call(
        functools.partial(_pack_weights_body, scaled_tiles=scaled // tile, scale=scale,
                          head_tiles=head // tile, gap=gap),
        out_shape=[jax.ShapeDtypeStruct((d, n_out), BF16), jax.ShapeDtypeStruct((d, LANES), BF16),
                   jax.ShapeDtypeStruct((16, d), BF16)],
        grid=(n_out // tile,),
        in_specs=[pl.BlockSpec((1, d, tile), lambda j: (idx, 0, j)),
                  pl.BlockSpec((1, d, LANES), lambda j: (idx, 0, (tile // LANES) * (j + 1)))],
        out_specs=[pl.BlockSpec((d, tile), lambda j: (0, j)),
                   pl.BlockSpec((d, LANES), lambda j: (0, 0)),
                   pl.BlockSpec((16, d), lambda j: (0, 0))],
        compiler_params=_cparams(("arbitrary",)),
        name="pack_inproj_weights",
    )(w_all, w_all)


def _router_weights(w_group, b_group, w_expert, b_expert):
    w = jnp.concatenate([w_expert, w_group], axis=1).astype(F32)
    b = jnp.concatenate([b_expert, b_group]).astype(F32)
    k = w.shape[1]
    w = jnp.pad(w, ((0, 0), (0, LANES - k)))
    w_hi = w.astype(BF16)
    w_lo = (w - w_hi.astype(F32)).astype(BF16)
    return jnp.concatenate([w_hi, w_lo], axis=1), jnp.pad(b, (0, LANES - k)).reshape(1, LANES)


def kernel(x, meta_tokens, norm_mix_g, norm_ffn_g, norm_final_g, ab_w_in, ab_forget_b, ab_conv_w, ab_conv_b, ab_ln_g, ab_ln_b, ab_w_out, cd_w_in, cd_qkv_conv_w, cd_a_log, cd_dt_bias, cd_out_norm_g, cd_lru_conv_w, cd_lru_conv_b, cd_w_rg, cd_b_rg, cd_w_ig, cd_b_ig, cd_lru_lambda, cd_w_out, moe_w_group, moe_b_group, moe_w_expert, moe_b_expert, moe_w_gate, moe_w_up, moe_w_down):
    bsz, seq, d = x.shape
    depth = norm_mix_g.shape[0]
    n_real = N_META + seq
    tp = -(-(OFF + seq) // UNIT) * UNIT
    n = bsz * tp
    hw = N_HEADS * HEAD_DIM
    bm = _pick(n, (768, 512, 256))
    bn = 1024

    h = _embed(x.astype(F32).reshape(bsz * seq, d), meta_tokens.astype(F32), bsz, seq, tp)
    row = lambda v: v.astype(F32).reshape(1, -1)

    for layer in range(depth):
        i = layer // 2
        g_mix = row(norm_mix_g[layer])
        if layer % 2 == 0:
            w_main, wsc, wst = _pack_inproj_weights(ab_w_in.astype(F32), i, 3 * hw, N_HEADS, scaled=hw,
                                                    scale=LOG2E * HEAD_DIM ** -0.5)
            z, _, f_slab = _norm_inproj(h, g_mix, w_main, wst, wsc, bm, bn)
            ke = _fox_prep(f_slab, ab_forget_b[i], bsz, tp, n_real)
            ya = _fox_attn(z, ke, bsz, tp)
            yb = _conformer(z, ab_conv_w[i], ab_conv_b[i], ab_ln_g[i], ab_ln_b[i], bsz, tp, 3, 4)
            w_out = ab_w_out[i].astype(BF16)
        else:
            w_main, wsc, wst = _pack_inproj_weights(cd_w_in.astype(F32), i, 3 * hw, 2 * N_HEADS)
            z, zt, zc = _norm_inproj(h, g_mix, w_main, wst, wsc, bm, bn)
            u, w, qd, kd, attn, eg = _gdn_pre(z, zc, zt, cd_qkv_conv_w[i], cd_a_log[i], cd_dt_bias[i],
                                              bsz, tp)
            ya = _gdn_scan(u, w, qd, kd, attn, eg, z, cd_out_norm_g[i], bsz, tp, 3)
            yb = _lru(z, cd_lru_conv_w[i], cd_lru_conv_b[i], cd_w_rg[i], cd_b_rg[i], cd_w_ig[i],
                      cd_b_ig[i], cd_lru_lambda[i], bsz, tp, n_real, 4, 5)
            w_out = cd_w_out[i].astype(BF16)
        g_ffn = row(norm_ffn_g[layer])
        w_r, b_r = _router_weights(moe_w_group[layer], moe_b_group[layer],
                                   moe_w_expert[layer], moe_b_expert[layer])
        h, route, counts, xp, er = _outproj_router(ya, yb, h, w_out, g_ffn, w_r, b_r, tp, n_real)
        h = _moe(h, xp, route, counts, er, moe_w_gate, moe_w_up, moe_w_down, layer,
                 row(norm_final_g), tp, n_real, final_norm=(layer == depth - 1))
    return h.reshape(bsz, seq, d).astype(x.dtype)
```

```python
import functools

import jax
import jax.numpy as jnp
from jax import lax
from jax.experimental import pallas as pl
from jax.experimental.pallas import tpu as pltpu

F32 = jnp.float32
BF16 = jnp.bfloat16
I32 = jnp.int32
U32 = jnp.uint32

EPS = 1e-6
N_META = 16
CHUNK = 64
FRONT = CHUNK - N_META
OFF = FRONT + N_META
LANES = 128
UNIT = 256
HEAD_DIM = 128
N_HEADS = 8
CONF_K = 31
SHORT_K = 4
LRU_C = 8.0
N_GROUPS = 4
EXPERTS_PER_GROUP = 8
N_EXPERTS = N_GROUPS * EXPERTS_PER_GROUP
MOE_BM = 256
NEG = -1e30
LOG2E = 1.4426950408889634
N_BIAS_PIECES = 3
VMEM_LIMIT = 56 * 1024 * 1024


def _cparams(sem):
    return pltpu.CompilerParams(dimension_semantics=sem, vmem_limit_bytes=VMEM_LIMIT)


def _row_valid(pos0, rows, n_real):
    pos = pos0 + lax.broadcasted_iota(I32, (rows, 1), 0)
    return (pos >= FRONT) & (pos < FRONT + n_real)


def _sigmoid(x):
    return 1.0 / (1.0 + jnp.exp(-x))


def _softplus(x):
    return jnp.maximum(x, 0.0) + jnp.log1p(jnp.exp(-jnp.abs(x)))


def _silu(x):
    return x * _sigmoid(x)


def _embed_body(x_ref, meta_ref, h_hbm, zero_ref, sem, *, seq, tp, rows):
    b = pl.program_id(0)
    i = pl.program_id(1)
    base = b * tp
    tail = tp - OFF - seq
    frames = pltpu.make_async_copy(x_ref, h_hbm.at[pl.ds(base + OFF + i * rows, rows)], sem.at[0])
    frames.start()

    @pl.when(i == 0)
    def _():
        zero_ref[...] = jnp.zeros_like(zero_ref)
        copies = [
            pltpu.make_async_copy(meta_ref, h_hbm.at[pl.ds(base + FRONT, N_META)], sem.at[1]),
            pltpu.make_async_copy(zero_ref.at[pl.ds(0, FRONT)], h_hbm.at[pl.ds(base, FRONT)], sem.at[2]),
        ]
        if tail:
            copies.append(pltpu.make_async_copy(zero_ref.at[pl.ds(0, tail)],
                                                h_hbm.at[pl.ds(base + OFF + seq, tail)], sem.at[3]))
        for c in copies:
            c.start()
        for c in copies:
            c.wait()

    frames.wait()


def _embed(x2d, meta_tokens, bsz, seq, tp):
    d = x2d.shape[1]
    rows = _pick(seq, (1024, 512, 256, 128, 64, 32, 16, 8))
    per_seq = seq // rows
    return pl.pallas_call(
        functools.partial(_embed_body, seq=seq, tp=tp, rows=rows),
        out_shape=jax.ShapeDtypeStruct((bsz * tp, d), F32),
        grid=(bsz, per_seq),
        in_specs=[pl.BlockSpec((rows, d), lambda b, i: (b * per_seq + i, 0)),
                  pl.BlockSpec((N_META, d), lambda b, i: (0, 0))],
        out_specs=pl.BlockSpec(memory_space=pl.ANY),
        scratch_shapes=[pltpu.VMEM((max(FRONT, tp - OFF - seq), d), F32), pltpu.SemaphoreType.DMA((4,))],
        compiler_params=_cparams(("arbitrary", "arbitrary")),
        name="embed",
    )(x2d, meta_tokens)


def _norm_inproj_body(h_ref, g_ref, w_ref, wst_ref, wsc_ref, o_ref, ot_ref, oc_ref, xn_ref):
    @pl.when(pl.program_id(1) == 0)
    def _():
        x = h_ref[...]
        ms = jnp.mean(x * x, axis=-1, keepdims=True)
        xn = (x * lax.rsqrt(ms + EPS) * g_ref[...]).astype(BF16)
        xn_ref[...] = xn
        ot_ref[...] = lax.dot_general(wst_ref[...], xn, (((1,), (1,)), ((), ())),
                                      preferred_element_type=F32)
        oc_ref[...] = jnp.dot(xn, wsc_ref[...], preferred_element_type=F32)

    o_ref[...] = jnp.dot(xn_ref[...], w_ref[...], preferred_element_type=F32).astype(o_ref.dtype)


def _norm_inproj(h, g, w_main, w_small_t, w_small_c, bm, bn):
    n, d = h.shape
    nw = w_main.shape[1]
    return pl.pallas_call(
        _norm_inproj_body,
        out_shape=[jax.ShapeDtypeStruct((n, nw), BF16),
                   jax.ShapeDtypeStruct((16, n), F32),
                   jax.ShapeDtypeStruct((n, LANES), F32)],
        grid=(n // bm, nw // bn),
        in_specs=[pl.BlockSpec((bm, d), lambda i, j: (i, 0)),
                  pl.BlockSpec((1, d), lambda i, j: (0, 0)),
                  pl.BlockSpec((d, bn), lambda i, j: (0, j)),
                  pl.BlockSpec((16, d), lambda i, j: (0, 0)),
                  pl.BlockSpec((d, LANES), lambda i, j: (0, 0))],
        out_specs=[pl.BlockSpec((bm, bn), lambda i, j: (i, j)),
                   pl.BlockSpec((16, bm), lambda i, j: (0, i)),
                   pl.BlockSpec((bm, LANES), lambda i, j: (i, 0))],
        scratch_shapes=[pltpu.VMEM((bm, d), BF16)],
        compiler_params=_cparams(("arbitrary", "arbitrary")),
        name="norm_inproj",
    )(h, g, w_main, w_small_t, w_small_c)


def _split3(x):
    p1 = x.astype(BF16)
    r1 = x - p1.astype(F32)
    p2 = r1.astype(BF16)
    p3 = (r1 - p2.astype(F32)).astype(BF16)
    return p1, p2, p3


def _fox_prep_body(f_ref, fb_ref, ke_ref, carry_ref, *, rows, n_real):
    i = pl.program_id(1)

    @pl.when(i == 0)
    def _():
        carry_ref[...] = jnp.zeros_like(carry_ref)

    f = f_ref[...] + fb_ref[...]
    lf = jnp.minimum(f, 0.0) - jnp.log1p(jnp.exp(-jnp.abs(f)))
    pos = i * rows + lax.broadcasted_iota(I32, (rows, 1), 0)
    lf = jnp.where((pos >= FRONT) & (pos < FRONT + n_real), lf, 0.0)
    rr = lax.broadcasted_iota(I32, (rows, rows), 0)
    cc = lax.broadcasted_iota(I32, (rows, rows), 1)
    tri = jnp.where(rr >= cc, 1.0, 0.0).astype(BF16)
    c = carry_ref[0:1, :]
    for piece in _split3(lf):
        c = c + jnp.dot(tri, piece, preferred_element_type=F32)
    carry_ref[...] = jnp.broadcast_to(c[rows - 1:rows, :], carry_ref.shape)
    cs = jnp.where(pos < FRONT, NEG, -LOG2E * c)
    p1, p2, p3 = (p.astype(F32) for p in _split3(cs))
    lane = lax.broadcasted_iota(I32, (rows, LANES), 1)
    for h in range(N_HEADS):
        ext = jnp.where(lane == 0, p1[:, h:h + 1],
                        jnp.where(lane == 1, p2[:, h:h + 1], jnp.where(lane == 2, p3[:, h:h + 1], 0.0)))
        ke_ref[:, h * HEAD_DIM:(h + 1) * HEAD_DIM] = ext.astype(BF16)


def _fox_prep(f_slab, forget_b, bsz, tp, n_real):
    n = f_slab.shape[0]
    hw = N_HEADS * HEAD_DIM
    rows = UNIT
    nt = tp // rows
    fb = jnp.pad(forget_b.astype(F32), (0, LANES - N_HEADS)).reshape(1, LANES)
    return pl.pallas_call(
        functools.partial(_fox_prep_body, rows=rows, n_real=n_real),
        out_shape=jax.ShapeDtypeStruct((n, hw), BF16),
        grid=(bsz, nt),
        in_specs=[pl.BlockSpec((rows, LANES), lambda b, i: (b * nt + i, 0)),
                  pl.BlockSpec((1, LANES), lambda b, i: (0, 0))],
        out_specs=pl.BlockSpec((rows, hw), lambda b, i: (b * nt + i, 0)),
        scratch_shapes=[pltpu.VMEM((8, LANES), F32)],
        compiler_params=_cparams(("arbitrary", "arbitrary")),
        name="fox_prep",
    )(f_slab, fb)


def _fox_attn_body(q_ref, k_ref, ke_ref, v_ref, ve_ref, o_ref, *, blk, n_split, group):
    qi = pl.program_id(2)
    half = blk // n_split
    lane = lax.broadcasted_iota(I32, (blk, HEAD_DIM), 1)
    q = jnp.concatenate([q_ref[...], jnp.where(lane < N_BIAS_PIECES, 1.0, 0.0).astype(BF16)], axis=1)
    qs = tuple(q[r * half:(r + 1) * half] for r in range(n_split))

    def step(js, carry, diag_last):
        starts = [pl.multiple_of(j * blk, blk) for j in js]
        ss = []
        for r, qh in enumerate(qs):
            srow = []
            for b, st in enumerate(starts):
                on_diag = diag_last and b == len(js) - 1
                ncol = (r + 1) * half if on_diag else blk
                k = jnp.concatenate([k_ref[pl.ds(st, ncol), :], ke_ref[pl.ds(st, ncol), :]], axis=1)
                s = lax.dot_general(qh, k, (((1,), (1,)), ((), ())), preferred_element_type=F32)
                if on_diag:
                    row = r * half + lax.broadcasted_iota(I32, (half, ncol), 0)
                    col = lax.broadcasted_iota(I32, (half, ncol), 1)
                    s = jnp.where(col <= row, s, NEG)
                srow.append((s, st, ncol))
            ss.append(srow)
        out = []
        for r in range(n_split):
            m = carry[2 * r]
            m_new = m
            for s, _, _ in ss[r]:
                m_new = jnp.maximum(m_new, jnp.max(s, axis=1, keepdims=True))
            acc = jnp.exp2(m - m_new) * carry[2 * r + 1]
            for s, st, ncol in ss[r]:
                v = jnp.concatenate([v_ref[pl.ds(st, ncol), :], ve_ref[0:ncol, :]], axis=1)
                acc = acc + jnp.dot(jnp.exp2(s - m_new).astype(BF16), v, preferred_element_type=F32)
            out += [m_new, acc]
        return tuple(out)

    init = (jnp.full((half, 1), NEG, F32), jnp.zeros((half, 2 * HEAD_DIM), F32)) * n_split
    carry = lax.fori_loop(0, qi // group,
                          lambda j, c: step(tuple(group * j + g for g in range(group)), c, False), init)
    tails = [functools.partial(lambda r, c: step(tuple(qi - r + g for g in range(r + 1)), c, True), r)
             for r in range(group)]
    res = lax.switch(qi % group, tails, carry)
    for r in range(n_split):
        acc = res[2 * r + 1]
        o_ref[r * half:(r + 1) * half, :] = (acc[:, 0:HEAD_DIM] / acc[:, HEAD_DIM:HEAD_DIM + 1]
                                             ).astype(o_ref.dtype)


def _fox_attn(z, ke, bsz, tp):
    n = z.shape[0]
    blk = _pick(tp, (768, 512, 256))
    nq = tp // blk
    v_ext = jnp.zeros((blk, HEAD_DIM), BF16).at[:, 0].set(1.0)
    return pl.pallas_call(
        functools.partial(_fox_attn_body, blk=blk, n_split=2, group=4),
        out_shape=jax.ShapeDtypeStruct((n, N_HEADS * HEAD_DIM), BF16),
        grid=(bsz, N_HEADS, nq),
        in_specs=[pl.BlockSpec((blk, HEAD_DIM), lambda b, h, i: (b * nq + i, h)),
                  pl.BlockSpec((tp, HEAD_DIM), lambda b, h, i: (b, N_HEADS + h)),
                  pl.BlockSpec((tp, HEAD_DIM), lambda b, h, i: (b, h)),
                  pl.BlockSpec((tp, HEAD_DIM), lambda b, h, i: (b, 2 * N_HEADS + h)),
                  pl.BlockSpec((blk, HEAD_DIM), lambda b, h, i: (0, 0))],
        out_specs=pl.BlockSpec((blk, HEAD_DIM), lambda b, h, i: (b * nq + i, h)),
        compiler_params=_cparams(("arbitrary", "arbitrary", "arbitrary")),
        name="fox_attn",
    )(z, z, ke, z, v_ext)


def _shifted_taps(x_ext, halo, rows, n_taps):
    for r in range(min(8, n_taps)):
        rolled = x_ext if r == 0 else pltpu.roll(x_ext, r, 0)
        for q in range(halo // 8):
            s = 8 * q + r
            if s > n_taps - 1:
                continue
            yield n_taps - 1 - s, rolled[halo - 8 * q:halo - 8 * q + rows]


def _conformer_body(a_ref, b_ref, w_ref, cb_ref, lg_ref, lb_ref, o_ref, halo_ref, *, bt):
    halo = halo_ref.shape[0]

    @pl.when(pl.program_id(1) == 0)
    def _():
        halo_ref[...] = jnp.zeros_like(halo_ref)

    u = a_ref[...].astype(F32) * _sigmoid(b_ref[...].astype(F32))
    x_ext = jnp.concatenate([halo_ref[...], u], axis=0)
    halo_ref[...] = u[bt - halo:bt]
    acc = jnp.zeros_like(u)
    for j, tap in _shifted_taps(x_ext, halo, bt, CONF_K):
        acc = acc + w_ref[j:j + 1, :] * tap
    y = acc + cb_ref[...]
    mu = jnp.mean(y, axis=-1, keepdims=True)
    yc = y - mu
    var = jnp.mean(yc * yc, axis=-1, keepdims=True)
    yn = yc * lax.rsqrt(var + EPS) * lg_ref[...] + lb_ref[...]
    o_ref[...] = _silu(yn).astype(o_ref.dtype)


def _conformer(z, conv_w, conv_b, ln_g, ln_b, bsz, tp, col_a, col_b):
    n = z.shape[0]
    c = conv_w.shape[1]
    bt = UNIT
    nt = tp // bt
    w = jnp.pad(conv_w.astype(F32), ((0, 32 - CONF_K), (0, 0)))
    row = lambda v: v.astype(F32).reshape(1, c)
    return pl.pallas_call(
        functools.partial(_conformer_body, bt=bt),
        out_shape=jax.ShapeDtypeStruct((n, c), BF16),
        grid=(bsz, nt),
        in_specs=[pl.BlockSpec((bt, c), lambda b, i: (b * nt + i, col_a)),
                  pl.BlockSpec((bt, c), lambda b, i: (b * nt + i, col_b)),
                  pl.BlockSpec((32, c), lambda b, i: (0, 0)),
                  pl.BlockSpec((1, c), lambda b, i: (0, 0)),
                  pl.BlockSpec((1, c), lambda b, i: (0, 0)),
                  pl.BlockSpec((1, c), lambda b, i: (0, 0))],
        out_specs=pl.BlockSpec((bt, c), lambda b, i: (b * nt + i, 0)),
        scratch_shapes=[pltpu.VMEM((32, c), F32)],
        compiler_params=_cparams(("arbitrary", "arbitrary")),
        name="conformer_conv",
    )(z, z, w, row(conv_b), row(ln_g), row(ln_b))


def _outproj_router_body(ya_ref, yb_ref, h_ref, w_ref, g_ref, wr_ref, br_ref, ho_ref, r_ref, cnt_ref,
                         xp_ref, er_ref, *, bm, blocks_per_seq, n_real):
    kh = ya_ref.shape[1]
    pos0 = (pl.program_id(0) % blocks_per_seq) * bm
    rh = bm // 2
    parts = [slice(r * rh, (r + 1) * rh) for r in range(2)]
    ys = [jnp.dot(ya_ref[p, :], w_ref[0:kh, :], preferred_element_type=F32)
          + jnp.dot(yb_ref[p, :], w_ref[kh:2 * kh, :], preferred_element_type=F32) for p in parts]
    lgs = []
    for r, (p, y) in enumerate(zip(parts, ys)):
        hn = jnp.where(_row_valid(pos0 + r * rh, rh, n_real), h_ref[p, :] + y, 0.0)
        ho_ref[p, :] = hn
        ms = jnp.mean(hn * hn, axis=-1, keepdims=True)
        xn = hn * lax.rsqrt(ms + EPS) * g_ref[...]
        xp_ref[p, :] = _pack_pairs(xn)
        xh = xn.astype(BF16)
        xl = (xn - xh.astype(F32)).astype(BF16)
        t = jnp.dot(xh, wr_ref[...], preferred_element_type=F32)
        lgs.append(t[:, 0:LANES] + t[:, LANES:2 * LANES]
                   + jnp.dot(xl, wr_ref[:, 0:LANES], preferred_element_type=F32) + br_ref[...])
    lg = jnp.concatenate(lgs, axis=0)
    lane = lax.broadcasted_iota(I32, lg.shape, 1)
    lane_f = lane.astype(F32)
    big = float(LANES)

    is_grp = (lane >= N_EXPERTS) & (lane < N_EXPERTS + N_GROUPS)
    gl = jnp.where(is_grp, lg, -jnp.inf)
    gmax = jnp.max(gl, axis=1, keepdims=True)
    gidx = jnp.min(jnp.where(gl == gmax, lane_f, big), axis=1, keepdims=True) - N_EXPERTS
    g_prob = 1.0 / jnp.sum(jnp.where(is_grp, jnp.exp(lg - gmax), 0.0), axis=1, keepdims=True)

    lo = gidx * EXPERTS_PER_GROUP
    in_grp = (lane_f >= lo) & (lane_f < lo + EXPERTS_PER_GROUP)
    el = jnp.where(in_grp, lg, -jnp.inf)
    m1 = jnp.max(el, axis=1, keepdims=True)
    i1 = jnp.min(jnp.where(el == m1, lane_f, big), axis=1, keepdims=True)
    el2 = jnp.where(lane_f == i1, -jnp.inf, el)
    m2 = jnp.max(el2, axis=1, keepdims=True)
    i2 = jnp.min(jnp.where(el2 == m2, lane_f, big), axis=1, keepdims=True)
    e2 = jnp.exp(m2 - m1)
    w1 = g_prob / (1.0 + e2)
    w2 = g_prob * e2 / (1.0 + e2)

    @pl.when(pl.program_id(0) == 0)
    def _():
        cnt_ref[...] = jnp.zeros_like(cnt_ref)

    oh1 = lane_f == i1
    oh2 = lane_f == i2
    onehots = jnp.concatenate([jnp.where(oh1, 1.0, 0.0), jnp.where(oh2, 1.0, 0.0)], axis=1).astype(BF16)
    rr = lax.broadcasted_iota(I32, (bm, bm), 0)
    cc = lax.broadcasted_iota(I32, (bm, bm), 1)
    tri = jnp.where(rr >= cc, 1.0, 0.0).astype(BF16)
    csum = jnp.dot(tri, onehots, preferred_element_type=F32)
    c1 = csum[:, 0:LANES]
    c2 = csum[:, LANES:2 * LANES]
    tot1 = c1[bm - 1:bm, :]
    tot2 = c2[bm - 1:bm, :]
    before = cnt_ref[0:1, :]
    rank1 = jnp.sum(jnp.where(oh1, before + c1 - 1.0, 0.0), axis=1, keepdims=True)
    rank2 = jnp.sum(jnp.where(oh2, before + tot1 + c2 - 1.0, 0.0), axis=1, keepdims=True)
    cnt_ref[...] = jnp.broadcast_to(before + tot1 + tot2, cnt_ref.shape)

    r_ref[...] = jnp.where(lane == 0, i1, jnp.where(lane == 1, i2,
                           jnp.where(lane == 2, w1, jnp.where(lane == 3, w2, 0.0))))
    eye = rr == cc
    er_ref[0] = jnp.concatenate(
        [jnp.sum(jnp.where(eye, col, 0.0), axis=0, keepdims=True) for col in (i1, i2, rank1, rank2)], axis=0)


def _outproj_router(ya, yb, h, w_out, g_ffn, w_router, b_router, tp, n_real):
    n, d = h.shape
    half = ya.shape[1]
    bm = UNIT
    return pl.pallas_call(
        functools.partial(_outproj_router_body, bm=bm, blocks_per_seq=tp // bm, n_real=n_real),
        out_shape=[jax.ShapeDtypeStruct((n, d), F32), jax.ShapeDtypeStruct((n, LANES), F32),
                   jax.ShapeDtypeStruct((8, LANES), F32), jax.ShapeDtypeStruct((n, d // 2), U32),
                   jax.ShapeDtypeStruct((n // bm, 4, bm), F32)],
        grid=(n // bm,),
        in_specs=[pl.BlockSpec((bm, half), lambda i: (i, 0)),
                  pl.BlockSpec((bm, half), lambda i: (i, 0)),
                  pl.BlockSpec((bm, d), lambda i: (i, 0)),
                  pl.BlockSpec((2 * half, d), lambda i: (0, 0)),
                  pl.BlockSpec((1, d), lambda i: (0, 0)),
                  pl.BlockSpec((d, 2 * LANES), lambda i: (0, 0)),
                  pl.BlockSpec((1, LANES), lambda i: (0, 0))],
        out_specs=[pl.BlockSpec((bm, d), lambda i: (i, 0)),
                   pl.BlockSpec((bm, LANES), lambda i: (i, 0)),
                   pl.BlockSpec((8, LANES), lambda i: (0, 0)),
                   pl.BlockSpec((bm, d // 2), lambda i: (i, 0)),
                   pl.BlockSpec((1, 4, bm), lambda i: (i, 0, 0))],
        compiler_params=_cparams(("arbitrary",)),
        name="outproj_router",
    )(ya, yb, h, w_out, g_ffn, w_router, b_router)


def _pack_pairs(x):
    c = x.shape[1] // 2
    hi = lax.bitcast_convert_type(x[:, :c].astype(BF16).astype(F32), U32)
    lo = lax.bitcast_convert_type(x[:, c:].astype(BF16).astype(F32), U32)
    return hi | (lo >> 16)


def _unpack_pairs(p):
    hi = lax.bitcast_convert_type(p & jnp.uint32(0xFFFF0000), F32)
    lo = lax.bitcast_convert_type(p << 16, F32)
    return hi, lo


def _dispatch_body(meta_ref, dest_ref, xin_ref, xs_hbm, xp_ref, zero_ref, sem, zsem, *, tb):
    i = pl.program_id(0)
    slot = i % 2

    def drain(s):
        for _ in range(2):
            pltpu.make_async_copy(zero_ref, xs_hbm.at[pl.ds(0, tb)], sem.at[s]).wait()

    for g in range(tb // 8):
        xp_ref[slot, g] = xin_ref[g * 8:(g + 1) * 8, :]

    def issue(g, _):
        for u in range(8):
            for k in range(2):
                d = dest_ref[0, 0, k * tb + g * 8 + u]
                pltpu.make_async_copy(xp_ref.at[slot, g, pl.ds(u, 1)], xs_hbm.at[pl.ds(d, 1)],
                                      sem.at[slot]).start()
        return 0

    lax.fori_loop(0, tb // 8, issue, 0)

    @pl.when(i > 0)
    def _():
        drain(1 - slot)

    @pl.when(i == pl.num_programs(0) - 1)
    def _():
        drain(slot)

    @pl.when(i == 0)
    def _():
        zero_ref[...] = jnp.zeros_like(zero_ref)

        def per_expert(e, _):
            cnt = meta_ref[e]
            start = meta_ref[N_EXPERTS + e]
            padded = meta_ref[2 * N_EXPERTS + e]

            def pad_copy(r):
                return pltpu.make_async_copy(zero_ref.at[pl.ds(0, 1)], xs_hbm.at[pl.ds(start + r, 1)], zsem)

            def zissue(r, _):
                pad_copy(r).start()
                return 0

            def zwait(r, _):
                pad_copy(r).wait()
                return 0

            lax.fori_loop(cnt, padded, zissue, 0)
            lax.fori_loop(cnt, padded, zwait, 0)
            return 0

        lax.fori_loop(0, N_EXPERTS, per_expert, 0)

        def tail_copy(blk):
            return pltpu.make_async_copy(zero_ref, xs_hbm.at[pl.ds(blk * MOE_BM, MOE_BM)], zsem)

        def tissue(blk, _):
            tail_copy(blk).start()
            return 0

        def twait(blk, _):
            tail_copy(blk).wait()
            return 0

        n_used = meta_ref[3 * N_EXPERTS]
        lax.fori_loop(n_used, xs_hbm.shape[0] // MOE_BM, tissue, 0)
        lax.fori_loop(n_used, xs_hbm.shape[0] // MOE_BM, twait, 0)


def _dispatch(xp, dest3, meta, n_rows):
    n, dp = xp.shape
    tb = UNIT
    return pl.pallas_call(
        functools.partial(_dispatch_body, tb=tb),
        out_shape=jax.ShapeDtypeStruct((n_rows, dp), U32),
        grid_spec=pltpu.PrefetchScalarGridSpec(
            num_scalar_prefetch=1,
            grid=(n // tb,),
            in_specs=[pl.BlockSpec((1, 1, 2 * tb), lambda i, m: (i, 0, 0), memory_space=pltpu.SMEM),
                      pl.BlockSpec((tb, dp), lambda i, m: (i, 0))],
            out_specs=pl.BlockSpec(memory_space=pl.ANY),
            scratch_shapes=[pltpu.VMEM((2, tb // 8, 8, dp), U32), pltpu.VMEM((MOE_BM, dp), U32),
                            pltpu.SemaphoreType.DMA((2,)), pltpu.SemaphoreType.DMA(())]),
        compiler_params=_cparams(("arbitrary",)),
        name="moe_dispatch",
    )(meta, dest3, xp)


def _experts_body(be_ref, nxt_ref, nu_ref, xs_ref, wg_hbm, wu_hbm, wd_hbm, ys_ref,
                  wg_st, wu_st, wd_st, wg_bf, wu_bf, wd_bf, ord_ref, sem, *, layer):
    i = pl.program_id(0)
    n_used = nu_ref[0]

    def weight_copies(e, s):
        return (pltpu.make_async_copy(wg_hbm.at[layer, e], wg_st.at[s], sem.at[s, 0]),
                pltpu.make_async_copy(wu_hbm.at[layer, e], wu_st.at[s], sem.at[s, 1]),
                pltpu.make_async_copy(wd_hbm.at[layer, e], wd_st.at[s], sem.at[s, 2]))

    @pl.when(i == 0)
    def _():
        ord_ref[0] = 0
        for c in weight_copies(be_ref[0], 0):
            c.start()

    @pl.when(i < n_used)
    def _():
        e = be_ref[i]
        prev = be_ref[jnp.maximum(i - 1, 0)]

        @pl.when((i == 0) | (e != prev))
        def _():
            s = ord_ref[0] % 2
            for c in weight_copies(e, s):
                c.wait()
            nxt = nxt_ref[i]

            @pl.when(nxt != e)
            def _():
                for c in weight_copies(nxt, 1 - s):
                    c.start()

            wg_bf[...] = wg_st[s].astype(BF16)
            wu_bf[...] = wu_st[s].astype(BF16)
            wd_bf[...] = wd_st[s].astype(BF16)
            ord_ref[0] = ord_ref[0] + 1

        half = MOE_BM // 2
        kh = wg_bf.shape[0] // 2
        xns = []
        for r in range(2):
            hi, lo = _unpack_pairs(xs_ref[r * half:(r + 1) * half, :])
            xns.append((hi.astype(BF16), lo.astype(BF16)))

        def up(w_bf):
            return [jnp.dot(xa, w_bf[0:kh, :], preferred_element_type=F32)
                    + jnp.dot(xb, w_bf[kh:2 * kh, :], preferred_element_type=F32) for xa, xb in xns]

        hgs = up(wg_bf)
        hus = up(wu_bf)
        hids = [(_silu(hg) * hu).astype(BF16) for hg, hu in zip(hgs, hus)]
        for r in range(2):
            ys_ref[r * half:(r + 1) * half, :] = _pack_pairs(
                jnp.dot(hids[r], wd_bf[...], preferred_element_type=F32))

    @pl.when(i >= n_used)
    def _():
        ys_ref[...] = jnp.zeros_like(ys_ref)


def _experts(xs, w_gate, w_up, w_down, layer, block_expert, next_expert, n_used):
    r, dp = xs.shape
    d = 2 * dp
    de = w_gate.shape[3]
    nb = r // MOE_BM
    return pl.pallas_call(
        functools.partial(_experts_body, layer=layer),
        out_shape=jax.ShapeDtypeStruct((r, dp), U32),
        grid_spec=pltpu.PrefetchScalarGridSpec(
            num_scalar_prefetch=3,
            grid=(nb,),
            in_specs=[pl.BlockSpec((MOE_BM, dp), lambda i, be, nx, nu: (i, 0)),
                      pl.BlockSpec(memory_space=pl.ANY),
                      pl.BlockSpec(memory_space=pl.ANY),
                      pl.BlockSpec(memory_space=pl.ANY)],
            out_specs=pl.BlockSpec((MOE_BM, dp), lambda i, be, nx, nu: (i, 0)),
            scratch_shapes=[pltpu.VMEM((2, d, de), F32), pltpu.VMEM((2, d, de), F32),
                            pltpu.VMEM((2, de, d), F32),
                            pltpu.VMEM((d, de), BF16), pltpu.VMEM((d, de), BF16), pltpu.VMEM((de, d), BF16),
                            pltpu.SMEM((1,), I32), pltpu.SemaphoreType.DMA((2, 3))]),
        compiler_params=_cparams(("arbitrary",)),
        name="moe_experts",
    )(block_expert, next_expert, n_used, xs, w_gate, w_up, w_down)


def _combine_body(dest_ref, dnext_ref, h_ref, r_ref, gf_ref, ys_hbm, o_ref, buf, sem, *out_stage,
                  tb, blocks_per_seq, n_real, final_norm):
    i = pl.program_id(0)
    slot = i % 2

    def gather(d_ref, s):
        def issue(t, _):
            for k in range(2):
                d = d_ref[0, 0, k * tb + t]
                pltpu.make_async_copy(ys_hbm.at[pl.ds(d, 1)], buf.at[s, k, pl.ds(t, 1)], sem.at[s]).start()
            return 0

        lax.fori_loop(0, tb, issue, 0, unroll=8)

    @pl.when(i == 0)
    def _():
        gather(dest_ref, 0)

    @pl.when(i + 1 < pl.num_programs(0))
    def _():
        gather(dnext_ref, 1 - slot)

    for k in range(2):
        pltpu.make_async_copy(ys_hbm.at[pl.ds(0, tb)], buf.at[slot, k], sem.at[slot]).wait()

    r = r_ref[...]
    w1 = r[:, 2:3]
    w2 = r[:, 3:4]
    y1 = _unpack_pairs(buf[slot, 0])
    y2 = _unpack_pairs(buf[slot, 1])
    y = jnp.concatenate([w1 * y1[0] + w2 * y2[0], w1 * y1[1] + w2 * y2[1]], axis=1)
    pos0 = (i % blocks_per_seq) * tb
    hn = jnp.where(_row_valid(pos0, tb, n_real), h_ref[...] + y, 0.0)
    if not final_norm:
        o_ref[...] = hn
        return

    obuf, osem = out_stage
    seq = n_real - N_META
    ms = jnp.mean(hn * hn, axis=-1, keepdims=True)
    obuf[slot] = hn * lax.rsqrt(ms + EPS) * gf_ref[...]

    def frame_copy(step, s, start):
        b = step // blocks_per_seq
        j = step % blocks_per_seq
        row0 = b * seq + j * tb - OFF
        tail = OFF + seq - (blocks_per_seq - 1) * tb

        def go(src, dst):
            cp = pltpu.make_async_copy(src, dst, osem.at[s])
            cp.start() if start else cp.wait()

        @pl.when(j == 0)
        def _():
            go(obuf.at[s, pl.ds(OFF, tb - OFF)], o_ref.at[pl.ds(b * seq, tb - OFF)])

        @pl.when((j > 0) & (j < blocks_per_seq - 1))
        def _():
            go(obuf.at[s], o_ref.at[pl.ds(row0, tb)])

        @pl.when(j == blocks_per_seq - 1)
        def _():
            go(obuf.at[s, pl.ds(0, tail)], o_ref.at[pl.ds(row0, tail)])

    frame_copy(i, slot, True)

    @pl.when(i > 0)
    def _():
        frame_copy(i - 1, 1 - slot, False)

    @pl.when(i == pl.num_programs(0) - 1)
    def _():
        frame_copy(i, slot, False)


def _combine(h, route, dest3, ys, g_final, tp, n_real, final_norm):
    n, d = h.shape
    tb = UNIT
    last = n // tb - 1
    assert tp // tb >= 2
    if final_norm:
        out_shape = jax.ShapeDtypeStruct((n // tp * (n_real - N_META), d), F32)
        out_spec = pl.BlockSpec(memory_space=pl.ANY)
        out_stage = [pltpu.VMEM((2, tb, d), F32), pltpu.SemaphoreType.DMA((2,))]
    else:
        out_shape = jax.ShapeDtypeStruct((n, d), F32)
        out_spec = pl.BlockSpec((tb, d), lambda i: (i, 0))
        out_stage = []
    return pl.pallas_call(
        functools.partial(_combine_body, tb=tb, blocks_per_seq=tp // tb, n_real=n_real,
                          final_norm=final_norm),
        out_shape=out_shape,
        grid=(n // tb,),
        in_specs=[pl.BlockSpec((1, 1, 2 * tb), lambda i: (i, 0, 0), memory_space=pltpu.SMEM),
                  pl.BlockSpec((1, 1, 2 * tb), lambda i: (jnp.minimum(i + 1, last), 0, 0),
                               memory_space=pltpu.SMEM),
                  pl.BlockSpec((tb, d), lambda i: (i, 0)),
                  pl.BlockSpec((tb, LANES), lambda i: (i, 0)),
                  pl.BlockSpec((1, d), lambda i: (0, 0)),
                  pl.BlockSpec(memory_space=pl.ANY)],
        out_specs=out_spec,
        scratch_shapes=[pltpu.VMEM((2, 2, tb, d // 2), U32), pltpu.SemaphoreType.DMA((2,))] + out_stage,
        compiler_params=_cparams(("arbitrary",)),
        name="moe_combine",
    )(dest3, dest3, h, route, g_final, ys)


def _moe(h, xp, route, counts_f, er, w_gate, w_up, w_down, layer, g_final, tp, n_real, final_norm):
    n, d = h.shape
    n_pairs = 2 * n
    counts = counts_f[0, 0:N_EXPERTS].astype(I32)
    padded = (counts + MOE_BM - 1) // MOE_BM * MOE_BM
    p_ends = jnp.cumsum(padded)
    p_starts = p_ends - padded
    e_t = er[:, 0:2, :].astype(I32)
    start_of = jnp.sum(jnp.where(e_t[..., None] == jnp.arange(N_EXPERTS, dtype=I32), p_starts, 0), axis=-1)
    dest = start_of + er[:, 2:4, :].astype(I32)
    n_blocks = -(-(n_pairs + N_EXPERTS * (MOE_BM - 1)) // MOE_BM)
    block_row0 = jnp.arange(n_blocks, dtype=I32) * MOE_BM
    block_expert = jnp.minimum(jnp.sum((p_ends[None, :] <= block_row0[:, None]).astype(I32), axis=1),
                               N_EXPERTS - 1)
    n_used = (p_ends[-1:] // MOE_BM).astype(I32)
    meta = jnp.concatenate([counts, p_starts, padded, n_used]).astype(I32)
    dest3 = dest.reshape(n // UNIT, 1, 2 * UNIT)

    xs = _dispatch(xp, dest3, meta, n_blocks * MOE_BM)
    ids = jnp.arange(N_EXPERTS, dtype=I32)
    owner = jnp.where(counts > 0, ids, N_EXPERTS)
    first_from = lax.cummin(owner[::-1])[::-1]
    after = jnp.concatenate([first_from[1:], jnp.full((1,), N_EXPERTS, I32)])
    next_expert = jnp.take(jnp.where(after < N_EXPERTS, after, ids), block_expert)
    ys = _experts(xs, w_gate, w_up, w_down, layer, block_expert, next_expert, n_used)
    return _combine(h, route, dest3, ys, g_final, tp, n_real, final_norm)


def _short_conv(x, halo_ref, w_ref, first):
    rows = x.shape[0]

    @pl.when(first)
    def _():
        halo_ref[...] = jnp.zeros_like(halo_ref)

    x_ext = jnp.concatenate([halo_ref[...], x], axis=0)
    halo_ref[...] = x[rows - 8:rows]
    acc = jnp.zeros_like(x)
    for j, tap in _shifted_taps(x_ext, 8, rows, SHORT_K):
        acc = acc + w_ref[j:j + 1, :] * tap
    return acc


def _bdot(a, b):
    return jnp.dot(a.astype(BF16), b.astype(BF16), preferred_element_type=F32)


def _gdn_pre_body(z_ref, sc_ref, st_ref, cw_ref, al_r_ref, dt_r_ref, al_c_ref, dt_c_ref,
                  u_ref, w_ref, qd_ref, kd_ref, at_ref, eg_ref, halo_ref):
    rows = z_ref.shape[0]
    hw = N_HEADS * HEAD_DIM
    x = _silu(_short_conv(z_ref[...].astype(F32), halo_ref, cw_ref, pl.program_id(1) == 0))

    sc = sc_ref[...]
    g_cols = -jnp.exp(al_r_ref[...]) * _softplus(sc + dt_r_ref[...])
    beta_cols = _sigmoid(sc)
    st = st_ref[...]
    g_rows = -jnp.exp(al_c_ref[:, 0:1]) * _softplus(st + dt_c_ref[:, 0:1])

    ri = lax.broadcasted_iota(I32, (rows, rows), 0)
    ci = lax.broadcasted_iota(I32, (rows, rows), 1)
    same64 = (ri >> 6) == (ci >> 6)
    same32 = (ri >> 5) == (ci >> 5)
    same16 = (ri >> 4) == (ci >> 4)
    lower = ri >= ci
    strict = ri > ci
    lane = lax.broadcasted_iota(I32, (rows, LANES), 1)
    eg_slab = jnp.zeros((rows, LANES), F32)
    scale = HEAD_DIM ** -0.5

    heads = range(N_HEADS)
    in_chunk_lower = same64 & lower
    in_chunk_upper = same64 & (ri <= ci)
    a_mats, nmats, rhss = [], [], []
    for h in heads:
        sl = slice(h * HEAD_DIM, (h + 1) * HEAD_DIM)
        q = x[:, sl]
        k = x[:, hw + h * HEAD_DIM:hw + (h + 1) * HEAD_DIM]
        v = x[:, 2 * hw + h * HEAD_DIM:2 * hw + (h + 1) * HEAD_DIM]
        q = q * lax.rsqrt(jnp.sum(q * q, axis=-1, keepdims=True) + EPS)
        k = k * lax.rsqrt(jnp.sum(k * k, axis=-1, keepdims=True) + EPS)
        g_col = g_cols[:, h:h + 1]
        beta = beta_cols[:, N_HEADS + h:N_HEADS + h + 1]
        g_row = g_rows[h:h + 1, :]

        gc_col = jnp.sum(jnp.where(in_chunk_lower, g_row, 0.0), axis=1, keepdims=True)
        gc_row = jnp.sum(jnp.where(in_chunk_upper, g_col, 0.0), axis=0, keepdims=True)
        gtot_col = jnp.sum(jnp.where(same64, g_row, 0.0), axis=1, keepdims=True)
        decay = jnp.exp(jnp.where(in_chunk_lower, gc_col - gc_row, NEG))
        eg = jnp.exp(gc_col)
        ekd = jnp.exp(gtot_col - gc_col)

        kb = k * beta
        kbf = k.astype(BF16)
        kk = lax.dot_general(kb.astype(BF16), kbf, (((1,), (1,)), ((), ())), preferred_element_type=F32)
        qs = q * scale
        qk = lax.dot_general(qs.astype(BF16), kbf, (((1,), (1,)), ((), ())), preferred_element_type=F32)
        attn = qk * decay
        a_mats.append(jnp.where(strict, kk * decay, 0.0))
        rhss.append(jnp.concatenate([v * beta, kb * eg], axis=1))
        qd_ref[:, sl] = (qs * eg).astype(BF16)
        kd_ref[:, sl] = (k * ekd).astype(BF16)
        for c in range(rows // CHUNK):
            cs = slice(c * CHUNK, (c + 1) * CHUNK)
            at_ref[h, cs, :] = attn[cs, cs].astype(BF16)
        eg_slab = jnp.where(lane == h, eg, eg_slab)
    eg_ref[...] = eg_slab

    nmats = [jnp.where(same16, -a, 0.0) for a in a_mats]
    qqs = nmats
    for _ in range(3):
        qqs = [_bdot(qq, qq) for qq in qqs]
        prods = [_bdot(qq, nm) for qq, nm in zip(qqs, nmats)]
        nmats = [nm + qq + pr for nm, qq, pr in zip(nmats, qqs, prods)]
    for level_mask, inner_mask in ((same32, same16), (same64, same32)):
        sel = level_mask & jnp.logical_not(inner_mask)
        offs = [jnp.where(sel, a, 0.0) for a in a_mats]
        bmats = [off + _bdot(nm, off) for nm, off in zip(nmats, offs)]
        prods = [_bdot(bm, nm) for bm, nm in zip(bmats, nmats)]
        nmats = [nm - bm - pr for nm, bm, pr in zip(nmats, bmats, prods)]

    for h in heads:
        sl = slice(h * HEAD_DIM, (h + 1) * HEAD_DIM)
        uw = rhss[h] + _bdot(nmats[h], rhss[h])
        u_ref[:, sl] = uw[:, 0:HEAD_DIM]
        w_ref[:, sl] = uw[:, HEAD_DIM:2 * HEAD_DIM].astype(BF16)


def _gdn_pre(z, sc, st, conv_w, a_log, dt_bias, bsz, tp):
    n = z.shape[0]
    hw = N_HEADS * HEAD_DIM
    rows = UNIT
    nu = tp // rows
    pad_r = lambda v: jnp.pad(v.astype(F32), (0, LANES - N_HEADS)).reshape(1, LANES)
    pad_c = lambda v: jnp.broadcast_to(jnp.pad(v.astype(F32), (0, 16 - N_HEADS))[:, None], (16, LANES))
    outs = pl.pallas_call(
        _gdn_pre_body,
        out_shape=[jax.ShapeDtypeStruct((n, hw), F32),
                   jax.ShapeDtypeStruct((n, hw), BF16),
                   jax.ShapeDtypeStruct((n, hw), BF16),
                   jax.ShapeDtypeStruct((n, hw), BF16),
                   jax.ShapeDtypeStruct((N_HEADS, n, CHUNK), BF16),
                   jax.ShapeDtypeStruct((n, LANES), F32)],
        grid=(bsz, nu),
        in_specs=[pl.BlockSpec((rows, 3 * hw), lambda b, i: (b * nu + i, 0)),
                  pl.BlockSpec((rows, LANES), lambda b, i: (b * nu + i, 0)),
                  pl.BlockSpec((16, rows), lambda b, i: (0, b * nu + i)),
                  pl.BlockSpec((8, 3 * hw), lambda b, i: (0, 0)),
                  pl.BlockSpec((1, LANES), lambda b, i: (0, 0)),
                  pl.BlockSpec((1, LANES), lambda b, i: (0, 0)),
                  pl.BlockSpec((16, LANES), lambda b, i: (0, 0)),
                  pl.BlockSpec((16, LANES), lambda b, i: (0, 0))],
        out_specs=[pl.BlockSpec((rows, hw), lambda b, i: (b * nu + i, 0)),
                   pl.BlockSpec((rows, hw), lambda b, i: (b * nu + i, 0)),
                   pl.BlockSpec((rows, hw), lambda b, i: (b * nu + i, 0)),
                   pl.BlockSpec((rows, hw), lambda b, i: (b * nu + i, 0)),
                   pl.BlockSpec((N_HEADS, rows, CHUNK), lambda b, i: (0, b * nu + i, 0)),
                   pl.BlockSpec((rows, LANES), lambda b, i: (b * nu + i, 0))],
        scratch_shapes=[pltpu.VMEM((8, 3 * hw), F32)],
        compiler_params=_cparams(("arbitrary", "arbitrary")),
        name="gdn_pre",
    )(z, sc, st, jnp.pad(conv_w.astype(F32), ((0, 8 - SHORT_K), (0, 0))),
      pad_r(a_log), pad_r(dt_bias), pad_c(a_log), pad_c(dt_bias))
    return outs


def _gdn_scan_body(u_ref, w_ref, qd_ref, kd_ref, at_ref, eg_ref, z_ref, gn_ref, o_ref, s_ref, *, bsz):
    @pl.when(pl.program_id(0) == 0)
    def _():
        s_ref[...] = jnp.zeros_like(s_ref)

    units = [(b, h, slice(h * HEAD_DIM, (h + 1) * HEAD_DIM)) for b in range(bsz) for h in range(N_HEADS)]
    states = [s_ref[b * N_HEADS + h] for b, h, _ in units]
    for c in range(u_ref.shape[1] // CHUNK):
        rows = slice(c * CHUNK, (c + 1) * CHUNK)
        rs = [jnp.dot(jnp.concatenate([w_ref[b, rows, sl], qd_ref[b, rows, sl]], axis=0), s.astype(BF16),
                      preferred_element_type=F32) for (b, h, sl), s in zip(units, states)]
        vbs = [(u_ref[b, rows, sl] - r[0:CHUNK]).astype(BF16) for (b, h, sl), r in zip(units, rs)]
        intra = [jnp.dot(at_ref[h, b, rows, :], vb, preferred_element_type=F32)
                 for (b, h, sl), vb in zip(units, vbs)]
        outer = [lax.dot_general(kd_ref[b, rows, sl], vb, (((0,), (0,)), ((), ())),
                                 preferred_element_type=F32) for (b, h, sl), vb in zip(units, vbs)]
        last = (c + 1) * CHUNK - 1
        states = [s * eg_ref[b, last:last + 1, h:h + 1] + kv for (b, h, sl), s, kv in zip(units, states, outer)]
        for (b, h, sl), r, a in zip(units, rs, intra):
            o = r[CHUNK:2 * CHUNK] + a
            on = o * lax.rsqrt(jnp.mean(o * o, axis=-1, keepdims=True) + EPS) * gn_ref[...]
            o_ref[b, rows, sl] = (on * _silu(z_ref[b, rows, sl].astype(F32))).astype(o_ref.dtype)
    for (b, h, sl), s in zip(units, states):
        s_ref[b * N_HEADS + h] = s


def _gdn_scan(u, w, qd, kd, attn, eg, z, out_norm_g, bsz, tp, z_col):
    hw = N_HEADS * HEAD_DIM
    rows = 2 * CHUNK
    nc = tp // rows
    v3 = lambda a: a.reshape(bsz, tp, a.shape[-1])
    blk3 = pl.BlockSpec((bsz, rows, hw), lambda c: (0, c, 0))
    return pl.pallas_call(
        functools.partial(_gdn_scan_body, bsz=bsz),
        out_shape=jax.ShapeDtypeStruct((bsz, tp, hw), BF16),
        grid=(nc,),
        in_specs=[blk3, blk3, blk3, blk3,
                  pl.BlockSpec((N_HEADS, bsz, rows, CHUNK), lambda c: (0, 0, c, 0)),
                  pl.BlockSpec((bsz, rows, LANES), lambda c: (0, c, 0)),
                  pl.BlockSpec((bsz, rows, hw), lambda c: (0, c, z_col)),
                  pl.BlockSpec((1, HEAD_DIM), lambda c: (0, 0))],
        out_specs=blk3,
        scratch_shapes=[pltpu.VMEM((bsz * N_HEADS, HEAD_DIM, HEAD_DIM), F32)],
        compiler_params=_cparams(("arbitrary",)),
        name="gdn_scan",
    )(v3(u), v3(w), v3(qd), v3(kd), attn.reshape(N_HEADS, bsz, tp, CHUNK), v3(eg), v3(z),
      out_norm_g.astype(F32).reshape(1, HEAD_DIM)).reshape(bsz * tp, hw)


def _lru_body(x_ref, gate_ref, cw_ref, cb_ref, wr_ref, br_ref, wi_ref, bi_ref, lam_ref, o_ref,
              halo_ref, hc_ref, *, bt, n_real):
    first = pl.program_id(1) == 0

    @pl.when(first)
    def _():
        hc_ref[...] = jnp.zeros_like(hc_ref)

    x = _short_conv(x_ref[...].astype(F32), halo_ref, cw_ref, first) + cb_ref[...]
    nblk = wr_ref.shape[0]
    bd = wr_ref.shape[1]
    xb = x.astype(BF16)
    rg = jnp.concatenate([jnp.dot(xb[:, n * bd:(n + 1) * bd], wr_ref[n], preferred_element_type=F32)
                          for n in range(nblk)], axis=1)
    ig = jnp.concatenate([jnp.dot(xb[:, n * bd:(n + 1) * bd], wi_ref[n], preferred_element_type=F32)
                          for n in range(nblk)], axis=1)
    r = _sigmoid(rg + br_ref[...])
    ig = _sigmoid(ig + bi_ref[...])
    log_a = (-LRU_C * _softplus(-lam_ref[...])) * r
    a = jnp.exp(log_a)
    th = jnp.tanh(log_a)
    b = jnp.sqrt(-2.0 * th / (1.0 - th)) * (ig * x)
    b = jnp.where(_row_valid(pl.program_id(1) * bt, bt, n_real), b, 0.0)

    sub = lax.broadcasted_iota(I32, (bt, 1), 0) & 7
    d = 1
    while d < 8:
        keep = sub >= d
        a_sh = jnp.where(keep, pltpu.roll(a, d, 0), 1.0)
        b_sh = jnp.where(keep, pltpu.roll(b, d, 0), 0.0)
        b = a * b_sh + b
        a = a * a_sh
        d *= 2
    carry = hc_ref[0:1, :]
    tiles = []
    for g in range(bt // 8):
        tiles.append(a[8 * g:8 * g + 8] * carry + b[8 * g:8 * g + 8])
        carry = tiles[-1][7:8, :]
    hc_ref[...] = jnp.broadcast_to(carry, hc_ref.shape)
    hs = jnp.concatenate(tiles, axis=0)
    o_ref[...] = (hs * jax.nn.gelu(gate_ref[...].astype(F32))).astype(o_ref.dtype)


def _lru(z, conv_w, conv_b, w_rg, b_rg, w_ig, b_ig, lam, bsz, tp, n_real, x_col, gate_col):
    n = z.shape[0]
    c = conv_w.shape[1]
    bt = UNIT
    nt = tp // bt
    row = lambda v: v.astype(F32).reshape(1, c)
    wspec = pl.BlockSpec(w_rg.shape, lambda b, i: (0, 0, 0))
    vspec = pl.BlockSpec((1, c), lambda b, i: (0, 0))
    return pl.pallas_call(
        functools.partial(_lru_body, bt=bt, n_real=n_real),
        out_shape=jax.ShapeDtypeStruct((n, c), BF16),
        grid=(bsz, nt),
        in_specs=[pl.BlockSpec((bt, c), lambda b, i: (b * nt + i, x_col)),
                  pl.BlockSpec((bt, c), lambda b, i: (b * nt + i, gate_col)),
                  pl.BlockSpec((8, c), lambda b, i: (0, 0)),
                  vspec, wspec, vspec, wspec, vspec, vspec],
        out_specs=pl.BlockSpec((bt, c), lambda b, i: (b * nt + i, 0)),
        scratch_shapes=[pltpu.VMEM((8, c), F32), pltpu.VMEM((8, c), F32)],
        compiler_params=_cparams(("arbitrary", "arbitrary")),
        name="rg_lru",
    )(z, z, jnp.pad(conv_w.astype(F32), ((0, 8 - SHORT_K), (0, 0))), row(conv_b),
      w_rg.astype(BF16), row(b_rg), w_ig.astype(BF16), row(b_ig), row(lam))


def _pick(n, candidates):
    for c in candidates:
        if n % c == 0:
            return c
    raise ValueError(f"no block size in {candidates} divides {n}")


def _pack_weights_body(a_ref, b_ref, o_ref, *, scaled_tiles, scale, head_tiles, gap):
    j = pl.program_id(0)
    tile = o_ref.shape[1]

    @pl.when(j < scaled_tiles)
    def _():
        o_ref[...] = (a_ref[0] * scale).astype(BF16)

    @pl.when((j >= scaled_tiles) & (j < head_tiles))
    def _():
        o_ref[...] = a_ref[0].astype(BF16)

    @pl.when(j >= head_tiles)
    def _():
        x = jnp.concatenate([a_ref[0], b_ref[0]], axis=1)
        o_ref[...] = x[:, gap:gap + tile].astype(BF16)


def _pack_inproj_weights(w_all, idx, head, gap, scaled=0, scale=1.0):
    _, d, total = w_all.shape
    tile = 512
    n_out = total - gap
    return pl.pallas_call(
        functools.partial(_pack_weights_body, scaled_tiles=scaled // tile, scale=scale,
                          head_tiles=head // tile, gap=gap),
        out_shape=jax.ShapeDtypeStruct((d, n_out), BF16),
        grid=(n_out // tile,),
        in_specs=[pl.BlockSpec((1, d, tile), lambda j: (idx, 0, j)),
                  pl.BlockSpec((1, d, LANES), lambda j: (idx, 0, (tile // LANES) * (j + 1)))],
        out_specs=pl.BlockSpec((d, tile), lambda j: (0, j)),
        compiler_params=_cparams(("arbitrary",)),
        name="pack_inproj_weights",
    )(w_all, w_all)


def _small_weights(cols):
    k = cols.shape[1]
    return (jnp.pad(cols.T, ((0, 16 - k), (0, 0))).astype(BF16),
            jnp.pad(cols, ((0, 0), (0, LANES - k))).astype(BF16))


def _router_weights(w_group, b_group, w_expert, b_expert):
    w = jnp.concatenate([w_expert, w_group], axis=1).astype(F32)
    b = jnp.concatenate([b_expert, b_group]).astype(F32)
    k = w.shape[1]
    w = jnp.pad(w, ((0, 0), (0, LANES - k)))
    w_hi = w.astype(BF16)
    w_lo = (w - w_hi.astype(F32)).astype(BF16)
    return jnp.concatenate([w_hi, w_lo], axis=1), jnp.pad(b, (0, LANES - k)).reshape(1, LANES)


def kernel(x, meta_tokens, norm_mix_g, norm_ffn_g, norm_final_g, ab_w_in, ab_forget_b, ab_conv_w, ab_conv_b, ab_ln_g, ab_ln_b, ab_w_out, cd_w_in, cd_qkv_conv_w, cd_a_log, cd_dt_bias, cd_out_norm_g, cd_lru_conv_w, cd_lru_conv_b, cd_w_rg, cd_b_rg, cd_w_ig, cd_b_ig, cd_lru_lambda, cd_w_out, moe_w_group, moe_b_group, moe_w_expert, moe_b_expert, moe_w_gate, moe_w_up, moe_w_down):
    bsz, seq, d = x.shape
    depth = norm_mix_g.shape[0]
    n_real = N_META + seq
    tp = -(-(OFF + seq) // UNIT) * UNIT
    n = bsz * tp
    hw = N_HEADS * HEAD_DIM
    bm = _pick(n, (768, 512, 256))
    bn = 1024

    h = _embed(x.astype(F32).reshape(bsz * seq, d), meta_tokens.astype(F32), bsz, seq, tp)
    row = lambda v: v.astype(F32).reshape(1, -1)

    for layer in range(depth):
        i = layer // 2
        g_mix = row(norm_mix_g[layer])
        if layer % 2 == 0:
            w_in = ab_w_in[i]
            w_main = _pack_inproj_weights(ab_w_in.astype(F32), i, 3 * hw, N_HEADS, scaled=hw,
                                          scale=LOG2E * HEAD_DIM ** -0.5)
            wst, wsc = _small_weights(w_in[:, 3 * hw:3 * hw + N_HEADS])
            z, _, f_slab = _norm_inproj(h, g_mix, w_main, wst, wsc, bm, bn)
            ke = _fox_prep(f_slab, ab_forget_b[i], bsz, tp, n_real)
            ya = _fox_attn(z, ke, bsz, tp)
            yb = _conformer(z, ab_conv_w[i], ab_conv_b[i], ab_ln_g[i], ab_ln_b[i], bsz, tp, 3, 4)
            w_out = ab_w_out[i].astype(BF16)
        else:
            w_in = cd_w_in[i]
            w_main = _pack_inproj_weights(cd_w_in.astype(F32), i, 3 * hw, 2 * N_HEADS)
            wst, wsc = _small_weights(w_in[:, 3 * hw:3 * hw + 2 * N_HEADS])
            z, zt, zc = _norm_inproj(h, g_mix, w_main, wst, wsc, bm, bn)
            u, w, qd, kd, attn, eg = _gdn_pre(z, zc, zt, cd_qkv_conv_w[i], cd_a_log[i], cd_dt_bias[i],
                                              bsz, tp)
            ya = _gdn_scan(u, w, qd, kd, attn, eg, z, cd_out_norm_g[i], bsz, tp, 3)
            yb = _lru(z, cd_lru_conv_w[i], cd_lru_conv_b[i], cd_w_rg[i], cd_b_rg[i], cd_w_ig[i],
                      cd_b_ig[i], cd_lru_lambda[i], bsz, tp, n_real, 4, 5)
            w_out = cd_w_out[i].astype(BF16)
        g_ffn = row(norm_ffn_g[layer])
        w_r, b_r = _router_weights(moe_w_group[layer], moe_b_group[layer],
                                   moe_w_expert[layer], moe_b_expert[layer])
        h, route, counts, xp, er = _outproj_router(ya, yb, h, w_out, g_ffn, w_r, b_r, tp, n_real)
        h = _moe(h, xp, route, counts, er, moe_w_gate, moe_w_up, moe_w_down, layer,
                 row(norm_final_g), tp, n_real, final_norm=(layer == depth - 1))
    return h.reshape(bsz, seq, d).astype(x.dtype)
```

```python
import functools

import jax
import jax.numpy as jnp
from jax import lax
from jax.experimental import pallas as pl
from jax.experimental.pallas import tpu as pltpu

F32 = jnp.float32
BF16 = jnp.bfloat16
I32 = jnp.int32
U32 = jnp.uint32

EPS = 1e-6
N_META = 16
CHUNK = 64
FRONT = CHUNK - N_META
OFF = FRONT + N_META
LANES = 128
UNIT = 256
HEAD_DIM = 128
N_HEADS = 8
CONF_K = 31
SHORT_K = 4
LRU_C = 8.0
N_GROUPS = 4
EXPERTS_PER_GROUP = 8
N_EXPERTS = N_GROUPS * EXPERTS_PER_GROUP
MOE_BM = 256
NEG = -1e30
LOG2E = 1.4426950408889634
N_BIAS_PIECES = 3
VMEM_LIMIT = 56 * 1024 * 1024


def _cparams(sem):
    return pltpu.CompilerParams(dimension_semantics=sem, vmem_limit_bytes=VMEM_LIMIT)


def _row_valid(pos0, rows, n_real):
    pos = pos0 + lax.broadcasted_iota(I32, (rows, 1), 0)
    return (pos >= FRONT) & (pos < FRONT + n_real)


def _sigmoid(x):
    return 1.0 / (1.0 + jnp.exp(-x))


def _softplus(x):
    return jnp.maximum(x, 0.0) + jnp.log1p(jnp.exp(-jnp.abs(x)))


def _silu(x):
    return x * _sigmoid(x)


def _embed_body(x_ref, meta_ref, h_hbm, zero_ref, sem, *, seq, tp, rows):
    b = pl.program_id(0)
    i = pl.program_id(1)
    base = b * tp
    tail = tp - OFF - seq
    frames = pltpu.make_async_copy(x_ref, h_hbm.at[pl.ds(base + OFF + i * rows, rows)], sem.at[0])
    frames.start()

    @pl.when(i == 0)
    def _():
        zero_ref[...] = jnp.zeros_like(zero_ref)
        copies = [
            pltpu.make_async_copy(meta_ref, h_hbm.at[pl.ds(base + FRONT, N_META)], sem.at[1]),
            pltpu.make_async_copy(zero_ref.at[pl.ds(0, FRONT)], h_hbm.at[pl.ds(base, FRONT)], sem.at[2]),
        ]
        if tail:
            copies.append(pltpu.make_async_copy(zero_ref.at[pl.ds(0, tail)],
                                                h_hbm.at[pl.ds(base + OFF + seq, tail)], sem.at[3]))
        for c in copies:
            c.start()
        for c in copies:
            c.wait()

    frames.wait()


def _embed(x2d, meta_tokens, bsz, seq, tp):
    d = x2d.shape[1]
    rows = _pick(seq, (1024, 512, 256, 128, 64, 32, 16, 8))
    per_seq = seq // rows
    return pl.pallas_call(
        functools.partial(_embed_body, seq=seq, tp=tp, rows=rows),
        out_shape=jax.ShapeDtypeStruct((bsz * tp, d), F32),
        grid=(bsz, per_seq),
        in_specs=[pl.BlockSpec((rows, d), lambda b, i: (b * per_seq + i, 0)),
                  pl.BlockSpec((N_META, d), lambda b, i: (0, 0))],
        out_specs=pl.BlockSpec(memory_space=pl.ANY),
        scratch_shapes=[pltpu.VMEM((max(FRONT, tp - OFF - seq), d), F32), pltpu.SemaphoreType.DMA((4,))],
        compiler_params=_cparams(("arbitrary", "arbitrary")),
        name="embed",
    )(x2d, meta_tokens)


def _norm_inproj_body(h_ref, g_ref, w_ref, wst_ref, wsc_ref, o_ref, ot_ref, oc_ref, xn_ref):
    @pl.when(pl.program_id(1) == 0)
    def _():
        x = h_ref[...]
        ms = jnp.mean(x * x, axis=-1, keepdims=True)
        xn = (x * lax.rsqrt(ms + EPS) * g_ref[...]).astype(BF16)
        xn_ref[...] = xn
        ot_ref[...] = lax.dot_general(wst_ref[...], xn, (((1,), (1,)), ((), ())),
                                      preferred_element_type=F32)
        oc_ref[...] = jnp.dot(xn, wsc_ref[...], preferred_element_type=F32)

    o_ref[...] = jnp.dot(xn_ref[...], w_ref[...], preferred_element_type=F32).astype(o_ref.dtype)


def _norm_inproj(h, g, w_main, w_small_t, w_small_c, bm, bn):
    n, d = h.shape
    nw = w_main.shape[1]
    return pl.pallas_call(
        _norm_inproj_body,
        out_shape=[jax.ShapeDtypeStruct((n, nw), BF16),
                   jax.ShapeDtypeStruct((16, n), F32),
                   jax.ShapeDtypeStruct((n, LANES), F32)],
        grid=(n // bm, nw // bn),
        in_specs=[pl.BlockSpec((bm, d), lambda i, j: (i, 0)),
                  pl.BlockSpec((1, d), lambda i, j: (0, 0)),
                  pl.BlockSpec((d, bn), lambda i, j: (0, j)),
                  pl.BlockSpec((16, d), lambda i, j: (0, 0)),
                  pl.BlockSpec((d, LANES), lambda i, j: (0, 0))],
        out_specs=[pl.BlockSpec((bm, bn), lambda i, j: (i, j)),
                   pl.BlockSpec((16, bm), lambda i, j: (0, i)),
                   pl.BlockSpec((bm, LANES), lambda i, j: (i, 0))],
        scratch_shapes=[pltpu.VMEM((bm, d), BF16)],
        compiler_params=_cparams(("arbitrary", "arbitrary")),
        name="norm_inproj",
    )(h, g, w_main, w_small_t, w_small_c)


def _split3(x):
    p1 = x.astype(BF16)
    r1 = x - p1.astype(F32)
    p2 = r1.astype(BF16)
    p3 = (r1 - p2.astype(F32)).astype(BF16)
    return p1, p2, p3


def _fox_prep_body(f_ref, fb_ref, ke_ref, carry_ref, *, rows, n_real):
    i = pl.program_id(1)

    @pl.when(i == 0)
    def _():
        carry_ref[...] = jnp.zeros_like(carry_ref)

    f = f_ref[...] + fb_ref[...]
    lf = jnp.minimum(f, 0.0) - jnp.log1p(jnp.exp(-jnp.abs(f)))
    pos = i * rows + lax.broadcasted_iota(I32, (rows, 1), 0)
    lf = jnp.where((pos >= FRONT) & (pos < FRONT + n_real), lf, 0.0)
    rr = lax.broadcasted_iota(I32, (rows, rows), 0)
    cc = lax.broadcasted_iota(I32, (rows, rows), 1)
    tri = jnp.where(rr >= cc, 1.0, 0.0).astype(BF16)
    c = carry_ref[0:1, :]
    for piece in _split3(lf):
        c = c + jnp.dot(tri, piece, preferred_element_type=F32)
    carry_ref[...] = jnp.broadcast_to(c[rows - 1:rows, :], carry_ref.shape)
    cs = jnp.where(pos < FRONT, NEG, -LOG2E * c)
    p1, p2, p3 = (p.astype(F32) for p in _split3(cs))
    lane = lax.broadcasted_iota(I32, (rows, LANES), 1)
    for h in range(N_HEADS):
        ext = jnp.where(lane == 0, p1[:, h:h + 1],
                        jnp.where(lane == 1, p2[:, h:h + 1], jnp.where(lane == 2, p3[:, h:h + 1], 0.0)))
        ke_ref[:, h * HEAD_DIM:(h + 1) * HEAD_DIM] = ext.astype(BF16)


def _fox_prep(f_slab, forget_b, bsz, tp, n_real):
    n = f_slab.shape[0]
    hw = N_HEADS * HEAD_DIM
    rows = UNIT
    nt = tp // rows
    fb = jnp.pad(forget_b.astype(F32), (0, LANES - N_HEADS)).reshape(1, LANES)
    return pl.pallas_call(
        functools.partial(_fox_prep_body, rows=rows, n_real=n_real),
        out_shape=jax.ShapeDtypeStruct((n, hw), BF16),
        grid=(bsz, nt),
        in_specs=[pl.BlockSpec((rows, LANES), lambda b, i: (b * nt + i, 0)),
                  pl.BlockSpec((1, LANES), lambda b, i: (0, 0))],
        out_specs=pl.BlockSpec((rows, hw), lambda b, i: (b * nt + i, 0)),
        scratch_shapes=[pltpu.VMEM((8, LANES), F32)],
        compiler_params=_cparams(("arbitrary", "arbitrary")),
        name="fox_prep",
    )(f_slab, fb)


def _fox_attn_body(q_ref, k_ref, ke_ref, v_ref, ve_ref, o_ref, *, blk, n_split, group):
    qi = pl.program_id(2)
    half = blk // n_split
    lane = lax.broadcasted_iota(I32, (blk, HEAD_DIM), 1)
    q = jnp.concatenate([q_ref[...], jnp.where(lane < N_BIAS_PIECES, 1.0, 0.0).astype(BF16)], axis=1)
    qs = tuple(q[r * half:(r + 1) * half] for r in range(n_split))

    def step(js, carry, diag_last):
        starts = [pl.multiple_of(j * blk, blk) for j in js]
        ss = []
        for r, qh in enumerate(qs):
            srow = []
            for b, st in enumerate(starts):
                on_diag = diag_last and b == len(js) - 1
                ncol = (r + 1) * half if on_diag else blk
                k = jnp.concatenate([k_ref[pl.ds(st, ncol), :], ke_ref[pl.ds(st, ncol), :]], axis=1)
                s = lax.dot_general(qh, k, (((1,), (1,)), ((), ())), preferred_element_type=F32)
                if on_diag:
                    row = r * half + lax.broadcasted_iota(I32, (half, ncol), 0)
                    col = lax.broadcasted_iota(I32, (half, ncol), 1)
                    s = jnp.where(col <= row, s, NEG)
                srow.append((s, st, ncol))
            ss.append(srow)
        out = []
        for r in range(n_split):
            m = carry[2 * r]
            m_new = m
            for s, _, _ in ss[r]:
                m_new = jnp.maximum(m_new, jnp.max(s, axis=1, keepdims=True))
            acc = jnp.exp2(m - m_new) * carry[2 * r + 1]
            for s, st, ncol in ss[r]:
                v = jnp.concatenate([v_ref[pl.ds(st, ncol), :], ve_ref[0:ncol, :]], axis=1)
                acc = acc + jnp.dot(jnp.exp2(s - m_new).astype(BF16), v, preferred_element_type=F32)
            out += [m_new, acc]
        return tuple(out)

    init = (jnp.full((half, 1), NEG, F32), jnp.zeros((half, 2 * HEAD_DIM), F32)) * n_split
    carry = lax.fori_loop(0, qi // group,
                          lambda j, c: step(tuple(group * j + g for g in range(group)), c, False), init)
    tails = [functools.partial(lambda r, c: step(tuple(qi - r + g for g in range(r + 1)), c, True), r)
             for r in range(group)]
    res = lax.switch(qi % group, tails, carry)
    for r in range(n_split):
        acc = res[2 * r + 1]
        o_ref[r * half:(r + 1) * half, :] = (acc[:, 0:HEAD_DIM] / acc[:, HEAD_DIM:HEAD_DIM + 1]
                                             ).astype(o_ref.dtype)


def _fox_attn(z, ke, bsz, tp):
    n = z.shape[0]
    blk = _pick(tp, (768, 512, 256))
    nq = tp // blk
    v_ext = jnp.zeros((blk, HEAD_DIM), BF16).at[:, 0].set(1.0)
    return pl.pallas_call(
        functools.partial(_fox_attn_body, blk=blk, n_split=2, group=4),
        out_shape=jax.ShapeDtypeStruct((n, N_HEADS * HEAD_DIM), BF16),
        grid=(bsz, N_HEADS, nq),
        in_specs=[pl.BlockSpec((blk, HEAD_DIM), lambda b, h, i: (b * nq + i, h)),
                  pl.BlockSpec((tp, HEAD_DIM), lambda b, h, i: (b, N_HEADS + h)),
                  pl.BlockSpec((tp, HEAD_DIM), lambda b, h, i: (b, h)),
                  pl.BlockSpec((tp, HEAD_DIM), lambda b, h, i: (b, 2 * N_HEADS + h)),
                  pl.BlockSpec((blk, HEAD_DIM), lambda b, h, i: (0, 0))],
        out_specs=pl.BlockSpec((blk, HEAD_DIM), lambda b, h, i: (b * nq + i, h)),
        compiler_params=_cparams(("arbitrary", "arbitrary", "arbitrary")),
        name="fox_attn",
    )(z, z, ke, z, v_ext)


def _shifted_taps(x_ext, halo, rows, n_taps):
    for r in range(min(8, n_taps)):
        rolled = x_ext if r == 0 else pltpu.roll(x_ext, r, 0)
        for q in range(halo // 8):
            s = 8 * q + r
            if s > n_taps - 1:
                continue
            yield n_taps - 1 - s, rolled[halo - 8 * q:halo - 8 * q + rows]


def _conformer_body(a_ref, b_ref, w_ref, cb_ref, lg_ref, lb_ref, o_ref, halo_ref, *, bt):
    halo = halo_ref.shape[0]

    @pl.when(pl.program_id(1) == 0)
    def _():
        halo_ref[...] = jnp.zeros_like(halo_ref)

    u = a_ref[...].astype(F32) * _sigmoid(b_ref[...].astype(F32))
    x_ext = jnp.concatenate([halo_ref[...], u], axis=0)
    halo_ref[...] = u[bt - halo:bt]
    acc = jnp.zeros_like(u)
    for j, tap in _shifted_taps(x_ext, halo, bt, CONF_K):
        acc = acc + w_ref[j:j + 1, :] * tap
    y = acc + cb_ref[...]
    mu = jnp.mean(y, axis=-1, keepdims=True)
    yc = y - mu
    var = jnp.mean(yc * yc, axis=-1, keepdims=True)
    yn = yc * lax.rsqrt(var + EPS) * lg_ref[...] + lb_ref[...]
    o_ref[...] = _silu(yn).astype(o_ref.dtype)


def _conformer(z, conv_w, conv_b, ln_g, ln_b, bsz, tp, col_a, col_b):
    n = z.shape[0]
    c = conv_w.shape[1]
    bt = UNIT
    nt = tp // bt
    w = jnp.pad(conv_w.astype(F32), ((0, 32 - CONF_K), (0, 0)))
    row = lambda v: v.astype(F32).reshape(1, c)
    return pl.pallas_call(
        functools.partial(_conformer_body, bt=bt),
        out_shape=jax.ShapeDtypeStruct((n, c), BF16),
        grid=(bsz, nt),
        in_specs=[pl.BlockSpec((bt, c), lambda b, i: (b * nt + i, col_a)),
                  pl.BlockSpec((bt, c), lambda b, i: (b * nt + i, col_b)),
                  pl.BlockSpec((32, c), lambda b, i: (0, 0)),
                  pl.BlockSpec((1, c), lambda b, i: (0, 0)),
                  pl.BlockSpec((1, c), lambda b, i: (0, 0)),
                  pl.BlockSpec((1, c), lambda b, i: (0, 0))],
        out_specs=pl.BlockSpec((bt, c), lambda b, i: (b * nt + i, 0)),
        scratch_shapes=[pltpu.VMEM((32, c), F32)],
        compiler_params=_cparams(("arbitrary", "arbitrary")),
        name="conformer_conv",
    )(z, z, w, row(conv_b), row(ln_g), row(ln_b))


def _outproj_router_body(ya_ref, yb_ref, h_ref, w_ref, g_ref, wr_ref, br_ref, ho_ref, r_ref, cnt_ref,
                         xp_ref, er_ref, *, bm, blocks_per_seq, n_real):
    kh = ya_ref.shape[1]
    pos0 = (pl.program_id(0) % blocks_per_seq) * bm
    rh = bm // 2
    parts = [slice(r * rh, (r + 1) * rh) for r in range(2)]
    ys = [jnp.dot(ya_ref[p, :], w_ref[0:kh, :], preferred_element_type=F32)
          + jnp.dot(yb_ref[p, :], w_ref[kh:2 * kh, :], preferred_element_type=F32) for p in parts]
    lgs = []
    for r, (p, y) in enumerate(zip(parts, ys)):
        hn = jnp.where(_row_valid(pos0 + r * rh, rh, n_real), h_ref[p, :] + y, 0.0)
        ho_ref[p, :] = hn
        ms = jnp.mean(hn * hn, axis=-1, keepdims=True)
        xn = hn * lax.rsqrt(ms + EPS) * g_ref[...]
        xp_ref[p, :] = _pack_pairs(xn)
        xh = xn.astype(BF16)
        xl = (xn - xh.astype(F32)).astype(BF16)
        t = jnp.dot(xh, wr_ref[...], preferred_element_type=F32)
        lgs.append(t[:, 0:LANES] + t[:, LANES:2 * LANES]
                   + jnp.dot(xl, wr_ref[:, 0:LANES], preferred_element_type=F32) + br_ref[...])
    lg = jnp.concatenate(lgs, axis=0)
    lane = lax.broadcasted_iota(I32, lg.shape, 1)
    lane_f = lane.astype(F32)
    big = float(LANES)

    is_grp = (lane >= N_EXPERTS) & (lane < N_EXPERTS + N_GROUPS)
    gl = jnp.where(is_grp, lg, -jnp.inf)
    gmax = jnp.max(gl, axis=1, keepdims=True)
    gidx = jnp.min(jnp.where(gl == gmax, lane_f, big), axis=1, keepdims=True) - N_EXPERTS
    g_prob = 1.0 / jnp.sum(jnp.where(is_grp, jnp.exp(lg - gmax), 0.0), axis=1, keepdims=True)

    lo = gidx * EXPERTS_PER_GROUP
    in_grp = (lane_f >= lo) & (lane_f < lo + EXPERTS_PER_GROUP)
    el = jnp.where(in_grp, lg, -jnp.inf)
    m1 = jnp.max(el, axis=1, keepdims=True)
    i1 = jnp.min(jnp.where(el == m1, lane_f, big), axis=1, keepdims=True)
    el2 = jnp.where(lane_f == i1, -jnp.inf, el)
    m2 = jnp.max(el2, axis=1, keepdims=True)
    i2 = jnp.min(jnp.where(el2 == m2, lane_f, big), axis=1, keepdims=True)
    e2 = jnp.exp(m2 - m1)
    w1 = g_prob / (1.0 + e2)
    w2 = g_prob * e2 / (1.0 + e2)

    @pl.when(pl.program_id(0) == 0)
    def _():
        cnt_ref[...] = jnp.zeros_like(cnt_ref)

    oh1 = lane_f == i1
    oh2 = lane_f == i2
    onehots = jnp.concatenate([jnp.where(oh1, 1.0, 0.0), jnp.where(oh2, 1.0, 0.0)], axis=1).astype(BF16)
    rr = lax.broadcasted_iota(I32, (bm, bm), 0)
    cc = lax.broadcasted_iota(I32, (bm, bm), 1)
    tri = jnp.where(rr >= cc, 1.0, 0.0).astype(BF16)
    csum = jnp.dot(tri, onehots, preferred_element_type=F32)
    c1 = csum[:, 0:LANES]
    c2 = csum[:, LANES:2 * LANES]
    tot1 = c1[bm - 1:bm, :]
    tot2 = c2[bm - 1:bm, :]
    before = cnt_ref[0:1, :]
    rank1 = jnp.sum(jnp.where(oh1, before + c1 - 1.0, 0.0), axis=1, keepdims=True)
    rank2 = jnp.sum(jnp.where(oh2, before + tot1 + c2 - 1.0, 0.0), axis=1, keepdims=True)
    cnt_ref[...] = jnp.broadcast_to(before + tot1 + tot2, cnt_ref.shape)

    r_ref[...] = jnp.where(lane == 0, i1, jnp.where(lane == 1, i2,
                           jnp.where(lane == 2, w1, jnp.where(lane == 3, w2, 0.0))))
    eye = rr == cc
    er_ref[0] = jnp.concatenate(
        [jnp.sum(jnp.where(eye, col, 0.0), axis=0, keepdims=True) for col in (i1, i2, rank1, rank2)], axis=0)


def _outproj_router(ya, yb, h, w_out, g_ffn, w_router, b_router, tp, n_real):
    n, d = h.shape
    half = ya.shape[1]
    bm = UNIT
    return pl.pallas_call(
        functools.partial(_outproj_router_body, bm=bm, blocks_per_seq=tp // bm, n_real=n_real),
        out_shape=[jax.ShapeDtypeStruct((n, d), F32), jax.ShapeDtypeStruct((n, LANES), F32),
                   jax.ShapeDtypeStruct((8, LANES), F32), jax.ShapeDtypeStruct((n, d // 2), U32),
                   jax.ShapeDtypeStruct((n // bm, 4, bm), F32)],
        grid=(n // bm,),
        in_specs=[pl.BlockSpec((bm, half), lambda i: (i, 0)),
                  pl.BlockSpec((bm, half), lambda i: (i, 0)),
                  pl.BlockSpec((bm, d), lambda i: (i, 0)),
                  pl.BlockSpec((2 * half, d), lambda i: (0, 0)),
                  pl.BlockSpec((1, d), lambda i: (0, 0)),
                  pl.BlockSpec((d, 2 * LANES), lambda i: (0, 0)),
                  pl.BlockSpec((1, LANES), lambda i: (0, 0))],
        out_specs=[pl.BlockSpec((bm, d), lambda i: (i, 0)),
                   pl.BlockSpec((bm, LANES), lambda i: (i, 0)),
                   pl.BlockSpec((8, LANES), lambda i: (0, 0)),
                   pl.BlockSpec((bm, d // 2), lambda i: (i, 0)),
                   pl.BlockSpec((1, 4, bm), lambda i: (i, 0, 0))],
        compiler_params=_cparams(("arbitrary",)),
        name="outproj_router",
    )(ya, yb, h, w_out, g_ffn, w_router, b_router)


def _pack_pairs(x):
    c = x.shape[1] // 2
    hi = lax.bitcast_convert_type(x[:, :c].astype(BF16).astype(F32), U32)
    lo = lax.bitcast_convert_type(x[:, c:].astype(BF16).astype(F32), U32)
    return hi | (lo >> 16)


def _unpack_pairs(p):
    hi = lax.bitcast_convert_type(p & jnp.uint32(0xFFFF0000), F32)
    lo = lax.bitcast_convert_type(p << 16, F32)
    return hi, lo


def _dispatch_body(meta_ref, dest_ref, xin_ref, xs_hbm, xp_ref, zero_ref, sem, zsem, *, tb):
    i = pl.program_id(0)
    slot = i % 2

    def drain(s):
        for _ in range(2):
            pltpu.make_async_copy(zero_ref, xs_hbm.at[pl.ds(0, tb)], sem.at[s]).wait()

    for g in range(tb // 8):
        xp_ref[slot, g] = xin_ref[g * 8:(g + 1) * 8, :]

    def issue(g, _):
        for u in range(8):
            for k in range(2):
                d = dest_ref[0, 0, k * tb + g * 8 + u]
                pltpu.make_async_copy(xp_ref.at[slot, g, pl.ds(u, 1)], xs_hbm.at[pl.ds(d, 1)],
                                      sem.at[slot]).start()
        return 0

    lax.fori_loop(0, tb // 8, issue, 0)

    @pl.when(i > 0)
    def _():
        drain(1 - slot)

    @pl.when(i == pl.num_programs(0) - 1)
    def _():
        drain(slot)

    @pl.when(i == 0)
    def _():
        zero_ref[...] = jnp.zeros_like(zero_ref)

        def per_expert(e, _):
            cnt = meta_ref[e]
            start = meta_ref[N_EXPERTS + e]
            padded = meta_ref[2 * N_EXPERTS + e]

            def pad_copy(r):
                return pltpu.make_async_copy(zero_ref.at[pl.ds(0, 1)], xs_hbm.at[pl.ds(start + r, 1)], zsem)

            def zissue(r, _):
                pad_copy(r).start()
                return 0

            def zwait(r, _):
                pad_copy(r).wait()
                return 0

            lax.fori_loop(cnt, padded, zissue, 0)
            lax.fori_loop(cnt, padded, zwait, 0)
            return 0

        lax.fori_loop(0, N_EXPERTS, per_expert, 0)

        def tail_copy(blk):
            return pltpu.make_async_copy(zero_ref, xs_hbm.at[pl.ds(blk * MOE_BM, MOE_BM)], zsem)

        def tissue(blk, _):
            tail_copy(blk).start()
            return 0

        def twait(blk, _):
            tail_copy(blk).wait()
            return 0

        n_used = meta_ref[3 * N_EXPERTS]
        lax.fori_loop(n_used, xs_hbm.shape[0] // MOE_BM, tissue, 0)
        lax.fori_loop(n_used, xs_hbm.shape[0] // MOE_BM, twait, 0)


def _dispatch(xp, dest3, meta, n_rows):
    n, dp = xp.shape
    tb = UNIT
    return pl.pallas_call(
        functools.partial(_dispatch_body, tb=tb),
        out_shape=jax.ShapeDtypeStruct((n_rows, dp), U32),
        grid_spec=pltpu.PrefetchScalarGridSpec(
            num_scalar_prefetch=1,
            grid=(n // tb,),
            in_specs=[pl.BlockSpec((1, 1, 2 * tb), lambda i, m: (i, 0, 0), memory_space=pltpu.SMEM),
                      pl.BlockSpec((tb, dp), lambda i, m: (i, 0))],
            out_specs=pl.BlockSpec(memory_space=pl.ANY),
            scratch_shapes=[pltpu.VMEM((2, tb // 8, 8, dp), U32), pltpu.VMEM((MOE_BM, dp), U32),
                            pltpu.SemaphoreType.DMA((2,)), pltpu.SemaphoreType.DMA(())]),
        compiler_params=_cparams(("arbitrary",)),
        name="moe_dispatch",
    )(meta, dest3, xp)


def _experts_body(be_ref, nxt_ref, nu_ref, xs_ref, wg_hbm, wu_hbm, wd_hbm, ys_ref,
                  wg_st, wu_st, wd_st, wg_bf, wu_bf, wd_bf, ord_ref, sem, *, layer):
    i = pl.program_id(0)
    n_used = nu_ref[0]

    def weight_copies(e, s):
        return (pltpu.make_async_copy(wg_hbm.at[layer, e], wg_st.at[s], sem.at[s, 0]),
                pltpu.make_async_copy(wu_hbm.at[layer, e], wu_st.at[s], sem.at[s, 1]),
                pltpu.make_async_copy(wd_hbm.at[layer, e], wd_st.at[s], sem.at[s, 2]))

    @pl.when(i == 0)
    def _():
        ord_ref[0] = 0
        for c in weight_copies(be_ref[0], 0):
            c.start()

    @pl.when(i < n_used)
    def _():
        e = be_ref[i]
        prev = be_ref[jnp.maximum(i - 1, 0)]

        @pl.when((i == 0) | (e != prev))
        def _():
            s = ord_ref[0] % 2
            for c in weight_copies(e, s):
                c.wait()
            nxt = nxt_ref[i]

            @pl.when(nxt != e)
            def _():
                for c in weight_copies(nxt, 1 - s):
                    c.start()

            wg_bf[...] = wg_st[s].astype(BF16)
            wu_bf[...] = wu_st[s].astype(BF16)
            wd_bf[...] = wd_st[s].astype(BF16)
            ord_ref[0] = ord_ref[0] + 1

        half = MOE_BM // 2
        kh = wg_bf.shape[0] // 2
        xns = []
        for r in range(2):
            hi, lo = _unpack_pairs(xs_ref[r * half:(r + 1) * half, :])
            xns.append((hi.astype(BF16), lo.astype(BF16)))

        def up(w_bf):
            return [jnp.dot(xa, w_bf[0:kh, :], preferred_element_type=F32)
                    + jnp.dot(xb, w_bf[kh:2 * kh, :], preferred_element_type=F32) for xa, xb in xns]

        hgs = up(wg_bf)
        hus = up(wu_bf)
        hids = [(_silu(hg) * hu).astype(BF16) for hg, hu in zip(hgs, hus)]
        for r in range(2):
            ys_ref[r * half:(r + 1) * half, :] = _pack_pairs(
                jnp.dot(hids[r], wd_bf[...], preferred_element_type=F32))

    @pl.when(i >= n_used)
    def _():
        ys_ref[...] = jnp.zeros_like(ys_ref)


def _experts(xs, w_gate, w_up, w_down, layer, block_expert, next_expert, n_used):
    r, dp = xs.shape
    d = 2 * dp
    de = w_gate.shape[3]
    nb = r // MOE_BM
    return pl.pallas_call(
        functools.partial(_experts_body, layer=layer),
        out_shape=jax.ShapeDtypeStruct((r, dp), U32),
        grid_spec=pltpu.PrefetchScalarGridSpec(
            num_scalar_prefetch=3,
            grid=(nb,),
            in_specs=[pl.BlockSpec((MOE_BM, dp), lambda i, be, nx, nu: (i, 0)),
                      pl.BlockSpec(memory_space=pl.ANY),
                      pl.BlockSpec(memory_space=pl.ANY),
                      pl.BlockSpec(memory_space=pl.ANY)],
            out_specs=pl.BlockSpec((MOE_BM, dp), lambda i, be, nx, nu: (i, 0)),
            scratch_shapes=[pltpu.VMEM((2, d, de), F32), pltpu.VMEM((2, d, de), F32),
                            pltpu.VMEM((2, de, d), F32),
                            pltpu.VMEM((d, de), BF16), pltpu.VMEM((d, de), BF16), pltpu.VMEM((de, d), BF16),
                            pltpu.SMEM((1,), I32), pltpu.SemaphoreType.DMA((2, 3))]),
        compiler_params=_cparams(("arbitrary",)),
        name="moe_experts",
    )(block_expert, next_expert, n_used, xs, w_gate, w_up, w_down)


def _combine_body(dest_ref, dnext_ref, h_ref, r_ref, gf_ref, ys_hbm, o_ref, buf, sem, *out_stage,
                  tb, blocks_per_seq, n_real, final_norm):
    i = pl.program_id(0)
    slot = i % 2

    def gather(d_ref, s):
        def issue(t, _):
            for k in range(2):
                d = d_ref[0, 0, k * tb + t]
                pltpu.make_async_copy(ys_hbm.at[pl.ds(d, 1)], buf.at[s, k, pl.ds(t, 1)], sem.at[s]).start()
            return 0

        lax.fori_loop(0, tb, issue, 0, unroll=8)

    @pl.when(i == 0)
    def _():
        gather(dest_ref, 0)

    @pl.when(i + 1 < pl.num_programs(0))
    def _():
        gather(dnext_ref, 1 - slot)

    for k in range(2):
        pltpu.make_async_copy(ys_hbm.at[pl.ds(0, tb)], buf.at[slot, k], sem.at[slot]).wait()

    r = r_ref[...]
    w1 = r[:, 2:3]
    w2 = r[:, 3:4]
    y1 = _unpack_pairs(buf[slot, 0])
    y2 = _unpack_pairs(buf[slot, 1])
    y = jnp.concatenate([w1 * y1[0] + w2 * y2[0], w1 * y1[1] + w2 * y2[1]], axis=1)
    pos0 = (i % blocks_per_seq) * tb
    hn = jnp.where(_row_valid(pos0, tb, n_real), h_ref[...] + y, 0.0)
    if not final_norm:
        o_ref[...] = hn
        return

    obuf, osem = out_stage
    seq = n_real - N_META
    ms = jnp.mean(hn * hn, axis=-1, keepdims=True)
    obuf[slot] = hn * lax.rsqrt(ms + EPS) * gf_ref[...]

    def frame_copy(step, s, start):
        b = step // blocks_per_seq
        j = step % blocks_per_seq
        row0 = b * seq + j * tb - OFF
        tail = OFF + seq - (blocks_per_seq - 1) * tb

        def go(src, dst):
            cp = pltpu.make_async_copy(src, dst, osem.at[s])
            cp.start() if start else cp.wait()

        @pl.when(j == 0)
        def _():
            go(obuf.at[s, pl.ds(OFF, tb - OFF)], o_ref.at[pl.ds(b * seq, tb - OFF)])

        @pl.when((j > 0) & (j < blocks_per_seq - 1))
        def _():
            go(obuf.at[s], o_ref.at[pl.ds(row0, tb)])

        @pl.when(j == blocks_per_seq - 1)
        def _():
            go(obuf.at[s, pl.ds(0, tail)], o_ref.at[pl.ds(row0, tail)])

    frame_copy(i, slot, True)

    @pl.when(i > 0)
    def _():
        frame_copy(i - 1, 1 - slot, False)

    @pl.when(i == pl.num_programs(0) - 1)
    def _():
        frame_copy(i, slot, False)


def _combine(h, route, dest3, ys, g_final, tp, n_real, final_norm):
    n, d = h.shape
    tb = UNIT
    last = n // tb - 1
    assert tp // tb >= 2
    if final_norm:
        out_shape = jax.ShapeDtypeStruct((n // tp * (n_real - N_META), d), F32)
        out_spec = pl.BlockSpec(memory_space=pl.ANY)
        out_stage = [pltpu.VMEM((2, tb, d), F32), pltpu.SemaphoreType.DMA((2,))]
    else:
        out_shape = jax.ShapeDtypeStruct((n, d), F32)
        out_spec = pl.BlockSpec((tb, d), lambda i: (i, 0))
        out_stage = []
    return pl.pallas_call(
        functools.partial(_combine_body, tb=tb, blocks_per_seq=tp // tb, n_real=n_real,
                          final_norm=final_norm),
        out_shape=out_shape,
        grid=(n // tb,),
        in_specs=[pl.BlockSpec((1, 1, 2 * tb), lambda i: (i, 0, 0), memory_space=pltpu.SMEM),
                  pl.BlockSpec((1, 1, 2 * tb), lambda i: (jnp.minimum(i + 1, last), 0, 0),
                               memory_space=pltpu.SMEM),
                  pl.BlockSpec((tb, d), lambda i: (i, 0)),
                  pl.BlockSpec((tb, LANES), lambda i: (i, 0)),
                  pl.BlockSpec((1, d), lambda i: (0, 0)),
                  pl.BlockSpec(memory_space=pl.ANY)],
        out_specs=out_spec,
        scratch_shapes=[pltpu.VMEM((2, 2, tb, d // 2), U32), pltpu.SemaphoreType.DMA((2,))] + out_stage,
        compiler_params=_cparams(("arbitrary",)),
        name="moe_combine",
    )(dest3, dest3, h, route, g_final, ys)


def _moe(h, xp, route, counts_f, er, w_gate, w_up, w_down, layer, g_final, tp, n_real, final_norm):
    n, d = h.shape
    n_pairs = 2 * n
    counts = counts_f[0, 0:N_EXPERTS].astype(I32)
    padded = (counts + MOE_BM - 1) // MOE_BM * MOE_BM
    p_ends = jnp.cumsum(padded)
    p_starts = p_ends - padded
    e_t = er[:, 0:2, :].astype(I32)
    start_of = jnp.sum(jnp.where(e_t[..., None] == jnp.arange(N_EXPERTS, dtype=I32), p_starts, 0), axis=-1)
    dest = start_of + er[:, 2:4, :].astype(I32)
    n_blocks = -(-(n_pairs + N_EXPERTS * (MOE_BM - 1)) // MOE_BM)
    block_row0 = jnp.arange(n_blocks, dtype=I32) * MOE_BM
    block_expert = jnp.minimum(jnp.sum((p_ends[None, :] <= block_row0[:, None]).astype(I32), axis=1),
                               N_EXPERTS - 1)
    n_used = (p_ends[-1:] // MOE_BM).astype(I32)
    meta = jnp.concatenate([counts, p_starts, padded, n_used]).astype(I32)
    dest3 = dest.reshape(n // UNIT, 1, 2 * UNIT)

    xs = _dispatch(xp, dest3, meta, n_blocks * MOE_BM)
    ids = jnp.arange(N_EXPERTS, dtype=I32)
    owner = jnp.where(counts > 0, ids, N_EXPERTS)
    first_from = lax.cummin(owner[::-1])[::-1]
    after = jnp.concatenate([first_from[1:], jnp.full((1,), N_EXPERTS, I32)])
    next_expert = jnp.take(jnp.where(after < N_EXPERTS, after, ids), block_expert)
    ys = _experts(xs, w_gate, w_up, w_down, layer, block_expert, next_expert, n_used)
    return _combine(h, route, dest3, ys, g_final, tp, n_real, final_norm)


def _short_conv(x, halo_ref, w_ref, first):
    rows = x.shape[0]

    @pl.when(first)
    def _():
        halo_ref[...] = jnp.zeros_like(halo_ref)

    x_ext = jnp.concatenate([halo_ref[...], x], axis=0)
    halo_ref[...] = x[rows - 8:rows]
    acc = jnp.zeros_like(x)
    for j, tap in _shifted_taps(x_ext, 8, rows, SHORT_K):
        acc = acc + w_ref[j:j + 1, :] * tap
    return acc


def _bdot(a, b):
    return jnp.dot(a.astype(BF16), b.astype(BF16), preferred_element_type=F32)


def _gdn_pre_body(z_ref, sc_ref, st_ref, cw_ref, al_r_ref, dt_r_ref, al_c_ref, dt_c_ref,
                  u_ref, w_ref, qd_ref, kd_ref, at_ref, eg_ref, halo_ref):
    rows = z_ref.shape[0]
    hw = N_HEADS * HEAD_DIM
    x = _silu(_short_conv(z_ref[...].astype(F32), halo_ref, cw_ref, pl.program_id(1) == 0))

    sc = sc_ref[...]
    g_cols = -jnp.exp(al_r_ref[...]) * _softplus(sc + dt_r_ref[...])
    beta_cols = _sigmoid(sc)
    st = st_ref[...]
    g_rows = -jnp.exp(al_c_ref[:, 0:1]) * _softplus(st + dt_c_ref[:, 0:1])

    ri = lax.broadcasted_iota(I32, (rows, rows), 0)
    ci = lax.broadcasted_iota(I32, (rows, rows), 1)
    same64 = (ri >> 6) == (ci >> 6)
    same32 = (ri >> 5) == (ci >> 5)
    same16 = (ri >> 4) == (ci >> 4)
    lower = ri >= ci
    strict = ri > ci
    lane = lax.broadcasted_iota(I32, (rows, LANES), 1)
    eg_slab = jnp.zeros((rows, LANES), F32)
    scale = HEAD_DIM ** -0.5

    heads = range(N_HEADS)
    in_chunk_lower = same64 & lower
    in_chunk_upper = same64 & (ri <= ci)
    a_mats, nmats, rhss = [], [], []
    for h in heads:
        sl = slice(h * HEAD_DIM, (h + 1) * HEAD_DIM)
        q = x[:, sl]
        k = x[:, hw + h * HEAD_DIM:hw + (h + 1) * HEAD_DIM]
        v = x[:, 2 * hw + h * HEAD_DIM:2 * hw + (h + 1) * HEAD_DIM]
        q = q * lax.rsqrt(jnp.sum(q * q, axis=-1, keepdims=True) + EPS)
        k = k * lax.rsqrt(jnp.sum(k * k, axis=-1, keepdims=True) + EPS)
        g_col = g_cols[:, h:h + 1]
        beta = beta_cols[:, N_HEADS + h:N_HEADS + h + 1]
        g_row = g_rows[h:h + 1, :]

        gc_col = jnp.sum(jnp.where(in_chunk_lower, g_row, 0.0), axis=1, keepdims=True)
        gc_row = jnp.sum(jnp.where(in_chunk_upper, g_col, 0.0), axis=0, keepdims=True)
        gtot_col = jnp.sum(jnp.where(same64, g_row, 0.0), axis=1, keepdims=True)
        decay = jnp.exp(jnp.where(in_chunk_lower, gc_col - gc_row, NEG))
        eg = jnp.exp(gc_col)
        ekd = jnp.exp(gtot_col - gc_col)

        kb = k * beta
        kbf = k.astype(BF16)
        kk = lax.dot_general(kb.astype(BF16), kbf, (((1,), (1,)), ((), ())), preferred_element_type=F32)
        qs = q * scale
        qk = lax.dot_general(qs.astype(BF16), kbf, (((1,), (1,)), ((), ())), preferred_element_type=F32)
        attn = qk * decay
        a_mats.append(jnp.where(strict, kk * decay, 0.0))
        rhss.append(jnp.concatenate([v * beta, kb * eg], axis=1))
        qd_ref[:, sl] = (qs * eg).astype(BF16)
        kd_ref[:, sl] = (k * ekd).astype(BF16)
        for c in range(rows // CHUNK):
            cs = slice(c * CHUNK, (c + 1) * CHUNK)
            at_ref[h, cs, :] = attn[cs, cs].astype(BF16)
        eg_slab = jnp.where(lane == h, eg, eg_slab)
    eg_ref[...] = eg_slab

    nmats = [jnp.where(same16, -a, 0.0) for a in a_mats]
    qqs = nmats
    for _ in range(3):
        qqs = [_bdot(qq, qq) for qq in qqs]
        prods = [_bdot(qq, nm) for qq, nm in zip(qqs, nmats)]
        nmats = [nm + qq + pr for nm, qq, pr in zip(nmats, qqs, prods)]
    for level_mask, inner_mask in ((same32, same16), (same64, same32)):
        sel = level_mask & jnp.logical_not(inner_mask)
        offs = [jnp.where(sel, a, 0.0) for a in a_mats]
        bmats = [off + _bdot(nm, off) for nm, off in zip(nmats, offs)]
        prods = [_bdot(bm, nm) for bm, nm in zip(bmats, nmats)]
        nmats = [nm - bm - pr for nm, bm, pr in zip(nmats, bmats, prods)]

    for h in heads:
        sl = slice(h * HEAD_DIM, (h + 1) * HEAD_DIM)
        uw = rhss[h] + _bdot(nmats[h], rhss[h])
        u_ref[:, sl] = uw[:, 0:HEAD_DIM]
        w_ref[:, sl] = uw[:, HEAD_DIM:2 * HEAD_DIM].astype(BF16)


def _gdn_pre(z, sc, st, conv_w, a_log, dt_bias, bsz, tp):
    n = z.shape[0]
    hw = N_HEADS * HEAD_DIM
    rows = UNIT
    nu = tp // rows
    pad_r = lambda v: jnp.pad(v.astype(F32), (0, LANES - N_HEADS)).reshape(1, LANES)
    pad_c = lambda v: jnp.broadcast_to(jnp.pad(v.astype(F32), (0, 16 - N_HEADS))[:, None], (16, LANES))
    outs = pl.pallas_call(
        _gdn_pre_body,
        out_shape=[jax.ShapeDtypeStruct((n, hw), F32),
                   jax.ShapeDtypeStruct((n, hw), BF16),
                   jax.ShapeDtypeStruct((n, hw), BF16),
                   jax.ShapeDtypeStruct((n, hw), BF16),
                   jax.ShapeDtypeStruct((N_HEADS, n, CHUNK), BF16),
                   jax.ShapeDtypeStruct((n, LANES), F32)],
        grid=(bsz, nu),
        in_specs=[pl.BlockSpec((rows, 3 * hw), lambda b, i: (b * nu + i, 0)),
                  pl.BlockSpec((rows, LANES), lambda b, i: (b * nu + i, 0)),
                  pl.BlockSpec((16, rows), lambda b, i: (0, b * nu + i)),
                  pl.BlockSpec((8, 3 * hw), lambda b, i: (0, 0)),
                  pl.BlockSpec((1, LANES), lambda b, i: (0, 0)),
                  pl.BlockSpec((1, LANES), lambda b, i: (0, 0)),
                  pl.BlockSpec((16, LANES), lambda b, i: (0, 0)),
                  pl.BlockSpec((16, LANES), lambda b, i: (0, 0))],
        out_specs=[pl.BlockSpec((rows, hw), lambda b, i: (b * nu + i, 0)),
                   pl.BlockSpec((rows, hw), lambda b, i: (b * nu + i, 0)),
                   pl.BlockSpec((rows, hw), lambda b, i: (b * nu + i, 0)),
                   pl.BlockSpec((rows, hw), lambda b, i: (b * nu + i, 0)),
                   pl.BlockSpec((N_HEADS, rows, CHUNK), lambda b, i: (0, b * nu + i, 0)),
                   pl.BlockSpec((rows, LANES), lambda b, i: (b * nu + i, 0))],
        scratch_shapes=[pltpu.VMEM((8, 3 * hw), F32)],
        compiler_params=_cparams(("arbitrary", "arbitrary")),
        name="gdn_pre",
    )(z, sc, st, jnp.pad(conv_w.astype(F32), ((0, 8 - SHORT_K), (0, 0))),
      pad_r(a_log), pad_r(dt_bias), pad_c(a_log), pad_c(dt_bias))
    return outs


def _gdn_scan_body(u_ref, w_ref, qd_ref, kd_ref, at_ref, eg_ref, z_ref, gn_ref, o_ref, s_ref, *, bsz):
    @pl.when(pl.program_id(0) == 0)
    def _():
        s_ref[...] = jnp.zeros_like(s_ref)

    units = [(b, h, slice(h * HEAD_DIM, (h + 1) * HEAD_DIM)) for b in range(bsz) for h in range(N_HEADS)]
    states = [s_ref[b * N_HEADS + h] for b, h, _ in units]
    for c in range(u_ref.shape[1] // CHUNK):
        rows = slice(c * CHUNK, (c + 1) * CHUNK)
        rs = [jnp.dot(jnp.concatenate([w_ref[b, rows, sl], qd_ref[b, rows, sl]], axis=0), s.astype(BF16),
                      preferred_element_type=F32) for (b, h, sl), s in zip(units, states)]
        vbs = [(u_ref[b, rows, sl] - r[0:CHUNK]).astype(BF16) for (b, h, sl), r in zip(units, rs)]
        intra = [jnp.dot(at_ref[h, b, rows, :], vb, preferred_element_type=F32)
                 for (b, h, sl), vb in zip(units, vbs)]
        outer = [lax.dot_general(kd_ref[b, rows, sl], vb, (((0,), (0,)), ((), ())),
                                 preferred_element_type=F32) for (b, h, sl), vb in zip(units, vbs)]
        last = (c + 1) * CHUNK - 1
        states = [s * eg_ref[b, last:last + 1, h:h + 1] + kv for (b, h, sl), s, kv in zip(units, states, outer)]
        for (b, h, sl), r, a in zip(units, rs, intra):
            o = r[CHUNK:2 * CHUNK] + a
            on = o * lax.rsqrt(jnp.mean(o * o, axis=-1, keepdims=True) + EPS) * gn_ref[...]
            o_ref[b, rows, sl] = (on * _silu(z_ref[b, rows, sl].astype(F32))).astype(o_ref.dtype)
    for (b, h, sl), s in zip(units, states):
        s_ref[b * N_HEADS + h] = s


def _gdn_scan(u, w, qd, kd, attn, eg, z, out_norm_g, bsz, tp, z_col):
    hw = N_HEADS * HEAD_DIM
    rows = UNIT
    nc = tp // rows
    v3 = lambda a: a.reshape(bsz, tp, a.shape[-1])
    blk3 = pl.BlockSpec((bsz, rows, hw), lambda c: (0, c, 0))
    return pl.pallas_call(
        functools.partial(_gdn_scan_body, bsz=bsz),
        out_shape=jax.ShapeDtypeStruct((bsz, tp, hw), BF16),
        grid=(nc,),
        in_specs=[blk3, blk3, blk3, blk3,
                  pl.BlockSpec((N_HEADS, bsz, rows, CHUNK), lambda c: (0, 0, c, 0)),
                  pl.BlockSpec((bsz, rows, LANES), lambda c: (0, c, 0)),
                  pl.BlockSpec((bsz, rows, hw), lambda c: (0, c, z_col)),
                  pl.BlockSpec((1, HEAD_DIM), lambda c: (0, 0))],
        out_specs=blk3,
        scratch_shapes=[pltpu.VMEM((bsz * N_HEADS, HEAD_DIM, HEAD_DIM), F32)],
        compiler_params=_cparams(("arbitrary",)),
        name="gdn_scan",
    )(v3(u), v3(w), v3(qd), v3(kd), attn.reshape(N_HEADS, bsz, tp, CHUNK), v3(eg), v3(z),
      out_norm_g.astype(F32).reshape(1, HEAD_DIM)).reshape(bsz * tp, hw)


def _lru_body(x_ref, gate_ref, cw_ref, cb_ref, wr_ref, br_ref, wi_ref, bi_ref, lam_ref, o_ref,
              halo_ref, hc_ref, *, bt, n_real):
    first = pl.program_id(1) == 0

    @pl.when(first)
    def _():
        hc_ref[...] = jnp.zeros_like(hc_ref)

    x = _short_conv(x_ref[...].astype(F32), halo_ref, cw_ref, first) + cb_ref[...]
    nblk = wr_ref.shape[0]
    bd = wr_ref.shape[1]
    xb = x.astype(BF16)
    rg = jnp.concatenate([jnp.dot(xb[:, n * bd:(n + 1) * bd], wr_ref[n], preferred_element_type=F32)
                          for n in range(nblk)], axis=1)
    ig = jnp.concatenate([jnp.dot(xb[:, n * bd:(n + 1) * bd], wi_ref[n], preferred_element_type=F32)
                          for n in range(nblk)], axis=1)
    r = _sigmoid(rg + br_ref[...])
    ig = _sigmoid(ig + bi_ref[...])
    log_a = (-LRU_C * _softplus(-lam_ref[...])) * r
    a = jnp.exp(log_a)
    th = jnp.tanh(log_a)
    b = jnp.sqrt(-2.0 * th / (1.0 - th)) * (ig * x)
    b = jnp.where(_row_valid(pl.program_id(1) * bt, bt, n_real), b, 0.0)

    sub = lax.broadcasted_iota(I32, (bt, 1), 0) & 7
    d = 1
    while d < 8:
        keep = sub >= d
        a_sh = jnp.where(keep, pltpu.roll(a, d, 0), 1.0)
        b_sh = jnp.where(keep, pltpu.roll(b, d, 0), 0.0)
        b = a * b_sh + b
        a = a * a_sh
        d *= 2
    carry = hc_ref[0:1, :]
    tiles = []
    for g in range(bt // 8):
        tiles.append(a[8 * g:8 * g + 8] * carry + b[8 * g:8 * g + 8])
        carry = tiles[-1][7:8, :]
    hc_ref[...] = jnp.broadcast_to(carry, hc_ref.shape)
    hs = jnp.concatenate(tiles, axis=0)
    o_ref[...] = (hs * jax.nn.gelu(gate_ref[...].astype(F32))).astype(o_ref.dtype)


def _lru(z, conv_w, conv_b, w_rg, b_rg, w_ig, b_ig, lam, bsz, tp, n_real, x_col, gate_col):
    n = z.shape[0]
    c = conv_w.shape[1]
    bt = UNIT
    nt = tp // bt
    row = lambda v: v.astype(F32).reshape(1, c)
    wspec = pl.BlockSpec(w_rg.shape, lambda b, i: (0, 0, 0))
    vspec = pl.BlockSpec((1, c), lambda b, i: (0, 0))
    return pl.pallas_call(
        functools.partial(_lru_body, bt=bt, n_real=n_real),
        out_shape=jax.ShapeDtypeStruct((n, c), BF16),
        grid=(bsz, nt),
        in_specs=[pl.BlockSpec((bt, c), lambda b, i: (b * nt + i, x_col)),
                  pl.BlockSpec((bt, c), lambda b, i: (b * nt + i, gate_col)),
                  pl.BlockSpec((8, c), lambda b, i: (0, 0)),
                  vspec, wspec, vspec, wspec, vspec, vspec],
        out_specs=pl.BlockSpec((bt, c), lambda b, i: (b * nt + i, 0)),
        scratch_shapes=[pltpu.VMEM((8, c), F32), pltpu.VMEM((8, c), F32)],
        compiler_params=_cparams(("arbitrary", "arbitrary")),
        name="rg_lru",
    )(z, z, jnp.pad(conv_w.astype(F32), ((0, 8 - SHORT_K), (0, 0))), row(conv_b),
      w_rg.astype(BF16), row(b_rg), w_ig.astype(BF16), row(b_ig), row(lam))


def _pick(n, candidates):
    for c in candidates:
        if n % c == 0:
            return c
    raise ValueError(f"no block size in {candidates} divides {n}")


def _pack_weights_body(a_ref, b_ref, o_ref, *, scaled_tiles, scale, head_tiles, gap):
    j = pl.program_id(0)
    tile = o_ref.shape[1]

    @pl.when(j < scaled_tiles)
    def _():
        o_ref[...] = (a_ref[0] * scale).astype(BF16)

    @pl.when((j >= scaled_tiles) & (j < head_tiles))
    def _():
        o_ref[...] = a_ref[0].astype(BF16)

    @pl.when(j >= head_tiles)
    def _():
        x = jnp.concatenate([a_ref[0], b_ref[0]], axis=1)
        o_ref[...] = x[:, gap:gap + tile].astype(BF16)


def _pack_inproj_weights(w_all, idx, head, gap, scaled=0, scale=1.0):
    _, d, total = w_all.shape
    tile = 512
    n_out = total - gap
    return pl.pallas_call(
        functools.partial(_pack_weights_body, scaled_tiles=scaled // tile, scale=scale,
                          head_tiles=head // tile, gap=gap),
        out_shape=jax.ShapeDtypeStruct((d, n_out), BF16),
        grid=(n_out // tile,),
        in_specs=[pl.BlockSpec((1, d, tile), lambda j: (idx, 0, j)),
                  pl.BlockSpec((1, d, LANES), lambda j: (idx, 0, (tile // LANES) * (j + 1)))],
        out_specs=pl.BlockSpec((d, tile), lambda j: (0, j)),
        compiler_params=_cparams(("arbitrary",)),
        name="pack_inproj_weights",
    )(w_all, w_all)


def _small_weights(cols):
    k = cols.shape[1]
    return (jnp.pad(cols.T, ((0, 16 - k), (0, 0))).astype(BF16),
            jnp.pad(cols, ((0, 0), (0, LANES - k))).astype(BF16))


def _router_weights(w_group, b_group, w_expert, b_expert):
    w = jnp.concatenate([w_expert, w_group], axis=1).astype(F32)
    b = jnp.concatenate([b_expert, b_group]).astype(F32)
    k = w.shape[1]
    w = jnp.pad(w, ((0, 0), (0, LANES - k)))
    w_hi = w.astype(BF16)
    w_lo = (w - w_hi.astype(F32)).astype(BF16)
    return jnp.concatenate([w_hi, w_lo], axis=1), jnp.pad(b, (0, LANES - k)).reshape(1, LANES)


def kernel(x, meta_tokens, norm_mix_g, norm_ffn_g, norm_final_g, ab_w_in, ab_forget_b, ab_conv_w, ab_conv_b, ab_ln_g, ab_ln_b, ab_w_out, cd_w_in, cd_qkv_conv_w, cd_a_log, cd_dt_bias, cd_out_norm_g, cd_lru_conv_w, cd_lru_conv_b, cd_w_rg, cd_b_rg, cd_w_ig, cd_b_ig, cd_lru_lambda, cd_w_out, moe_w_group, moe_b_group, moe_w_expert, moe_b_expert, moe_w_gate, moe_w_up, moe_w_down):
    bsz, seq, d = x.shape
    depth = norm_mix_g.shape[0]
    n_real = N_META + seq
    tp = -(-(OFF + seq) // UNIT) * UNIT
    n = bsz * tp
    hw = N_HEADS * HEAD_DIM
    bm = _pick(n, (768, 512, 256))
    bn = 1024

    h = _embed(x.astype(F32).reshape(bsz * seq, d), meta_tokens.astype(F32), bsz, seq, tp)
    row = lambda v: v.astype(F32).reshape(1, -1)

    for layer in range(depth):
        i = layer // 2
        g_mix = row(norm_mix_g[layer])
        if layer % 2 == 0:
            w_in = ab_w_in[i]
            w_main = _pack_inproj_weights(ab_w_in.astype(F32), i, 3 * hw, N_HEADS, scaled=hw,
                                          scale=LOG2E * HEAD_DIM ** -0.5)
            wst, wsc = _small_weights(w_in[:, 3 * hw:3 * hw + N_HEADS])
            z, _, f_slab = _norm_inproj(h, g_mix, w_main, wst, wsc, bm, bn)
            ke = _fox_prep(f_slab, ab_forget_b[i], bsz, tp, n_real)
            ya = _fox_attn(z, ke, bsz, tp)
            yb = _conformer(z, ab_conv_w[i], ab_conv_b[i], ab_ln_g[i], ab_ln_b[i], bsz, tp, 3, 4)
            w_out = ab_w_out[i].astype(BF16)
        else:
            w_in = cd_w_in[i]
            w_main = _pack_inproj_weights(cd_w_in.astype(F32), i, 3 * hw, 2 * N_HEADS)
            wst, wsc = _small_weights(w_in[:, 3 * hw:3 * hw + 2 * N_HEADS])
            z, zt, zc = _norm_inproj(h, g_mix, w_main, wst, wsc, bm, bn)
            u, w, qd, kd, attn, eg = _gdn_pre(z, zc, zt, cd_qkv_conv_w[i], cd_a_log[i], cd_dt_bias[i],
                                              bsz, tp)
            ya = _gdn_scan(u, w, qd, kd, attn, eg, z, cd_out_norm_g[i], bsz, tp, 3)
            yb = _lru(z, cd_lru_conv_w[i], cd_lru_conv_b[i], cd_w_rg[i], cd_b_rg[i], cd_w_ig[i],
                      cd_b_ig[i], cd_lru_lambda[i], bsz, tp, n_real, 4, 5)
            w_out = cd_w_out[i].astype(BF16)
        g_ffn = row(norm_ffn_g[layer])
        w_r, b_r = _router_weights(moe_w_group[layer], moe_b_group[layer],
                                   moe_w_expert[layer], moe_b_expert[layer])
        h, route, counts, xp, er = _outproj_router(ya, yb, h, w_out, g_ffn, w_r, b_r, tp, n_real)
        h = _moe(h, xp, route, counts, er, moe_w_gate, moe_w_up, moe_w_down, layer,
                 row(norm_final_g), tp, n_real, final_norm=(layer == depth - 1))
    return h.reshape(bsz, seq, d).astype(x.dtype)
```

```python
import functools

import jax
import jax.numpy as jnp
from jax import lax
from jax.experimental import pallas as pl
from jax.experimental.pallas import tpu as pltpu

F32 = jnp.float32
BF16 = jnp.bfloat16
I32 = jnp.int32
U32 = jnp.uint32

EPS = 1e-6
N_META = 16
CHUNK = 64
FRONT = CHUNK - N_META
OFF = FRONT + N_META
LANES = 128
UNIT = 256
HEAD_DIM = 128
N_HEADS = 8
CONF_K = 31
SHORT_K = 4
LRU_C = 8.0
N_GROUPS = 4
EXPERTS_PER_GROUP = 8
N_EXPERTS = N_GROUPS * EXPERTS_PER_GROUP
MOE_BM = 256
NEG = -1e30
LOG2E = 1.4426950408889634
N_BIAS_PIECES = 3
VMEM_LIMIT = 56 * 1024 * 1024


def _cparams(sem):
    return pltpu.CompilerParams(dimension_semantics=sem, vmem_limit_bytes=VMEM_LIMIT)


def _row_valid(pos0, rows, n_real):
    pos = pos0 + lax.broadcasted_iota(I32, (rows, 1), 0)
    return (pos >= FRONT) & (pos < FRONT + n_real)


def _sigmoid(x):
    return 1.0 / (1.0 + jnp.exp(-x))


def _softplus(x):
    return jnp.maximum(x, 0.0) + jnp.log1p(jnp.exp(-jnp.abs(x)))


def _silu(x):
    return x * _sigmoid(x)


def _embed_body(x_ref, meta_ref, h_hbm, zero_ref, sem, *, seq, tp, rows):
    b = pl.program_id(0)
    i = pl.program_id(1)
    base = b * tp
    tail = tp - OFF - seq
    frames = pltpu.make_async_copy(x_ref, h_hbm.at[pl.ds(base + OFF + i * rows, rows)], sem.at[0])
    frames.start()

    @pl.when(i == 0)
    def _():
        zero_ref[...] = jnp.zeros_like(zero_ref)
        copies = [
            pltpu.make_async_copy(meta_ref, h_hbm.at[pl.ds(base + FRONT, N_META)], sem.at[1]),
            pltpu.make_async_copy(zero_ref.at[pl.ds(0, FRONT)], h_hbm.at[pl.ds(base, FRONT)], sem.at[2]),
        ]
        if tail:
            copies.append(pltpu.make_async_copy(zero_ref.at[pl.ds(0, tail)],
                                                h_hbm.at[pl.ds(base + OFF + seq, tail)], sem.at[3]))
        for c in copies:
            c.start()
        for c in copies:
            c.wait()

    frames.wait()


def _embed(x2d, meta_tokens, bsz, seq, tp):
    d = x2d.shape[1]
    rows = _pick(seq, (1024, 512, 256, 128, 64, 32, 16, 8))
    per_seq = seq // rows
    return pl.pallas_call(
        functools.partial(_embed_body, seq=seq, tp=tp, rows=rows),
        out_shape=jax.ShapeDtypeStruct((bsz * tp, d), F32),
        grid=(bsz, per_seq),
        in_specs=[pl.BlockSpec((rows, d), lambda b, i: (b * per_seq + i, 0)),
                  pl.BlockSpec((N_META, d), lambda b, i: (0, 0))],
        out_specs=pl.BlockSpec(memory_space=pl.ANY),
        scratch_shapes=[pltpu.VMEM((max(FRONT, tp - OFF - seq), d), F32), pltpu.SemaphoreType.DMA((4,))],
        compiler_params=_cparams(("arbitrary", "arbitrary")),
        name="embed",
    )(x2d, meta_tokens)


def _norm_inproj_body(h_ref, g_ref, w_ref, wst_ref, wsc_ref, o_ref, ot_ref, oc_ref, xn_ref):
    @pl.when(pl.program_id(1) == 0)
    def _():
        x = h_ref[...]
        ms = jnp.mean(x * x, axis=-1, keepdims=True)
        xn = (x * lax.rsqrt(ms + EPS) * g_ref[...]).astype(BF16)
        xn_ref[...] = xn
        ot_ref[...] = lax.dot_general(wst_ref[...], xn, (((1,), (1,)), ((), ())),
                                      preferred_element_type=F32)
        oc_ref[...] = jnp.dot(xn, wsc_ref[...], preferred_element_type=F32)

    o_ref[...] = jnp.dot(xn_ref[...], w_ref[...], preferred_element_type=F32).astype(o_ref.dtype)


def _norm_inproj(h, g, w_main, w_small_t, w_small_c, bm, bn):
    n, d = h.shape
    nw = w_main.shape[1]
    return pl.pallas_call(
        _norm_inproj_body,
        out_shape=[jax.ShapeDtypeStruct((n, nw), BF16),
                   jax.ShapeDtypeStruct((16, n), F32),
                   jax.ShapeDtypeStruct((n, LANES), F32)],
        grid=(n // bm, nw // bn),
        in_specs=[pl.BlockSpec((bm, d), lambda i, j: (i, 0)),
                  pl.BlockSpec((1, d), lambda i, j: (0, 0)),
                  pl.BlockSpec((d, bn), lambda i, j: (0, j)),
                  pl.BlockSpec((16, d), lambda i, j: (0, 0)),
                  pl.BlockSpec((d, LANES), lambda i, j: (0, 0))],
        out_specs=[pl.BlockSpec((bm, bn), lambda i, j: (i, j)),
                   pl.BlockSpec((16, bm), lambda i, j: (0, i)),
                   pl.BlockSpec((bm, LANES), lambda i, j: (i, 0))],
        scratch_shapes=[pltpu.VMEM((bm, d), BF16)],
        compiler_params=_cparams(("arbitrary", "arbitrary")),
        name="norm_inproj",
    )(h, g, w_main, w_small_t, w_small_c)


def _split3(x):
    p1 = x.astype(BF16)
    r1 = x - p1.astype(F32)
    p2 = r1.astype(BF16)
    p3 = (r1 - p2.astype(F32)).astype(BF16)
    return p1, p2, p3


def _fox_prep_body(f_ref, fb_ref, ke_ref, carry_ref, *, rows, n_real):
    i = pl.program_id(1)

    @pl.when(i == 0)
    def _():
        carry_ref[...] = jnp.zeros_like(carry_ref)

    f = f_ref[...] + fb_ref[...]
    lf = jnp.minimum(f, 0.0) - jnp.log1p(jnp.exp(-jnp.abs(f)))
    pos = i * rows + lax.broadcasted_iota(I32, (rows, 1), 0)
    lf = jnp.where((pos >= FRONT) & (pos < FRONT + n_real), lf, 0.0)
    rr = lax.broadcasted_iota(I32, (rows, rows), 0)
    cc = lax.broadcasted_iota(I32, (rows, rows), 1)
    tri = jnp.where(rr >= cc, 1.0, 0.0).astype(BF16)
    c = carry_ref[0:1, :]
    for piece in _split3(lf):
        c = c + jnp.dot(tri, piece, preferred_element_type=F32)
    carry_ref[...] = jnp.broadcast_to(c[rows - 1:rows, :], carry_ref.shape)
    cs = jnp.where(pos < FRONT, NEG, -LOG2E * c)
    p1, p2, p3 = (p.astype(F32) for p in _split3(cs))
    lane = lax.broadcasted_iota(I32, (rows, LANES), 1)
    for h in range(N_HEADS):
        ext = jnp.where(lane == 0, p1[:, h:h + 1],
                        jnp.where(lane == 1, p2[:, h:h + 1], jnp.where(lane == 2, p3[:, h:h + 1], 0.0)))
        ke_ref[:, h * HEAD_DIM:(h + 1) * HEAD_DIM] = ext.astype(BF16)


def _fox_prep(f_slab, forget_b, bsz, tp, n_real):
    n = f_slab.shape[0]
    hw = N_HEADS * HEAD_DIM
    rows = UNIT
    nt = tp // rows
    fb = jnp.pad(forget_b.astype(F32), (0, LANES - N_HEADS)).reshape(1, LANES)
    return pl.pallas_call(
        functools.partial(_fox_prep_body, rows=rows, n_real=n_real),
        out_shape=jax.ShapeDtypeStruct((n, hw), BF16),
        grid=(bsz, nt),
        in_specs=[pl.BlockSpec((rows, LANES), lambda b, i: (b * nt + i, 0)),
                  pl.BlockSpec((1, LANES), lambda b, i: (0, 0))],
        out_specs=pl.BlockSpec((rows, hw), lambda b, i: (b * nt + i, 0)),
        scratch_shapes=[pltpu.VMEM((8, LANES), F32)],
        compiler_params=_cparams(("arbitrary", "arbitrary")),
        name="fox_prep",
    )(f_slab, fb)


def _fox_attn_body(q_ref, k_ref, ke_ref, v_ref, ve_ref, o_ref, *, blk, n_split, group):
    qi = pl.program_id(2)
    half = blk // n_split
    lane = lax.broadcasted_iota(I32, (blk, HEAD_DIM), 1)
    q = jnp.concatenate([q_ref[...], jnp.where(lane < N_BIAS_PIECES, 1.0, 0.0).astype(BF16)], axis=1)
    qs = tuple(q[r * half:(r + 1) * half] for r in range(n_split))

    def step(js, carry, diag_last):
        starts = [pl.multiple_of(j * blk, blk) for j in js]
        ss = []
        for r, qh in enumerate(qs):
            srow = []
            for b, st in enumerate(starts):
                on_diag = diag_last and b == len(js) - 1
                ncol = (r + 1) * half if on_diag else blk
                k = jnp.concatenate([k_ref[pl.ds(st, ncol), :], ke_ref[pl.ds(st, ncol), :]], axis=1)
                s = lax.dot_general(qh, k, (((1,), (1,)), ((), ())), preferred_element_type=F32)
                if on_diag:
                    row = r * half + lax.broadcasted_iota(I32, (half, ncol), 0)
                    col = lax.broadcasted_iota(I32, (half, ncol), 1)
                    s = jnp.where(col <= row, s, NEG)
                srow.append((s, st, ncol))
            ss.append(srow)
        out = []
        for r in range(n_split):
            m = carry[2 * r]
            m_new = m
            for s, _, _ in ss[r]:
                m_new = jnp.maximum(m_new, jnp.max(s, axis=1, keepdims=True))
            acc = jnp.exp2(m - m_new) * carry[2 * r + 1]
            for s, st, ncol in ss[r]:
                v = jnp.concatenate([v_ref[pl.ds(st, ncol), :], ve_ref[0:ncol, :]], axis=1)
                acc = acc + jnp.dot(jnp.exp2(s - m_new).astype(BF16), v, preferred_element_type=F32)
            out += [m_new, acc]
        return tuple(out)

    init = (jnp.full((half, 1), NEG, F32), jnp.zeros((half, 2 * HEAD_DIM), F32)) * n_split
    carry = lax.fori_loop(0, qi // group,
                          lambda j, c: step(tuple(group * j + g for g in range(group)), c, False), init)
    tails = [functools.partial(lambda r, c: step(tuple(qi - r + g for g in range(r + 1)), c, True), r)
             for r in range(group)]
    res = lax.switch(qi % group, tails, carry)
    for r in range(n_split):
        acc = res[2 * r + 1]
        o_ref[r * half:(r + 1) * half, :] = (acc[:, 0:HEAD_DIM] / acc[:, HEAD_DIM:HEAD_DIM + 1]
                                             ).astype(o_ref.dtype)


def _fox_attn(z, ke, bsz, tp):
    n = z.shape[0]
    blk = _pick(tp, (768, 512, 256))
    nq = tp // blk
    v_ext = jnp.zeros((blk, HEAD_DIM), BF16).at[:, 0].set(1.0)
    return pl.pallas_call(
        functools.partial(_fox_attn_body, blk=blk, n_split=2, group=4),
        out_shape=jax.ShapeDtypeStruct((n, N_HEADS * HEAD_DIM), BF16),
        grid=(bsz, N_HEADS, nq),
        in_specs=[pl.BlockSpec((blk, HEAD_DIM), lambda b, h, i: (b * nq + i, h)),
                  pl.BlockSpec((tp, HEAD_DIM), lambda b, h, i: (b, N_HEADS + h)),
                  pl.BlockSpec((tp, HEAD_DIM), lambda b, h, i: (b, h)),
                  pl.BlockSpec((tp, HEAD_DIM), lambda b, h, i: (b, 2 * N_HEADS + h)),
                  pl.BlockSpec((blk, HEAD_DIM), lambda b, h, i: (0, 0))],
        out_specs=pl.BlockSpec((blk, HEAD_DIM), lambda b, h, i: (b * nq + i, h)),
        compiler_params=_cparams(("arbitrary", "arbitrary", "arbitrary")),
        name="fox_attn",
    )(z, z, ke, z, v_ext)


def _shifted_taps(x_ext, halo, rows, n_taps):
    for r in range(min(8, n_taps)):
        rolled = x_ext if r == 0 else pltpu.roll(x_ext, r, 0)
        for q in range(halo // 8):
            s = 8 * q + r
            if s > n_taps - 1:
                continue
            yield n_taps - 1 - s, rolled[halo - 8 * q:halo - 8 * q + rows]


def _conformer_body(a_ref, b_ref, w_ref, cb_ref, lg_ref, lb_ref, o_ref, halo_ref, *, bt):
    halo = halo_ref.shape[0]

    @pl.when(pl.program_id(1) == 0)
    def _():
        halo_ref[...] = jnp.zeros_like(halo_ref)

    u = a_ref[...].astype(F32) * _sigmoid(b_ref[...].astype(F32))
    x_ext = jnp.concatenate([halo_ref[...], u], axis=0)
    halo_ref[...] = u[bt - halo:bt]
    acc = jnp.zeros_like(u)
    for j, tap in _shifted_taps(x_ext, halo, bt, CONF_K):
        acc = acc + w_ref[j:j + 1, :] * tap
    y = acc + cb_ref[...]
    mu = jnp.mean(y, axis=-1, keepdims=True)
    yc = y - mu
    var = jnp.mean(yc * yc, axis=-1, keepdims=True)
    yn = yc * lax.rsqrt(var + EPS) * lg_ref[...] + lb_ref[...]
    o_ref[...] = _silu(yn).astype(o_ref.dtype)


def _conformer(z, conv_w, conv_b, ln_g, ln_b, bsz, tp, col_a, col_b):
    n = z.shape[0]
    c = conv_w.shape[1]
    bt = UNIT
    nt = tp // bt
    w = jnp.pad(conv_w.astype(F32), ((0, 32 - CONF_K), (0, 0)))
    row = lambda v: v.astype(F32).reshape(1, c)
    return pl.pallas_call(
        functools.partial(_conformer_body, bt=bt),
        out_shape=jax.ShapeDtypeStruct((n, c), BF16),
        grid=(bsz, nt),
        in_specs=[pl.BlockSpec((bt, c), lambda b, i: (b * nt + i, col_a)),
                  pl.BlockSpec((bt, c), lambda b, i: (b * nt + i, col_b)),
                  pl.BlockSpec((32, c), lambda b, i: (0, 0)),
                  pl.BlockSpec((1, c), lambda b, i: (0, 0)),
                  pl.BlockSpec((1, c), lambda b, i: (0, 0)),
                  pl.BlockSpec((1, c), lambda b, i: (0, 0))],
        out_specs=pl.BlockSpec((bt, c), lambda b, i: (b * nt + i, 0)),
        scratch_shapes=[pltpu.VMEM((32, c), F32)],
        compiler_params=_cparams(("arbitrary", "arbitrary")),
        name="conformer_conv",
    )(z, z, w, row(conv_b), row(ln_g), row(ln_b))


def _outproj_router_body(ya_ref, yb_ref, h_ref, w_ref, g_ref, wr_ref, br_ref, ho_ref, r_ref, cnt_ref,
                         xp_ref, er_ref, *, bm, blocks_per_seq, n_real):
    kh = ya_ref.shape[1]
    pos0 = (pl.program_id(0) % blocks_per_seq) * bm
    rh = bm // 2
    parts = [slice(r * rh, (r + 1) * rh) for r in range(2)]
    ys = [jnp.dot(ya_ref[p, :], w_ref[0:kh, :], preferred_element_type=F32)
          + jnp.dot(yb_ref[p, :], w_ref[kh:2 * kh, :], preferred_element_type=F32) for p in parts]
    lgs = []
    for r, (p, y) in enumerate(zip(parts, ys)):
        hn = jnp.where(_row_valid(pos0 + r * rh, rh, n_real), h_ref[p, :] + y, 0.0)
        ho_ref[p, :] = hn
        ms = jnp.mean(hn * hn, axis=-1, keepdims=True)
        xn = hn * lax.rsqrt(ms + EPS) * g_ref[...]
        xp_ref[p, :] = _pack_pairs(xn)
        xh = xn.astype(BF16)
        xl = (xn - xh.astype(F32)).astype(BF16)
        t = jnp.dot(xh, wr_ref[...], preferred_element_type=F32)
        lgs.append(t[:, 0:LANES] + t[:, LANES:2 * LANES]
                   + jnp.dot(xl, wr_ref[:, 0:LANES], preferred_element_type=F32) + br_ref[...])
    lg = jnp.concatenate(lgs, axis=0)
    lane = lax.broadcasted_iota(I32, lg.shape, 1)
    lane_f = lane.astype(F32)
    big = float(LANES)

    is_grp = (lane >= N_EXPERTS) & (lane < N_EXPERTS + N_GROUPS)
    gl = jnp.where(is_grp, lg, -jnp.inf)
    gmax = jnp.max(gl, axis=1, keepdims=True)
    gidx = jnp.min(jnp.where(gl == gmax, lane_f, big), axis=1, keepdims=True) - N_EXPERTS
    g_prob = 1.0 / jnp.sum(jnp.where(is_grp, jnp.exp(lg - gmax), 0.0), axis=1, keepdims=True)

    lo = gidx * EXPERTS_PER_GROUP
    in_grp = (lane_f >= lo) & (lane_f < lo + EXPERTS_PER_GROUP)
    el = jnp.where(in_grp, lg, -jnp.inf)
    m1 = jnp.max(el, axis=1, keepdims=True)
    i1 = jnp.min(jnp.where(el == m1, lane_f, big), axis=1, keepdims=True)
    el2 = jnp.where(lane_f == i1, -jnp.inf, el)
    m2 = jnp.max(el2, axis=1, keepdims=True)
    i2 = jnp.min(jnp.where(el2 == m2, lane_f, big), axis=1, keepdims=True)
    e2 = jnp.exp(m2 - m1)
    w1 = g_prob / (1.0 + e2)
    w2 = g_prob * e2 / (1.0 + e2)

    @pl.when(pl.program_id(0) == 0)
    def _():
        cnt_ref[...] = jnp.zeros_like(cnt_ref)

    oh1 = lane_f == i1
    oh2 = lane_f == i2
    onehots = jnp.concatenate([jnp.where(oh1, 1.0, 0.0), jnp.where(oh2, 1.0, 0.0)], axis=1).astype(BF16)
    rr = lax.broadcasted_iota(I32, (bm, bm), 0)
    cc = lax.broadcasted_iota(I32, (bm, bm), 1)
    tri = jnp.where(rr >= cc, 1.0, 0.0).astype(BF16)
    csum = jnp.dot(tri, onehots, preferred_element_type=F32)
    c1 = csum[:, 0:LANES]
    c2 = csum[:, LANES:2 * LANES]
    tot1 = c1[bm - 1:bm, :]
    tot2 = c2[bm - 1:bm, :]
    before = cnt_ref[0:1, :]
    rank1 = jnp.sum(jnp.where(oh1, before + c1 - 1.0, 0.0), axis=1, keepdims=True)
    rank2 = jnp.sum(jnp.where(oh2, before + tot1 + c2 - 1.0, 0.0), axis=1, keepdims=True)
    cnt_ref[...] = jnp.broadcast_to(before + tot1 + tot2, cnt_ref.shape)

    r_ref[...] = jnp.where(lane == 0, i1, jnp.where(lane == 1, i2,
                           jnp.where(lane == 2, w1, jnp.where(lane == 3, w2, 0.0))))
    eye = rr == cc
    er_ref[0] = jnp.concatenate(
        [jnp.sum(jnp.where(eye, col, 0.0), axis=0, keepdims=True) for col in (i1, i2, rank1, rank2)], axis=0)


def _outproj_router(ya, yb, h, w_out, g_ffn, w_router, b_router, tp, n_real):
    n, d = h.shape
    half = ya.shape[1]
    bm = UNIT
    return pl.pallas_call(
        functools.partial(_outproj_router_body, bm=bm, blocks_per_seq=tp // bm, n_real=n_real),
        out_shape=[jax.ShapeDtypeStruct((n, d), F32), jax.ShapeDtypeStruct((n, LANES), F32),
                   jax.ShapeDtypeStruct((8, LANES), F32), jax.ShapeDtypeStruct((n, d // 2), U32),
                   jax.ShapeDtypeStruct((n // bm, 4, bm), F32)],
        grid=(n // bm,),
        in_specs=[pl.BlockSpec((bm, half), lambda i: (i, 0)),
                  pl.BlockSpec((bm, half), lambda i: (i, 0)),
                  pl.BlockSpec((bm, d), lambda i: (i, 0)),
                  pl.BlockSpec((2 * half, d), lambda i: (0, 0)),
                  pl.BlockSpec((1, d), lambda i: (0, 0)),
                  pl.BlockSpec((d, 2 * LANES), lambda i: (0, 0)),
                  pl.BlockSpec((1, LANES), lambda i: (0, 0))],
        out_specs=[pl.BlockSpec((bm, d), lambda i: (i, 0)),
                   pl.BlockSpec((bm, LANES), lambda i: (i, 0)),
                   pl.BlockSpec((8, LANES), lambda i: (0, 0)),
                   pl.BlockSpec((bm, d // 2), lambda i: (i, 0)),
                   pl.BlockSpec((1, 4, bm), lambda i: (i, 0, 0))],
        compiler_params=_cparams(("arbitrary",)),
        name="outproj_router",
    )(ya, yb, h, w_out, g_ffn, w_router, b_router)


def _pack_pairs(x):
    c = x.shape[1] // 2
    hi = lax.bitcast_convert_type(x[:, :c].astype(BF16).astype(F32), U32)
    lo = lax.bitcast_convert_type(x[:, c:].astype(BF16).astype(F32), U32)
    return hi | (lo >> 16)


def _unpack_pairs(p):
    hi = lax.bitcast_convert_type(p & jnp.uint32(0xFFFF0000), F32)
    lo = lax.bitcast_convert_type(p << 16, F32)
    return hi, lo


def _dispatch_body(meta_ref, dest_ref, xin_ref, xs_hbm, xp_ref, zero_ref, sem, zsem, *, tb):
    i = pl.program_id(0)
    slot = i % 2

    def drain(s):
        for _ in range(2):
            pltpu.make_async_copy(zero_ref, xs_hbm.at[pl.ds(0, tb)], sem.at[s]).wait()

    for g in range(tb // 8):
        xp_ref[slot, g] = xin_ref[g * 8:(g + 1) * 8, :]

    def issue(g, _):
        for u in range(8):
            for k in range(2):
                d = dest_ref[0, 0, k * tb + g * 8 + u]
                pltpu.make_async_copy(xp_ref.at[slot, g, pl.ds(u, 1)], xs_hbm.at[pl.ds(d, 1)],
                                      sem.at[slot]).start(priority=k)
        return 0

    lax.fori_loop(0, tb // 8, issue, 0)

    @pl.when(i > 0)
    def _():
        drain(1 - slot)

    @pl.when(i == pl.num_programs(0) - 1)
    def _():
        drain(slot)

    @pl.when(i == 0)
    def _():
        zero_ref[...] = jnp.zeros_like(zero_ref)

        def per_expert(e, _):
            cnt = meta_ref[e]
            start = meta_ref[N_EXPERTS + e]
            padded = meta_ref[2 * N_EXPERTS + e]

            def pad_copy(r):
                return pltpu.make_async_copy(zero_ref.at[pl.ds(0, 1)], xs_hbm.at[pl.ds(start + r, 1)], zsem)

            def zissue(r, _):
                pad_copy(r).start()
                return 0

            def zwait(r, _):
                pad_copy(r).wait()
                return 0

            lax.fori_loop(cnt, padded, zissue, 0)
            lax.fori_loop(cnt, padded, zwait, 0)
            return 0

        lax.fori_loop(0, N_EXPERTS, per_expert, 0)

        def tail_copy(blk):
            return pltpu.make_async_copy(zero_ref, xs_hbm.at[pl.ds(blk * MOE_BM, MOE_BM)], zsem)

        def tissue(blk, _):
            tail_copy(blk).start()
            return 0

        def twait(blk, _):
            tail_copy(blk).wait()
            return 0

        n_used = meta_ref[3 * N_EXPERTS]
        lax.fori_loop(n_used, xs_hbm.shape[0] // MOE_BM, tissue, 0)
        lax.fori_loop(n_used, xs_hbm.shape[0] // MOE_BM, twait, 0)


def _dispatch(xp, dest3, meta, n_rows):
    n, dp = xp.shape
    tb = UNIT
    return pl.pallas_call(
        functools.partial(_dispatch_body, tb=tb),
        out_shape=jax.ShapeDtypeStruct((n_rows, dp), U32),
        grid_spec=pltpu.PrefetchScalarGridSpec(
            num_scalar_prefetch=1,
            grid=(n // tb,),
            in_specs=[pl.BlockSpec((1, 1, 2 * tb), lambda i, m: (i, 0, 0), memory_space=pltpu.SMEM),
                      pl.BlockSpec((tb, dp), lambda i, m: (i, 0))],
            out_specs=pl.BlockSpec(memory_space=pl.ANY),
            scratch_shapes=[pltpu.VMEM((2, tb // 8, 8, dp), U32), pltpu.VMEM((MOE_BM, dp), U32),
                            pltpu.SemaphoreType.DMA((2,)), pltpu.SemaphoreType.DMA(())]),
        compiler_params=_cparams(("arbitrary",)),
        name="moe_dispatch",
    )(meta, dest3, xp)


def _experts_body(be_ref, nxt_ref, nu_ref, xs_ref, wg_hbm, wu_hbm, wd_hbm, ys_ref,
                  wg_st, wu_st, wd_st, wg_bf, wu_bf, wd_bf, ord_ref, sem, *, layer):
    i = pl.program_id(0)
    n_used = nu_ref[0]

    def weight_copies(e, s):
        return (pltpu.make_async_copy(wg_hbm.at[layer, e], wg_st.at[s], sem.at[s, 0]),
                pltpu.make_async_copy(wu_hbm.at[layer, e], wu_st.at[s], sem.at[s, 1]),
                pltpu.make_async_copy(wd_hbm.at[layer, e], wd_st.at[s], sem.at[s, 2]))

    @pl.when(i == 0)
    def _():
        ord_ref[0] = 0
        for c in weight_copies(be_ref[0], 0):
            c.start()

    @pl.when(i < n_used)
    def _():
        e = be_ref[i]
        prev = be_ref[jnp.maximum(i - 1, 0)]

        @pl.when((i == 0) | (e != prev))
        def _():
            s = ord_ref[0] % 2
            for c in weight_copies(e, s):
                c.wait()
            nxt = nxt_ref[i]

            @pl.when(nxt != e)
            def _():
                for c in weight_copies(nxt, 1 - s):
                    c.start()

            wg_bf[...] = wg_st[s].astype(BF16)
            wu_bf[...] = wu_st[s].astype(BF16)
            wd_bf[...] = wd_st[s].astype(BF16)
            ord_ref[0] = ord_ref[0] + 1

        half = MOE_BM // 2
        kh = wg_bf.shape[0] // 2
        xns = []
        for r in range(2):
            hi, lo = _unpack_pairs(xs_ref[r * half:(r + 1) * half, :])
            xns.append((hi.astype(BF16), lo.astype(BF16)))

        def up(w_bf):
            return [jnp.dot(xa, w_bf[0:kh, :], preferred_element_type=F32)
                    + jnp.dot(xb, w_bf[kh:2 * kh, :], preferred_element_type=F32) for xa, xb in xns]

        hgs = up(wg_bf)
        hus = up(wu_bf)
        hids = [(_silu(hg) * hu).astype(BF16) for hg, hu in zip(hgs, hus)]
        for r in range(2):
            ys_ref[r * half:(r + 1) * half, :] = _pack_pairs(
                jnp.dot(hids[r], wd_bf[...], preferred_element_type=F32))

    @pl.when(i >= n_used)
    def _():
        ys_ref[...] = jnp.zeros_like(ys_ref)


def _experts(xs, w_gate, w_up, w_down, layer, block_expert, next_expert, n_used):
    r, dp = xs.shape
    d = 2 * dp
    de = w_gate.shape[3]
    nb = r // MOE_BM
    return pl.pallas_call(
        functools.partial(_experts_body, layer=layer),
        out_shape=jax.ShapeDtypeStruct((r, dp), U32),
        grid_spec=pltpu.PrefetchScalarGridSpec(
            num_scalar_prefetch=3,
            grid=(nb,),
            in_specs=[pl.BlockSpec((MOE_BM, dp), lambda i, be, nx, nu: (i, 0)),
                      pl.BlockSpec(memory_space=pl.ANY),
                      pl.BlockSpec(memory_space=pl.ANY),
                      pl.BlockSpec(memory_space=pl.ANY)],
            out_specs=pl.BlockSpec((MOE_BM, dp), lambda i, be, nx, nu: (i, 0)),
            scratch_shapes=[pltpu.VMEM((2, d, de), F32), pltpu.VMEM((2, d, de), F32),
                            pltpu.VMEM((2, de, d), F32),
                            pltpu.VMEM((d, de), BF16), pltpu.VMEM((d, de), BF16), pltpu.VMEM((de, d), BF16),
                            pltpu.SMEM((1,), I32), pltpu.SemaphoreType.DMA((2, 3))]),
        compiler_params=_cparams(("arbitrary",)),
        name="moe_experts",
    )(block_expert, next_expert, n_used, xs, w_gate, w_up, w_down)


def _combine_body(dest_ref, dnext_ref, h_ref, r_ref, gf_ref, ys_hbm, o_ref, buf, sem, *out_stage,
                  tb, blocks_per_seq, n_real, final_norm):
    i = pl.program_id(0)
    slot = i % 2

    def gather(d_ref, s):
        def issue(t, _):
            for k in range(2):
                d = d_ref[0, 0, k * tb + t]
                pltpu.make_async_copy(ys_hbm.at[pl.ds(d, 1)], buf.at[s, k, pl.ds(t, 1)],
                                      sem.at[s]).start(priority=k)
            return 0

        lax.fori_loop(0, tb, issue, 0, unroll=8)

    @pl.when(i == 0)
    def _():
        gather(dest_ref, 0)

    @pl.when(i + 1 < pl.num_programs(0))
    def _():
        gather(dnext_ref, 1 - slot)

    for k in range(2):
        pltpu.make_async_copy(ys_hbm.at[pl.ds(0, tb)], buf.at[slot, k], sem.at[slot]).wait()

    r = r_ref[...]
    w1 = r[:, 2:3]
    w2 = r[:, 3:4]
    y1 = _unpack_pairs(buf[slot, 0])
    y2 = _unpack_pairs(buf[slot, 1])
    y = jnp.concatenate([w1 * y1[0] + w2 * y2[0], w1 * y1[1] + w2 * y2[1]], axis=1)
    pos0 = (i % blocks_per_seq) * tb
    hn = jnp.where(_row_valid(pos0, tb, n_real), h_ref[...] + y, 0.0)
    if not final_norm:
        o_ref[...] = hn
        return

    obuf, osem = out_stage
    seq = n_real - N_META
    ms = jnp.mean(hn * hn, axis=-1, keepdims=True)
    obuf[slot] = hn * lax.rsqrt(ms + EPS) * gf_ref[...]

    def frame_copy(step, s, start):
        b = step // blocks_per_seq
        j = step % blocks_per_seq
        row0 = b * seq + j * tb - OFF
        tail = OFF + seq - (blocks_per_seq - 1) * tb

        def go(src, dst):
            cp = pltpu.make_async_copy(src, dst, osem.at[s])
            cp.start() if start else cp.wait()

        @pl.when(j == 0)
        def _():
            go(obuf.at[s, pl.ds(OFF, tb - OFF)], o_ref.at[pl.ds(b * seq, tb - OFF)])

        @pl.when((j > 0) & (j < blocks_per_seq - 1))
        def _():
            go(obuf.at[s], o_ref.at[pl.ds(row0, tb)])

        @pl.when(j == blocks_per_seq - 1)
        def _():
            go(obuf.at[s, pl.ds(0, tail)], o_ref.at[pl.ds(row0, tail)])

    frame_copy(i, slot, True)

    @pl.when(i > 0)
    def _():
        frame_copy(i - 1, 1 - slot, False)

    @pl.when(i == pl.num_programs(0) - 1)
    def _():
        frame_copy(i, slot, False)


def _combine(h, route, dest3, ys, g_final, tp, n_real, final_norm):
    n, d = h.shape
    tb = UNIT
    last = n // tb - 1
    assert tp // tb >= 2
    if final_norm:
        out_shape = jax.ShapeDtypeStruct((n // tp * (n_real - N_META), d), F32)
        out_spec = pl.BlockSpec(memory_space=pl.ANY)
        out_stage = [pltpu.VMEM((2, tb, d), F32), pltpu.SemaphoreType.DMA((2,))]
    else:
        out_shape = jax.ShapeDtypeStruct((n, d), F32)
        out_spec = pl.BlockSpec((tb, d), lambda i: (i, 0))
        out_stage = []
    return pl.pallas_call(
        functools.partial(_combine_body, tb=tb, blocks_per_seq=tp // tb, n_real=n_real,
                          final_norm=final_norm),
        out_shape=out_shape,
        grid=(n // tb,),
        in_specs=[pl.BlockSpec((1, 1, 2 * tb), lambda i: (i, 0, 0), memory_space=pltpu.SMEM),
                  pl.BlockSpec((1, 1, 2 * tb), lambda i: (jnp.minimum(i + 1, last), 0, 0),
                               memory_space=pltpu.SMEM),
                  pl.BlockSpec((tb, d), lambda i: (i, 0)),
                  pl.BlockSpec((tb, LANES), lambda i: (i, 0)),
                  pl.BlockSpec((1, d), lambda i: (0, 0)),
                  pl.BlockSpec(memory_space=pl.ANY)],
        out_specs=out_spec,
        scratch_shapes=[pltpu.VMEM((2, 2, tb, d // 2), U32), pltpu.SemaphoreType.DMA((2,))] + out_stage,
        compiler_params=_cparams(("arbitrary",)),
        name="moe_combine",
    )(dest3, dest3, h, route, g_final, ys)


def _moe(h, xp, route, counts_f, er, w_gate, w_up, w_down, layer, g_final, tp, n_real, final_norm):
    n, d = h.shape
    n_pairs = 2 * n
    counts = counts_f[0, 0:N_EXPERTS].astype(I32)
    padded = (counts + MOE_BM - 1) // MOE_BM * MOE_BM
    p_ends = jnp.cumsum(padded)
    p_starts = p_ends - padded
    e_t = er[:, 0:2, :].astype(I32)
    start_of = jnp.sum(jnp.where(e_t[..., None] == jnp.arange(N_EXPERTS, dtype=I32), p_starts, 0), axis=-1)
    dest = start_of + er[:, 2:4, :].astype(I32)
    n_blocks = -(-(n_pairs + N_EXPERTS * (MOE_BM - 1)) // MOE_BM)
    block_row0 = jnp.arange(n_blocks, dtype=I32) * MOE_BM
    block_expert = jnp.minimum(jnp.sum((p_ends[None, :] <= block_row0[:, None]).astype(I32), axis=1),
                               N_EXPERTS - 1)
    n_used = (p_ends[-1:] // MOE_BM).astype(I32)
    meta = jnp.concatenate([counts, p_starts, padded, n_used]).astype(I32)
    dest3 = dest.reshape(n // UNIT, 1, 2 * UNIT)

    xs = _dispatch(xp, dest3, meta, n_blocks * MOE_BM)
    ids = jnp.arange(N_EXPERTS, dtype=I32)
    owner = jnp.where(counts > 0, ids, N_EXPERTS)
    first_from = lax.cummin(owner[::-1])[::-1]
    after = jnp.concatenate([first_from[1:], jnp.full((1,), N_EXPERTS, I32)])
    next_expert = jnp.take(jnp.where(after < N_EXPERTS, after, ids), block_expert)
    ys = _experts(xs, w_gate, w_up, w_down, layer, block_expert, next_expert, n_used)
    return _combine(h, route, dest3, ys, g_final, tp, n_real, final_norm)


def _short_conv(x, halo_ref, w_ref, first):
    rows = x.shape[0]

    @pl.when(first)
    def _():
        halo_ref[...] = jnp.zeros_like(halo_ref)

    x_ext = jnp.concatenate([halo_ref[...], x], axis=0)
    halo_ref[...] = x[rows - 8:rows]
    acc = jnp.zeros_like(x)
    for j, tap in _shifted_taps(x_ext, 8, rows, SHORT_K):
        acc = acc + w_ref[j:j + 1, :] * tap
    return acc


def _bdot(a, b):
    return jnp.dot(a.astype(BF16), b.astype(BF16), preferred_element_type=F32)


def _gdn_pre_body(z_ref, sc_ref, st_ref, cw_ref, al_r_ref, dt_r_ref, al_c_ref, dt_c_ref,
                  u_ref, w_ref, qd_ref, kd_ref, at_ref, eg_ref, halo_ref):
    rows = z_ref.shape[0]
    hw = N_HEADS * HEAD_DIM
    x = _silu(_short_conv(z_ref[...].astype(F32), halo_ref, cw_ref, pl.program_id(1) == 0))

    sc = sc_ref[...]
    g_cols = -jnp.exp(al_r_ref[...]) * _softplus(sc + dt_r_ref[...])
    beta_cols = _sigmoid(sc)
    st = st_ref[...]
    g_rows = -jnp.exp(al_c_ref[:, 0:1]) * _softplus(st + dt_c_ref[:, 0:1])

    ri = lax.broadcasted_iota(I32, (rows, rows), 0)
    ci = lax.broadcasted_iota(I32, (rows, rows), 1)
    same64 = (ri >> 6) == (ci >> 6)
    same32 = (ri >> 5) == (ci >> 5)
    same16 = (ri >> 4) == (ci >> 4)
    lower = ri >= ci
    strict = ri > ci
    lane = lax.broadcasted_iota(I32, (rows, LANES), 1)
    eg_slab = jnp.zeros((rows, LANES), F32)
    scale = HEAD_DIM ** -0.5

    heads = range(N_HEADS)
    in_chunk_lower = same64 & lower
    in_chunk_upper = same64 & (ri <= ci)
    a_mats, nmats, rhss = [], [], []
    for h in heads:
        sl = slice(h * HEAD_DIM, (h + 1) * HEAD_DIM)
        q = x[:, sl]
        k = x[:, hw + h * HEAD_DIM:hw + (h + 1) * HEAD_DIM]
        v = x[:, 2 * hw + h * HEAD_DIM:2 * hw + (h + 1) * HEAD_DIM]
        q = q * lax.rsqrt(jnp.sum(q * q, axis=-1, keepdims=True) + EPS)
        k = k * lax.rsqrt(jnp.sum(k * k, axis=-1, keepdims=True) + EPS)
        g_col = g_cols[:, h:h + 1]
        beta = beta_cols[:, N_HEADS + h:N_HEADS + h + 1]
        g_row = g_rows[h:h + 1, :]

        gc_col = jnp.sum(jnp.where(in_chunk_lower, g_row, 0.0), axis=1, keepdims=True)
        gc_row = jnp.sum(jnp.where(in_chunk_upper, g_col, 0.0), axis=0, keepdims=True)
        gtot_col = jnp.sum(jnp.where(same64, g_row, 0.0), axis=1, keepdims=True)
        decay = jnp.exp(jnp.where(in_chunk_lower, gc_col - gc_row, NEG))
        eg = jnp.exp(gc_col)
        ekd = jnp.exp(gtot_col - gc_col)

        kb = k * beta
        kbf = k.astype(BF16)
        kk = lax.dot_general(kb.astype(BF16), kbf, (((1,), (1,)), ((), ())), preferred_element_type=F32)
        qs = q * scale
        qk = lax.dot_general(qs.astype(BF16), kbf, (((1,), (1,)), ((), ())), preferred_element_type=F32)
        attn = qk * decay
        a_mats.append(jnp.where(strict, kk * decay, 0.0))
        rhss.append(jnp.concatenate([v * beta, kb * eg], axis=1))
        qd_ref[:, sl] = (qs * eg).astype(BF16)
        kd_ref[:, sl] = (k * ekd).astype(BF16)
        for c in range(rows // CHUNK):
            cs = slice(c * CHUNK, (c + 1) * CHUNK)
            at_ref[h, cs, :] = attn[cs, cs].astype(BF16)
        eg_slab = jnp.where(lane == h, eg, eg_slab)
    eg_ref[...] = eg_slab

    nmats = [jnp.where(same16, -a, 0.0) for a in a_mats]
    qqs = nmats
    for _ in range(3):
        qqs = [_bdot(qq, qq) for qq in qqs]
        prods = [_bdot(qq, nm) for qq, nm in zip(qqs, nmats)]
        nmats = [nm + qq + pr for nm, qq, pr in zip(nmats, qqs, prods)]
    for level_mask, inner_mask in ((same32, same16), (same64, same32)):
        sel = level_mask & jnp.logical_not(inner_mask)
        offs = [jnp.where(sel, a, 0.0) for a in a_mats]
        bmats = [off + _bdot(nm, off) for nm, off in zip(nmats, offs)]
        prods = [_bdot(bm, nm) for bm, nm in zip(bmats, nmats)]
        nmats = [nm - bm - pr for nm, bm, pr in zip(nmats, bmats, prods)]

    for h in heads:
        sl = slice(h * HEAD_DIM, (h + 1) * HEAD_DIM)
        uw = rhss[h] + _bdot(nmats[h], rhss[h])
        u_ref[:, sl] = uw[:, 0:HEAD_DIM]
        w_ref[:, sl] = uw[:, HEAD_DIM:2 * HEAD_DIM].astype(BF16)


def _gdn_pre(z, sc, st, conv_w, a_log, dt_bias, bsz, tp):
    n = z.shape[0]
    hw = N_HEADS * HEAD_DIM
    rows = UNIT
    nu = tp // rows
    pad_r = lambda v: jnp.pad(v.astype(F32), (0, LANES - N_HEADS)).reshape(1, LANES)
    pad_c = lambda v: jnp.broadcast_to(jnp.pad(v.astype(F32), (0, 16 - N_HEADS))[:, None], (16, LANES))
    outs = pl.pallas_call(
        _gdn_pre_body,
        out_shape=[jax.ShapeDtypeStruct((n, hw), F32),
                   jax.ShapeDtypeStruct((n, hw), BF16),
                   jax.ShapeDtypeStruct((n, hw), BF16),
                   jax.ShapeDtypeStruct((n, hw), BF16),
                   jax.ShapeDtypeStruct((N_HEADS, n, CHUNK), BF16),
                   jax.ShapeDtypeStruct((n, LANES), F32)],
        grid=(bsz, nu),
        in_specs=[pl.BlockSpec((rows, 3 * hw), lambda b, i: (b * nu + i, 0)),
                  pl.BlockSpec((rows, LANES), lambda b, i: (b * nu + i, 0)),
                  pl.BlockSpec((16, rows), lambda b, i: (0, b * nu + i)),
                  pl.BlockSpec((8, 3 * hw), lambda b, i: (0, 0)),
                  pl.BlockSpec((1, LANES), lambda b, i: (0, 0)),
                  pl.BlockSpec((1, LANES), lambda b, i: (0, 0)),
                  pl.BlockSpec((16, LANES), lambda b, i: (0, 0)),
                  pl.BlockSpec((16, LANES), lambda b, i: (0, 0))],
        out_specs=[pl.BlockSpec((rows, hw), lambda b, i: (b * nu + i, 0)),
                   pl.BlockSpec((rows, hw), lambda b, i: (b * nu + i, 0)),
                   pl.BlockSpec((rows, hw), lambda b, i: (b * nu + i, 0)),
                   pl.BlockSpec((rows, hw), lambda b, i: (b * nu + i, 0)),
                   pl.BlockSpec((N_HEADS, rows, CHUNK), lambda b, i: (0, b * nu + i, 0)),
                   pl.BlockSpec((rows, LANES), lambda b, i: (b * nu + i, 0))],
        scratch_shapes=[pltpu.VMEM((8, 3 * hw), F32)],
        compiler_params=_cparams(("arbitrary", "arbitrary")),
        name="gdn_pre",
    )(z, sc, st, jnp.pad(conv_w.astype(F32), ((0, 8 - SHORT_K), (0, 0))),
      pad_r(a_log), pad_r(dt_bias), pad_c(a_log), pad_c(dt_bias))
    return outs


def _gdn_scan_body(u_ref, w_ref, qd_ref, kd_ref, at_ref, eg_ref, z_ref, gn_ref, o_ref, s_ref, *, bsz):
    @pl.when(pl.program_id(0) == 0)
    def _():
        s_ref[...] = jnp.zeros_like(s_ref)

    units = [(b, h, slice(h * HEAD_DIM, (h + 1) * HEAD_DIM)) for b in range(bsz) for h in range(N_HEADS)]
    states = [s_ref[b * N_HEADS + h] for b, h, _ in units]
    for c in range(u_ref.shape[1] // CHUNK):
        rows = slice(c * CHUNK, (c + 1) * CHUNK)
        rs = [jnp.dot(jnp.concatenate([w_ref[b, rows, sl], qd_ref[b, rows, sl]], axis=0), s.astype(BF16),
                      preferred_element_type=F32) for (b, h, sl), s in zip(units, states)]
        vbs = [(u_ref[b, rows, sl] - r[0:CHUNK]).astype(BF16) for (b, h, sl), r in zip(units, rs)]
        intra = [jnp.dot(at_ref[h, b, rows, :], vb, preferred_element_type=F32)
                 for (b, h, sl), vb in zip(units, vbs)]
        outer = [lax.dot_general(kd_ref[b, rows, sl], vb, (((0,), (0,)), ((), ())),
                                 preferred_element_type=F32) for (b, h, sl), vb in zip(units, vbs)]
        last = (c + 1) * CHUNK - 1
        states = [s * eg_ref[b, last:last + 1, h:h + 1] + kv for (b, h, sl), s, kv in zip(units, states, outer)]
        for (b, h, sl), r, a in zip(units, rs, intra):
            o = r[CHUNK:2 * CHUNK] + a
            on = o * lax.rsqrt(jnp.mean(o * o, axis=-1, keepdims=True) + EPS) * gn_ref[...]
            o_ref[b, rows, sl] = (on * _silu(z_ref[b, rows, sl].astype(F32))).astype(o_ref.dtype)
    for (b, h, sl), s in zip(units, states):
        s_ref[b * N_HEADS + h] = s


def _gdn_scan(u, w, qd, kd, attn, eg, z, out_norm_g, bsz, tp, z_col):
    hw = N_HEADS * HEAD_DIM
    rows = UNIT
    nc = tp // rows
    v3 = lambda a: a.reshape(bsz, tp, a.shape[-1])
    blk3 = pl.BlockSpec((bsz, rows, hw), lambda c: (0, c, 0))
    return pl.pallas_call(
        functools.partial(_gdn_scan_body, bsz=bsz),
        out_shape=jax.ShapeDtypeStruct((bsz, tp, hw), BF16),
        grid=(nc,),
        in_specs=[blk3, blk3, blk3, blk3,
                  pl.BlockSpec((N_HEADS, bsz, rows, CHUNK), lambda c: (0, 0, c, 0)),
                  pl.BlockSpec((bsz, rows, LANES), lambda c: (0, c, 0)),
                  pl.BlockSpec((bsz, rows, hw), lambda c: (0, c, z_col)),
                  pl.BlockSpec((1, HEAD_DIM), lambda c: (0, 0))],
        out_specs=blk3,
        scratch_shapes=[pltpu.VMEM((bsz * N_HEADS, HEAD_DIM, HEAD_DIM), F32)],
        compiler_params=_cparams(("arbitrary",)),
        name="gdn_scan",
    )(v3(u), v3(w), v3(qd), v3(kd), attn.reshape(N_HEADS, bsz, tp, CHUNK), v3(eg), v3(z),
      out_norm_g.astype(F32).reshape(1, HEAD_DIM)).reshape(bsz * tp, hw)


def _lru_body(x_ref, gate_ref, cw_ref, cb_ref, wr_ref, br_ref, wi_ref, bi_ref, lam_ref, o_ref,
              halo_ref, hc_ref, *, bt, n_real):
    first = pl.program_id(1) == 0

    @pl.when(first)
    def _():
        hc_ref[...] = jnp.zeros_like(hc_ref)

    x = _short_conv(x_ref[...].astype(F32), halo_ref, cw_ref, first) + cb_ref[...]
    nblk = wr_ref.shape[0]
    bd = wr_ref.shape[1]
    xb = x.astype(BF16)
    rg = jnp.concatenate([jnp.dot(xb[:, n * bd:(n + 1) * bd], wr_ref[n], preferred_element_type=F32)
                          for n in range(nblk)], axis=1)
    ig = jnp.concatenate([jnp.dot(xb[:, n * bd:(n + 1) * bd], wi_ref[n], preferred_element_type=F32)
                          for n in range(nblk)], axis=1)
    r = _sigmoid(rg + br_ref[...])
    ig = _sigmoid(ig + bi_ref[...])
    log_a = (-LRU_C * _softplus(-lam_ref[...])) * r
    a = jnp.exp(log_a)
    th = jnp.tanh(log_a)
    b = jnp.sqrt(-2.0 * th / (1.0 - th)) * (ig * x)
    b = jnp.where(_row_valid(pl.program_id(1) * bt, bt, n_real), b, 0.0)

    sub = lax.broadcasted_iota(I32, (bt, 1), 0) & 7
    d = 1
    while d < 8:
        keep = sub >= d
        a_sh = jnp.where(keep, pltpu.roll(a, d, 0), 1.0)
        b_sh = jnp.where(keep, pltpu.roll(b, d, 0), 0.0)
        b = a * b_sh + b
        a = a * a_sh
        d *= 2
    carry = hc_ref[0:1, :]
    tiles = []
    for g in range(bt // 8):
        tiles.append(a[8 * g:8 * g + 8] * carry + b[8 * g:8 * g + 8])
        carry = tiles[-1][7:8, :]
    hc_ref[...] = jnp.broadcast_to(carry, hc_ref.shape)
    hs = jnp.concatenate(tiles, axis=0)
    o_ref[...] = (hs * jax.nn.gelu(gate_ref[...].astype(F32))).astype(o_ref.dtype)


def _lru(z, conv_w, conv_b, w_rg, b_rg, w_ig, b_ig, lam, bsz, tp, n_real, x_col, gate_col):
    n = z.shape[0]
    c = conv_w.shape[1]
    bt = UNIT
    nt = tp // bt
    row = lambda v: v.astype(F32).reshape(1, c)
    wspec = pl.BlockSpec(w_rg.shape, lambda b, i: (0, 0, 0))
    vspec = pl.BlockSpec((1, c), lambda b, i: (0, 0))
    return pl.pallas_call(
        functools.partial(_lru_body, bt=bt, n_real=n_real),
        out_shape=jax.ShapeDtypeStruct((n, c), BF16),
        grid=(bsz, nt),
        in_specs=[pl.BlockSpec((bt, c), lambda b, i: (b * nt + i, x_col)),
                  pl.BlockSpec((bt, c), lambda b, i: (b * nt + i, gate_col)),
                  pl.BlockSpec((8, c), lambda b, i: (0, 0)),
                  vspec, wspec, vspec, wspec, vspec, vspec],
        out_specs=pl.BlockSpec((bt, c), lambda b, i: (b * nt + i, 0)),
        scratch_shapes=[pltpu.VMEM((8, c), F32), pltpu.VMEM((8, c), F32)],
        compiler_params=_cparams(("arbitrary", "arbitrary")),
        name="rg_lru",
    )(z, z, jnp.pad(conv_w.astype(F32), ((0, 8 - SHORT_K), (0, 0))), row(conv_b),
      w_rg.astype(BF16), row(b_rg), w_ig.astype(BF16), row(b_ig), row(lam))


def _pick(n, candidates):
    for c in candidates:
        if n % c == 0:
            return c
    raise ValueError(f"no block size in {candidates} divides {n}")


def _pack_weights_body(a_ref, b_ref, o_ref, *, scaled_tiles, scale, head_tiles, gap):
    j = pl.program_id(0)
    tile = o_ref.shape[1]

    @pl.when(j < scaled_tiles)
    def _():
        o_ref[...] = (a_ref[0] * scale).astype(BF16)

    @pl.when((j >= scaled_tiles) & (j < head_tiles))
    def _():
        o_ref[...] = a_ref[0].astype(BF16)

    @pl.when(j >= head_tiles)
    def _():
        x = jnp.concatenate([a_ref[0], b_ref[0]], axis=1)
        o_ref[...] = x[:, gap:gap + tile].astype(BF16)


def _pack_inproj_weights(w_all, idx, head, gap, scaled=0, scale=1.0):
    _, d, total = w_all.shape
    tile = 512
    n_out = total - gap
    return pl.pallas_call(
        functools.partial(_pack_weights_body, scaled_tiles=scaled // tile, scale=scale,
                          head_tiles=head // tile, gap=gap),
        out_shape=jax.ShapeDtypeStruct((d, n_out), BF16),
        grid=(n_out // tile,),
        in_specs=[pl.BlockSpec((1, d, tile), lambda j: (idx, 0, j)),
                  pl.BlockSpec((1, d, LANES), lambda j: (idx, 0, (tile // LANES) * (j + 1)))],
        out_specs=pl.BlockSpec((d, tile), lambda j: (0, j)),
        compiler_params=_cparams(("arbitrary",)),
        name="pack_inproj_weights",
    )(w_all, w_all)


def _small_weights(cols):
    k = cols.shape[1]
    return (jnp.pad(cols.T, ((0, 16 - k), (0, 0))).astype(BF16),
            jnp.pad(cols, ((0, 0), (0, LANES - k))).astype(BF16))


def _router_weights(w_group, b_group, w_expert, b_expert):
    w = jnp.concatenate([w_expert, w_group], axis=1).astype(F32)
    b = jnp.concatenate([b_expert, b_group]).astype(F32)
    k = w.shape[1]
    w = jnp.pad(w, ((0, 0), (0, LANES - k)))
    w_hi = w.astype(BF16)
    w_lo = (w - w_hi.astype(F32)).astype(BF16)
    return jnp.concatenate([w_hi, w_lo], axis=1), jnp.pad(b, (0, LANES - k)).reshape(1, LANES)


def kernel(x, meta_tokens, norm_mix_g, norm_ffn_g, norm_final_g, ab_w_in, ab_forget_b, ab_conv_w, ab_conv_b, ab_ln_g, ab_ln_b, ab_w_out, cd_w_in, cd_qkv_conv_w, cd_a_log, cd_dt_bias, cd_out_norm_g, cd_lru_conv_w, cd_lru_conv_b, cd_w_rg, cd_b_rg, cd_w_ig, cd_b_ig, cd_lru_lambda, cd_w_out, moe_w_group, moe_b_group, moe_w_expert, moe_b_expert, moe_w_gate, moe_w_up, moe_w_down):
    bsz, seq, d = x.shape
    depth = norm_mix_g.shape[0]
    n_real = N_META + seq
    tp = -(-(OFF + seq) // UNIT) * UNIT
    n = bsz * tp
    hw = N_HEADS * HEAD_DIM
    bm = _pick(n, (768, 512, 256))
    bn = 1024

    h = _embed(x.astype(F32).reshape(bsz * seq, d), meta_tokens.astype(F32), bsz, seq, tp)
    row = lambda v: v.astype(F32).reshape(1, -1)

    for layer in range(depth):
        i = layer // 2
        g_mix = row(norm_mix_g[layer])
        if layer % 2 == 0:
            w_in = ab_w_in[i]
            w_main = _pack_inproj_weights(ab_w_in.astype(F32), i, 3 * hw, N_HEADS, scaled=hw,
                                          scale=LOG2E * HEAD_DIM ** -0.5)
            wst, wsc = _small_weights(w_in[:, 3 * hw:3 * hw + N_HEADS])
            z, _, f_slab = _norm_inproj(h, g_mix, w_main, wst, wsc, bm, bn)
            ke = _fox_prep(f_slab, ab_forget_b[i], bsz, tp, n_real)
            ya = _fox_attn(z, ke, bsz, tp)
            yb = _conformer(z, ab_conv_w[i], ab_conv_b[i], ab_ln_g[i], ab_ln_b[i], bsz, tp, 3, 4)
            w_out = ab_w_out[i].astype(BF16)
        else:
            w_in = cd_w_in[i]
            w_main = _pack_inproj_weights(cd_w_in.astype(F32), i, 3 * hw, 2 * N_HEADS)
            wst, wsc = _small_weights(w_in[:, 3 * hw:3 * hw + 2 * N_HEADS])
            z, zt, zc = _norm_inproj(h, g_mix, w_main, wst, wsc, bm, bn)
            u, w, qd, kd, attn, eg = _gdn_pre(z, zc, zt, cd_qkv_conv_w[i], cd_a_log[i], cd_dt_bias[i],
                                              bsz, tp)
            ya = _gdn_scan(u, w, qd, kd, attn, eg, z, cd_out_norm_g[i], bsz, tp, 3)
            yb = _lru(z, cd_lru_conv_w[i], cd_lru_conv_b[i], cd_w_rg[i], cd_b_rg[i], cd_w_ig[i],
                      cd_b_ig[i], cd_lru_lambda[i], bsz, tp, n_real, 4, 5)
            w_out = cd_w_out[i].astype(BF16)
        g_ffn = row(norm_ffn_g[layer])
        w_r, b_r = _router_weights(moe_w_group[layer], moe_b_group[layer],
                                   moe_w_expert[layer], moe_b_expert[layer])
        h, route, counts, xp, er = _outproj_router(ya, yb, h, w_out, g_ffn, w_r, b_r, tp, n_real)
        h = _moe(h, xp, route, counts, er, moe_w_gate, moe_w_up, moe_w_down, layer,
                 row(norm_final_g), tp, n_real, final_norm=(layer == depth - 1))
    return h.reshape(bsz, seq, d).astype(x.dtype)
```
